```python
import math
import jax, jax.numpy as jnp
from jax import lax
import numpy as np

D_MODEL = 1024
BATCH = 8
SEQ = 4096
DEPTH = 4

DN_HEADS = 4
DN_DK = 128
DN_DV = 128
DN_CONV = 4
DN_CHUNK = 64
DN_QK = DN_HEADS * DN_DK
DN_VW = DN_HEADS * DN_DV
DN_QKV = 2 * DN_QK + DN_VW
CF_WIDTH = 512
CF_KERNEL = 31
S5_WIDTH = 512
S5_GROUP = 16
S5_GROUPS = S5_WIDTH // S5_GROUP
S5_STATE = 64
GLA_HEADS = 4
GLA_DK = 64
GLA_DV = 128
GLA_QK = GLA_HEADS * GLA_DK
GLA_VW = GLA_HEADS * GLA_DV
GLA_RANK = 16
GLA_TAU = 16.0
GLA_CHUNK = 16
N_BRANCH = 4
D_FF = 2816
FFN_CONV = 3
LN_EPS = 1e-5
DEEPNORM_ALPHA = (2.0 * DEPTH) ** 0.25
DEEPNORM_BETA = (8.0 * DEPTH) ** -0.25
IN_SIZES = (DN_QKV, DN_HEADS, DN_HEADS, DN_VW, 2 * CF_WIDTH, S5_WIDTH,
            GLA_QK, GLA_QK, GLA_VW, GLA_VW, GLA_RANK, N_BRANCH * D_MODEL)
IN_COLS = DN_QKV + 2 * DN_HEADS + DN_VW + 2 * CF_WIDTH + S5_WIDTH + 2 * GLA_QK + 2 * GLA_VW + GLA_RANK + N_BRANCH * D_MODEL

kernel_name = 'hybrid_deltanet_conformer_s5_gla_deepnorm'


def layer_norm(x, g, b):
    xf = x.astype(jnp.float32)
    mu = xf.mean(-1, keepdims=True)
    var = jnp.square(xf - mu).mean(-1, keepdims=True)
    return ((xf - mu) * lax.rsqrt(var + LN_EPS) * g.astype(jnp.float32) + b.astype(jnp.float32)).astype(x.dtype)


def rms_norm(x, g):
    xf = x.astype(jnp.float32)
    return (xf * lax.rsqrt(jnp.mean(xf * xf, -1, keepdims=True) + LN_EPS) * g.astype(jnp.float32)).astype(x.dtype)


def l2_normalize(x):
    xf = x.astype(jnp.float32)
    return (xf * lax.rsqrt(jnp.sum(xf * xf, -1, keepdims=True) + 1e-6)).astype(x.dtype)


def causal_dwconv(x, w):
    width, ch = w.shape
    return lax.conv_general_dilated(x, w[:, None, :].astype(x.dtype), window_strides=(1,),
                                    padding=[(width - 1, 0)],
                                    dimension_numbers=('NWC', 'WIO', 'NWC'),
                                    feature_group_count=ch)


def to_heads(t, h):
    b, l, c = t.shape
    return t.reshape(b, l, h, c // h).transpose(0, 2, 1, 3)


def from_heads(t):
    b, h, l, d = t.shape
    return t.transpose(0, 2, 1, 3).reshape(b, l, h * d)


def gated_delta_rule(q, k, v, g, beta):
    bsz, nh, seqlen, dk = q.shape
    dv = v.shape[-1]
    c = DN_CHUNK
    n = seqlen // c
    f32 = jnp.float32
    q, k, v = (t.astype(f32).reshape(bsz, nh, n, c, -1) for t in (q, k, v))
    g = g.astype(f32).reshape(bsz, nh, n, c)
    beta = beta.astype(f32).reshape(bsz, nh, n, c)
    gc = jnp.cumsum(g, axis=-1)
    causal = jnp.tril(jnp.ones((c, c), dtype=bool))
    strict = jnp.tril(jnp.ones((c, c), dtype=bool), k=-1)
    decay = jnp.where(causal, jnp.exp(jnp.where(causal, gc[..., :, None] - gc[..., None, :], 0.0)), 0.0)
    kb = k * beta[..., None]
    lower = jnp.where(strict, jnp.einsum('bhnid,bhnjd->bhnij', kb, k) * decay, 0.0)
    system = lower + jnp.eye(c, dtype=f32)
    rhs = jnp.concatenate([v * beta[..., None], kb * jnp.exp(gc)[..., None]], axis=-1)
    sol = lax.linalg.triangular_solve(system, rhs, left_side=True, lower=True, unit_diagonal=True)
    u, w = sol[..., :dv], sol[..., dv:]
    intra = jnp.einsum('bhnid,bhnjd->bhnij', q, k) * decay
    q_dec = q * jnp.exp(gc)[..., None]
    k_dec = k * jnp.exp(gc[..., -1:] - gc)[..., None]
    last = jnp.exp(gc[..., -1])

    def step(state, xs):
        u_c, w_c, a_c, q_c, k_c, l_c = xs
        v_new = u_c - jnp.einsum('bhcd,bhde->bhce', w_c, state)
        o_c = jnp.einsum('bhcd,bhde->bhce', q_c, state) + jnp.einsum('bhij,bhje->bhie', a_c, v_new)
        state = state * l_c[..., None, None] + jnp.einsum('bhcd,bhce->bhde', k_c, v_new)
        return state, o_c

    xs = tuple(jnp.moveaxis(t, 2, 0) for t in (u, w, intra, q_dec, k_dec, last))
    _, o = lax.scan(step, jnp.zeros((bsz, nh, dk, dv), f32), xs)
    return jnp.moveaxis(o, 0, 2).reshape(bsz, nh, seqlen, dv)


def gla_chunked(q, k, v, log_a):
    bsz, nh, seqlen, dk = q.shape
    dv = v.shape[-1]
    c = GLA_CHUNK
    n = seqlen // c
    f32 = jnp.float32
    q, k, v, log_a = (t.astype(f32).reshape(bsz, nh, n, c, -1) for t in (q, k, v, log_a))
    gc = jnp.cumsum(log_a, axis=3)
    q_dec = q * jnp.exp(gc)
    k_inv = k * jnp.exp(-gc)
    k_dec = k * jnp.exp(gc[..., -1:, :] - gc)
    last = jnp.exp(gc[..., -1, :])
    causal = jnp.tril(jnp.ones((c, c), dtype=bool))
    scores = jnp.where(causal, jnp.einsum('bhnid,bhnjd->bhnij', q_dec, k_inv), 0.0)
    intra = jnp.einsum('bhnij,bhnje->bhnie', scores, v)

    def step(state, xs):
        q_c, k_c, v_c, l_c = xs
        o_c = jnp.einsum('bhcd,bhde->bhce', q_c, state)
        state = state * l_c[..., :, None] + jnp.einsum('bhcd,bhce->bhde', k_c, v_c)
        return state, o_c

    xs = tuple(jnp.moveaxis(t, 2, 0) for t in (q_dec, k_dec, v, last))
    _, inter = lax.scan(step, jnp.zeros((bsz, nh, dk, dv), f32), xs)
    return (jnp.moveaxis(inter, 0, 2) + intra).reshape(bsz, nh, seqlen, dv)


def s5_ssm(u, a_re, a_im, log_dt, b_re, b_im, c_re, c_im, d):
    bsz, seqlen, _ = u.shape
    f32 = jnp.float32
    uf = u.astype(f32).reshape(bsz, seqlen, S5_GROUPS, S5_GROUP)
    dt = jnp.exp(log_dt.astype(f32))[:, None]
    ar, ai = a_re.astype(f32), a_im.astype(f32)
    mag = jnp.exp(dt * ar)
    abar_re, abar_im = mag * jnp.cos(dt * ai), mag * jnp.sin(dt * ai)
    den = ar * ar + ai * ai
    nr, ni = abar_re - 1.0, abar_im
    fr, fi = (nr * ar + ni * ai) / den, (ni * ar - nr * ai) / den
    br, bi = b_re.astype(f32), b_im.astype(f32)
    bbar_re = fr[..., None] * br - fi[..., None] * bi
    bbar_im = fr[..., None] * bi + fi[..., None] * br
    bu_re = jnp.einsum('blgh,gnh->blgn', uf, bbar_re)
    bu_im = jnp.einsum('blgh,gnh->blgn', uf, bbar_im)
    a_re_s = jnp.broadcast_to(abar_re, (1, seqlen, S5_GROUPS, S5_STATE))
    a_im_s = jnp.broadcast_to(abar_im, (1, seqlen, S5_GROUPS, S5_STATE))

    def combine(e1, e2):
        a1r, a1i, b1r, b1i = e1
        a2r, a2i, b2r, b2i = e2
        return (a2r * a1r - a2i * a1i, a2r * a1i + a2i * a1r,
                a2r * b1r - a2i * b1i + b2r, a2r * b1i + a2i * b1r + b2i)

    _, _, xr, xi = lax.associative_scan(combine, (a_re_s, a_im_s, bu_re, bu_im), axis=1)
    y = (jnp.einsum('blgn,ghn->blgh', xr, c_re.astype(f32))
         - jnp.einsum('blgn,ghn->blgh', xi, c_im.astype(f32))
         + d.astype(f32).reshape(S5_GROUPS, S5_GROUP) * uf)
    return y.reshape(bsz, seqlen, S5_WIDTH).astype(u.dtype)


def token_mixer(x, w_in, dn_conv, dn_a_log, dn_dt_bias, dn_norm, w_br_dn,
                cf_dw, cf_dw_bias, cf_ln_g, cf_ln_b, w_br_cf,
                s5_a_re, s5_a_im, s5_log_dt, s5_b_re, s5_b_im, s5_c_re, s5_c_im, s5_d, w_br_s5,
                gla_w_alpha, gla_b_alpha, gla_norm, w_br_gla, w_o):
    bsz, seqlen, _ = x.shape
    splits = [int(s) for s in np.cumsum(IN_SIZES)[:-1]]
    h = x @ w_in
    (dn_qkv, dn_a, dn_b, dn_gate, cf_in, s5_in,
     gla_q, gla_k, gla_v, gla_g, gla_lr, gate_logits) = jnp.split(h, splits, axis=-1)

    qkv = jax.nn.silu(causal_dwconv(dn_qkv, dn_conv))
    q, k, v = jnp.split(qkv, [DN_QK, 2 * DN_QK], axis=-1)
    q = l2_normalize(to_heads(q, DN_HEADS)) * (DN_DK ** -0.5)
    k = l2_normalize(to_heads(k, DN_HEADS))
    v = to_heads(v, DN_HEADS)
    g = -jnp.exp(dn_a_log.astype(jnp.float32)) * jax.nn.softplus(dn_a.astype(jnp.float32) + dn_dt_bias.astype(jnp.float32))
    beta = jax.nn.sigmoid(dn_b.astype(jnp.float32))
    o_dn = gated_delta_rule(q, k, v, g.transpose(0, 2, 1), beta.transpose(0, 2, 1)).astype(x.dtype)
    o_dn = rms_norm(o_dn, dn_norm) * jax.nn.silu(to_heads(dn_gate, DN_HEADS))
    y_a = from_heads(o_dn) @ w_br_dn

    cf_a, cf_g = jnp.split(cf_in, 2, axis=-1)
    c = causal_dwconv(cf_a * jax.nn.sigmoid(cf_g), cf_dw) + cf_dw_bias
    y_b = jax.nn.silu(layer_norm(c, cf_ln_g, cf_ln_b)) @ w_br_cf

    z = jax.nn.gelu(s5_ssm(s5_in, s5_a_re, s5_a_im, s5_log_dt, s5_b_re, s5_b_im, s5_c_re, s5_c_im, s5_d))
    z_val, z_gate = jnp.split(z @ w_br_s5, 2, axis=-1)
    y_c = z_val * jax.nn.sigmoid(z_gate)

    log_a = jax.nn.log_sigmoid((gla_lr @ gla_w_alpha + gla_b_alpha).astype(jnp.float32)) / GLA_TAU
    o_gla = gla_chunked(to_heads(gla_q, GLA_HEADS) * (GLA_DK ** -0.5), to_heads(gla_k, GLA_HEADS),
                        to_heads(gla_v, GLA_HEADS), to_heads(log_a, GLA_HEADS)).astype(x.dtype)
    o_gla = rms_norm(o_gla, gla_norm) * jax.nn.silu(to_heads(gla_g, GLA_HEADS))
    y_d = from_heads(o_gla) @ w_br_gla

    g_a, g_b, g_c, g_d = jnp.split(jax.nn.sigmoid(gate_logits), N_BRANCH, axis=-1)
    merged = g_a * y_a + g_b * y_b + g_c * y_c + g_d * y_d
    return merged @ w_o


def conv_ffn(x, w_up, ffn_conv, w_down):
    u = causal_dwconv(x @ w_up, ffn_conv)
    a, b = jnp.split(u, 2, axis=-1)
    return (jax.nn.silu(a) * b) @ w_down


def _fwd_setup_inputs(seed: int = 0) -> dict:
    key = jax.random.key(seed)
    ks = jax.random.split(key, 40)
    f32 = jnp.float32
    L = DEPTH

    def nrm(k, shape, scale):
        return scale * jax.random.normal(k, shape, f32)

    def gain(k, shape):
        return 1.0 + 0.02 * jax.random.normal(k, shape, f32)

    dt_dn = jnp.exp(jax.random.uniform(ks[4], (L, DN_HEADS), f32, math.log(1e-3), math.log(1e-1)))
    n_idx = jnp.arange(S5_STATE, dtype=f32)
    return {
        'x': nrm(ks[0], (BATCH, SEQ, D_MODEL), 1.0),
        'w_in': nrm(ks[1], (L, D_MODEL, IN_COLS), D_MODEL ** -0.5),
        'dn_conv': nrm(ks[2], (L, DN_CONV, DN_QKV), DN_CONV ** -0.5),
        'dn_a_log': jnp.log(jax.random.uniform(ks[3], (L, DN_HEADS), f32, 1.0, 16.0)),
        'dn_dt_bias': dt_dn + jnp.log(-jnp.expm1(-dt_dn)),
        'dn_norm': gain(ks[5], (L, DN_DV)),
        'w_br_dn': nrm(ks[6], (L, DN_VW, D_MODEL), DN_VW ** -0.5),
        'cf_dw': nrm(ks[7], (L, CF_KERNEL, CF_WIDTH), CF_KERNEL ** -0.5),
        'cf_dw_bias': nrm(ks[8], (L, CF_WIDTH), 0.02),
        'cf_ln_g': gain(ks[9], (L, CF_WIDTH)),
        'cf_ln_b': nrm(ks[10], (L, CF_WIDTH), 0.02),
        'w_br_cf': nrm(ks[11], (L, CF_WIDTH, D_MODEL), CF_WIDTH ** -0.5),
        's5_a_re': -0.5 + nrm(ks[12], (L, S5_GROUPS, S5_STATE), 0.01),
        's5_a_im': math.pi * n_idx + nrm(ks[13], (L, S5_GROUPS, S5_STATE), 0.01),
        's5_log_dt': jax.random.uniform(ks[14], (L, S5_GROUPS), f32, math.log(1e-3), math.log(1e-1)),
        's5_b_re': nrm(ks[15], (L, S5_GROUPS, S5_STATE, S5_GROUP), (2.0 * S5_GROUP) ** -0.5),
        's5_b_im': nrm(ks[16], (L, S5_GROUPS, S5_STATE, S5_GROUP), (2.0 * S5_GROUP) ** -0.5),
        's5_c_re': nrm(ks[17], (L, S5_GROUPS, S5_GROUP, S5_STATE), S5_STATE ** -0.5),
        's5_c_im': nrm(ks[18], (L, S5_GROUPS, S5_GROUP, S5_STATE), S5_STATE ** -0.5),
        's5_d': nrm(ks[19], (L, S5_WIDTH), 1.0),
        'w_br_s5': nrm(ks[20], (L, S5_WIDTH, 2 * D_MODEL), S5_WIDTH ** -0.5),
        'gla_w_alpha': nrm(ks[21], (L, GLA_RANK, GLA_QK), GLA_RANK ** -0.5),
        'gla_b_alpha': nrm(ks[22], (L, GLA_QK), 0.02),
        'gla_norm': gain(ks[23], (L, GLA_DV)),
        'w_br_gla': nrm(ks[24], (L, GLA_VW, D_MODEL), GLA_VW ** -0.5),
        'w_o': nrm(ks[25], (L, D_MODEL, D_MODEL), DEEPNORM_BETA * D_MODEL ** -0.5),
        'ln1_g': gain(ks[26], (L, D_MODEL)),
        'ln1_b': nrm(ks[27], (L, D_MODEL), 0.02),
        'w_up': nrm(ks[28], (L, D_MODEL, 2 * D_FF), D_MODEL ** -0.5),
        'ffn_conv': nrm(ks[29], (L, FFN_CONV, 2 * D_FF), FFN_CONV ** -0.5),
        'w_down': nrm(ks[30], (L, D_FF, D_MODEL), DEEPNORM_BETA * D_FF ** -0.5),
        'ln2_g': gain(ks[31], (L, D_MODEL)),
        'ln2_b': nrm(ks[32], (L, D_MODEL), 0.02),
    }


def _fwd_reference(x, w_in, dn_conv, dn_a_log, dn_dt_bias, dn_norm, w_br_dn,
              cf_dw, cf_dw_bias, cf_ln_g, cf_ln_b, w_br_cf,
              s5_a_re, s5_a_im, s5_log_dt, s5_b_re, s5_b_im, s5_c_re, s5_c_im, s5_d, w_br_s5,
              gla_w_alpha, gla_b_alpha, gla_norm, w_br_gla, w_o, ln1_g, ln1_b,
              w_up, ffn_conv, w_down, ln2_g, ln2_b):
    for l in range(DEPTH):
        mix = token_mixer(x, w_in[l], dn_conv[l], dn_a_log[l], dn_dt_bias[l], dn_norm[l], w_br_dn[l],
                          cf_dw[l], cf_dw_bias[l], cf_ln_g[l], cf_ln_b[l], w_br_cf[l],
                          s5_a_re[l], s5_a_im[l], s5_log_dt[l], s5_b_re[l], s5_b_im[l],
                          s5_c_re[l], s5_c_im[l], s5_d[l], w_br_s5[l],
                          gla_w_alpha[l], gla_b_alpha[l], gla_norm[l], w_br_gla[l], w_o[l])
        x = layer_norm(DEEPNORM_ALPHA * x + mix, ln1_g[l], ln1_b[l])
        x = layer_norm(DEEPNORM_ALPHA * x + conv_ffn(x, w_up[l], ffn_conv[l], w_down[l]), ln2_g[l], ln2_b[l])
    return x


import jax as _jax
import jax.numpy as _jnp

TWIN_FORMAT = 'train_step'
FWD_PARAMS = ['x', 'w_in', 'dn_conv', 'dn_a_log', 'dn_dt_bias', 'dn_norm', 'w_br_dn', 'cf_dw', 'cf_dw_bias', 'cf_ln_g', 'cf_ln_b', 'w_br_cf', 's5_a_re', 's5_a_im', 's5_log_dt', 's5_b_re', 's5_b_im', 's5_c_re', 's5_c_im', 's5_d', 'w_br_s5', 'gla_w_alpha', 'gla_b_alpha', 'gla_norm', 'w_br_gla', 'w_o', 'ln1_g', 'ln1_b', 'w_up', 'ffn_conv', 'w_down', 'ln2_g', 'ln2_b']
TWIN_WEIGHTS = ['w_in', 'dn_conv', 'dn_a_log', 'dn_dt_bias', 'dn_norm', 'w_br_dn', 'cf_dw', 'cf_dw_bias', 'cf_ln_g', 'cf_ln_b', 'w_br_cf', 's5_a_re', 's5_a_im', 's5_log_dt', 's5_b_re', 's5_b_im', 's5_c_re', 's5_c_im', 's5_d', 'w_br_s5', 'gla_w_alpha', 'gla_b_alpha', 'gla_norm', 'w_br_gla', 'w_o', 'ln1_g', 'ln1_b', 'w_up', 'ffn_conv', 'w_down', 'ln2_g', 'ln2_b']
TWIN_DIFF_INPUT = 'x'
TWIN_INPUTS = ['x', 'w_in', 'dn_conv', 'dn_a_log', 'dn_dt_bias', 'dn_norm', 'w_br_dn', 'cf_dw', 'cf_dw_bias', 'cf_ln_g', 'cf_ln_b', 'w_br_cf', 's5_a_re', 's5_a_im', 's5_log_dt', 's5_b_re', 's5_b_im', 's5_c_re', 's5_c_im', 's5_d', 'w_br_s5', 'gla_w_alpha', 'gla_b_alpha', 'gla_norm', 'w_br_gla', 'w_o', 'ln1_g', 'ln1_b', 'w_up', 'ffn_conv', 'w_down', 'ln2_g', 'ln2_b', 'loss_target', 'm_w_in', 'm_dn_conv', 'm_dn_a_log', 'm_dn_dt_bias', 'm_dn_norm', 'm_w_br_dn', 'm_cf_dw', 'm_cf_dw_bias', 'm_cf_ln_g', 'm_cf_ln_b', 'm_w_br_cf', 'm_s5_a_re', 'm_s5_a_im', 'm_s5_log_dt', 'm_s5_b_re', 'm_s5_b_im', 'm_s5_c_re', 'm_s5_c_im', 'm_s5_d', 'm_w_br_s5', 'm_gla_w_alpha', 'm_gla_b_alpha', 'm_gla_norm', 'm_w_br_gla', 'm_w_o', 'm_ln1_g', 'm_ln1_b', 'm_w_up', 'm_ffn_conv', 'm_w_down', 'm_ln2_g', 'm_ln2_b', 'v_w_in', 'v_dn_conv', 'v_dn_a_log', 'v_dn_dt_bias', 'v_dn_norm', 'v_w_br_dn', 'v_cf_dw', 'v_cf_dw_bias', 'v_cf_ln_g', 'v_cf_ln_b', 'v_w_br_cf', 'v_s5_a_re', 'v_s5_a_im', 'v_s5_log_dt', 'v_s5_b_re', 'v_s5_b_im', 'v_s5_c_re', 'v_s5_c_im', 'v_s5_d', 'v_w_br_s5', 'v_gla_w_alpha', 'v_gla_b_alpha', 'v_gla_norm', 'v_w_br_gla', 'v_w_o', 'v_ln1_g', 'v_ln1_b', 'v_w_up', 'v_ffn_conv', 'v_w_down', 'v_ln2_g', 'v_ln2_b']
TWIN_OUTPUTS = ['loss', 'grad_x', 'grad_w_in', 'grad_dn_conv', 'grad_dn_a_log', 'grad_dn_dt_bias', 'grad_dn_norm', 'grad_w_br_dn', 'grad_cf_dw', 'grad_cf_dw_bias', 'grad_cf_ln_g', 'grad_cf_ln_b', 'grad_w_br_cf', 'grad_s5_a_re', 'grad_s5_a_im', 'grad_s5_log_dt', 'grad_s5_b_re', 'grad_s5_b_im', 'grad_s5_c_re', 'grad_s5_c_im', 'grad_s5_d', 'grad_w_br_s5', 'grad_gla_w_alpha', 'grad_gla_b_alpha', 'grad_gla_norm', 'grad_w_br_gla', 'grad_w_o', 'grad_ln1_g', 'grad_ln1_b', 'grad_w_up', 'grad_ffn_conv', 'grad_w_down', 'grad_ln2_g', 'grad_ln2_b', 'delta_w_in', 'delta_dn_conv', 'delta_dn_a_log', 'delta_dn_dt_bias', 'delta_dn_norm', 'delta_w_br_dn', 'delta_cf_dw', 'delta_cf_dw_bias', 'delta_cf_ln_g', 'delta_cf_ln_b', 'delta_w_br_cf', 'delta_s5_a_re', 'delta_s5_a_im', 'delta_s5_log_dt', 'delta_s5_b_re', 'delta_s5_b_im', 'delta_s5_c_re', 'delta_s5_c_im', 'delta_s5_d', 'delta_w_br_s5', 'delta_gla_w_alpha', 'delta_gla_b_alpha', 'delta_gla_norm', 'delta_w_br_gla', 'delta_w_o', 'delta_ln1_g', 'delta_ln1_b', 'delta_w_up', 'delta_ffn_conv', 'delta_w_down', 'delta_ln2_g', 'delta_ln2_b', 'new_m_w_in', 'new_m_dn_conv', 'new_m_dn_a_log', 'new_m_dn_dt_bias', 'new_m_dn_norm', 'new_m_w_br_dn', 'new_m_cf_dw', 'new_m_cf_dw_bias', 'new_m_cf_ln_g', 'new_m_cf_ln_b', 'new_m_w_br_cf', 'new_m_s5_a_re', 'new_m_s5_a_im', 'new_m_s5_log_dt', 'new_m_s5_b_re', 'new_m_s5_b_im', 'new_m_s5_c_re', 'new_m_s5_c_im', 'new_m_s5_d', 'new_m_w_br_s5', 'new_m_gla_w_alpha', 'new_m_gla_b_alpha', 'new_m_gla_norm', 'new_m_w_br_gla', 'new_m_w_o', 'new_m_ln1_g', 'new_m_ln1_b', 'new_m_w_up', 'new_m_ffn_conv', 'new_m_w_down', 'new_m_ln2_g', 'new_m_ln2_b', 'new_v_w_in', 'new_v_dn_conv', 'new_v_dn_a_log', 'new_v_dn_dt_bias', 'new_v_dn_norm', 'new_v_w_br_dn', 'new_v_cf_dw', 'new_v_cf_dw_bias', 'new_v_cf_ln_g', 'new_v_cf_ln_b', 'new_v_w_br_cf', 'new_v_s5_a_re', 'new_v_s5_a_im', 'new_v_s5_log_dt', 'new_v_s5_b_re', 'new_v_s5_b_im', 'new_v_s5_c_re', 'new_v_s5_c_im', 'new_v_s5_d', 'new_v_w_br_s5', 'new_v_gla_w_alpha', 'new_v_gla_b_alpha', 'new_v_gla_norm', 'new_v_w_br_gla', 'new_v_w_o', 'new_v_ln1_g', 'new_v_ln1_b', 'new_v_w_up', 'new_v_ffn_conv', 'new_v_w_down', 'new_v_ln2_g', 'new_v_ln2_b']
TWIN_LEAF_KINDS = {'loss': 'loss', 'grad_x': 'grad_x', 'grad_w_in': 'grad_w', 'grad_dn_conv': 'grad_w', 'grad_dn_a_log': 'grad_w', 'grad_dn_dt_bias': 'grad_w', 'grad_dn_norm': 'grad_w', 'grad_w_br_dn': 'grad_w', 'grad_cf_dw': 'grad_w', 'grad_cf_dw_bias': 'grad_w', 'grad_cf_ln_g': 'grad_w', 'grad_cf_ln_b': 'grad_w', 'grad_w_br_cf': 'grad_w', 'grad_s5_a_re': 'grad_w', 'grad_s5_a_im': 'grad_w', 'grad_s5_log_dt': 'grad_w', 'grad_s5_b_re': 'grad_w', 'grad_s5_b_im': 'grad_w', 'grad_s5_c_re': 'grad_w', 'grad_s5_c_im': 'grad_w', 'grad_s5_d': 'grad_w', 'grad_w_br_s5': 'grad_w', 'grad_gla_w_alpha': 'grad_w', 'grad_gla_b_alpha': 'grad_w', 'grad_gla_norm': 'grad_w', 'grad_w_br_gla': 'grad_w', 'grad_w_o': 'grad_w', 'grad_ln1_g': 'grad_w', 'grad_ln1_b': 'grad_w', 'grad_w_up': 'grad_w', 'grad_ffn_conv': 'grad_w', 'grad_w_down': 'grad_w', 'grad_ln2_g': 'grad_w', 'grad_ln2_b': 'grad_w', 'delta_w_in': 'delta_w', 'delta_dn_conv': 'delta_w', 'delta_dn_a_log': 'delta_w', 'delta_dn_dt_bias': 'delta_w', 'delta_dn_norm': 'delta_w', 'delta_w_br_dn': 'delta_w', 'delta_cf_dw': 'delta_w', 'delta_cf_dw_bias': 'delta_w', 'delta_cf_ln_g': 'delta_w', 'delta_cf_ln_b': 'delta_w', 'delta_w_br_cf': 'delta_w', 'delta_s5_a_re': 'delta_w', 'delta_s5_a_im': 'delta_w', 'delta_s5_log_dt': 'delta_w', 'delta_s5_b_re': 'delta_w', 'delta_s5_b_im': 'delta_w', 'delta_s5_c_re': 'delta_w', 'delta_s5_c_im': 'delta_w', 'delta_s5_d': 'delta_w', 'delta_w_br_s5': 'delta_w', 'delta_gla_w_alpha': 'delta_w', 'delta_gla_b_alpha': 'delta_w', 'delta_gla_norm': 'delta_w', 'delta_w_br_gla': 'delta_w', 'delta_w_o': 'delta_w', 'delta_ln1_g': 'delta_w', 'delta_ln1_b': 'delta_w', 'delta_w_up': 'delta_w', 'delta_ffn_conv': 'delta_w', 'delta_w_down': 'delta_w', 'delta_ln2_g': 'delta_w', 'delta_ln2_b': 'delta_w', 'new_m_w_in': 'new_m', 'new_m_dn_conv': 'new_m', 'new_m_dn_a_log': 'new_m', 'new_m_dn_dt_bias': 'new_m', 'new_m_dn_norm': 'new_m', 'new_m_w_br_dn': 'new_m', 'new_m_cf_dw': 'new_m', 'new_m_cf_dw_bias': 'new_m', 'new_m_cf_ln_g': 'new_m', 'new_m_cf_ln_b': 'new_m', 'new_m_w_br_cf': 'new_m', 'new_m_s5_a_re': 'new_m', 'new_m_s5_a_im': 'new_m', 'new_m_s5_log_dt': 'new_m', 'new_m_s5_b_re': 'new_m', 'new_m_s5_b_im': 'new_m', 'new_m_s5_c_re': 'new_m', 'new_m_s5_c_im': 'new_m', 'new_m_s5_d': 'new_m', 'new_m_w_br_s5': 'new_m', 'new_m_gla_w_alpha': 'new_m', 'new_m_gla_b_alpha': 'new_m', 'new_m_gla_norm': 'new_m', 'new_m_w_br_gla': 'new_m', 'new_m_w_o': 'new_m', 'new_m_ln1_g': 'new_m', 'new_m_ln1_b': 'new_m', 'new_m_w_up': 'new_m', 'new_m_ffn_conv': 'new_m', 'new_m_w_down': 'new_m', 'new_m_ln2_g': 'new_m', 'new_m_ln2_b': 'new_m', 'new_v_w_in': 'new_v', 'new_v_dn_conv': 'new_v', 'new_v_dn_a_log': 'new_v', 'new_v_dn_dt_bias': 'new_v', 'new_v_dn_norm': 'new_v', 'new_v_w_br_dn': 'new_v', 'new_v_cf_dw': 'new_v', 'new_v_cf_dw_bias': 'new_v', 'new_v_cf_ln_g': 'new_v', 'new_v_cf_ln_b': 'new_v', 'new_v_w_br_cf': 'new_v', 'new_v_s5_a_re': 'new_v', 'new_v_s5_a_im': 'new_v', 'new_v_s5_log_dt': 'new_v', 'new_v_s5_b_re': 'new_v', 'new_v_s5_b_im': 'new_v', 'new_v_s5_c_re': 'new_v', 'new_v_s5_c_im': 'new_v', 'new_v_s5_d': 'new_v', 'new_v_w_br_s5': 'new_v', 'new_v_gla_w_alpha': 'new_v', 'new_v_gla_b_alpha': 'new_v', 'new_v_gla_norm': 'new_v', 'new_v_w_br_gla': 'new_v', 'new_v_w_o': 'new_v', 'new_v_ln1_g': 'new_v', 'new_v_ln1_b': 'new_v', 'new_v_w_up': 'new_v', 'new_v_ffn_conv': 'new_v', 'new_v_w_down': 'new_v', 'new_v_ln2_g': 'new_v', 'new_v_ln2_b': 'new_v'}


def _forward(args):
    return _fwd_reference(*[args[k] for k in FWD_PARAMS])


def _output_shape():
    out = _jax.eval_shape(lambda: _forward(_fwd_setup_inputs(0)))
    return out.shape, out.dtype

N_MICROBATCH = 1
ADAM_LR = 0.001
ADAM_B1 = 0.9
ADAM_B2 = 0.999
ADAM_EPS = 1e-08
ADAM_WD = 0.01
ADAM_STEP = 10
PER_EXAMPLE_BATCH_AXIS = {'x': 0, 'loss_target': 0}
SHARED_INPUTS = []
_WEIGHT_DTYPES = {'w_in': _jnp.float32, 'dn_conv': _jnp.float32, 'dn_a_log': _jnp.float32, 'dn_dt_bias': _jnp.float32, 'dn_norm': _jnp.float32, 'w_br_dn': _jnp.float32, 'cf_dw': _jnp.float32, 'cf_dw_bias': _jnp.float32, 'cf_ln_g': _jnp.float32, 'cf_ln_b': _jnp.float32, 'w_br_cf': _jnp.float32, 's5_a_re': _jnp.float32, 's5_a_im': _jnp.float32, 's5_log_dt': _jnp.float32, 's5_b_re': _jnp.float32, 's5_b_im': _jnp.float32, 's5_c_re': _jnp.float32, 's5_c_im': _jnp.float32, 's5_d': _jnp.float32, 'w_br_s5': _jnp.float32, 'gla_w_alpha': _jnp.float32, 'gla_b_alpha': _jnp.float32, 'gla_norm': _jnp.float32, 'w_br_gla': _jnp.float32, 'w_o': _jnp.float32, 'ln1_g': _jnp.float32, 'ln1_b': _jnp.float32, 'w_up': _jnp.float32, 'ffn_conv': _jnp.float32, 'w_down': _jnp.float32, 'ln2_g': _jnp.float32, 'ln2_b': _jnp.float32}
MOMENT_SCALE = {'w_in': 1.471259e-02, 'dn_conv': 1.582891e-02, 'dn_a_log': 8.302237e-02, 'dn_dt_bias': 8.216400e-02, 'dn_norm': 4.061877e-02, 'w_br_dn': 1.450164e-02, 'cf_dw': 2.213798e-02, 'cf_dw_bias': 6.258713e-02, 'cf_ln_g': 3.183349e-02, 'cf_ln_b': 3.932470e-02, 'w_br_cf': 1.748159e-02, 's5_a_re': 1.051502e-03, 's5_a_im': 1.000356e-03, 's5_log_dt': 8.698966e-01, 's5_b_re': 6.473831e-04, 's5_b_im': 6.603299e-04, 's5_c_re': 9.168085e-04, 's5_c_im': 9.401845e-04, 's5_d': 1.594852e-02, 'w_br_s5': 8.023937e-03, 'gla_w_alpha': 3.667308e-03, 'gla_b_alpha': 1.484113e-02, 'gla_norm': 4.275599e-02, 'w_br_gla': 1.493945e-02, 'w_o': 6.939161e-02, 'ln1_g': 1.034271e+00, 'ln1_b': 4.535254e-01, 'w_up': 1.729890e-02, 'ffn_conv': 1.727989e-02, 'w_down': 6.729019e-02, 'ln2_g': 1.605733e+01, 'ln2_b': 1.267537e+00}


def _to_microbatches(a, axis):
    t = _jnp.moveaxis(a, axis, 0)
    t = t.reshape((N_MICROBATCH, t.shape[0] // N_MICROBATCH) + t.shape[1:])
    return _jnp.moveaxis(t, 1, axis + 1)


def setup_inputs(seed: int = 0) -> dict:
    inp = _fwd_setup_inputs(seed)
    key = _jax.random.fold_in(_jax.random.key(seed), 7919)
    shape, _ = _output_shape()
    out = dict(inp)
    out["loss_target"] = _jax.random.normal(_jax.random.fold_in(key, 0), shape, _jnp.float32)
    for i, name in enumerate(TWIN_WEIGHTS):
        w = inp[name].astype(_jnp.float32)
        if MOMENT_SCALE is None:
            s = _jnp.sqrt(_jnp.mean(_jnp.square(w)) + 1e-30)
        else:
            s = MOMENT_SCALE[name]
        km, kv = _jax.random.split(_jax.random.fold_in(key, i + 1))
        out[name] = w
        out["m_" + name] = s * _jax.random.normal(km, w.shape, _jnp.float32)
        out["v_" + name] = (s * s) * _jax.random.uniform(kv, w.shape, _jnp.float32, 0.5, 1.5)
    if N_MICROBATCH > 1:
        for name, axis in PER_EXAMPLE_BATCH_AXIS.items():
            out[name] = _to_microbatches(out[name], axis)
    return {'x': out['x'], 'w_in': out['w_in'], 'dn_conv': out['dn_conv'], 'dn_a_log': out['dn_a_log'], 'dn_dt_bias': out['dn_dt_bias'], 'dn_norm': out['dn_norm'], 'w_br_dn': out['w_br_dn'], 'cf_dw': out['cf_dw'], 'cf_dw_bias': out['cf_dw_bias'], 'cf_ln_g': out['cf_ln_g'], 'cf_ln_b': out['cf_ln_b'], 'w_br_cf': out['w_br_cf'], 's5_a_re': out['s5_a_re'], 's5_a_im': out['s5_a_im'], 's5_log_dt': out['s5_log_dt'], 's5_b_re': out['s5_b_re'], 's5_b_im': out['s5_b_im'], 's5_c_re': out['s5_c_re'], 's5_c_im': out['s5_c_im'], 's5_d': out['s5_d'], 'w_br_s5': out['w_br_s5'], 'gla_w_alpha': out['gla_w_alpha'], 'gla_b_alpha': out['gla_b_alpha'], 'gla_norm': out['gla_norm'], 'w_br_gla': out['w_br_gla'], 'w_o': out['w_o'], 'ln1_g': out['ln1_g'], 'ln1_b': out['ln1_b'], 'w_up': out['w_up'], 'ffn_conv': out['ffn_conv'], 'w_down': out['w_down'], 'ln2_g': out['ln2_g'], 'ln2_b': out['ln2_b'], 'loss_target': out['loss_target'], 'm_w_in': out['m_w_in'], 'm_dn_conv': out['m_dn_conv'], 'm_dn_a_log': out['m_dn_a_log'], 'm_dn_dt_bias': out['m_dn_dt_bias'], 'm_dn_norm': out['m_dn_norm'], 'm_w_br_dn': out['m_w_br_dn'], 'm_cf_dw': out['m_cf_dw'], 'm_cf_dw_bias': out['m_cf_dw_bias'], 'm_cf_ln_g': out['m_cf_ln_g'], 'm_cf_ln_b': out['m_cf_ln_b'], 'm_w_br_cf': out['m_w_br_cf'], 'm_s5_a_re': out['m_s5_a_re'], 'm_s5_a_im': out['m_s5_a_im'], 'm_s5_log_dt': out['m_s5_log_dt'], 'm_s5_b_re': out['m_s5_b_re'], 'm_s5_b_im': out['m_s5_b_im'], 'm_s5_c_re': out['m_s5_c_re'], 'm_s5_c_im': out['m_s5_c_im'], 'm_s5_d': out['m_s5_d'], 'm_w_br_s5': out['m_w_br_s5'], 'm_gla_w_alpha': out['m_gla_w_alpha'], 'm_gla_b_alpha': out['m_gla_b_alpha'], 'm_gla_norm': out['m_gla_norm'], 'm_w_br_gla': out['m_w_br_gla'], 'm_w_o': out['m_w_o'], 'm_ln1_g': out['m_ln1_g'], 'm_ln1_b': out['m_ln1_b'], 'm_w_up': out['m_w_up'], 'm_ffn_conv': out['m_ffn_conv'], 'm_w_down': out['m_w_down'], 'm_ln2_g': out['m_ln2_g'], 'm_ln2_b': out['m_ln2_b'], 'v_w_in': out['v_w_in'], 'v_dn_conv': out['v_dn_conv'], 'v_dn_a_log': out['v_dn_a_log'], 'v_dn_dt_bias': out['v_dn_dt_bias'], 'v_dn_norm': out['v_dn_norm'], 'v_w_br_dn': out['v_w_br_dn'], 'v_cf_dw': out['v_cf_dw'], 'v_cf_dw_bias': out['v_cf_dw_bias'], 'v_cf_ln_g': out['v_cf_ln_g'], 'v_cf_ln_b': out['v_cf_ln_b'], 'v_w_br_cf': out['v_w_br_cf'], 'v_s5_a_re': out['v_s5_a_re'], 'v_s5_a_im': out['v_s5_a_im'], 'v_s5_log_dt': out['v_s5_log_dt'], 'v_s5_b_re': out['v_s5_b_re'], 'v_s5_b_im': out['v_s5_b_im'], 'v_s5_c_re': out['v_s5_c_re'], 'v_s5_c_im': out['v_s5_c_im'], 'v_s5_d': out['v_s5_d'], 'v_w_br_s5': out['v_w_br_s5'], 'v_gla_w_alpha': out['v_gla_w_alpha'], 'v_gla_b_alpha': out['v_gla_b_alpha'], 'v_gla_norm': out['v_gla_norm'], 'v_w_br_gla': out['v_w_br_gla'], 'v_w_o': out['v_w_o'], 'v_ln1_g': out['v_ln1_g'], 'v_ln1_b': out['v_ln1_b'], 'v_w_up': out['v_w_up'], 'v_ffn_conv': out['v_ffn_conv'], 'v_w_down': out['v_w_down'], 'v_ln2_g': out['v_ln2_g'], 'v_ln2_b': out['v_ln2_b']}


def _loss(weights, diff, rest, loss_target):
    with _jax.named_scope("forward"):
        args = {**rest, TWIN_DIFF_INPUT: diff, **{k: w.astype(_WEIGHT_DTYPES[k]) for k, w in weights.items()}}
        y = _forward(args)
    with _jax.named_scope("loss_head"):
        err = _jnp.square(y.astype(_jnp.float32) - loss_target)
        return 0.5 * _jnp.sum(_jnp.mean(err, axis=-1)) if err.ndim else 0.5 * err


def _adamw(w, g, m, v):
    m = ADAM_B1 * m + (1.0 - ADAM_B1) * g
    v = ADAM_B2 * v + (1.0 - ADAM_B2) * _jnp.square(g)
    m_hat = m / (1.0 - ADAM_B1 ** ADAM_STEP)
    v_hat = v / (1.0 - ADAM_B2 ** ADAM_STEP)
    delta = -ADAM_LR * (m_hat / (_jnp.sqrt(v_hat) + ADAM_EPS) + ADAM_WD * w)
    return delta, m, v


def reference(x, w_in, dn_conv, dn_a_log, dn_dt_bias, dn_norm, w_br_dn, cf_dw, cf_dw_bias, cf_ln_g, cf_ln_b, w_br_cf, s5_a_re, s5_a_im, s5_log_dt, s5_b_re, s5_b_im, s5_c_re, s5_c_im, s5_d, w_br_s5, gla_w_alpha, gla_b_alpha, gla_norm, w_br_gla, w_o, ln1_g, ln1_b, w_up, ffn_conv, w_down, ln2_g, ln2_b, loss_target, m_w_in, m_dn_conv, m_dn_a_log, m_dn_dt_bias, m_dn_norm, m_w_br_dn, m_cf_dw, m_cf_dw_bias, m_cf_ln_g, m_cf_ln_b, m_w_br_cf, m_s5_a_re, m_s5_a_im, m_s5_log_dt, m_s5_b_re, m_s5_b_im, m_s5_c_re, m_s5_c_im, m_s5_d, m_w_br_s5, m_gla_w_alpha, m_gla_b_alpha, m_gla_norm, m_w_br_gla, m_w_o, m_ln1_g, m_ln1_b, m_w_up, m_ffn_conv, m_w_down, m_ln2_g, m_ln2_b, v_w_in, v_dn_conv, v_dn_a_log, v_dn_dt_bias, v_dn_norm, v_w_br_dn, v_cf_dw, v_cf_dw_bias, v_cf_ln_g, v_cf_ln_b, v_w_br_cf, v_s5_a_re, v_s5_a_im, v_s5_log_dt, v_s5_b_re, v_s5_b_im, v_s5_c_re, v_s5_c_im, v_s5_d, v_w_br_s5, v_gla_w_alpha, v_gla_b_alpha, v_gla_norm, v_w_br_gla, v_w_o, v_ln1_g, v_ln1_b, v_w_up, v_ffn_conv, v_w_down, v_ln2_g, v_ln2_b):
    given = dict(x=x, w_in=w_in, dn_conv=dn_conv, dn_a_log=dn_a_log, dn_dt_bias=dn_dt_bias, dn_norm=dn_norm, w_br_dn=w_br_dn, cf_dw=cf_dw, cf_dw_bias=cf_dw_bias, cf_ln_g=cf_ln_g, cf_ln_b=cf_ln_b, w_br_cf=w_br_cf, s5_a_re=s5_a_re, s5_a_im=s5_a_im, s5_log_dt=s5_log_dt, s5_b_re=s5_b_re, s5_b_im=s5_b_im, s5_c_re=s5_c_re, s5_c_im=s5_c_im, s5_d=s5_d, w_br_s5=w_br_s5, gla_w_alpha=gla_w_alpha, gla_b_alpha=gla_b_alpha, gla_norm=gla_norm, w_br_gla=w_br_gla, w_o=w_o, ln1_g=ln1_g, ln1_b=ln1_b, w_up=w_up, ffn_conv=ffn_conv, w_down=w_down, ln2_g=ln2_g, ln2_b=ln2_b, loss_target=loss_target, m_w_in=m_w_in, m_dn_conv=m_dn_conv, m_dn_a_log=m_dn_a_log, m_dn_dt_bias=m_dn_dt_bias, m_dn_norm=m_dn_norm, m_w_br_dn=m_w_br_dn, m_cf_dw=m_cf_dw, m_cf_dw_bias=m_cf_dw_bias, m_cf_ln_g=m_cf_ln_g, m_cf_ln_b=m_cf_ln_b, m_w_br_cf=m_w_br_cf, m_s5_a_re=m_s5_a_re, m_s5_a_im=m_s5_a_im, m_s5_log_dt=m_s5_log_dt, m_s5_b_re=m_s5_b_re, m_s5_b_im=m_s5_b_im, m_s5_c_re=m_s5_c_re, m_s5_c_im=m_s5_c_im, m_s5_d=m_s5_d, m_w_br_s5=m_w_br_s5, m_gla_w_alpha=m_gla_w_alpha, m_gla_b_alpha=m_gla_b_alpha, m_gla_norm=m_gla_norm, m_w_br_gla=m_w_br_gla, m_w_o=m_w_o, m_ln1_g=m_ln1_g, m_ln1_b=m_ln1_b, m_w_up=m_w_up, m_ffn_conv=m_ffn_conv, m_w_down=m_w_down, m_ln2_g=m_ln2_g, m_ln2_b=m_ln2_b, v_w_in=v_w_in, v_dn_conv=v_dn_conv, v_dn_a_log=v_dn_a_log, v_dn_dt_bias=v_dn_dt_bias, v_dn_norm=v_dn_norm, v_w_br_dn=v_w_br_dn, v_cf_dw=v_cf_dw, v_cf_dw_bias=v_cf_dw_bias, v_cf_ln_g=v_cf_ln_g, v_cf_ln_b=v_cf_ln_b, v_w_br_cf=v_w_br_cf, v_s5_a_re=v_s5_a_re, v_s5_a_im=v_s5_a_im, v_s5_log_dt=v_s5_log_dt, v_s5_b_re=v_s5_b_re, v_s5_b_im=v_s5_b_im, v_s5_c_re=v_s5_c_re, v_s5_c_im=v_s5_c_im, v_s5_d=v_s5_d, v_w_br_s5=v_w_br_s5, v_gla_w_alpha=v_gla_w_alpha, v_gla_b_alpha=v_gla_b_alpha, v_gla_norm=v_gla_norm, v_w_br_gla=v_w_br_gla, v_w_o=v_w_o, v_ln1_g=v_ln1_g, v_ln1_b=v_ln1_b, v_w_up=v_w_up, v_ffn_conv=v_ffn_conv, v_w_down=v_w_down, v_ln2_g=v_ln2_g, v_ln2_b=v_ln2_b)
    weights = {n: given[n] for n in TWIN_WEIGHTS}
    shared = {n: given[n] for n in SHARED_INPUTS}
    per_example = {n: given[n] for n in ['x']}
    grad_fn = _jax.value_and_grad(_loss, argnums=(0, 1))

    def one_microbatch(ex, loss_target):
        ex = dict(ex)
        diff = ex.pop(TWIN_DIFF_INPUT)
        return grad_fn(weights, diff, {**shared, **ex}, loss_target)

    if N_MICROBATCH == 1:
        loss, (grad_w, grad_x) = one_microbatch(per_example, given["loss_target"])
    else:
        def body(carry, xs):
            loss_sum, grad_sum = carry
            l_k, (gw_k, gx_k) = one_microbatch(xs[0], xs[1])
            with _jax.named_scope("update"):
                return (loss_sum + l_k, _jax.tree.map(_jnp.add, grad_sum, gw_k)), gx_k

        init = (_jnp.zeros((), _jnp.float32), _jax.tree.map(_jnp.zeros_like, weights))
        (loss, grad_w), grad_x = _jax.lax.scan(body, init, (per_example, given["loss_target"]))
    with _jax.named_scope("update"):
        delta_w, new_m, new_v = {}, {}, {}
        for n in TWIN_WEIGHTS:
            delta_w[n], new_m[n], new_v[n] = _adamw(weights[n], grad_w[n], given["m_" + n], given["v_" + n])
    return (loss, grad_x, *[grad_w[n] for n in TWIN_WEIGHTS], *[delta_w[n] for n in TWIN_WEIGHTS],
            *[new_m[n] for n in TWIN_WEIGHTS], *[new_v[n] for n in TWIN_WEIGHTS])
```

```python
import functools

import jax
import jax.numpy as jnp
from jax import lax
from jax.experimental import pallas as pl
from jax.experimental.pallas import tpu as pltpu

f32 = jnp.float32
bf16 = jnp.bfloat16
HI = lax.Precision.HIGHEST

D_MODEL = 1024
DEPTH = 4
DN_HEADS, DN_DK, DN_CHUNK = 4, 128, 64
GLA_HEADS, GLA_DK, GLA_CHUNK, GLA_TAU = 4, 64, 16, 16.0
GLA_BLOCK = 128
S5_GROUPS, S5_GROUP, S5_STATE = 32, 16, 64
S5_SB = 4
S5_TILE = 256
D_FF = 2816
LN_EPS = 1e-5
ALPHA = (2.0 * DEPTH) ** 0.25
ADAM_LR, ADAM_B1, ADAM_B2, ADAM_EPS, ADAM_WD, ADAM_STEP = 0.001, 0.9, 0.999, 1e-08, 0.01, 10

VMEM_LIMIT_V7X = 56 * 1024 * 1024
HALO = 32
N_CHIPS = 4
N_DEV = 8

IN_ORIG = dict(dn_qkv=(0, 1536), dn_a=(1536, 1540), dn_b=(1540, 1544), dn_gate=(1544, 2056), cf=(2056, 3080),
               s5=(3080, 3592), gla_q=(3592, 3848), gla_k=(3848, 4104), gla_v=(4104, 4616), gla_g=(4616, 5128),
               gla_lr=(5128, 5144), gates=(5144, 9240))
IN_COLS = 9240
WM_COLS = 9728
WS_COLS = 128


def _cparams(sem):
    return pltpu.CompilerParams(dimension_semantics=sem, vmem_limit_bytes=VMEM_LIMIT_V7X)


def _pick(dim, pref):
    for t in (pref, 512, 256, 128, 64, 32, 16, 8):
        if t <= pref and dim % t == 0:
            return t
    return dim


NN = (((1,), (0,)), ((), ()))
NT = (((1,), (1,)), ((), ()))
TN = (((0,), (0,)), ((), ()))


def _dot(a, b, dims=NN):
    return lax.dot_general(a, b, dims, precision=HI, preferred_element_type=f32)


def _round(a):
    return a.astype(bf16).astype(f32)


def _rdot(a, b, dims=NN):
    return lax.dot_general(a.astype(bf16), b.astype(bf16), dims, preferred_element_type=f32)


@functools.partial(jax.custom_vjp, nondiff_argnums=(2,))
def _bdot(a, b, dims=NN):
    return _rdot(a, b, dims)


def _bdot_fwd(a, b, dims):
    return _rdot(a, b, dims), (a, b)


def _bdot_bwd(dims, res, g):
    a, b = res
    if dims == NN:
        return _rdot(g, b, NT), _rdot(a, g, TN)
    if dims == NT:
        return _rdot(g, b, NN), _rdot(g, a, TN)
    assert dims == TN
    return _rdot(b, g, NT), _rdot(a, g, NN)


_bdot.defvjp(_bdot_fwd, _bdot_bwd)


def _iota(shape, axis):
    return lax.broadcasted_iota(jnp.int32, shape, axis)


def _mm(a, b, mode, name, tm=512, tn=512):
    if mode == 'nn':
        (m, k), n = a.shape, b.shape[1]
    elif mode == 'nt':
        (m, k), n = a.shape, b.shape[0]
    else:
        (k, m), n = a.shape, b.shape[1]
    tm, tn = _pick(m, tm), _pick(n, tn)
    tk = k if k * tm * 4 <= 8 * 1024 * 1024 else 512
    assert k % tk == 0
    nk = k // tk
    if mode == 'nn':
        dims = NN
        a_spec = pl.BlockSpec((tm, tk), lambda i, j, kk: (i, kk))
        b_spec = pl.BlockSpec((tk, tn), lambda i, j, kk: (kk, j))
    elif mode == 'nt':
        dims = NT
        a_spec = pl.BlockSpec((tm, tk), lambda i, j, kk: (i, kk))
        b_spec = pl.BlockSpec((tn, tk), lambda i, j, kk: (j, kk))
    else:
        dims = TN
        a_spec = pl.BlockSpec((tk, tm), lambda i, j, kk: (kk, i))
        b_spec = pl.BlockSpec((tk, tn), lambda i, j, kk: (kk, j))

    def kern(a_ref, b_ref, o_ref):
        p = lax.dot_general(a_ref[...].astype(bf16), b_ref[...].astype(bf16), dims, preferred_element_type=f32)
        if nk == 1:
            o_ref[...] = p
        else:
            kk = pl.program_id(2)

            @pl.when(kk == 0)
            def _():
                o_ref[...] = p

            @pl.when(kk > 0)
            def _():
                o_ref[...] += p

    return pl.pallas_call(
        kern, name=name, grid=(m // tm, n // tn, nk),
        in_specs=[a_spec, b_spec], out_specs=pl.BlockSpec((tm, tn), lambda i, j, kk: (i, j)),
        out_shape=jax.ShapeDtypeStruct((m, n), f32),
        compiler_params=_cparams(("parallel", "parallel", "arbitrary")),
    )(a, b)


def _full_spec(p):
    nd = p.ndim
    return pl.BlockSpec(p.shape, lambda *_, nd=nd: (0,) * nd)


def _rowwise(name, f, ins, params, out_widths, tile=256):
    rows = ins[0][0].shape[0]
    tile = _pick(rows, tile)
    n_x = len(ins) + len(params)

    def kern(*refs):
        for o_ref, r in zip(refs[n_x:], f(*[r[...] for r in refs[:n_x]])):
            o_ref[...] = r

    in_specs = [pl.BlockSpec((tile, w), lambda i, c=c: (i, c)) for (_, w, c) in ins] + [_full_spec(p) for p in params]
    return pl.pallas_call(
        kern, name=name, grid=(rows // tile,), in_specs=in_specs,
        out_specs=[pl.BlockSpec((tile, w), lambda i: (i, 0)) for w in out_widths],
        out_shape=[jax.ShapeDtypeStruct((rows, w), f32) for w in out_widths],
        compiler_params=_cparams(("parallel",)),
    )(*[a for (a, _, _) in ins], *params)


def _rowwise_bwd(name, f, ins, params, douts, want, tile=256):
    rows = ins[0][0].shape[0]
    tile = _pick(rows, tile)
    n_in, n_p = len(ins), len(params)
    parts = [p for d in douts for p in d]
    n_x, n_d = n_in + n_p, len(parts)

    def kern(*refs):
        xs = [r[...] for r in refs[:n_x]]
        d_refs, o_refs = refs[n_x:n_x + n_d], refs[n_x + n_d:]
        cts, pos = [], 0
        for d in douts:
            acc = d_refs[pos][...]
            for r in d_refs[pos + 1:pos + len(d)]:
                acc = acc + r[...]
            pos += len(d)
            cts.append(acc)
        grads = jax.vjp(f, *xs)[1](tuple(cts))
        k = 0
        for j in range(n_in):
            if want[j]:
                o_refs[k][...] = grads[j]
                k += 1
        first = pl.program_id(0) == 0
        for j in range(n_p):
            g, o_ref = grads[n_in + j], o_refs[k + j]

            @pl.when(first)
            def _(o_ref=o_ref, g=g):
                o_ref[...] = g

            @pl.when(jnp.logical_not(first))
            def _(o_ref=o_ref, g=g):
                o_ref[...] += g

    in_specs = ([pl.BlockSpec((tile, w), lambda i, c=c: (i, c)) for (_, w, c) in ins] + [_full_spec(p) for p in params]
                + [pl.BlockSpec((tile, w), lambda i, c=c: (i, c)) for (_, w, c) in parts])
    out_specs, out_shape = [], []
    for j in range(n_in):
        if want[j]:
            out_specs.append(pl.BlockSpec((tile, ins[j][1]), lambda i: (i, 0)))
            out_shape.append(jax.ShapeDtypeStruct((rows, ins[j][1]), f32))
    n_g = len(out_specs)
    for p in params:
        out_specs.append(_full_spec(p))
        out_shape.append(jax.ShapeDtypeStruct(p.shape, f32))
    res = pl.pallas_call(
        kern, name=name, grid=(rows // tile,), in_specs=in_specs, out_specs=out_specs, out_shape=out_shape,
        compiler_params=_cparams(("arbitrary",)),
    )(*[a for (a, _, _) in ins], *params, *[a for (a, _, _) in parts])
    return list(res[:n_g]), list(res[n_g:])


_sigmoid = jax.nn.sigmoid
_silu = jax.nn.silu
_softplus = jax.nn.softplus
_log_sigmoid = jax.nn.log_sigmoid


def _f_ln(x, r, g, b):
    t = ALPHA * x + r
    mu = jnp.mean(t, -1, keepdims=True)
    var = jnp.mean(jnp.square(t - mu), -1, keepdims=True)
    return ((t - mu) * lax.rsqrt(var + LN_EPS) * g + b,)


def _f_pre(c1, hs, a_log, dt_bias, w_alpha, b_alpha):
    s = _silu(c1)
    outs = []
    for j in range(3 * DN_HEADS):
        t = s[:, j * DN_DK:(j + 1) * DN_DK]
        if j < 2 * DN_HEADS:
            t = t * lax.rsqrt(jnp.sum(t * t, -1, keepdims=True) + 1e-6)
        if j < DN_HEADS:
            t = t * (DN_DK ** -0.5)
        outs.append(t)
    qkvn = jnp.concatenate(outs, axis=1)
    lane = _iota(hs.shape, 1)
    g = -jnp.exp(a_log) * _softplus(hs + dt_bias)
    beta = _sigmoid(hs)
    gb = jnp.where(lane < DN_HEADS, g, jnp.where(lane < 2 * DN_HEADS, beta, 0.0))
    la = _log_sigmoid(_bdot(hs, w_alpha) + b_alpha) * (1.0 / GLA_TAU)
    lane5 = _iota(la.shape, 1)
    la = jnp.where((lane5 % 128) < GLA_DK, la, 0.0)
    return qkvn, gb, la


def _f_post(o, gate, w):
    outs = []
    for j in range(4):
        t = o[:, j * 128:(j + 1) * 128]
        outs.append(t * lax.rsqrt(jnp.mean(t * t, -1, keepdims=True) + LN_EPS) * w)
    return (jnp.concatenate(outs, axis=1) * _silu(gate),)


def _f_glu(a, g):
    return (a * _sigmoid(g),)


def _f_cfpost(c, bias, g, b):
    t = c + bias
    mu = jnp.mean(t, -1, keepdims=True)
    var = jnp.mean(jnp.square(t - mu), -1, keepdims=True)
    return (_silu((t - mu) * lax.rsqrt(var + LN_EPS) * g + b),)


def _f_gelu(y):
    return (jax.nn.gelu(y),)


def _f_merge(ga, gb_, gc, gd, ya, yb, zv, zg, yd):
    return (_sigmoid(ga) * ya + _sigmoid(gb_) * yb + _sigmoid(gc) * (zv * _sigmoid(zg)) + _sigmoid(gd) * yd,)


def _f_act(a, b):
    return (_silu(a) * b,)


def _conv_tiles(rows, ch):
    return _pick(rows, 256), _pick(ch, 512)


def _conv_fwd(name, x, width, colblk0, w, taps):
    rows = x.shape[0]
    tr, cb = _conv_tiles(rows, width)
    hb = tr // HALO

    def kern(prev_ref, x_ref, w_ref, o_ref, ext):
        i = pl.program_id(1)
        ext[pl.ds(0, HALO), :] = jnp.where(i > 0, _round(prev_ref[...]), 0.0)
        ext[pl.ds(HALO, tr), :] = _round(x_ref[...])
        wv = _round(w_ref[...])
        acc = jnp.zeros((tr, cb), f32)
        for k in range(taps):
            acc = acc + wv[k:k + 1, :] * ext[pl.ds(HALO - taps + 1 + k, tr), :]
        o_ref[...] = acc

    c0 = colblk0 * (width // cb)
    return pl.pallas_call(
        kern, name=name, grid=(width // cb, rows // tr),
        in_specs=[pl.BlockSpec((HALO, cb), lambda c, i: (jnp.maximum(i * hb - 1, 0), c0 + c)),
                  pl.BlockSpec((tr, cb), lambda c, i: (i, c0 + c)),
                  pl.BlockSpec((w.shape[0], cb), lambda c, i: (0, c))],
        out_specs=pl.BlockSpec((tr, cb), lambda c, i: (i, c)),
        out_shape=jax.ShapeDtypeStruct((rows, width), f32),
        scratch_shapes=[pltpu.VMEM((HALO + tr, cb), f32)],
        compiler_params=_cparams(("parallel", "arbitrary")),
    )(x, x, w)


def _conv_bwd(name, x, width, colblk0, w, taps, dy):
    rows = x.shape[0]
    tr, cb = _conv_tiles(rows, width)
    hb = tr // HALO
    nt = rows // tr
    wr = w.shape[0]

    def kern(prev_ref, x_ref, w_ref, dy_ref, next_ref, dx_ref, dw_ref, ext, dext):
        i = pl.program_id(1)
        ext[pl.ds(0, HALO), :] = jnp.where(i > 0, _round(prev_ref[...]), 0.0)
        ext[pl.ds(HALO, tr), :] = _round(x_ref[...])
        dyv = _round(dy_ref[...])
        dext[pl.ds(0, tr), :] = dyv
        dext[pl.ds(tr, HALO), :] = jnp.where(i < nt - 1, _round(next_ref[...]), 0.0)
        wv = _round(w_ref[...])
        acc = jnp.zeros((tr, cb), f32)
        rows_w = []
        for k in range(taps):
            acc = acc + wv[k:k + 1, :] * dext[pl.ds(taps - 1 - k, tr), :]
            rows_w.append(jnp.sum(dyv * ext[pl.ds(HALO - taps + 1 + k, tr), :], axis=0, keepdims=True))
        dx_ref[...] = acc
        if wr > taps:
            rows_w.append(jnp.zeros((wr - taps, cb), f32))
        dwv = jnp.concatenate(rows_w, axis=0)

        @pl.when(i == 0)
        def _():
            dw_ref[...] = dwv

        @pl.when(i > 0)
        def _():
            dw_ref[...] += dwv

    c0 = colblk0 * (width // cb)
    return pl.pallas_call(
        kern, name=name, grid=(width // cb, nt),
        in_specs=[pl.BlockSpec((HALO, cb), lambda c, i: (jnp.maximum(i * hb - 1, 0), c0 + c)),
                  pl.BlockSpec((tr, cb), lambda c, i: (i, c0 + c)),
                  pl.BlockSpec((wr, cb), lambda c, i: (0, c)),
                  pl.BlockSpec((tr, cb), lambda c, i: (i, c)),
                  pl.BlockSpec((HALO, cb), lambda c, i: (jnp.minimum((i + 1) * hb, nt * hb - 1), c))],
        out_specs=[pl.BlockSpec((tr, cb), lambda c, i: (i, c)), pl.BlockSpec((wr, cb), lambda c, i: (0, c))],
        out_shape=[jax.ShapeDtypeStruct((rows, width), f32), jax.ShapeDtypeStruct((wr, width), f32)],
        scratch_shapes=[pltpu.VMEM((HALO + tr, cb), f32), pltpu.VMEM((HALO + tr, cb), f32)],
        compiler_params=_cparams(("parallel", "arbitrary")),
    )(x, x, w, dy, dy)


def _dn_head(state, q, k, v, gc, gl, beta):
    c = DN_CHUNK
    ii, jj = _iota((c, c), 0), _iota((c, c), 1)
    causal, strict = ii >= jj, ii > jj
    gcb = jnp.broadcast_to(gc, (c, c))
    decay = jnp.where(causal, jnp.exp(jnp.where(causal, gcb - gcb.T, 0.0)), 0.0)
    kb = k * beta
    neg = -jnp.where(strict, _bdot(kb, k, NT) * decay, 0.0)
    inv = jnp.where(ii == jj, 1.0, 0.0) + neg
    p = neg
    for _ in range(5):
        p = _dot(p, p)
        inv = inv + _dot(inv, p)
    egc = jnp.exp(gc)
    u = _dot(inv, v * beta)
    w = _dot(inv, kb * egc)
    intra = _bdot(q, k, NT) * decay
    v_new = u - _bdot(w, state)
    o = _bdot(q * egc, state) + _bdot(intra, v_new)
    new_state = state * jnp.exp(gl) + _bdot(k * jnp.exp(gl - gc), v_new, TN)
    return o, new_state


def _dn_cum(gb):
    c = DN_CHUNK
    tril = jnp.where(_iota((c, c), 0) >= _iota((c, c), 1), 1.0, 0.0)
    return _dot(tril, gb), jnp.sum(gb, axis=0, keepdims=True)


def _dn_fwd(qkvn, gb):
    rows = qkvn.shape[0]
    c, h, d = DN_CHUNK, DN_HEADS, DN_DK
    nc = rows // c

    def kern(qkv_ref, gb_ref, o_ref, st_ref, state):
        @pl.when(pl.program_id(0) == 0)
        def _():
            state[...] = jnp.zeros_like(state)

        gbv = gb_ref[...]
        cum, tot = _dn_cum(gbv)
        for j in range(h):
            st = state[j]
            st_ref[0, j] = st
            o, new = _dn_head(st, qkv_ref[:, j * d:(j + 1) * d], qkv_ref[:, (h + j) * d:(h + j + 1) * d],
                              qkv_ref[:, (2 * h + j) * d:(2 * h + j + 1) * d],
                              cum[:, j:j + 1], tot[:, j:j + 1], gbv[:, h + j:h + j + 1])
            o_ref[:, j * d:(j + 1) * d] = o
            state[j] = new

    return pl.pallas_call(
        kern, name="dn_fwd", grid=(nc,),
        in_specs=[pl.BlockSpec((c, 3 * h * d), lambda i: (i, 0)), pl.BlockSpec((c, 128), lambda i: (i, 0))],
        out_specs=[pl.BlockSpec((c, h * d), lambda i: (i, 0)), pl.BlockSpec((1, h, d, d), lambda i: (i, 0, 0, 0))],
        out_shape=[jax.ShapeDtypeStruct((rows, h * d), f32), jax.ShapeDtypeStruct((nc, h, d, d), f32)],
        scratch_shapes=[pltpu.VMEM((h, d, d), f32)],
        compiler_params=_cparams(("arbitrary",)),
    )(qkvn, gb)


def _dn_bwd(qkvn, gb, states, do):
    rows = qkvn.shape[0]
    c, h, d = DN_CHUNK, DN_HEADS, DN_DK
    nc = rows // c

    def kern(qkv_ref, gb_ref, st_ref, do_ref, dqkv_ref, dgb_ref, dstate):
        @pl.when(pl.program_id(0) == 0)
        def _():
            dstate[...] = jnp.zeros_like(dstate)

        gbv = gb_ref[...]
        cum, tot = _dn_cum(gbv)
        lane = _iota((c, 128), 1)
        dcum = jnp.zeros((c, 128), f32)
        dgb = jnp.zeros((c, 128), f32)
        for j in range(h):
            args = (st_ref[0, j], qkv_ref[:, j * d:(j + 1) * d], qkv_ref[:, (h + j) * d:(h + j + 1) * d],
                    qkv_ref[:, (2 * h + j) * d:(2 * h + j + 1) * d],
                    cum[:, j:j + 1], tot[:, j:j + 1], gbv[:, h + j:h + j + 1])
            ds, dq, dk, dv, dgc, dgl, dbeta = jax.vjp(_dn_head, *args)[1]((do_ref[:, j * d:(j + 1) * d], dstate[j]))
            dstate[j] = ds
            dqkv_ref[:, j * d:(j + 1) * d] = dq
            dqkv_ref[:, (h + j) * d:(h + j + 1) * d] = dk
            dqkv_ref[:, (2 * h + j) * d:(2 * h + j + 1) * d] = dv
            dcum = dcum + jnp.where(lane == j, dgc, 0.0)
            dgb = dgb + jnp.where(lane == j, dgl, 0.0) + jnp.where(lane == h + j, dbeta, 0.0)
        triu = jnp.where(_iota((c, c), 0) <= _iota((c, c), 1), 1.0, 0.0)
        dgb_ref[...] = dgb + _dot(triu, dcum)

    rev = lambda i: (nc - 1 - i, 0)
    return pl.pallas_call(
        kern, name="dn_bwd", grid=(nc,),
        in_specs=[pl.BlockSpec((c, 3 * h * d), rev), pl.BlockSpec((c, 128), rev),
                  pl.BlockSpec((1, h, d, d), lambda i: (nc - 1 - i, 0, 0, 0)), pl.BlockSpec((c, h * d), rev)],
        out_specs=[pl.BlockSpec((c, 3 * h * d), rev), pl.BlockSpec((c, 128), rev)],
        out_shape=[jax.ShapeDtypeStruct((rows, 3 * h * d), f32), jax.ShapeDtypeStruct((rows, 128), f32)],
        scratch_shapes=[pltpu.VMEM((h, d, d), f32)],
        compiler_params=_cparams(("arbitrary",)),
    )(qkvn, gb, states, do)


def _gla_block(state_t, q, k, v, la):
    c = GLA_CHUNK
    ii, jj = _iota((c, c), 0), _iota((c, c), 1)
    causal = ii >= jj
    tril = jnp.where(causal, 1.0, 0.0)
    outs = []
    for n in range(q.shape[0] // c):
        sl = slice(n * c, (n + 1) * c)
        qn, kn, vn, ln = q[sl] * (GLA_DK ** -0.5), k[sl], v[sl], la[sl]
        gc = _dot(tril, ln)
        gl = jnp.sum(ln, axis=0, keepdims=True)
        q_dec = qn * jnp.exp(gc)
        scores = jnp.where(causal, _bdot(q_dec, kn * jnp.exp(-gc), NT), 0.0)
        outs.append(_bdot(q_dec, state_t, NT) + _bdot(scores, vn))
        state_t = state_t * jnp.exp(gl) + _bdot(vn, kn * jnp.exp(gl - gc), TN)
    return jnp.concatenate(outs, axis=0), state_t


GLA_Q0, GLA_K0, GLA_V0 = 7680 // 128, 8192 // 128, 8704 // 128


def _gla_fwd(hmain, la):
    rows = hmain.shape[0]
    b, h = _pick(rows, GLA_BLOCK), GLA_HEADS
    nb = rows // b

    def kern(q_ref, k_ref, v_ref, la_ref, o_ref, st_ref, state):
        @pl.when(pl.program_id(1) == 0)
        def _():
            state[...] = jnp.zeros_like(state)

        st = state[...]
        st_ref[0, 0] = st
        o, new = _gla_block(st, q_ref[...], k_ref[...], v_ref[...], la_ref[...])
        o_ref[...] = o
        state[...] = new

    return pl.pallas_call(
        kern, name="gla_fwd", grid=(h, nb),
        in_specs=[pl.BlockSpec((b, 128), lambda j, i: (i, GLA_Q0 + j)), pl.BlockSpec((b, 128), lambda j, i: (i, GLA_K0 + j)),
                  pl.BlockSpec((b, 128), lambda j, i: (i, GLA_V0 + j)), pl.BlockSpec((b, 128), lambda j, i: (i, j))],
        out_specs=[pl.BlockSpec((b, 128), lambda j, i: (i, j)), pl.BlockSpec((1, 1, 128, 128), lambda j, i: (j, i, 0, 0))],
        out_shape=[jax.ShapeDtypeStruct((rows, h * 128), f32), jax.ShapeDtypeStruct((h, nb, 128, 128), f32)],
        scratch_shapes=[pltpu.VMEM((128, 128), f32)],
        compiler_params=_cparams(("parallel", "arbitrary")),
    )(hmain, hmain, hmain, la)


def _gla_bwd(hmain, la, states, do):
    rows = hmain.shape[0]
    b, h = _pick(rows, GLA_BLOCK), GLA_HEADS
    nb = rows // b

    def kern(q_ref, k_ref, v_ref, la_ref, st_ref, do_ref, dq_ref, dk_ref, dv_ref, dla_ref, dstate):
        @pl.when(pl.program_id(1) == 0)
        def _():
            dstate[...] = jnp.zeros_like(dstate)

        args = (st_ref[0, 0], q_ref[...], k_ref[...], v_ref[...], la_ref[...])
        ds, dq, dk, dv, dla = jax.vjp(_gla_block, *args)[1]((do_ref[...], dstate[...]))
        dstate[...] = ds
        dq_ref[...] = dq
        dk_ref[...] = dk
        dv_ref[...] = dv
        dla_ref[...] = dla

    return pl.pallas_call(
        kern, name="gla_bwd", grid=(h, nb),
        in_specs=[pl.BlockSpec((b, 128), lambda j, i: (nb - 1 - i, GLA_Q0 + j)),
                  pl.BlockSpec((b, 128), lambda j, i: (nb - 1 - i, GLA_K0 + j)),
                  pl.BlockSpec((b, 128), lambda j, i: (nb - 1 - i, GLA_V0 + j)),
                  pl.BlockSpec((b, 128), lambda j, i: (nb - 1 - i, j)),
                  pl.BlockSpec((1, 1, 128, 128), lambda j, i: (j, nb - 1 - i, 0, 0)),
                  pl.BlockSpec((b, 128), lambda j, i: (nb - 1 - i, j))],
        out_specs=[pl.BlockSpec((b, 128), lambda j, i: (nb - 1 - i, j))] * 4,
        out_shape=[jax.ShapeDtypeStruct((rows, h * 128), f32)] * 4,
        scratch_shapes=[pltpu.VMEM((128, 128), f32)],
        compiler_params=_cparams(("parallel", "arbitrary")),
    )(hmain, hmain, hmain, la, states, do)


def _f_s5_params(a_re, a_im, log_dt, b_re, b_im):
    dt = jnp.exp(log_dt)
    mag = jnp.exp(dt * a_re)
    abar_re, abar_im = mag * jnp.cos(dt * a_im), mag * jnp.sin(dt * a_im)
    den = a_re * a_re + a_im * a_im
    nr, ni = abar_re - 1.0, abar_im
    fr, fi = (nr * a_re + ni * a_im) / den, (ni * a_re - nr * a_im) / den
    return abar_re, abar_im, fr[None] * b_re - fi[None] * b_im, fr[None] * b_im + fi[None] * b_re


def _s5_params(a_re, a_im, log_dt, b_re, b_im):
    def kern(*refs):
        for o_ref, r in zip(refs[5:], _f_s5_params(*[r[...] for r in refs[:5]])):
            o_ref[...] = r

    ins = (a_re, a_im, log_dt, b_re, b_im)
    return pl.pallas_call(
        kern, name="s5_params", out_shape=[jax.ShapeDtypeStruct(a_re.shape, f32)] * 2 + [jax.ShapeDtypeStruct(b_re.shape, f32)] * 2,
    )(*ins)


def _s5_params_bwd(a_re, a_im, log_dt, b_re, b_im, d_ar, d_ai, d_br, d_bi):
    def kern(*refs):
        grads = jax.vjp(_f_s5_params, *[r[...] for r in refs[:5]])[1](tuple(r[...] for r in refs[5:9]))
        for o_ref, g in zip(refs[9:], grads):
            o_ref[...] = g

    ins = (a_re, a_im, log_dt, b_re, b_im)
    return pl.pallas_call(
        kern, name="s5_params_bwd", out_shape=[jax.ShapeDtypeStruct(t.shape, f32) for t in ins],
    )(*ins, d_ar, d_ai, d_br, d_bi)


def _cmul(ar, ai, br, bi):
    return ar * br - ai * bi, ar * bi + ai * br


def _s5_scan(xr, xi, ar, ai, reverse):
    t = xr.shape[0]
    row = _iota(xr.shape, 0)
    s = 1
    while s < t:
        if reverse:
            keep = row < t - s
            sr, si = pltpu.roll(xr, t - s, 0), pltpu.roll(xi, t - s, 0)
        else:
            keep = row >= s
            sr, si = pltpu.roll(xr, s, 0), pltpu.roll(xi, s, 0)
        sr, si = jnp.where(keep, sr, 0.0), jnp.where(keep, si, 0.0)
        pr, pi = _cmul(ar, ai, sr, si)
        xr, xi = xr + pr, xi + pi
        ar, ai = _cmul(ar, ai, ar, ai)
        s *= 2
    return xr, xi


def _s5_powers(ar, ai, t, reverse):
    row = _iota((t, ar.shape[1]), 0)
    at = (row == (t - 1 if reverse else 0))
    return _s5_scan(jnp.where(at, ar, 0.0), jnp.where(at, ai, 0.0), ar, ai, reverse)


S5_U0 = 7168 // 128


def _s5_fwd(hmain, abar, bmat_re, bmat_im, cmat_re, cmat_im, dvec):
    rows = hmain.shape[0]
    t = _pick(rows, S5_TILE)
    nt, ns = rows // t, 512

    def kern(u_ref, a_ref, br_ref, bi_ref, cr_ref, ci_ref, d_ref, y_ref, xr_ref, xi_ref, pw, carry):
        ar, ai = a_ref[0, 0], a_ref[1, 0]

        @pl.when(pl.program_id(1) == 0)
        def _():
            pr, pi = _s5_powers(ar, ai, t, False)
            pw[0], pw[1] = pr, pi
            carry[...] = jnp.zeros_like(carry)

        u = u_ref[...]
        xr, xi = _s5_scan(_rdot(u, br_ref[0]), _rdot(u, bi_ref[0]), ar, ai, False)
        cr, ci = carry[0:1, :], carry[1:2, :]
        qr, qi = _cmul(pw[0], pw[1], cr, ci)
        xr, xi = xr + qr, xi + qi
        xr_ref[...] = xr
        xi_ref[...] = xi
        carry[0:1, :] = xr[t - 1:t, :]
        carry[1:2, :] = xi[t - 1:t, :]
        y_ref[...] = _rdot(xr, cr_ref[0]) - _rdot(xi, ci_ref[0]) + d_ref[...] * u

    sb3 = lambda b, i: (b, 0, 0)
    return pl.pallas_call(
        kern, name="s5_fwd", grid=(S5_SB, nt),
        in_specs=[pl.BlockSpec((t, 128), lambda b, i: (i, S5_U0 + b)), pl.BlockSpec((2, 1, 1, ns), lambda b, i: (0, b, 0, 0)),
                  pl.BlockSpec((1, 128, ns), sb3), pl.BlockSpec((1, 128, ns), sb3),
                  pl.BlockSpec((1, ns, 128), sb3), pl.BlockSpec((1, ns, 128), sb3), pl.BlockSpec((1, 128), lambda b, i: (0, b))],
        out_specs=[pl.BlockSpec((t, 128), lambda b, i: (i, b)), pl.BlockSpec((t, ns), lambda b, i: (i, b)),
                   pl.BlockSpec((t, ns), lambda b, i: (i, b))],
        out_shape=[jax.ShapeDtypeStruct((rows, 512), f32), jax.ShapeDtypeStruct((rows, S5_SB * ns), f32),
                   jax.ShapeDtypeStruct((rows, S5_SB * ns), f32)],
        scratch_shapes=[pltpu.VMEM((2, t, ns), f32), pltpu.VMEM((8, ns), f32)],
        compiler_params=_cparams(("parallel", "arbitrary")),
    )(hmain, abar, bmat_re, bmat_im, cmat_re, cmat_im, dvec)


def _s5_bwd(hmain, abar, bmat_re, bmat_im, cmat_re, cmat_im, dvec, x_re, x_im, dy):
    rows = hmain.shape[0]
    t = _pick(rows, S5_TILE)
    nt, ns = rows // t, 512
    t8 = t // 8

    def kern(u_ref, a_ref, br_ref, bi_ref, cr_ref, ci_ref, d_ref, xr_ref, xi_ref, xpr_ref, xpi_ref, dy_ref,
             du_ref, da_ref, dbr_ref, dbi_ref, dcr_ref, dci_ref, dd_ref, pw, carry):
        i = pl.program_id(1)
        ar, ai = a_ref[0, 0], -a_ref[1, 0]

        @pl.when(i == 0)
        def _():
            pr, pi = _s5_powers(ar, ai, t, True)
            pw[0], pw[1] = pr, pi
            carry[...] = jnp.zeros_like(carry)

        u, gy = u_ref[...], dy_ref[...]
        lr, li = _s5_scan(_rdot(gy, cr_ref[0], NT), -_rdot(gy, ci_ref[0], NT), ar, ai, True)
        qr, qi = _cmul(pw[0], pw[1], carry[0:1, :], carry[1:2, :])
        lr, li = lr + qr, li + qi
        carry[0:1, :] = lr[0:1, :]
        carry[1:2, :] = li[0:1, :]
        du_ref[...] = _rdot(lr, br_ref[0], NT) + _rdot(li, bi_ref[0], NT) + d_ref[...] * gy
        xr, xi = xr_ref[...], xi_ref[...]
        row = _iota(xr.shape, 0)
        first_r = jnp.where(i < nt - 1, xpr_ref[7:8, :], 0.0)
        first_i = jnp.where(i < nt - 1, xpi_ref[7:8, :], 0.0)
        xpr = jnp.where(row == 0, first_r, pltpu.roll(xr, 1, 0))
        xpi = jnp.where(row == 0, first_i, pltpu.roll(xi, 1, 0))
        da_r = jnp.sum(lr * xpr + li * xpi, axis=0, keepdims=True)
        da_i = jnp.sum(li * xpr - lr * xpi, axis=0, keepdims=True)
        upd = [(da_ref.at[0, 0], da_r), (da_ref.at[1, 0], da_i),
               (dbr_ref.at[0], _rdot(u, lr, TN)), (dbi_ref.at[0], _rdot(u, li, TN)),
               (dcr_ref.at[0], _rdot(xr, gy, TN)), (dci_ref.at[0], -_rdot(xi, gy, TN)),
               (dd_ref, jnp.sum(gy * u, axis=0, keepdims=True))]

        @pl.when(i == 0)
        def _():
            for ref, val in upd:
                ref[...] = val

        @pl.when(i > 0)
        def _():
            for ref, val in upd:
                ref[...] += val

    sb3 = lambda b, i: (b, 0, 0)
    rev = lambda b, i: (nt - 1 - i, b)
    prev8 = lambda b, i: (jnp.maximum((nt - 1 - i) * t8 - 1, 0), b)
    return pl.pallas_call(
        kern, name="s5_bwd", grid=(S5_SB, nt),
        in_specs=[pl.BlockSpec((t, 128), lambda b, i: (nt - 1 - i, S5_U0 + b)), pl.BlockSpec((2, 1, 1, ns), lambda b, i: (0, b, 0, 0)),
                  pl.BlockSpec((1, 128, ns), sb3), pl.BlockSpec((1, 128, ns), sb3),
                  pl.BlockSpec((1, ns, 128), sb3), pl.BlockSpec((1, ns, 128), sb3), pl.BlockSpec((1, 128), lambda b, i: (0, b)),
                  pl.BlockSpec((t, ns), rev), pl.BlockSpec((t, ns), rev), pl.BlockSpec((8, ns), prev8), pl.BlockSpec((8, ns), prev8),
                  pl.BlockSpec((t, 128), rev)],
        out_specs=[pl.BlockSpec((t, 128), rev), pl.BlockSpec((2, 1, 1, ns), lambda b, i: (0, b, 0, 0)),
                   pl.BlockSpec((1, 128, ns), sb3), pl.BlockSpec((1, 128, ns), sb3),
                   pl.BlockSpec((1, ns, 128), sb3), pl.BlockSpec((1, ns, 128), sb3), pl.BlockSpec((1, 128), lambda b, i: (0, b))],
        out_shape=[jax.ShapeDtypeStruct((rows, 512), f32), jax.ShapeDtypeStruct((2, S5_SB, 1, ns), f32),
                   jax.ShapeDtypeStruct((S5_SB, 128, ns), f32), jax.ShapeDtypeStruct((S5_SB, 128, ns), f32),
                   jax.ShapeDtypeStruct((S5_SB, ns, 128), f32), jax.ShapeDtypeStruct((S5_SB, ns, 128), f32),
                   jax.ShapeDtypeStruct((1, 512), f32)],
        scratch_shapes=[pltpu.VMEM((2, t, ns), f32), pltpu.VMEM((8, ns), f32)],
        compiler_params=_cparams(("parallel", "arbitrary")),
    )(hmain, abar, bmat_re, bmat_im, cmat_re, cmat_im, dvec, x_re, x_im, x_re, x_im, dy)


def _loss_head(y, target):
    rows, feat = y.shape
    tile = _pick(rows, 256)

    def kern(y_ref, t_ref, l_ref, dy_ref):
        e = y_ref[...] - t_ref[...]
        dy_ref[...] = e * (1.0 / feat)
        part = jnp.broadcast_to(0.5 * jnp.sum(e * e) * (1.0 / feat), l_ref.shape)

        @pl.when(pl.program_id(0) == 0)
        def _():
            l_ref[...] = part

        @pl.when(pl.program_id(0) > 0)
        def _():
            l_ref[...] += part

    return pl.pallas_call(
        kern, name="loss_head", grid=(rows // tile,),
        in_specs=[pl.BlockSpec((tile, feat), lambda i: (i, 0))] * 2,
        out_specs=[pl.BlockSpec((8, 128), lambda i: (0, 0)), pl.BlockSpec((tile, feat), lambda i: (i, 0))],
        out_shape=[jax.ShapeDtypeStruct((8, 128), f32), jax.ShapeDtypeStruct((rows, feat), f32)],
        compiler_params=_cparams(("arbitrary",)),
    )(y, target)


def _sum_parts(name, parts, scale_dtype=f32):
    rows, cols = parts[0].shape
    tile = _pick(rows, 512)

    def kern(*refs):
        acc = refs[0][...].astype(f32)
        for r in refs[1:-1]:
            acc = acc + r[...].astype(f32)
        refs[-1][...] = acc

    return pl.pallas_call(
        kern, name=name, grid=(rows // tile,),
        in_specs=[pl.BlockSpec((tile, cols), lambda i: (i, 0))] * len(parts),
        out_specs=pl.BlockSpec((tile, cols), lambda i: (i, 0)),
        out_shape=jax.ShapeDtypeStruct((rows, cols), f32),
        compiler_params=_cparams(("parallel",)),
    )(*parts)


def _adamw(name, w, m, v, g_parts):
    rows, cols = w.shape
    tile = _pick(rows, 512)
    n_g = len(g_parts)
    c1, c2 = 1.0 / (1.0 - ADAM_B1 ** ADAM_STEP), 1.0 / (1.0 - ADAM_B2 ** ADAM_STEP)

    def kern(*refs):
        w_ref, m_ref, v_ref = refs[:3]
        g = refs[3][...]
        for r in refs[4:3 + n_g]:
            g = g + r[...]
        g_ref, d_ref, nm_ref, nv_ref = refs[3 + n_g:]
        nm = ADAM_B1 * m_ref[...] + (1.0 - ADAM_B1) * g
        nv = ADAM_B2 * v_ref[...] + (1.0 - ADAM_B2) * (g * g)
        g_ref[...] = g
        nm_ref[...] = nm
        nv_ref[...] = nv
        d_ref[...] = -ADAM_LR * ((nm * c1) / (jnp.sqrt(nv * c2) + ADAM_EPS) + ADAM_WD * w_ref[...])

    spec = pl.BlockSpec((tile, cols), lambda i: (i, 0))
    return pl.pallas_call(
        kern, name=name, grid=(rows // tile,), in_specs=[spec] * (3 + n_g), out_specs=[spec] * 4,
        out_shape=[jax.ShapeDtypeStruct((rows, cols), f32)] * 4,
        compiler_params=_cparams(("parallel",)),
    )(w, m, v, *g_parts)


MESH = pl.DeviceIdType.MESH
HBM_SPEC = pl.BlockSpec(memory_space=pltpu.HBM)


def _other_chips(x, y):
    return [(1 - x, y), (x, 1 - y), (1 - x, 1 - y)]


def _gather_chips(bufs):
    n = len(bufs)

    def kern(*refs):
        in_refs, out_refs = refs[:n], refs[n:2 * n]
        send_sems, recv_sems, local_sems = refs[2 * n:]
        x, y, c = lax.axis_index("x"), lax.axis_index("y"), lax.axis_index("c")
        me = 2 * x + y
        chips = _other_chips(x, y)
        local = [pltpu.make_async_copy(in_refs[a], out_refs[a].at[me], local_sems.at[a]) for a in range(n)]
        for cp in local:
            cp.start()
        sends = []
        for a in range(n):
            for k, chip in enumerate(chips):
                sends.append(pltpu.make_async_remote_copy(
                    src_ref=in_refs[a], dst_ref=out_refs[a].at[me], send_sem=send_sems.at[a * 3 + k],
                    recv_sem=recv_sems.at[a * 3 + k], device_id=(*chip, c), device_id_type=MESH))
        for cp in sends:
            cp.start()
        for a in range(n):
            for k, chip in enumerate(chips):
                pltpu.make_async_remote_copy(
                    src_ref=in_refs[a], dst_ref=out_refs[a].at[2 * chip[0] + chip[1]], send_sem=send_sems.at[a * 3 + k],
                    recv_sem=recv_sems.at[a * 3 + k], device_id=(*chip, c), device_id_type=MESH).wait_recv()
        for cp in sends:
            cp.wait_send()
        for cp in local:
            cp.wait()

    return pl.pallas_call(
        kern, name="gather_chips", in_specs=[HBM_SPEC] * n, out_specs=[HBM_SPEC] * n,
        out_shape=[jax.ShapeDtypeStruct((N_CHIPS,) + b.shape, b.dtype) for b in bufs],
        scratch_shapes=[pltpu.SemaphoreType.DMA((3 * n,)), pltpu.SemaphoreType.DMA((3 * n,)), pltpu.SemaphoreType.DMA((n,))],
    )(*bufs)


def _scatter_chips(pieces):
    _, rows, cols = pieces.shape

    def kern(p_ref, out_ref, send_sems, recv_sems):
        x, y, c = lax.axis_index("x"), lax.axis_index("y"), lax.axis_index("c")
        chips = _other_chips(x, y)
        sends = [pltpu.make_async_remote_copy(
            src_ref=p_ref.at[2 * chip[0] + chip[1]], dst_ref=out_ref.at[k], send_sem=send_sems.at[k],
            recv_sem=recv_sems.at[k], device_id=(*chip, c), device_id_type=MESH) for k, chip in enumerate(chips)]
        for cp in sends:
            cp.start()
        for cp in sends:
            cp.wait()

    return pl.pallas_call(
        kern, name="scatter_chips", in_specs=[HBM_SPEC], out_specs=HBM_SPEC,
        out_shape=jax.ShapeDtypeStruct((3, rows, cols), pieces.dtype),
        scratch_shapes=[pltpu.SemaphoreType.DMA((3,)), pltpu.SemaphoreType.DMA((3,))],
    )(pieces)


def _swap_sibling(buf):
    def kern(b_ref, out_ref, send_sem, recv_sem):
        x, y, c = lax.axis_index("x"), lax.axis_index("y"), lax.axis_index("c")
        cp = pltpu.make_async_remote_copy(src_ref=b_ref, dst_ref=out_ref, send_sem=send_sem, recv_sem=recv_sem,
                                          device_id=(x, y, 1 - c), device_id_type=MESH)
        cp.start()
        cp.wait()

    return pl.pallas_call(
        kern, name="swap_sibling", in_specs=[HBM_SPEC], out_specs=HBM_SPEC,
        out_shape=jax.ShapeDtypeStruct(buf.shape, buf.dtype),
        scratch_shapes=[pltpu.SemaphoreType.DMA, pltpu.SemaphoreType.DMA],
    )(buf)


def _gather_all(buf):
    rows, cols = buf.shape

    def kern(b_ref, out_ref, send_sems, recv_sems, local_sem):
        x, y, c = lax.axis_index("x"), lax.axis_index("y"), lax.axis_index("c")
        me = 4 * x + 2 * y + c
        local = pltpu.make_async_copy(b_ref, out_ref.at[me], local_sem)
        local.start()
        peers = []
        for k in range(1, N_DEV):
            fx, fy, fc = (k >> 2) & 1, (k >> 1) & 1, k & 1
            peers.append((x ^ fx, y ^ fy, c ^ fc))
        sends = [pltpu.make_async_remote_copy(
            src_ref=b_ref, dst_ref=out_ref.at[me], send_sem=send_sems.at[k], recv_sem=recv_sems.at[k],
            device_id=peer, device_id_type=MESH) for k, peer in enumerate(peers)]
        for cp in sends:
            cp.start()
        for k, peer in enumerate(peers):
            pltpu.make_async_remote_copy(
                src_ref=b_ref, dst_ref=out_ref.at[4 * peer[0] + 2 * peer[1] + peer[2]], send_sem=send_sems.at[k],
                recv_sem=recv_sems.at[k], device_id=peer, device_id_type=MESH).wait_recv()
        for cp in sends:
            cp.wait_send()
        local.wait()

    return pl.pallas_call(
        kern, name="gather_all", in_specs=[HBM_SPEC], out_specs=HBM_SPEC,
        out_shape=jax.ShapeDtypeStruct((N_DEV, rows, cols), buf.dtype),
        scratch_shapes=[pltpu.SemaphoreType.DMA((N_DEV - 1,)), pltpu.SemaphoreType.DMA((N_DEV - 1,)), pltpu.SemaphoreType.DMA],
    )(buf)


def _sum_slots(name, buf):
    n, rows, cols = buf.shape
    tile = _pick(rows, 256)

    def kern(b_ref, o_ref):
        acc = b_ref[0]
        for k in range(1, n):
            acc = acc + b_ref[k]
        o_ref[...] = acc

    return pl.pallas_call(
        kern, name=name, grid=(rows // tile,),
        in_specs=[pl.BlockSpec((n, tile, cols), lambda i: (0, i, 0))], out_specs=pl.BlockSpec((tile, cols), lambda i: (i, 0)),
        out_shape=jax.ShapeDtypeStruct((rows, cols), f32), compiler_params=_cparams(("parallel",)),
    )(buf)


def _pad_heads(t):
    r = t.shape[0]
    return jnp.pad(t.reshape(r, GLA_HEADS, GLA_DK), ((0, 0), (0, 0), (0, 128 - GLA_DK))).reshape(r, GLA_HEADS * 128)


def _unpad_heads(t):
    r = t.shape[0]
    return t.reshape(r, GLA_HEADS, 128)[:, :, :GLA_DK].reshape(r, GLA_HEADS * GLA_DK)


def _blockdiag(t):
    _, r, c = t.shape
    eye = jnp.eye(8, dtype=t.dtype).reshape(1, 8, 1, 8, 1)
    return (t.reshape(S5_SB, 8, r, 1, c) * eye).reshape(S5_SB, 8 * r, 8 * c)


def _blockdiag_extract(m, r, c):
    m5 = m.reshape(S5_SB, 8, r, 8, c)
    return jnp.stack([m5[:, g, :, g, :] for g in range(8)], axis=1).reshape(S5_GROUPS, r, c)


def _row(v, width=None):
    v = v[None]
    return v if width is None else jnp.pad(v, ((0, 0), (0, width - v.shape[1])))


def _layer_operands(p):
    w_in = p['w_in']

    def seg(n):
        return w_in[:, IN_ORIG[n][0]:IN_ORIG[n][1]]

    o = dict(p)
    o['wm'] = jnp.concatenate([seg('dn_qkv'), seg('dn_gate'), seg('cf'), seg('gates'), seg('s5'), _pad_heads(seg('gla_q')),
                               _pad_heads(seg('gla_k')), seg('gla_v'), seg('gla_g')], axis=1)
    o['ws'] = jnp.pad(jnp.concatenate([seg('dn_a'), seg('dn_b'), seg('gla_lr')], axis=1), ((0, 0), (0, WS_COLS - 24)))
    o['dn_conv8'] = jnp.pad(p['dn_conv'], ((0, 4), (0, 0)))
    o['a_log_r'] = _row(p['dn_a_log'], 128)
    o['dt_bias_r'] = _row(p['dn_dt_bias'], 128)
    o['dn_norm_r'] = _row(p['dn_norm'])
    o['cf_dw32'] = jnp.pad(p['cf_dw'], ((0, 1), (0, 0)))
    o['cf_bias_r'], o['cf_g_r'], o['cf_b_r'] = _row(p['cf_dw_bias']), _row(p['cf_ln_g']), _row(p['cf_ln_b'])
    o['w_alpha_p'] = jnp.pad(_pad_heads(p['gla_w_alpha']), ((8, 128 - 24), (0, 0)))
    o['b_alpha_r'] = _pad_heads(_row(p['gla_b_alpha']))
    o['gla_norm_r'] = _row(p['gla_norm'])
    o['ln1_g_r'], o['ln1_b_r'], o['ln2_g_r'], o['ln2_b_r'] = (_row(p[n]) for n in ('ln1_g', 'ln1_b', 'ln2_g', 'ln2_b'))
    o['ffn_conv8'] = jnp.pad(p['ffn_conv'], ((0, 5), (0, 0)))
    o['s5_in'] = (p['s5_a_re'], p['s5_a_im'], p['s5_log_dt'][:, None],
                  p['s5_b_re'].transpose(2, 0, 1), p['s5_b_im'].transpose(2, 0, 1))
    abar_re, abar_im, bbar_re, bbar_im = _s5_params(*o['s5_in'])
    o['abar'] = jnp.stack([abar_re, abar_im]).reshape(2, S5_SB, 1, 512)
    o['bmat_re'], o['bmat_im'] = _blockdiag(bbar_re.transpose(1, 0, 2)), _blockdiag(bbar_im.transpose(1, 0, 2))
    o['cmat_re'], o['cmat_im'] = _blockdiag(p['s5_c_re'].transpose(0, 2, 1)), _blockdiag(p['s5_c_im'].transpose(0, 2, 1))
    o['dvec'] = _row(p['s5_d'])
    return o


def _whole(a):
    return (a, a.shape[1], 0)


def _merge_ins(h, s):
    return [(h, 1024, 3), (h, 1024, 4), (h, 1024, 5), (h, 1024, 6), (s['y_a'], 1024, 0), (s['y_b'], 1024, 0),
            (s['zz'], 1024, 0), (s['zz'], 1024, 1), (s['y_d'], 1024, 0)]


def _layer_fwd(x, o):
    s = {}
    h = s['h'] = _mm(x, o['wm'], 'nn', 'mm_h')
    hs = s['hs'] = _mm(x, o['ws'], 'nn', 'mm_hs')
    s['c1'] = _conv_fwd('conv_dn', h, 1536, 0, o['dn_conv8'], 4)
    s['qkvn'], s['gb'], s['la'] = _rowwise('pre', _f_pre, [_whole(s['c1']), _whole(hs)],
                                           [o['a_log_r'], o['dt_bias_r'], o['w_alpha_p'], o['b_alpha_r']], [1536, 128, 512])
    s['o_dn'], s['st_dn'] = _dn_fwd(s['qkvn'], s['gb'])
    (s['on_dn'],) = _rowwise('post_dn', _f_post, [_whole(s['o_dn']), (h, 512, 3)], [o['dn_norm_r']], [512])
    s['y_a'] = _mm(s['on_dn'], o['w_br_dn'], 'nn', 'mm_br')
    (s['cfp'],) = _rowwise('glu_cf', _f_glu, [(h, 512, 4), (h, 512, 5)], [], [512])
    s['cc'] = _conv_fwd('conv_cf', s['cfp'], 512, 0, o['cf_dw32'], 31)
    (s['cfo'],) = _rowwise('post_cf', _f_cfpost, [_whole(s['cc'])], [o['cf_bias_r'], o['cf_g_r'], o['cf_b_r']], [512])
    s['y_b'] = _mm(s['cfo'], o['w_br_cf'], 'nn', 'mm_br')
    s['ys5'], s['xr'], s['xi'] = _s5_fwd(h, o['abar'], o['bmat_re'], o['bmat_im'], o['cmat_re'], o['cmat_im'], o['dvec'])
    (s['z'],) = _rowwise('gelu', _f_gelu, [_whole(s['ys5'])], [], [512])
    s['zz'] = _mm(s['z'], o['w_br_s5'], 'nn', 'mm_br_s5')
    s['o_gla'], s['st_gla'] = _gla_fwd(h, s['la'])
    (s['on_gla'],) = _rowwise('post_gla', _f_post, [_whole(s['o_gla']), (h, 512, 18)], [o['gla_norm_r']], [512])
    s['y_d'] = _mm(s['on_gla'], o['w_br_gla'], 'nn', 'mm_br')
    (s['merged'],) = _rowwise('merge', _f_merge, _merge_ins(h, s), [], [1024], tile=128)
    s['mix'] = _mm(s['merged'], o['w_o'], 'nn', 'mm_o')
    (s['x1'],) = _rowwise('ln', _f_ln, [_whole(x), _whole(s['mix'])], [o['ln1_g_r'], o['ln1_b_r']], [1024])
    s['up'] = _mm(s['x1'], o['w_up'], 'nn', 'mm_up')
    s['u'] = _conv_fwd('conv_ffn', s['up'], 2 * D_FF, 0, o['ffn_conv8'], 3)
    (s['act'],) = _rowwise('act', _f_act, [(s['u'], D_FF, 0), (s['u'], D_FF, 1)], [], [D_FF])
    s['ffn'] = _mm(s['act'], o['w_down'], 'nn', 'mm_down')
    (x2,) = _rowwise('ln', _f_ln, [_whole(s['x1']), _whole(s['ffn'])], [o['ln2_g_r'], o['ln2_b_r']], [1024])
    return x2, s


def _layer_bwd(x, o, s, dparts):
    h, g = s['h'], {}
    (dx1_a, dffn), (g['ln2_g'], g['ln2_b']) = _rowwise_bwd(
        'ln_bwd', _f_ln, [_whole(s['x1']), _whole(s['ffn'])], [o['ln2_g_r'], o['ln2_b_r']], [[_whole(d) for d in dparts]], [True, True])
    dact = _mm(dffn, o['w_down'], 'nt', 'mm_down_dx')
    g['w_down'] = _mm(s['act'], dffn, 'tn', 'mm_down_dw')
    (du_a, du_b), _ = _rowwise_bwd('act_bwd', _f_act, [(s['u'], D_FF, 0), (s['u'], D_FF, 1)], [], [[_whole(dact)]], [True, True])
    dup, dw = _conv_bwd('conv_ffn_bwd', s['up'], 2 * D_FF, 0, o['ffn_conv8'], 3, jnp.concatenate([du_a, du_b], axis=1))
    g['ffn_conv'] = dw[:3]
    dx1_b = _mm(dup, o['w_up'], 'nt', 'mm_up_dx')
    g['w_up'] = _mm(s['x1'], dup, 'tn', 'mm_up_dw')
    (dx_a, dmix), (g['ln1_g'], g['ln1_b']) = _rowwise_bwd(
        'ln_bwd', _f_ln, [_whole(x), _whole(s['mix'])], [o['ln1_g_r'], o['ln1_b_r']], [[_whole(dx1_a), _whole(dx1_b)]], [True, True])
    dmerged = _mm(dmix, o['w_o'], 'nt', 'mm_o_dx')
    g['w_o'] = _mm(s['merged'], dmix, 'tn', 'mm_o_dw')
    (dga, dgb_, dgc, dgd, dya, dyb, dzv, dzg, dyd), _ = _rowwise_bwd(
        'merge_bwd', _f_merge, _merge_ins(h, s), [], [[_whole(dmerged)]], [True] * 9, tile=128)
    dzz = jnp.concatenate([dzv, dzg], axis=1)
    don = _mm(dya, o['w_br_dn'], 'nt', 'mm_br_dx')
    g['w_br_dn'] = _mm(s['on_dn'], dya, 'tn', 'mm_br_dw')
    (do_dn, dgate_dn), (g['dn_norm'],) = _rowwise_bwd(
        'post_bwd', _f_post, [_whole(s['o_dn']), (h, 512, 3)], [o['dn_norm_r']], [[_whole(don)]], [True, True])
    dqkvn, dgb = _dn_bwd(s['qkvn'], s['gb'], s['st_dn'], do_dn)
    don = _mm(dyd, o['w_br_gla'], 'nt', 'mm_br_dx')
    g['w_br_gla'] = _mm(s['on_gla'], dyd, 'tn', 'mm_br_dw')
    (do_gla, dgate_gla), (g['gla_norm'],) = _rowwise_bwd(
        'post_bwd', _f_post, [_whole(s['o_gla']), (h, 512, 18)], [o['gla_norm_r']], [[_whole(don)]], [True, True])
    dq_gla, dk_gla, dv_gla, dla = _gla_bwd(h, s['la'], s['st_gla'], do_gla)
    (dc1, dhs), (d_alog, d_dtb, d_walpha, d_balpha) = _rowwise_bwd(
        'pre_bwd', _f_pre, [_whole(s['c1']), _whole(s['hs'])], [o['a_log_r'], o['dt_bias_r'], o['w_alpha_p'], o['b_alpha_r']],
        [[_whole(dqkvn)], [_whole(dgb)], [_whole(dla)]], [True, True])
    g['dn_a_log'], g['dn_dt_bias'] = d_alog[0, :DN_HEADS], d_dtb[0, :DN_HEADS]
    g['gla_w_alpha'], g['gla_b_alpha'] = _unpad_heads(d_walpha[8:24]), _unpad_heads(d_balpha)[0]
    d_dnqkv, dw = _conv_bwd('conv_dn_bwd', h, 1536, 0, o['dn_conv8'], 4, dc1)
    g['dn_conv'] = dw[:4]
    dcfo = _mm(dyb, o['w_br_cf'], 'nt', 'mm_br_dx')
    g['w_br_cf'] = _mm(s['cfo'], dyb, 'tn', 'mm_br_dw')
    (dcc,), (g['cf_dw_bias'], g['cf_ln_g'], g['cf_ln_b']) = _rowwise_bwd(
        'post_cf_bwd', _f_cfpost, [_whole(s['cc'])], [o['cf_bias_r'], o['cf_g_r'], o['cf_b_r']], [[_whole(dcfo)]], [True])
    dcfp, dw = _conv_bwd('conv_cf_bwd', s['cfp'], 512, 0, o['cf_dw32'], 31, dcc)
    g['cf_dw'] = dw[:31]
    (dcf_a, dcf_g), _ = _rowwise_bwd('glu_bwd', _f_glu, [(h, 512, 4), (h, 512, 5)], [], [[_whole(dcfp)]], [True, True])
    dz = _mm(dzz, o['w_br_s5'], 'nt', 'mm_br_s5_dx')
    g['w_br_s5'] = _mm(s['z'], dzz, 'tn', 'mm_br_s5_dw')
    (dys5,), _ = _rowwise_bwd('gelu_bwd', _f_gelu, [_whole(s['ys5'])], [], [[_whole(dz)]], [True])
    du_s5, d_abar, dbm_re, dbm_im, dcm_re, dcm_im, d_dvec = _s5_bwd(
        h, o['abar'], o['bmat_re'], o['bmat_im'], o['cmat_re'], o['cmat_im'], o['dvec'], s['xr'], s['xi'], dys5)
    d_bbar = [_blockdiag_extract(m, S5_GROUP, S5_STATE).transpose(1, 0, 2) for m in (dbm_re, dbm_im)]
    da_re, da_im, dlog_dt, db_re, db_im = _s5_params_bwd(
        *o['s5_in'], d_abar[0].reshape(S5_GROUPS, S5_STATE), d_abar[1].reshape(S5_GROUPS, S5_STATE), *d_bbar)
    g['s5_a_re'], g['s5_a_im'], g['s5_log_dt'] = da_re, da_im, dlog_dt[:, 0]
    g['s5_b_re'], g['s5_b_im'] = db_re.transpose(1, 2, 0), db_im.transpose(1, 2, 0)
    g['s5_c_re'], g['s5_c_im'] = (_blockdiag_extract(m, S5_STATE, S5_GROUP).transpose(0, 2, 1) for m in (dcm_re, dcm_im))
    g['s5_d'] = d_dvec[0]
    for n in ('dn_norm', 'gla_norm', 'cf_dw_bias', 'cf_ln_g', 'cf_ln_b', 'ln1_g', 'ln1_b', 'ln2_g', 'ln2_b'):
        g[n] = g[n][0]
    dh = jnp.concatenate([d_dnqkv, dgate_dn, dcf_a, dcf_g, dga, dgb_, dgc, dgd, du_s5, dq_gla, dk_gla, dv_gla, dgate_gla], axis=1)
    dwm = _mm(x, dh, 'tn', 'mm_h_dw')
    dws = _mm(x, dhs, 'tn', 'mm_hs_dw')
    g['w_in'] = jnp.concatenate([
        dwm[:, 0:1536], dws[:, 0:8], dwm[:, 1536:2048], dwm[:, 2048:3072], dwm[:, 7168:7680], _unpad_heads(dwm[:, 7680:8192]),
        _unpad_heads(dwm[:, 8192:8704]), dwm[:, 8704:9216], dwm[:, 9216:9728], dws[:, 8:24], dwm[:, 3072:7168]], axis=1)
    return [dx_a, _mm(dh, o['wm'], 'nt', 'mm_h_dx'), _mm(dhs, o['ws'], 'nt', 'mm_hs_dx')], g


WEIGHTS = ('w_in', 'dn_conv', 'dn_a_log', 'dn_dt_bias', 'dn_norm', 'w_br_dn', 'cf_dw', 'cf_dw_bias', 'cf_ln_g', 'cf_ln_b',
           'w_br_cf', 's5_a_re', 's5_a_im', 's5_log_dt', 's5_b_re', 's5_b_im', 's5_c_re', 's5_c_im', 's5_d', 'w_br_s5',
           'gla_w_alpha', 'gla_b_alpha', 'gla_norm', 'w_br_gla', 'w_o', 'ln1_g', 'ln1_b', 'w_up', 'ffn_conv', 'w_down',
           'ln2_g', 'ln2_b')
LARGE = ('w_in', 'w_br_dn', 'w_br_cf', 'w_br_s5', 'w_br_gla', 'w_o', 'w_up', 'w_down')
SHARD_AXIS = dict(w_in=2, w_br_dn=2, w_br_cf=2, w_br_s5=2, w_br_gla=2, w_o=1, w_up=2, w_down=1,
                  dn_conv=2, cf_dw=2, gla_w_alpha=2, ffn_conv=2)
SMALL = tuple(n for n in WEIGHTS if n not in LARGE)
SMALL_SHARDED = tuple(n for n in SMALL if n in SHARD_AXIS)


def _local_step(x, target, full):
    ops, saved, xs = [], [], [x]
    for l in range(DEPTH):
        o = _layer_operands({n: full[n][l] for n in WEIGHTS})
        y, s = _layer_fwd(xs[-1], o)
        ops.append(o)
        saved.append(s)
        xs.append(y)
    loss, dy = _loss_head(xs[-1], target)
    dparts, grads = [dy], [None] * DEPTH
    for l in reversed(range(DEPTH)):
        dparts, grads[l] = _layer_bwd(xs[l], ops[l], saved[l], dparts)
    grad_x = _sum_parts('sum_dx', dparts)
    return loss[0, 0], grad_x, {n: jnp.stack([grads[l][n] for l in range(DEPTH)]) for n in WEIGHTS}


def _pack(arrs, rows, dtype=f32):
    flat = jnp.concatenate([a.reshape(-1).astype(dtype) for a in arrs])
    return jnp.pad(flat, (0, rows * 1024 - flat.shape[0])).reshape(rows, 1024)


def _unpack(buf, shapes):
    flat, out, pos = buf.reshape(-1), [], 0
    for shp in shapes:
        n = 1
        for d in shp:
            n *= d
        out.append(flat[pos:pos + n].reshape(shp))
        pos += n
    return out


def _rows_for(shapes, mult):
    n = 0
    for shp in shapes:
        k = 1
        for d in shp:
            k *= d
        n += k
    rows = -(-n // 1024)
    return -(-rows // mult) * mult


def _shard(a, axis, chip):
    size = a.shape[axis] // N_CHIPS
    return lax.dynamic_slice_in_dim(a, chip * size, size, axis)


def kernel(x, w_in, dn_conv, dn_a_log, dn_dt_bias, dn_norm, w_br_dn, cf_dw, cf_dw_bias, cf_ln_g, cf_ln_b, w_br_cf, s5_a_re, s5_a_im, s5_log_dt, s5_b_re, s5_b_im, s5_c_re, s5_c_im, s5_d, w_br_s5, gla_w_alpha, gla_b_alpha, gla_norm, w_br_gla, w_o, ln1_g, ln1_b, w_up, ffn_conv, w_down, ln2_g, ln2_b, loss_target, m_w_in, m_dn_conv, m_dn_a_log, m_dn_dt_bias, m_dn_norm, m_w_br_dn, m_cf_dw, m_cf_dw_bias, m_cf_ln_g, m_cf_ln_b, m_w_br_cf, m_s5_a_re, m_s5_a_im, m_s5_log_dt, m_s5_b_re, m_s5_b_im, m_s5_c_re, m_s5_c_im, m_s5_d, m_w_br_s5, m_gla_w_alpha, m_gla_b_alpha, m_gla_norm, m_w_br_gla, m_w_o, m_ln1_g, m_ln1_b, m_w_up, m_ffn_conv, m_w_down, m_ln2_g, m_ln2_b, v_w_in, v_dn_conv, v_dn_a_log, v_dn_dt_bias, v_dn_norm, v_w_br_dn, v_cf_dw, v_cf_dw_bias, v_cf_ln_g, v_cf_ln_b, v_w_br_cf, v_s5_a_re, v_s5_a_im, v_s5_log_dt, v_s5_b_re, v_s5_b_im, v_s5_c_re, v_s5_c_im, v_s5_d, v_w_br_s5, v_gla_w_alpha, v_gla_b_alpha, v_gla_norm, v_w_br_gla, v_w_o, v_ln1_g, v_ln1_b, v_w_up, v_ffn_conv, v_w_down, v_ln2_g, v_ln2_b):
    env = locals()
    w = {n: env[n] for n in WEIGHTS}
    m = {n: env['m_' + n] for n in WEIGHTS}
    v = {n: env['v_' + n] for n in WEIGHTS}
    chip = 2 * lax.axis_index("x") + lax.axis_index("y")

    large_shapes = [w[n].shape for n in LARGE]
    ssh_shapes = [w[n].shape for n in SMALL_SHARDED]
    large_rows, ssh_rows = _rows_for(large_shapes, 512), _rows_for(ssh_shapes, 8)
    got_large, got_ssh = _gather_chips([_pack([w[n] for n in LARGE], large_rows, bf16),
                                        _pack([w[n] for n in SMALL_SHARDED], ssh_rows)])
    full = {n: w[n] for n in SMALL if n not in SHARD_AXIS}
    per_chip = [dict(zip(LARGE + SMALL_SHARDED, _unpack(got_large[k], large_shapes) + _unpack(got_ssh[k], ssh_shapes)))
                for k in range(N_CHIPS)]
    for n in LARGE + SMALL_SHARDED:
        full[n] = jnp.concatenate([per_chip[k][n] for k in range(N_CHIPS)], axis=SHARD_AXIS[n])

    loss, grad_x, g = _local_step(x[0], loss_target[0], full)
    loss = lax.psum(loss, ("x", "y", "c"))

    def piece(k, dtype):
        size = lambda n: g[n].shape[SHARD_AXIS[n]] // N_CHIPS
        return _pack([lax.slice_in_dim(g[n], k * size(n), (k + 1) * size(n), axis=SHARD_AXIS[n]) for n in LARGE], large_rows, dtype)

    from_chips = _scatter_chips(jnp.stack([piece(k, bf16) for k in range(N_CHIPS)]))
    mine = _pack([_shard(g[n], SHARD_AXIS[n], chip) for n in LARGE], large_rows)
    core_sum = _sum_parts('sum_chips', [mine, from_chips[0], from_chips[1], from_chips[2]])
    outs_large = _adamw('adamw_large', _pack([w[n] for n in LARGE], large_rows), _pack([m[n] for n in LARGE], large_rows),
                        _pack([v[n] for n in LARGE], large_rows), [core_sum, _swap_sibling(core_sum)])

    small_full_shapes = [g[n].shape for n in SMALL]
    small_rows = _rows_for(small_full_shapes, 8)
    small_sum = _sum_slots('sum_devices', _gather_all(_pack([g[n] for n in SMALL], small_rows)))
    gs = dict(zip(SMALL, _unpack(small_sum, small_full_shapes)))
    for n in SMALL_SHARDED:
        gs[n] = _shard(gs[n], SHARD_AXIS[n], chip)
    small_shapes = [w[n].shape for n in SMALL]
    upd_rows = _rows_for(small_shapes, 8)
    outs_small = _adamw('adamw_small', _pack([w[n] for n in SMALL], upd_rows), _pack([m[n] for n in SMALL], upd_rows),
                        _pack([v[n] for n in SMALL], upd_rows), [_pack([gs[n] for n in SMALL], upd_rows)])

    res = []
    for kind in range(4):
        d = dict(zip(LARGE, _unpack(outs_large[kind], large_shapes)))
        d.update(zip(SMALL, _unpack(outs_small[kind], small_shapes)))
        res.append([d[n] for n in WEIGHTS])
    return (loss, grad_x[None], *res[0], *res[1], *res[2], *res[3])
```

```python
import functools

import jax
import jax.numpy as jnp
from jax import lax
from jax.experimental import pallas as pl
from jax.experimental.pallas import tpu as pltpu

f32 = jnp.float32
bf16 = jnp.bfloat16
HI = lax.Precision.HIGHEST

D_MODEL = 1024
DEPTH = 4
DN_HEADS, DN_DK, DN_CHUNK = 4, 128, 64
GLA_HEADS, GLA_DK, GLA_CHUNK, GLA_TAU = 4, 64, 16, 16.0
GLA_BLOCK = 128
S5_GROUPS, S5_GROUP, S5_STATE = 32, 16, 64
S5_SB = 4
S5_TILE = 256
D_FF = 2816
LN_EPS = 1e-5
ALPHA = (2.0 * DEPTH) ** 0.25
ADAM_LR, ADAM_B1, ADAM_B2, ADAM_EPS, ADAM_WD, ADAM_STEP = 0.001, 0.9, 0.999, 1e-08, 0.01, 10

VMEM_LIMIT_V7X = 56 * 1024 * 1024
MM_A_BLOCK_BYTES = 8 * 1024 * 1024
HALO = 32
N_CHIPS = 4
N_DEV = 8

IN_ORIG = dict(dn_qkv=(0, 1536), dn_a=(1536, 1540), dn_b=(1540, 1544), dn_gate=(1544, 2056), cf=(2056, 3080),
               s5=(3080, 3592), gla_q=(3592, 3848), gla_k=(3848, 4104), gla_v=(4104, 4616), gla_g=(4616, 5128),
               gla_lr=(5128, 5144), gates=(5144, 9240))
IN_COLS = 9240
WM_COLS = 9728
WS_COLS = 128


def _cparams(sem):
    return pltpu.CompilerParams(dimension_semantics=sem, vmem_limit_bytes=VMEM_LIMIT_V7X)


def _pick(dim, pref):
    for t in (pref, 512, 256, 128, 64, 32, 16, 8):
        if t <= pref and dim % t == 0:
            return t
    return dim


NN = (((1,), (0,)), ((), ()))
NT = (((1,), (1,)), ((), ()))
TN = (((0,), (0,)), ((), ()))


def _dot(a, b, dims=NN):
    return lax.dot_general(a, b, dims, precision=HI, preferred_element_type=f32)


def _round(a):
    return a.astype(bf16).astype(f32)


def _rdot(a, b, dims=NN):
    return lax.dot_general(a.astype(bf16), b.astype(bf16), dims, preferred_element_type=f32)


@functools.partial(jax.custom_vjp, nondiff_argnums=(2,))
def _bdot(a, b, dims=NN):
    return _rdot(a, b, dims)


def _bdot_fwd(a, b, dims):
    return _rdot(a, b, dims), (a, b)


def _bdot_bwd(dims, res, g):
    a, b = res
    if dims == NN:
        return _rdot(g, b, NT), _rdot(a, g, TN)
    if dims == NT:
        return _rdot(g, b, NN), _rdot(g, a, TN)
    assert dims == TN
    return _rdot(b, g, NT), _rdot(a, g, NN)


_bdot.defvjp(_bdot_fwd, _bdot_bwd)


def _iota(shape, axis):
    return lax.broadcasted_iota(jnp.int32, shape, axis)


def _mm(a, b, mode, name, tm=512, tn=512, out_dtype=f32):
    if mode == 'nn':
        (m, k), n = a.shape, b.shape[1]
    elif mode == 'nt':
        (m, k), n = a.shape, b.shape[0]
    else:
        (k, m), n = a.shape, b.shape[1]
    tm, tn = _pick(m, tm), _pick(n, tn)
    nk = 1
    while (k // nk) * tm * a.dtype.itemsize > MM_A_BLOCK_BYTES or k % nk or (k // nk) % 128:
        nk += 1
    tk = k // nk
    assert nk == 1 or out_dtype == f32
    if mode == 'nn':
        dims = NN
        a_spec = pl.BlockSpec((tm, tk), lambda i, j, kk: (i, kk))
        b_spec = pl.BlockSpec((tk, tn), lambda i, j, kk: (kk, j))
    elif mode == 'nt':
        dims = NT
        a_spec = pl.BlockSpec((tm, tk), lambda i, j, kk: (i, kk))
        b_spec = pl.BlockSpec((tn, tk), lambda i, j, kk: (j, kk))
    else:
        dims = TN
        a_spec = pl.BlockSpec((tk, tm), lambda i, j, kk: (kk, i))
        b_spec = pl.BlockSpec((tk, tn), lambda i, j, kk: (kk, j))

    def kern(a_ref, b_ref, o_ref):
        p = lax.dot_general(a_ref[...].astype(bf16), b_ref[...].astype(bf16), dims, preferred_element_type=f32)
        if nk == 1:
            o_ref[...] = p.astype(o_ref.dtype)
        else:
            kk = pl.program_id(2)

            @pl.when(kk == 0)
            def _():
                o_ref[...] = p

            @pl.when(kk > 0)
            def _():
                o_ref[...] += p

    return pl.pallas_call(
        kern, name=name, grid=(m // tm, n // tn, nk),
        in_specs=[a_spec, b_spec], out_specs=pl.BlockSpec((tm, tn), lambda i, j, kk: (i, j)),
        out_shape=jax.ShapeDtypeStruct((m, n), out_dtype),
        compiler_params=_cparams(("parallel", "parallel", "arbitrary")),
    )(a, b)


def _full_spec(p):
    nd = p.ndim
    return pl.BlockSpec(p.shape, lambda *_, nd=nd: (0,) * nd)


def _rowwise(name, f, ins, params, out_widths, tile=256, out_dtype=f32):
    rows = ins[0][0].shape[0]
    tile = _pick(rows, tile)
    n_x = len(ins) + len(params)

    def kern(*refs):
        for o_ref, r in zip(refs[n_x:], f(*[r[...] for r in refs[:n_x]])):
            o_ref[...] = r.astype(o_ref.dtype)

    in_specs = [pl.BlockSpec((tile, w), lambda i, c=c: (i, c)) for (_, w, c) in ins] + [_full_spec(p) for p in params]
    return pl.pallas_call(
        kern, name=name, grid=(rows // tile,), in_specs=in_specs,
        out_specs=[pl.BlockSpec((tile, w), lambda i: (i, 0)) for w in out_widths],
        out_shape=[jax.ShapeDtypeStruct((rows, w), out_dtype) for w in out_widths],
        compiler_params=_cparams(("parallel",)),
    )(*[a for (a, _, _) in ins], *params)


def _rowwise_bwd(name, f, ins, params, douts, want, tile=256):
    rows = ins[0][0].shape[0]
    tile = _pick(rows, tile)
    n_in, n_p = len(ins), len(params)
    parts = [p for d in douts for p in d]
    n_x, n_d = n_in + n_p, len(parts)

    def kern(*refs):
        xs = [r[...] for r in refs[:n_x]]
        d_refs, o_refs = refs[n_x:n_x + n_d], refs[n_x + n_d:]
        cts, pos = [], 0
        for d in douts:
            acc = d_refs[pos][...]
            for r in d_refs[pos + 1:pos + len(d)]:
                acc = acc + r[...]
            pos += len(d)
            cts.append(acc)
        grads = jax.vjp(f, *xs)[1](tuple(cts))
        k = 0
        for j in range(n_in):
            if want[j]:
                o_refs[k][...] = grads[j].astype(o_refs[k].dtype)
                k += 1
        first = pl.program_id(0) == 0
        for j in range(n_p):
            g, o_ref = grads[n_in + j], o_refs[k + j]

            @pl.when(first)
            def _(o_ref=o_ref, g=g):
                o_ref[...] = g

            @pl.when(jnp.logical_not(first))
            def _(o_ref=o_ref, g=g):
                o_ref[...] += g

    in_specs = ([pl.BlockSpec((tile, w), lambda i, c=c: (i, c)) for (_, w, c) in ins] + [_full_spec(p) for p in params]
                + [pl.BlockSpec((tile, w), lambda i, c=c: (i, c)) for (_, w, c) in parts])
    out_specs, out_shape = [], []
    for j in range(n_in):
        if want[j]:
            out_specs.append(pl.BlockSpec((tile, ins[j][1]), lambda i: (i, 0)))
            out_shape.append(jax.ShapeDtypeStruct((rows, ins[j][1]), want[j]))
    n_g = len(out_specs)
    for p in params:
        out_specs.append(_full_spec(p))
        out_shape.append(jax.ShapeDtypeStruct(p.shape, f32))
    res = pl.pallas_call(
        kern, name=name, grid=(rows // tile,), in_specs=in_specs, out_specs=out_specs, out_shape=out_shape,
        compiler_params=_cparams(("arbitrary",)),
    )(*[a for (a, _, _) in ins], *params, *[a for (a, _, _) in parts])
    return list(res[:n_g]), list(res[n_g:])


_sigmoid = jax.nn.sigmoid
_silu = jax.nn.silu
_softplus = jax.nn.softplus
_log_sigmoid = jax.nn.log_sigmoid


def _f_ln(x, r, g, b):
    t = ALPHA * x + r
    mu = jnp.mean(t, -1, keepdims=True)
    var = jnp.mean(jnp.square(t - mu), -1, keepdims=True)
    return ((t - mu) * lax.rsqrt(var + LN_EPS) * g + b,)


def _f_pre(c1, hs, a_log, dt_bias, w_alpha, b_alpha):
    s = _silu(c1)
    outs = []
    for j in range(3 * DN_HEADS):
        t = s[:, j * DN_DK:(j + 1) * DN_DK]
        if j < 2 * DN_HEADS:
            t = t * lax.rsqrt(jnp.sum(t * t, -1, keepdims=True) + 1e-6)
        if j < DN_HEADS:
            t = t * (DN_DK ** -0.5)
        outs.append(t)
    qkvn = jnp.concatenate(outs, axis=1)
    lane = _iota(hs.shape, 1)
    g = -jnp.exp(a_log) * _softplus(hs + dt_bias)
    beta = _sigmoid(hs)
    gb = jnp.where(lane < DN_HEADS, g, jnp.where(lane < 2 * DN_HEADS, beta, 0.0))
    la = _log_sigmoid(_bdot(hs, w_alpha) + b_alpha) * (1.0 / GLA_TAU)
    lane5 = _iota(la.shape, 1)
    la = jnp.where((lane5 % 128) < GLA_DK, la, 0.0)
    return qkvn, gb, la


def _f_post(o, gate, w):
    outs = []
    for j in range(4):
        t = o[:, j * 128:(j + 1) * 128]
        outs.append(t * lax.rsqrt(jnp.mean(t * t, -1, keepdims=True) + LN_EPS) * w)
    return (jnp.concatenate(outs, axis=1) * _silu(gate),)


def _f_glu(a, g):
    return (a * _sigmoid(g),)


def _f_cfpost(c, bias, g, b):
    t = c + bias
    mu = jnp.mean(t, -1, keepdims=True)
    var = jnp.mean(jnp.square(t - mu), -1, keepdims=True)
    return (_silu((t - mu) * lax.rsqrt(var + LN_EPS) * g + b),)


def _f_gelu(y):
    return (jax.nn.gelu(y),)


def _f_merge(ga, gb_, gc, gd, ya, yb, zv, zg, yd):
    return (_sigmoid(ga) * ya + _sigmoid(gb_) * yb + _sigmoid(gc) * (zv * _sigmoid(zg)) + _sigmoid(gd) * yd,)


def _f_act(a, b):
    return (_silu(a) * b,)


def _conv_tiles(rows, ch):
    return _pick(rows, 256), _pick(ch, 512)


def _conv_fwd(name, x, width, colblk0, w, taps):
    rows = x.shape[0]
    tr, cb = _conv_tiles(rows, width)
    hb = tr // HALO

    def kern(prev_ref, x_ref, w_ref, o_ref, ext):
        i = pl.program_id(1)
        ext[pl.ds(0, HALO), :] = jnp.where(i > 0, _round(prev_ref[...]), 0.0)
        ext[pl.ds(HALO, tr), :] = _round(x_ref[...])
        wv = _round(w_ref[...])
        acc = jnp.zeros((tr, cb), f32)
        for k in range(taps):
            acc = acc + wv[k:k + 1, :] * ext[pl.ds(HALO - taps + 1 + k, tr), :]
        o_ref[...] = acc

    c0 = colblk0 * (width // cb)
    return pl.pallas_call(
        kern, name=name, grid=(width // cb, rows // tr),
        in_specs=[pl.BlockSpec((HALO, cb), lambda c, i: (jnp.maximum(i * hb - 1, 0), c0 + c)),
                  pl.BlockSpec((tr, cb), lambda c, i: (i, c0 + c)),
                  pl.BlockSpec((w.shape[0], cb), lambda c, i: (0, c))],
        out_specs=pl.BlockSpec((tr, cb), lambda c, i: (i, c)),
        out_shape=jax.ShapeDtypeStruct((rows, width), f32),
        scratch_shapes=[pltpu.VMEM((HALO + tr, cb), f32)],
        compiler_params=_cparams(("parallel", "arbitrary")),
    )(x, x, w)


def _conv_bwd(name, x, width, colblk0, w, taps, dy, dx_dtype=f32):
    rows = x.shape[0]
    tr, cb = _conv_tiles(rows, width)
    hb = tr // HALO
    nt = rows // tr
    wr = w.shape[0]

    def kern(prev_ref, x_ref, w_ref, dy_ref, next_ref, dx_ref, dw_ref, ext, dext):
        i = pl.program_id(1)
        ext[pl.ds(0, HALO), :] = jnp.where(i > 0, _round(prev_ref[...]), 0.0)
        ext[pl.ds(HALO, tr), :] = _round(x_ref[...])
        dyv = _round(dy_ref[...])
        dext[pl.ds(0, tr), :] = dyv
        dext[pl.ds(tr, HALO), :] = jnp.where(i < nt - 1, _round(next_ref[...]), 0.0)
        wv = _round(w_ref[...])
        acc = jnp.zeros((tr, cb), f32)
        rows_w = []
        for k in range(taps):
            acc = acc + wv[k:k + 1, :] * dext[pl.ds(taps - 1 - k, tr), :]
            rows_w.append(jnp.sum(dyv * ext[pl.ds(HALO - taps + 1 + k, tr), :], axis=0, keepdims=True))
        dx_ref[...] = acc.astype(dx_ref.dtype)
        if wr > taps:
            rows_w.append(jnp.zeros((wr - taps, cb), f32))
        dwv = jnp.concatenate(rows_w, axis=0)

        @pl.when(i == 0)
        def _():
            dw_ref[...] = dwv

        @pl.when(i > 0)
        def _():
            dw_ref[...] += dwv

    c0 = colblk0 * (width // cb)
    return pl.pallas_call(
        kern, name=name, grid=(width // cb, nt),
        in_specs=[pl.BlockSpec((HALO, cb), lambda c, i: (jnp.maximum(i * hb - 1, 0), c0 + c)),
                  pl.BlockSpec((tr, cb), lambda c, i: (i, c0 + c)),
                  pl.BlockSpec((wr, cb), lambda c, i: (0, c)),
                  pl.BlockSpec((tr, cb), lambda c, i: (i, c)),
                  pl.BlockSpec((HALO, cb), lambda c, i: (jnp.minimum((i + 1) * hb, nt * hb - 1), c))],
        out_specs=[pl.BlockSpec((tr, cb), lambda c, i: (i, c)), pl.BlockSpec((wr, cb), lambda c, i: (0, c))],
        out_shape=[jax.ShapeDtypeStruct((rows, width), dx_dtype), jax.ShapeDtypeStruct((wr, width), f32)],
        scratch_shapes=[pltpu.VMEM((HALO + tr, cb), f32), pltpu.VMEM((HALO + tr, cb), f32)],
        compiler_params=_cparams(("parallel", "arbitrary")),
    )(x, x, w, dy, dy)


def _series_inverse(neg):
    n = neg.shape[0]
    inv = jnp.where(_iota((n, n), 0) == _iota((n, n), 1), 1.0, 0.0) + neg
    p = neg
    for _ in range(5):
        p = _dot(p, p)
        inv = inv + _dot(inv, p)
    return inv


def _inverse_bwd(inv, g):
    return _dot(_dot(inv, g, TN), inv, NT)


@jax.custom_vjp
def _unit_lower_inverse(neg):
    return _series_inverse(neg)


_unit_lower_inverse.defvjp(lambda neg: (_series_inverse(neg),) * 2, lambda inv, g: (_inverse_bwd(inv, g),))


@jax.custom_vjp
def _known_inverse(neg, inv):
    return inv


_known_inverse.defvjp(lambda neg, inv: (inv, inv), lambda inv, g: (_inverse_bwd(inv, g), jnp.zeros_like(inv)))


def _dn_head(state, q, k, v, gc, gl, beta, inv_saved=None):
    c = DN_CHUNK
    ii, jj = _iota((c, c), 0), _iota((c, c), 1)
    causal, strict = ii >= jj, ii > jj
    gcb = jnp.broadcast_to(gc, (c, c))
    decay = jnp.where(causal, jnp.exp(jnp.where(causal, gcb - gcb.T, 0.0)), 0.0)
    kb = k * beta
    neg = -jnp.where(strict, _bdot(kb, k, NT) * decay, 0.0)
    inv = _unit_lower_inverse(neg) if inv_saved is None else _known_inverse(neg, inv_saved)
    egc = jnp.exp(gc)
    u = _dot(inv, v * beta)
    w = _dot(inv, kb * egc)
    intra = _bdot(q, k, NT) * decay
    v_new = u - _bdot(w, state)
    o = _bdot(q * egc, state) + _bdot(intra, v_new)
    new_state = state * jnp.exp(gl) + _bdot(k * jnp.exp(gl - gc), v_new, TN)
    return (o, new_state, inv) if inv_saved is None else (o, new_state)


def _dn_cum(gb):
    c = DN_CHUNK
    tril = jnp.where(_iota((c, c), 0) >= _iota((c, c), 1), 1.0, 0.0)
    return _dot(tril, gb), jnp.sum(gb, axis=0, keepdims=True)


def _dn_fwd(qkvn, gb):
    rows = qkvn.shape[0]
    c, h, d = DN_CHUNK, DN_HEADS, DN_DK
    nc = rows // c

    def kern(qkv_ref, gb_ref, o_ref, st_ref, inv_ref, state):
        @pl.when(pl.program_id(0) == 0)
        def _():
            state[...] = jnp.zeros_like(state)

        gbv = gb_ref[...]
        cum, tot = _dn_cum(gbv)
        for j in range(h):
            st = state[j]
            st_ref[0, j] = st
            o, new, inv = _dn_head(st, qkv_ref[:, j * d:(j + 1) * d], qkv_ref[:, (h + j) * d:(h + j + 1) * d],
                                   qkv_ref[:, (2 * h + j) * d:(2 * h + j + 1) * d],
                                   cum[:, j:j + 1], tot[:, j:j + 1], gbv[:, h + j:h + j + 1])
            o_ref[:, j * d:(j + 1) * d] = o
            inv_ref[0, j] = inv
            state[j] = new

    return pl.pallas_call(
        kern, name="dn_fwd", grid=(nc,),
        in_specs=[pl.BlockSpec((c, 3 * h * d), lambda i: (i, 0)), pl.BlockSpec((c, 128), lambda i: (i, 0))],
        out_specs=[pl.BlockSpec((c, h * d), lambda i: (i, 0)), pl.BlockSpec((1, h, d, d), lambda i: (i, 0, 0, 0)),
                   pl.BlockSpec((1, h, c, c), lambda i: (i, 0, 0, 0))],
        out_shape=[jax.ShapeDtypeStruct((rows, h * d), f32), jax.ShapeDtypeStruct((nc, h, d, d), f32),
                   jax.ShapeDtypeStruct((nc, h, c, c), f32)],
        scratch_shapes=[pltpu.VMEM((h, d, d), f32)],
        compiler_params=_cparams(("arbitrary",)),
    )(qkvn, gb)


def _dn_bwd(qkvn, gb, states, invs, do):
    rows = qkvn.shape[0]
    c, h, d = DN_CHUNK, DN_HEADS, DN_DK
    nc = rows // c

    def kern(qkv_ref, gb_ref, st_ref, inv_ref, do_ref, dqkv_ref, dgb_ref, dstate):
        @pl.when(pl.program_id(0) == 0)
        def _():
            dstate[...] = jnp.zeros_like(dstate)

        gbv = gb_ref[...]
        cum, tot = _dn_cum(gbv)
        lane = _iota((c, 128), 1)
        dcum = jnp.zeros((c, 128), f32)
        dgb = jnp.zeros((c, 128), f32)
        for j in range(h):
            args = (st_ref[0, j], qkv_ref[:, j * d:(j + 1) * d], qkv_ref[:, (h + j) * d:(h + j + 1) * d],
                    qkv_ref[:, (2 * h + j) * d:(2 * h + j + 1) * d],
                    cum[:, j:j + 1], tot[:, j:j + 1], gbv[:, h + j:h + j + 1])
            head = functools.partial(_dn_head, inv_saved=inv_ref[0, j])
            ds, dq, dk, dv, dgc, dgl, dbeta = jax.vjp(head, *args)[1]((do_ref[:, j * d:(j + 1) * d], dstate[j]))
            dstate[j] = ds
            dqkv_ref[:, j * d:(j + 1) * d] = dq
            dqkv_ref[:, (h + j) * d:(h + j + 1) * d] = dk
            dqkv_ref[:, (2 * h + j) * d:(2 * h + j + 1) * d] = dv
            dcum = dcum + jnp.where(lane == j, dgc, 0.0)
            dgb = dgb + jnp.where(lane == j, dgl, 0.0) + jnp.where(lane == h + j, dbeta, 0.0)
        triu = jnp.where(_iota((c, c), 0) <= _iota((c, c), 1), 1.0, 0.0)
        dgb_ref[...] = dgb + _dot(triu, dcum)

    rev = lambda i: (nc - 1 - i, 0)
    return pl.pallas_call(
        kern, name="dn_bwd", grid=(nc,),
        in_specs=[pl.BlockSpec((c, 3 * h * d), rev), pl.BlockSpec((c, 128), rev),
                  pl.BlockSpec((1, h, d, d), lambda i: (nc - 1 - i, 0, 0, 0)),
                  pl.BlockSpec((1, h, c, c), lambda i: (nc - 1 - i, 0, 0, 0)), pl.BlockSpec((c, h * d), rev)],
        out_specs=[pl.BlockSpec((c, 3 * h * d), rev), pl.BlockSpec((c, 128), rev)],
        out_shape=[jax.ShapeDtypeStruct((rows, 3 * h * d), f32), jax.ShapeDtypeStruct((rows, 128), f32)],
        scratch_shapes=[pltpu.VMEM((h, d, d), f32)],
        compiler_params=_cparams(("arbitrary",)),
    )(qkvn, gb, states, invs, do)


def _gla_block(state_t, q, k, v, la):
    c = GLA_CHUNK
    ii, jj = _iota((c, c), 0), _iota((c, c), 1)
    causal = ii >= jj
    tril = jnp.where(causal, 1.0, 0.0)
    outs = []
    for n in range(q.shape[0] // c):
        sl = slice(n * c, (n + 1) * c)
        qn, kn, vn, ln = q[sl] * (GLA_DK ** -0.5), k[sl], v[sl], la[sl]
        gc = _dot(tril, ln)
        gl = jnp.sum(ln, axis=0, keepdims=True)
        q_dec = qn * jnp.exp(gc)
        scores = jnp.where(causal, _bdot(q_dec, kn * jnp.exp(-gc), NT), 0.0)
        outs.append(_bdot(q_dec, state_t, NT) + _bdot(scores, vn))
        state_t = state_t * jnp.exp(gl) + _bdot(vn, kn * jnp.exp(gl - gc), TN)
    return jnp.concatenate(outs, axis=0), state_t


GLA_Q0, GLA_K0, GLA_V0 = 7680 // 128, 8192 // 128, 8704 // 128


def _gla_fwd(hmain, la):
    rows = hmain.shape[0]
    b, h = _pick(rows, GLA_BLOCK), GLA_HEADS
    nb = rows // b

    def kern(q_ref, k_ref, v_ref, la_ref, o_ref, st_ref, state):
        @pl.when(pl.program_id(1) == 0)
        def _():
            state[...] = jnp.zeros_like(state)

        st = state[...]
        st_ref[0, 0] = st
        o, new = _gla_block(st, q_ref[...], k_ref[...], v_ref[...], la_ref[...])
        o_ref[...] = o
        state[...] = new

    return pl.pallas_call(
        kern, name="gla_fwd", grid=(h, nb),
        in_specs=[pl.BlockSpec((b, 128), lambda j, i: (i, GLA_Q0 + j)), pl.BlockSpec((b, 128), lambda j, i: (i, GLA_K0 + j)),
                  pl.BlockSpec((b, 128), lambda j, i: (i, GLA_V0 + j)), pl.BlockSpec((b, 128), lambda j, i: (i, j))],
        out_specs=[pl.BlockSpec((b, 128), lambda j, i: (i, j)), pl.BlockSpec((1, 1, 128, 128), lambda j, i: (j, i, 0, 0))],
        out_shape=[jax.ShapeDtypeStruct((rows, h * 128), f32), jax.ShapeDtypeStruct((h, nb, 128, 128), f32)],
        scratch_shapes=[pltpu.VMEM((128, 128), f32)],
        compiler_params=_cparams(("parallel", "arbitrary")),
    )(hmain, hmain, hmain, la)


def _gla_bwd(hmain, la, states, do):
    rows = hmain.shape[0]
    b, h = _pick(rows, GLA_BLOCK), GLA_HEADS
    nb = rows // b

    def kern(q_ref, k_ref, v_ref, la_ref, st_ref, do_ref, dq_ref, dk_ref, dv_ref, dla_ref, dstate):
        @pl.when(pl.program_id(1) == 0)
        def _():
            dstate[...] = jnp.zeros_like(dstate)

        args = (st_ref[0, 0], q_ref[...], k_ref[...], v_ref[...], la_ref[...])
        ds, dq, dk, dv, dla = jax.vjp(_gla_block, *args)[1]((do_ref[...], dstate[...]))
        dstate[...] = ds
        dq_ref[...] = dq.astype(bf16)
        dk_ref[...] = dk.astype(bf16)
        dv_ref[...] = dv.astype(bf16)
        dla_ref[...] = dla

    return pl.pallas_call(
        kern, name="gla_bwd", grid=(h, nb),
        in_specs=[pl.BlockSpec((b, 128), lambda j, i: (nb - 1 - i, GLA_Q0 + j)),
                  pl.BlockSpec((b, 128), lambda j, i: (nb - 1 - i, GLA_K0 + j)),
                  pl.BlockSpec((b, 128), lambda j, i: (nb - 1 - i, GLA_V0 + j)),
                  pl.BlockSpec((b, 128), lambda j, i: (nb - 1 - i, j)),
                  pl.BlockSpec((1, 1, 128, 128), lambda j, i: (j, nb - 1 - i, 0, 0)),
                  pl.BlockSpec((b, 128), lambda j, i: (nb - 1 - i, j))],
        out_specs=[pl.BlockSpec((b, 128), lambda j, i: (nb - 1 - i, j))] * 4,
        out_shape=[jax.ShapeDtypeStruct((rows, h * 128), bf16)] * 3 + [jax.ShapeDtypeStruct((rows, h * 128), f32)],
        scratch_shapes=[pltpu.VMEM((128, 128), f32)],
        compiler_params=_cparams(("parallel", "arbitrary")),
    )(hmain, hmain, hmain, la, states, do)


def _f_s5_params(a_re, a_im, log_dt, b_re, b_im):
    dt = jnp.exp(log_dt)
    mag = jnp.exp(dt * a_re)
    abar_re, abar_im = mag * jnp.cos(dt * a_im), mag * jnp.sin(dt * a_im)
    den = a_re * a_re + a_im * a_im
    nr, ni = abar_re - 1.0, abar_im
    fr, fi = (nr * a_re + ni * a_im) / den, (ni * a_re - nr * a_im) / den
    return abar_re, abar_im, fr[None] * b_re - fi[None] * b_im, fr[None] * b_im + fi[None] * b_re


def _s5_params(a_re, a_im, log_dt, b_re, b_im):
    def kern(*refs):
        for o_ref, r in zip(refs[5:], _f_s5_params(*[r[...] for r in refs[:5]])):
            o_ref[...] = r

    ins = (a_re, a_im, log_dt, b_re, b_im)
    return pl.pallas_call(
        kern, name="s5_params", out_shape=[jax.ShapeDtypeStruct(a_re.shape, f32)] * 2 + [jax.ShapeDtypeStruct(b_re.shape, f32)] * 2,
    )(*ins)


def _s5_params_bwd(a_re, a_im, log_dt, b_re, b_im, d_ar, d_ai, d_br, d_bi):
    def kern(*refs):
        grads = jax.vjp(_f_s5_params, *[r[...] for r in refs[:5]])[1](tuple(r[...] for r in refs[5:9]))
        for o_ref, g in zip(refs[9:], grads):
            o_ref[...] = g

    ins = (a_re, a_im, log_dt, b_re, b_im)
    return pl.pallas_call(
        kern, name="s5_params_bwd", out_shape=[jax.ShapeDtypeStruct(t.shape, f32) for t in ins],
    )(*ins, d_ar, d_ai, d_br, d_bi)


def _cmul(ar, ai, br, bi):
    return ar * br - ai * bi, ar * bi + ai * br


def _s5_scan(xr, xi, ar, ai, reverse):
    t = xr.shape[0]
    row = _iota(xr.shape, 0)
    s = 1
    while s < t:
        if reverse:
            keep = row < t - s
            sr, si = pltpu.roll(xr, t - s, 0), pltpu.roll(xi, t - s, 0)
        else:
            keep = row >= s
            sr, si = pltpu.roll(xr, s, 0), pltpu.roll(xi, s, 0)
        sr, si = jnp.where(keep, sr, 0.0), jnp.where(keep, si, 0.0)
        pr, pi = _cmul(ar, ai, sr, si)
        xr, xi = xr + pr, xi + pi
        ar, ai = _cmul(ar, ai, ar, ai)
        s *= 2
    return xr, xi


def _s5_powers(ar, ai, t, reverse):
    row = _iota((t, ar.shape[1]), 0)
    at = (row == (t - 1 if reverse else 0))
    return _s5_scan(jnp.where(at, ar, 0.0), jnp.where(at, ai, 0.0), ar, ai, reverse)


S5_U0 = 7168 // 128


def _s5_fwd(hmain, abar, bmat_re, bmat_im, cmat_re, cmat_im, dvec):
    rows = hmain.shape[0]
    t = _pick(rows, S5_TILE)
    nt, ns = rows // t, 512

    def kern(u_ref, a_ref, br_ref, bi_ref, cr_ref, ci_ref, d_ref, y_ref, xr_ref, xi_ref, pw, carry):
        ar, ai = a_ref[0, 0], a_ref[1, 0]

        @pl.when(pl.program_id(1) == 0)
        def _():
            pr, pi = _s5_powers(ar, ai, t, False)
            pw[0], pw[1] = pr, pi
            carry[...] = jnp.zeros_like(carry)

        u = u_ref[...]
        xr, xi = _s5_scan(_rdot(u, br_ref[0]), _rdot(u, bi_ref[0]), ar, ai, False)
        cr, ci = carry[0:1, :], carry[1:2, :]
        qr, qi = _cmul(pw[0], pw[1], cr, ci)
        xr, xi = xr + qr, xi + qi
        xr_ref[...] = xr
        xi_ref[...] = xi
        carry[0:1, :] = xr[t - 1:t, :]
        carry[1:2, :] = xi[t - 1:t, :]
        y_ref[...] = _rdot(xr, cr_ref[0]) - _rdot(xi, ci_ref[0]) + d_ref[...] * u

    sb3 = lambda b, i: (b, 0, 0)
    return pl.pallas_call(
        kern, name="s5_fwd", grid=(S5_SB, nt),
        in_specs=[pl.BlockSpec((t, 128), lambda b, i: (i, S5_U0 + b)), pl.BlockSpec((2, 1, 1, ns), lambda b, i: (0, b, 0, 0)),
                  pl.BlockSpec((1, 128, ns), sb3), pl.BlockSpec((1, 128, ns), sb3),
                  pl.BlockSpec((1, ns, 128), sb3), pl.BlockSpec((1, ns, 128), sb3), pl.BlockSpec((1, 128), lambda b, i: (0, b))],
        out_specs=[pl.BlockSpec((t, 128), lambda b, i: (i, b)), pl.BlockSpec((t, ns), lambda b, i: (i, b)),
                   pl.BlockSpec((t, ns), lambda b, i: (i, b))],
        out_shape=[jax.ShapeDtypeStruct((rows, 512), f32), jax.ShapeDtypeStruct((rows, S5_SB * ns), f32),
                   jax.ShapeDtypeStruct((rows, S5_SB * ns), f32)],
        scratch_shapes=[pltpu.VMEM((2, t, ns), f32), pltpu.VMEM((8, ns), f32)],
        compiler_params=_cparams(("parallel", "arbitrary")),
    )(hmain, abar, bmat_re, bmat_im, cmat_re, cmat_im, dvec)


def _s5_bwd(hmain, abar, bmat_re, bmat_im, cmat_re, cmat_im, dvec, x_re, x_im, dy):
    rows = hmain.shape[0]
    t = _pick(rows, S5_TILE)
    nt, ns = rows // t, 512
    t8 = t // 8

    def kern(u_ref, a_ref, br_ref, bi_ref, cr_ref, ci_ref, d_ref, xr_ref, xi_ref, xpr_ref, xpi_ref, dy_ref,
             du_ref, da_ref, dbr_ref, dbi_ref, dcr_ref, dci_ref, dd_ref, pw, carry):
        i = pl.program_id(1)
        ar, ai = a_ref[0, 0], -a_ref[1, 0]

        @pl.when(i == 0)
        def _():
            pr, pi = _s5_powers(ar, ai, t, True)
            pw[0], pw[1] = pr, pi
            carry[...] = jnp.zeros_like(carry)

        u, gy = u_ref[...], dy_ref[...]
        lr, li = _s5_scan(_rdot(gy, cr_ref[0], NT), -_rdot(gy, ci_ref[0], NT), ar, ai, True)
        qr, qi = _cmul(pw[0], pw[1], carry[0:1, :], carry[1:2, :])
        lr, li = lr + qr, li + qi
        carry[0:1, :] = lr[0:1, :]
        carry[1:2, :] = li[0:1, :]
        du_ref[...] = (_rdot(lr, br_ref[0], NT) + _rdot(li, bi_ref[0], NT) + d_ref[...] * gy).astype(bf16)
        xr, xi = xr_ref[...], xi_ref[...]
        row = _iota(xr.shape, 0)
        first_r = jnp.where(i < nt - 1, xpr_ref[7:8, :], 0.0)
        first_i = jnp.where(i < nt - 1, xpi_ref[7:8, :], 0.0)
        xpr = jnp.where(row == 0, first_r, pltpu.roll(xr, 1, 0))
        xpi = jnp.where(row == 0, first_i, pltpu.roll(xi, 1, 0))
        da_r = jnp.sum(lr * xpr + li * xpi, axis=0, keepdims=True)
        da_i = jnp.sum(li * xpr - lr * xpi, axis=0, keepdims=True)
        upd = [(da_ref.at[0, 0], da_r), (da_ref.at[1, 0], da_i),
               (dbr_ref.at[0], _rdot(u, lr, TN)), (dbi_ref.at[0], _rdot(u, li, TN)),
               (dcr_ref.at[0], _rdot(xr, gy, TN)), (dci_ref.at[0], -_rdot(xi, gy, TN)),
               (dd_ref, jnp.sum(gy * u, axis=0, keepdims=True))]

        @pl.when(i == 0)
        def _():
            for ref, val in upd:
                ref[...] = val

        @pl.when(i > 0)
        def _():
            for ref, val in upd:
                ref[...] += val

    sb3 = lambda b, i: (b, 0, 0)
    rev = lambda b, i: (nt - 1 - i, b)
    prev8 = lambda b, i: (jnp.maximum((nt - 1 - i) * t8 - 1, 0), b)
    return pl.pallas_call(
        kern, name="s5_bwd", grid=(S5_SB, nt),
        in_specs=[pl.BlockSpec((t, 128), lambda b, i: (nt - 1 - i, S5_U0 + b)), pl.BlockSpec((2, 1, 1, ns), lambda b, i: (0, b, 0, 0)),
                  pl.BlockSpec((1, 128, ns), sb3), pl.BlockSpec((1, 128, ns), sb3),
                  pl.BlockSpec((1, ns, 128), sb3), pl.BlockSpec((1, ns, 128), sb3), pl.BlockSpec((1, 128), lambda b, i: (0, b)),
                  pl.BlockSpec((t, ns), rev), pl.BlockSpec((t, ns), rev), pl.BlockSpec((8, ns), prev8), pl.BlockSpec((8, ns), prev8),
                  pl.BlockSpec((t, 128), rev)],
        out_specs=[pl.BlockSpec((t, 128), rev), pl.BlockSpec((2, 1, 1, ns), lambda b, i: (0, b, 0, 0)),
                   pl.BlockSpec((1, 128, ns), sb3), pl.BlockSpec((1, 128, ns), sb3),
                   pl.BlockSpec((1, ns, 128), sb3), pl.BlockSpec((1, ns, 128), sb3), pl.BlockSpec((1, 128), lambda b, i: (0, b))],
        out_shape=[jax.ShapeDtypeStruct((rows, 512), bf16), jax.ShapeDtypeStruct((2, S5_SB, 1, ns), f32),
                   jax.ShapeDtypeStruct((S5_SB, 128, ns), f32), jax.ShapeDtypeStruct((S5_SB, 128, ns), f32),
                   jax.ShapeDtypeStruct((S5_SB, ns, 128), f32), jax.ShapeDtypeStruct((S5_SB, ns, 128), f32),
                   jax.ShapeDtypeStruct((1, 512), f32)],
        scratch_shapes=[pltpu.VMEM((2, t, ns), f32), pltpu.VMEM((8, ns), f32)],
        compiler_params=_cparams(("parallel", "arbitrary")),
    )(hmain, abar, bmat_re, bmat_im, cmat_re, cmat_im, dvec, x_re, x_im, x_re, x_im, dy)


def _loss_head(y, target):
    rows, feat = y.shape
    tile = _pick(rows, 256)

    def kern(y_ref, t_ref, l_ref, dy_ref):
        e = y_ref[...] - t_ref[...]
        dy_ref[...] = e * (1.0 / feat)
        part = jnp.broadcast_to(0.5 * jnp.sum(e * e) * (1.0 / feat), l_ref.shape)

        @pl.when(pl.program_id(0) == 0)
        def _():
            l_ref[...] = part

        @pl.when(pl.program_id(0) > 0)
        def _():
            l_ref[...] += part

    return pl.pallas_call(
        kern, name="loss_head", grid=(rows // tile,),
        in_specs=[pl.BlockSpec((tile, feat), lambda i: (i, 0))] * 2,
        out_specs=[pl.BlockSpec((8, 128), lambda i: (0, 0)), pl.BlockSpec((tile, feat), lambda i: (i, 0))],
        out_shape=[jax.ShapeDtypeStruct((8, 128), f32), jax.ShapeDtypeStruct((rows, feat), f32)],
        compiler_params=_cparams(("arbitrary",)),
    )(y, target)


def _sum_parts(name, parts, scale_dtype=f32):
    rows, cols = parts[0].shape
    tile = _pick(rows, 512)

    def kern(*refs):
        acc = refs[0][...].astype(f32)
        for r in refs[1:-1]:
            acc = acc + r[...].astype(f32)
        refs[-1][...] = acc

    return pl.pallas_call(
        kern, name=name, grid=(rows // tile,),
        in_specs=[pl.BlockSpec((tile, cols), lambda i: (i, 0))] * len(parts),
        out_specs=pl.BlockSpec((tile, cols), lambda i: (i, 0)),
        out_shape=jax.ShapeDtypeStruct((rows, cols), f32),
        compiler_params=_cparams(("parallel",)),
    )(*parts)


ADAMW_BLOCK_BYTES = 1536 * 1024


def _adamw(name, w, m, v, g):
    rows, cols = w.shape
    tile = _pick(rows, 512)
    while tile > 8 and tile * cols * 4 > ADAMW_BLOCK_BYTES and rows % (tile // 2) == 0:
        tile //= 2
    c1, c2 = 1.0 / (1.0 - ADAM_B1 ** ADAM_STEP), 1.0 / (1.0 - ADAM_B2 ** ADAM_STEP)

    def kern(w_ref, m_ref, v_ref, g_ref, d_ref, nm_ref, nv_ref):
        gv = g_ref[...]
        nm = ADAM_B1 * m_ref[...] + (1.0 - ADAM_B1) * gv
        nv = ADAM_B2 * v_ref[...] + (1.0 - ADAM_B2) * (gv * gv)
        nm_ref[...] = nm
        nv_ref[...] = nv
        d_ref[...] = -ADAM_LR * ((nm * c1) / (jnp.sqrt(nv * c2) + ADAM_EPS) + ADAM_WD * w_ref[...])

    spec = pl.BlockSpec((tile, cols), lambda i: (i, 0))
    return pl.pallas_call(
        kern, name=name, grid=(rows // tile,), in_specs=[spec] * 4, out_specs=[spec] * 3,
        out_shape=[jax.ShapeDtypeStruct((rows, cols), f32)] * 3,
        compiler_params=_cparams(("parallel",)),
    )(w, m, v, g)


MESH = pl.DeviceIdType.MESH
HBM_SPEC = pl.BlockSpec(memory_space=pltpu.HBM)


def _other_chips(x, y):
    return [(1 - x, y), (x, 1 - y), (1 - x, 1 - y)]


def _gather_chips(bufs):
    n = len(bufs)

    def kern(*refs):
        in_refs, out_refs = refs[:n], refs[n:2 * n]
        send_sems, recv_sems, local_sems = refs[2 * n:]
        x, y, c = lax.axis_index("x"), lax.axis_index("y"), lax.axis_index("c")
        me = 2 * x + y
        chips = _other_chips(x, y)
        local = [pltpu.make_async_copy(in_refs[a], out_refs[a].at[me], local_sems.at[a]) for a in range(n)]
        for cp in local:
            cp.start()
        sends = []
        for a in range(n):
            for k, chip in enumerate(chips):
                sends.append(pltpu.make_async_remote_copy(
                    src_ref=in_refs[a], dst_ref=out_refs[a].at[me], send_sem=send_sems.at[a * 3 + k],
                    recv_sem=recv_sems.at[a * 3 + k], device_id=(*chip, c), device_id_type=MESH))
        for cp in sends:
            cp.start()
        for a in range(n):
            for k, chip in enumerate(chips):
                pltpu.make_async_remote_copy(
                    src_ref=in_refs[a], dst_ref=out_refs[a].at[2 * chip[0] + chip[1]], send_sem=send_sems.at[a * 3 + k],
                    recv_sem=recv_sems.at[a * 3 + k], device_id=(*chip, c), device_id_type=MESH).wait_recv()
        for cp in sends:
            cp.wait_send()
        for cp in local:
            cp.wait()

    return pl.pallas_call(
        kern, name="gather_chips", in_specs=[HBM_SPEC] * n, out_specs=[HBM_SPEC] * n,
        out_shape=[jax.ShapeDtypeStruct((N_CHIPS,) + b.shape, b.dtype) for b in bufs],
        scratch_shapes=[pltpu.SemaphoreType.DMA((3 * n,)), pltpu.SemaphoreType.DMA((3 * n,)), pltpu.SemaphoreType.DMA((n,))],
    )(*bufs)


def _scatter_chips(pieces):
    _, rows, cols = pieces.shape

    def kern(p_ref, out_ref, send_sems, recv_sems):
        x, y, c = lax.axis_index("x"), lax.axis_index("y"), lax.axis_index("c")
        chips = _other_chips(x, y)
        sends = [pltpu.make_async_remote_copy(
            src_ref=p_ref.at[2 * chip[0] + chip[1]], dst_ref=out_ref.at[k], send_sem=send_sems.at[k],
            recv_sem=recv_sems.at[k], device_id=(*chip, c), device_id_type=MESH) for k, chip in enumerate(chips)]
        for cp in sends:
            cp.start()
        for cp in sends:
            cp.wait()

    return pl.pallas_call(
        kern, name="scatter_chips", in_specs=[HBM_SPEC], out_specs=HBM_SPEC,
        out_shape=jax.ShapeDtypeStruct((3, rows, cols), pieces.dtype),
        scratch_shapes=[pltpu.SemaphoreType.DMA((3,)), pltpu.SemaphoreType.DMA((3,))],
    )(pieces)


def _swap_sibling(buf):
    def kern(b_ref, out_ref, send_sem, recv_sem):
        x, y, c = lax.axis_index("x"), lax.axis_index("y"), lax.axis_index("c")
        cp = pltpu.make_async_remote_copy(src_ref=b_ref, dst_ref=out_ref, send_sem=send_sem, recv_sem=recv_sem,
                                          device_id=(x, y, 1 - c), device_id_type=MESH)
        cp.start()
        cp.wait()

    return pl.pallas_call(
        kern, name="swap_sibling", in_specs=[HBM_SPEC], out_specs=HBM_SPEC,
        out_shape=jax.ShapeDtypeStruct(buf.shape, buf.dtype),
        scratch_shapes=[pltpu.SemaphoreType.DMA, pltpu.SemaphoreType.DMA],
    )(buf)


def _gather_all(buf):
    rows, cols = buf.shape

    def kern(b_ref, out_ref, send_sems, recv_sems, local_sem):
        x, y, c = lax.axis_index("x"), lax.axis_index("y"), lax.axis_index("c")
        me = 4 * x + 2 * y + c
        local = pltpu.make_async_copy(b_ref, out_ref.at[me], local_sem)
        local.start()
        peers = []
        for k in range(1, N_DEV):
            fx, fy, fc = (k >> 2) & 1, (k >> 1) & 1, k & 1
            peers.append((x ^ fx, y ^ fy, c ^ fc))
        sends = [pltpu.make_async_remote_copy(
            src_ref=b_ref, dst_ref=out_ref.at[me], send_sem=send_sems.at[k], recv_sem=recv_sems.at[k],
            device_id=peer, device_id_type=MESH) for k, peer in enumerate(peers)]
        for cp in sends:
            cp.start()
        for k, peer in enumerate(peers):
            pltpu.make_async_remote_copy(
                src_ref=b_ref, dst_ref=out_ref.at[4 * peer[0] + 2 * peer[1] + peer[2]], send_sem=send_sems.at[k],
                recv_sem=recv_sems.at[k], device_id=peer, device_id_type=MESH).wait_recv()
        for cp in sends:
            cp.wait_send()
        local.wait()

    return pl.pallas_call(
        kern, name="gather_all", in_specs=[HBM_SPEC], out_specs=HBM_SPEC,
        out_shape=jax.ShapeDtypeStruct((N_DEV, rows, cols), buf.dtype),
        scratch_shapes=[pltpu.SemaphoreType.DMA((N_DEV - 1,)), pltpu.SemaphoreType.DMA((N_DEV - 1,)), pltpu.SemaphoreType.DMA],
    )(buf)


def _sum_slots(name, buf):
    n, rows, cols = buf.shape
    tile = _pick(rows, 256)

    def kern(b_ref, o_ref):
        acc = b_ref[0]
        for k in range(1, n):
            acc = acc + b_ref[k]
        o_ref[...] = acc

    return pl.pallas_call(
        kern, name=name, grid=(rows // tile,),
        in_specs=[pl.BlockSpec((n, tile, cols), lambda i: (0, i, 0))], out_specs=pl.BlockSpec((tile, cols), lambda i: (i, 0)),
        out_shape=jax.ShapeDtypeStruct((rows, cols), f32), compiler_params=_cparams(("parallel",)),
    )(buf)


def _pad_heads(t):
    r = t.shape[0]
    return jnp.pad(t.reshape(r, GLA_HEADS, GLA_DK), ((0, 0), (0, 0), (0, 128 - GLA_DK))).reshape(r, GLA_HEADS * 128)


def _unpad_heads(t):
    r = t.shape[0]
    return t.reshape(r, GLA_HEADS, 128)[:, :, :GLA_DK].reshape(r, GLA_HEADS * GLA_DK)


def _blockdiag(t):
    _, r, c = t.shape
    eye = jnp.eye(8, dtype=t.dtype).reshape(1, 8, 1, 8, 1)
    return (t.reshape(S5_SB, 8, r, 1, c) * eye).reshape(S5_SB, 8 * r, 8 * c)


def _blockdiag_extract(m, r, c):
    m5 = m.reshape(S5_SB, 8, r, 8, c)
    return jnp.stack([m5[:, g, :, g, :] for g in range(8)], axis=1).reshape(S5_GROUPS, r, c)


def _row(v, width=None):
    v = v[None]
    return v if width is None else jnp.pad(v, ((0, 0), (0, width - v.shape[1])))


def _layer_operands(p):
    w_in = p['w_in']

    def seg(n):
        return w_in[:, IN_ORIG[n][0]:IN_ORIG[n][1]]

    o = dict(p)
    o['wm'] = jnp.concatenate([seg('dn_qkv'), seg('dn_gate'), seg('cf'), seg('gates'), seg('s5'), _pad_heads(seg('gla_q')),
                               _pad_heads(seg('gla_k')), seg('gla_v'), seg('gla_g')], axis=1)
    o['ws'] = jnp.pad(jnp.concatenate([seg('dn_a'), seg('dn_b'), seg('gla_lr')], axis=1), ((0, 0), (0, WS_COLS - 24)))
    o['dn_conv8'] = jnp.pad(p['dn_conv'], ((0, 4), (0, 0)))
    o['a_log_r'] = _row(p['dn_a_log'], 128)
    o['dt_bias_r'] = _row(p['dn_dt_bias'], 128)
    o['dn_norm_r'] = _row(p['dn_norm'])
    o['cf_dw32'] = jnp.pad(p['cf_dw'], ((0, 1), (0, 0)))
    o['cf_bias_r'], o['cf_g_r'], o['cf_b_r'] = _row(p['cf_dw_bias']), _row(p['cf_ln_g']), _row(p['cf_ln_b'])
    o['w_alpha_p'] = jnp.pad(_pad_heads(p['gla_w_alpha']), ((8, 128 - 24), (0, 0)))
    o['b_alpha_r'] = _pad_heads(_row(p['gla_b_alpha']))
    o['gla_norm_r'] = _row(p['gla_norm'])
    o['ln1_g_r'], o['ln1_b_r'], o['ln2_g_r'], o['ln2_b_r'] = (_row(p[n]) for n in ('ln1_g', 'ln1_b', 'ln2_g', 'ln2_b'))
    o['ffn_conv8'] = jnp.pad(p['ffn_conv'], ((0, 5), (0, 0)))
    o['s5_in'] = (p['s5_a_re'], p['s5_a_im'], p['s5_log_dt'][:, None],
                  p['s5_b_re'].transpose(2, 0, 1), p['s5_b_im'].transpose(2, 0, 1))
    abar_re, abar_im, bbar_re, bbar_im = _s5_params(*o['s5_in'])
    o['abar'] = jnp.stack([abar_re, abar_im]).reshape(2, S5_SB, 1, 512)
    o['bmat_re'], o['bmat_im'] = _blockdiag(bbar_re.transpose(1, 0, 2)), _blockdiag(bbar_im.transpose(1, 0, 2))
    o['cmat_re'], o['cmat_im'] = _blockdiag(p['s5_c_re'].transpose(0, 2, 1)), _blockdiag(p['s5_c_im'].transpose(0, 2, 1))
    o['dvec'] = _row(p['s5_d'])
    return o


def _whole(a):
    return (a, a.shape[1], 0)


def _merge_ins(h, s):
    return [(h, 1024, 3), (h, 1024, 4), (h, 1024, 5), (h, 1024, 6), (s['y_a'], 1024, 0), (s['y_b'], 1024, 0),
            (s['zz'], 1024, 0), (s['zz'], 1024, 1), (s['y_d'], 1024, 0)]


def _layer_fwd(x, o):
    s = {}
    h = s['h'] = _mm(x, o['wm'], 'nn', 'mm_h')
    hs = s['hs'] = _mm(x, o['ws'], 'nn', 'mm_hs')
    s['c1'] = _conv_fwd('conv_dn', h, 1536, 0, o['dn_conv8'], 4)
    s['qkvn'], s['gb'], s['la'] = _rowwise('pre', _f_pre, [_whole(s['c1']), _whole(hs)],
                                           [o['a_log_r'], o['dt_bias_r'], o['w_alpha_p'], o['b_alpha_r']], [1536, 128, 512])
    s['o_dn'], s['st_dn'], s['inv_dn'] = _dn_fwd(s['qkvn'], s['gb'])
    (s['on_dn'],) = _rowwise('post_dn', _f_post, [_whole(s['o_dn']), (h, 512, 3)], [o['dn_norm_r']], [512], out_dtype=bf16)
    s['y_a'] = _mm(s['on_dn'], o['w_br_dn'], 'nn', 'mm_br')
    (s['cfp'],) = _rowwise('glu_cf', _f_glu, [(h, 512, 4), (h, 512, 5)], [], [512], out_dtype=bf16)
    s['cc'] = _conv_fwd('conv_cf', s['cfp'], 512, 0, o['cf_dw32'], 31)
    (s['cfo'],) = _rowwise('post_cf', _f_cfpost, [_whole(s['cc'])], [o['cf_bias_r'], o['cf_g_r'], o['cf_b_r']], [512], out_dtype=bf16)
    s['y_b'] = _mm(s['cfo'], o['w_br_cf'], 'nn', 'mm_br')
    s['ys5'], s['xr'], s['xi'] = _s5_fwd(h, o['abar'], o['bmat_re'], o['bmat_im'], o['cmat_re'], o['cmat_im'], o['dvec'])
    (s['z'],) = _rowwise('gelu', _f_gelu, [_whole(s['ys5'])], [], [512], out_dtype=bf16)
    s['zz'] = _mm(s['z'], o['w_br_s5'], 'nn', 'mm_br_s5')
    s['o_gla'], s['st_gla'] = _gla_fwd(h, s['la'])
    (s['on_gla'],) = _rowwise('post_gla', _f_post, [_whole(s['o_gla']), (h, 512, 18)], [o['gla_norm_r']], [512], out_dtype=bf16)
    s['y_d'] = _mm(s['on_gla'], o['w_br_gla'], 'nn', 'mm_br')
    (s['merged'],) = _rowwise('merge', _f_merge, _merge_ins(h, s), [], [1024], tile=128, out_dtype=bf16)
    s['mix'] = _mm(s['merged'], o['w_o'], 'nn', 'mm_o')
    (s['x1'],) = _rowwise('ln', _f_ln, [_whole(x), _whole(s['mix'])], [o['ln1_g_r'], o['ln1_b_r']], [1024])
    s['up'] = _mm(s['x1'], o['w_up'], 'nn', 'mm_up', out_dtype=bf16)
    s['u'] = _conv_fwd('conv_ffn', s['up'], 2 * D_FF, 0, o['ffn_conv8'], 3)
    (s['act'],) = _rowwise('act', _f_act, [(s['u'], D_FF, 0), (s['u'], D_FF, 1)], [], [D_FF], out_dtype=bf16)
    s['ffn'] = _mm(s['act'], o['w_down'], 'nn', 'mm_down')
    (x2,) = _rowwise('ln', _f_ln, [_whole(s['x1']), _whole(s['ffn'])], [o['ln2_g_r'], o['ln2_b_r']], [1024])
    return x2, s


def _layer_bwd(x, o, s, dparts):
    h, g = s['h'], {}
    (dx1_a, dffn), (g['ln2_g'], g['ln2_b']) = _rowwise_bwd(
        'ln_bwd', _f_ln, [_whole(s['x1']), _whole(s['ffn'])], [o['ln2_g_r'], o['ln2_b_r']], [[_whole(d) for d in dparts]], [f32, bf16])
    dact = _mm(dffn, o['w_down'], 'nt', 'mm_down_dx')
    g['w_down'] = _mm(s['act'], dffn, 'tn', 'mm_down_dw')
    (du_a, du_b), _ = _rowwise_bwd('act_bwd', _f_act, [(s['u'], D_FF, 0), (s['u'], D_FF, 1)], [], [[_whole(dact)]], [bf16, bf16])
    dup, dw = _conv_bwd('conv_ffn_bwd', s['up'], 2 * D_FF, 0, o['ffn_conv8'], 3, jnp.concatenate([du_a, du_b], axis=1), dx_dtype=bf16)
    g['ffn_conv'] = dw[:3]
    dx1_b = _mm(dup, o['w_up'], 'nt', 'mm_up_dx')
    g['w_up'] = _mm(s['x1'], dup, 'tn', 'mm_up_dw')
    (dx_a, dmix), (g['ln1_g'], g['ln1_b']) = _rowwise_bwd(
        'ln_bwd', _f_ln, [_whole(x), _whole(s['mix'])], [o['ln1_g_r'], o['ln1_b_r']], [[_whole(dx1_a), _whole(dx1_b)]], [f32, bf16])
    dmerged = _mm(dmix, o['w_o'], 'nt', 'mm_o_dx')
    g['w_o'] = _mm(s['merged'], dmix, 'tn', 'mm_o_dw')
    (dga, dgb_, dgc, dgd, dya, dyb, dzv, dzg, dyd), _ = _rowwise_bwd(
        'merge_bwd', _f_merge, _merge_ins(h, s), [], [[_whole(dmerged)]], [bf16] * 9, tile=128)
    dzz = jnp.concatenate([dzv, dzg], axis=1)
    don = _mm(dya, o['w_br_dn'], 'nt', 'mm_br_dx')
    g['w_br_dn'] = _mm(s['on_dn'], dya, 'tn', 'mm_br_dw')
    (do_dn, dgate_dn), (g['dn_norm'],) = _rowwise_bwd(
        'post_bwd', _f_post, [_whole(s['o_dn']), (h, 512, 3)], [o['dn_norm_r']], [[_whole(don)]], [f32, bf16])
    dqkvn, dgb = _dn_bwd(s['qkvn'], s['gb'], s['st_dn'], s['inv_dn'], do_dn)
    don = _mm(dyd, o['w_br_gla'], 'nt', 'mm_br_dx')
    g['w_br_gla'] = _mm(s['on_gla'], dyd, 'tn', 'mm_br_dw')
    (do_gla, dgate_gla), (g['gla_norm'],) = _rowwise_bwd(
        'post_bwd', _f_post, [_whole(s['o_gla']), (h, 512, 18)], [o['gla_norm_r']], [[_whole(don)]], [f32, bf16])
    dq_gla, dk_gla, dv_gla, dla = _gla_bwd(h, s['la'], s['st_gla'], do_gla)
    (dc1, dhs), (d_alog, d_dtb, d_walpha, d_balpha) = _rowwise_bwd(
        'pre_bwd', _f_pre, [_whole(s['c1']), _whole(s['hs'])], [o['a_log_r'], o['dt_bias_r'], o['w_alpha_p'], o['b_alpha_r']],
        [[_whole(dqkvn)], [_whole(dgb)], [_whole(dla)]], [bf16, bf16])
    g['dn_a_log'], g['dn_dt_bias'] = d_alog[0, :DN_HEADS], d_dtb[0, :DN_HEADS]
    g['gla_w_alpha'], g['gla_b_alpha'] = _unpad_heads(d_walpha[8:24]), _unpad_heads(d_balpha)[0]
    d_dnqkv, dw = _conv_bwd('conv_dn_bwd', h, 1536, 0, o['dn_conv8'], 4, dc1, dx_dtype=bf16)
    g['dn_conv'] = dw[:4]
    dcfo = _mm(dyb, o['w_br_cf'], 'nt', 'mm_br_dx')
    g['w_br_cf'] = _mm(s['cfo'], dyb, 'tn', 'mm_br_dw')
    (dcc,), (g['cf_dw_bias'], g['cf_ln_g'], g['cf_ln_b']) = _rowwise_bwd(
        'post_cf_bwd', _f_cfpost, [_whole(s['cc'])], [o['cf_bias_r'], o['cf_g_r'], o['cf_b_r']], [[_whole(dcfo)]], [bf16])
    dcfp, dw = _conv_bwd('conv_cf_bwd', s['cfp'], 512, 0, o['cf_dw32'], 31, dcc)
    g['cf_dw'] = dw[:31]
    (dcf_a, dcf_g), _ = _rowwise_bwd('glu_bwd', _f_glu, [(h, 512, 4), (h, 512, 5)], [], [[_whole(dcfp)]], [bf16, bf16])
    dz = _mm(dzz, o['w_br_s5'], 'nt', 'mm_br_s5_dx')
    g['w_br_s5'] = _mm(s['z'], dzz, 'tn', 'mm_br_s5_dw')
    (dys5,), _ = _rowwise_bwd('gelu_bwd', _f_gelu, [_whole(s['ys5'])], [], [[_whole(dz)]], [f32])
    du_s5, d_abar, dbm_re, dbm_im, dcm_re, dcm_im, d_dvec = _s5_bwd(
        h, o['abar'], o['bmat_re'], o['bmat_im'], o['cmat_re'], o['cmat_im'], o['dvec'], s['xr'], s['xi'], dys5)
    d_bbar = [_blockdiag_extract(m, S5_GROUP, S5_STATE).transpose(1, 0, 2) for m in (dbm_re, dbm_im)]
    da_re, da_im, dlog_dt, db_re, db_im = _s5_params_bwd(
        *o['s5_in'], d_abar[0].reshape(S5_GROUPS, S5_STATE), d_abar[1].reshape(S5_GROUPS, S5_STATE), *d_bbar)
    g['s5_a_re'], g['s5_a_im'], g['s5_log_dt'] = da_re, da_im, dlog_dt[:, 0]
    g['s5_b_re'], g['s5_b_im'] = db_re.transpose(1, 2, 0), db_im.transpose(1, 2, 0)
    g['s5_c_re'], g['s5_c_im'] = (_blockdiag_extract(m, S5_STATE, S5_GROUP).transpose(0, 2, 1) for m in (dcm_re, dcm_im))
    g['s5_d'] = d_dvec[0]
    for n in ('dn_norm', 'gla_norm', 'cf_dw_bias', 'cf_ln_g', 'cf_ln_b', 'ln1_g', 'ln1_b', 'ln2_g', 'ln2_b'):
        g[n] = g[n][0]
    dh = jnp.concatenate([d_dnqkv, dgate_dn, dcf_a, dcf_g, dga, dgb_, dgc, dgd, du_s5, dq_gla, dk_gla, dv_gla, dgate_gla], axis=1)
    dwm = _mm(x, dh, 'tn', 'mm_h_dw')
    dws = _mm(x, dhs, 'tn', 'mm_hs_dw')
    g['w_in'] = jnp.concatenate([
        dwm[:, 0:1536], dws[:, 0:8], dwm[:, 1536:2048], dwm[:, 2048:3072], dwm[:, 7168:7680], _unpad_heads(dwm[:, 7680:8192]),
        _unpad_heads(dwm[:, 8192:8704]), dwm[:, 8704:9216], dwm[:, 9216:9728], dws[:, 8:24], dwm[:, 3072:7168]], axis=1)
    return [dx_a, _mm(dh, o['wm'], 'nt', 'mm_h_dx'), _mm(dhs, o['ws'], 'nt', 'mm_hs_dx')], g


WEIGHTS = ('w_in', 'dn_conv', 'dn_a_log', 'dn_dt_bias', 'dn_norm', 'w_br_dn', 'cf_dw', 'cf_dw_bias', 'cf_ln_g', 'cf_ln_b',
           'w_br_cf', 's5_a_re', 's5_a_im', 's5_log_dt', 's5_b_re', 's5_b_im', 's5_c_re', 's5_c_im', 's5_d', 'w_br_s5',
           'gla_w_alpha', 'gla_b_alpha', 'gla_norm', 'w_br_gla', 'w_o', 'ln1_g', 'ln1_b', 'w_up', 'ffn_conv', 'w_down',
           'ln2_g', 'ln2_b')
LARGE = ('w_in', 'w_br_dn', 'w_br_cf', 'w_br_s5', 'w_br_gla', 'w_o', 'w_up', 'w_down')
SHARD_AXIS = dict(w_in=2, w_br_dn=2, w_br_cf=2, w_br_s5=2, w_br_gla=2, w_o=1, w_up=2, w_down=1,
                  dn_conv=2, cf_dw=2, gla_w_alpha=2, ffn_conv=2)
SMALL = tuple(n for n in WEIGHTS if n not in LARGE)
SMALL_SHARDED = tuple(n for n in SMALL if n in SHARD_AXIS)


def _local_step(x, target, full):
    ops, saved, xs = [], [], [x]
    for l in range(DEPTH):
        o = _layer_operands({n: full[n][l] for n in WEIGHTS})
        y, s = _layer_fwd(xs[-1], o)
        ops.append(o)
        saved.append(s)
        xs.append(y)
    loss, dy = _loss_head(xs[-1], target)
    dparts, grads = [dy], [None] * DEPTH
    for l in reversed(range(DEPTH)):
        dparts, grads[l] = _layer_bwd(xs[l], ops[l], saved[l], dparts)
    grad_x = _sum_parts('sum_dx', dparts)
    return loss[0, 0], grad_x, {n: jnp.stack([grads[l][n] for l in range(DEPTH)]) for n in WEIGHTS}


def _pack(arrs, rows, dtype=f32):
    flat = jnp.concatenate([a.reshape(-1).astype(dtype) for a in arrs])
    return jnp.pad(flat, (0, rows * 1024 - flat.shape[0])).reshape(rows, 1024)


def _unpack(buf, shapes):
    flat, out, pos = buf.reshape(-1), [], 0
    for shp in shapes:
        n = 1
        for d in shp:
            n *= d
        out.append(flat[pos:pos + n].reshape(shp))
        pos += n
    return out


def _rows_for(shapes, mult):
    n = 0
    for shp in shapes:
        k = 1
        for d in shp:
            k *= d
        n += k
    rows = -(-n // 1024)
    return -(-rows // mult) * mult


def _shard(a, axis, chip):
    size = a.shape[axis] // N_CHIPS
    return lax.dynamic_slice_in_dim(a, chip * size, size, axis)


def kernel(x, w_in, dn_conv, dn_a_log, dn_dt_bias, dn_norm, w_br_dn, cf_dw, cf_dw_bias, cf_ln_g, cf_ln_b, w_br_cf, s5_a_re, s5_a_im, s5_log_dt, s5_b_re, s5_b_im, s5_c_re, s5_c_im, s5_d, w_br_s5, gla_w_alpha, gla_b_alpha, gla_norm, w_br_gla, w_o, ln1_g, ln1_b, w_up, ffn_conv, w_down, ln2_g, ln2_b, loss_target, m_w_in, m_dn_conv, m_dn_a_log, m_dn_dt_bias, m_dn_norm, m_w_br_dn, m_cf_dw, m_cf_dw_bias, m_cf_ln_g, m_cf_ln_b, m_w_br_cf, m_s5_a_re, m_s5_a_im, m_s5_log_dt, m_s5_b_re, m_s5_b_im, m_s5_c_re, m_s5_c_im, m_s5_d, m_w_br_s5, m_gla_w_alpha, m_gla_b_alpha, m_gla_norm, m_w_br_gla, m_w_o, m_ln1_g, m_ln1_b, m_w_up, m_ffn_conv, m_w_down, m_ln2_g, m_ln2_b, v_w_in, v_dn_conv, v_dn_a_log, v_dn_dt_bias, v_dn_norm, v_w_br_dn, v_cf_dw, v_cf_dw_bias, v_cf_ln_g, v_cf_ln_b, v_w_br_cf, v_s5_a_re, v_s5_a_im, v_s5_log_dt, v_s5_b_re, v_s5_b_im, v_s5_c_re, v_s5_c_im, v_s5_d, v_w_br_s5, v_gla_w_alpha, v_gla_b_alpha, v_gla_norm, v_w_br_gla, v_w_o, v_ln1_g, v_ln1_b, v_w_up, v_ffn_conv, v_w_down, v_ln2_g, v_ln2_b):
    env = locals()
    w = {n: env[n] for n in WEIGHTS}
    m = {n: env['m_' + n] for n in WEIGHTS}
    v = {n: env['v_' + n] for n in WEIGHTS}
    chip = 2 * lax.axis_index("x") + lax.axis_index("y")

    large_shapes = [w[n].shape for n in LARGE]
    ssh_shapes = [w[n].shape for n in SMALL_SHARDED]
    large_rows, ssh_rows = _rows_for(large_shapes, 512), _rows_for(ssh_shapes, 8)
    got_large, got_ssh = _gather_chips([_pack([w[n] for n in LARGE], large_rows, bf16),
                                        _pack([w[n] for n in SMALL_SHARDED], ssh_rows)])
    full = {n: w[n] for n in SMALL if n not in SHARD_AXIS}
    per_chip = [dict(zip(LARGE + SMALL_SHARDED, _unpack(got_large[k], large_shapes) + _unpack(got_ssh[k], ssh_shapes)))
                for k in range(N_CHIPS)]
    for n in LARGE + SMALL_SHARDED:
        full[n] = jnp.concatenate([per_chip[k][n] for k in range(N_CHIPS)], axis=SHARD_AXIS[n])

    loss, grad_x, g = _local_step(x[0], loss_target[0], full)
    loss = lax.psum(loss, ("x", "y", "c"))

    def piece(k, dtype):
        size = lambda n: g[n].shape[SHARD_AXIS[n]] // N_CHIPS
        return _pack([lax.slice_in_dim(g[n], k * size(n), (k + 1) * size(n), axis=SHARD_AXIS[n]) for n in LARGE], large_rows, dtype)

    from_chips = _scatter_chips(jnp.stack([piece(k, bf16) for k in range(N_CHIPS)]))
    mine = _pack([_shard(g[n], SHARD_AXIS[n], chip) for n in LARGE], large_rows)
    core_sum = _sum_parts('sum_chips', [mine, from_chips[0], from_chips[1], from_chips[2]])
    chip_sum = _sum_parts('sum_cores', [core_sum, _swap_sibling(core_sum)])
    res = {0: dict(zip(LARGE, _unpack(chip_sum, large_shapes))), 1: {}, 2: {}, 3: {}}
    for n in LARGE:
        two_d = (w[n].shape[0] * w[n].shape[1], w[n].shape[2])
        upd = _adamw('adamw_large', w[n].reshape(two_d), m[n].reshape(two_d), v[n].reshape(two_d), res[0][n].reshape(two_d))
        for kind in range(3):
            res[kind + 1][n] = upd[kind].reshape(w[n].shape)

    small_full_shapes = [g[n].shape for n in SMALL]
    small_rows = _rows_for(small_full_shapes, 8)
    small_sum = _sum_slots('sum_devices', _gather_all(_pack([g[n] for n in SMALL], small_rows)))
    gs = dict(zip(SMALL, _unpack(small_sum, small_full_shapes)))
    for n in SMALL_SHARDED:
        gs[n] = _shard(gs[n], SHARD_AXIS[n], chip)
    small_shapes = [w[n].shape for n in SMALL]
    upd_rows = _rows_for(small_shapes, 8)
    upd = _adamw('adamw_small', _pack([w[n] for n in SMALL], upd_rows), _pack([m[n] for n in SMALL], upd_rows),
                 _pack([v[n] for n in SMALL], upd_rows), _pack([gs[n] for n in SMALL], upd_rows))
    res[0].update(gs)
    for kind in range(3):
        res[kind + 1].update(zip(SMALL, _unpack(upd[kind], small_shapes)))
    return (loss, grad_x[None], *[res[kind][n] for kind in range(4) for n in WEIGHTS])
```

```python
import functools

import jax
import jax.numpy as jnp
from jax import lax
from jax.experimental import pallas as pl
from jax.experimental.pallas import tpu as pltpu

f32 = jnp.float32
bf16 = jnp.bfloat16
HI = lax.Precision.HIGHEST

D_MODEL = 1024
DEPTH = 4
DN_HEADS, DN_DK, DN_CHUNK = 4, 128, 64
GLA_HEADS, GLA_DK, GLA_CHUNK, GLA_TAU = 4, 64, 16, 16.0
GLA_BLOCK = 128
S5_GROUPS, S5_GROUP, S5_STATE = 32, 16, 64
S5_SB = 4
S5_TILE = 256
D_FF = 2816
LN_EPS = 1e-5
ALPHA = (2.0 * DEPTH) ** 0.25
ADAM_LR, ADAM_B1, ADAM_B2, ADAM_EPS, ADAM_WD, ADAM_STEP = 0.001, 0.9, 0.999, 1e-08, 0.01, 10

VMEM_LIMIT_V7X = 56 * 1024 * 1024
MM_A_BLOCK_BYTES = 8 * 1024 * 1024
HALO = 32
N_CHIPS = 4
N_DEV = 8

IN_ORIG = dict(dn_qkv=(0, 1536), dn_a=(1536, 1540), dn_b=(1540, 1544), dn_gate=(1544, 2056), cf=(2056, 3080),
               s5=(3080, 3592), gla_q=(3592, 3848), gla_k=(3848, 4104), gla_v=(4104, 4616), gla_g=(4616, 5128),
               gla_lr=(5128, 5144), gates=(5144, 9240))
IN_COLS = 9240
WM_COLS = 9728
WS_COLS = 128


def _cparams(sem):
    return pltpu.CompilerParams(dimension_semantics=sem, vmem_limit_bytes=VMEM_LIMIT_V7X)


def _pick(dim, pref):
    for t in (pref, 512, 256, 128, 64, 32, 16, 8):
        if t <= pref and dim % t == 0:
            return t
    return dim


NN = (((1,), (0,)), ((), ()))
NT = (((1,), (1,)), ((), ()))
TN = (((0,), (0,)), ((), ()))


def _dot(a, b, dims=NN):
    return lax.dot_general(a, b, dims, precision=HI, preferred_element_type=f32)


def _round(a):
    return a.astype(bf16).astype(f32)


def _rdot(a, b, dims=NN):
    return lax.dot_general(a.astype(bf16), b.astype(bf16), dims, preferred_element_type=f32)


@functools.partial(jax.custom_vjp, nondiff_argnums=(2,))
def _bdot(a, b, dims=NN):
    return _rdot(a, b, dims)


def _bdot_fwd(a, b, dims):
    return _rdot(a, b, dims), (a, b)


def _bdot_bwd(dims, res, g):
    a, b = res
    if dims == NN:
        return _rdot(g, b, NT), _rdot(a, g, TN)
    if dims == NT:
        return _rdot(g, b, NN), _rdot(g, a, TN)
    assert dims == TN
    return _rdot(b, g, NT), _rdot(a, g, NN)


_bdot.defvjp(_bdot_fwd, _bdot_bwd)


def _iota(shape, axis):
    return lax.broadcasted_iota(jnp.int32, shape, axis)


def _mm(a, b, mode, name, tn=512, out_dtype=f32):
    if mode == 'nn':
        (m, k), n = a.shape, b.shape[1]
    elif mode == 'nt':
        (m, k), n = a.shape, b.shape[0]
    else:
        (k, m), n = a.shape, b.shape[1]
    tm, tn = _pick(m, 512 if mode == 'tn' else 1024), _pick(n, tn)
    nk = 1
    while (k // nk) * tm * a.dtype.itemsize > MM_A_BLOCK_BYTES or k % nk or (k // nk) % 128:
        nk += 1
    tk = k // nk
    assert nk == 1 or out_dtype == f32
    if mode == 'nn':
        dims = NN
        a_spec = pl.BlockSpec((tm, tk), lambda i, j, kk: (i, kk))
        b_spec = pl.BlockSpec((tk, tn), lambda i, j, kk: (kk, j))
    elif mode == 'nt':
        dims = NT
        a_spec = pl.BlockSpec((tm, tk), lambda i, j, kk: (i, kk))
        b_spec = pl.BlockSpec((tn, tk), lambda i, j, kk: (j, kk))
    else:
        dims = TN
        a_spec = pl.BlockSpec((tk, tm), lambda i, j, kk: (kk, i))
        b_spec = pl.BlockSpec((tk, tn), lambda i, j, kk: (kk, j))

    def kern(a_ref, b_ref, o_ref):
        p = lax.dot_general(a_ref[...].astype(bf16), b_ref[...].astype(bf16), dims, preferred_element_type=f32)
        if nk == 1:
            o_ref[...] = p.astype(o_ref.dtype)
        else:
            kk = pl.program_id(2)

            @pl.when(kk == 0)
            def _():
                o_ref[...] = p

            @pl.when(kk > 0)
            def _():
                o_ref[...] += p

    return pl.pallas_call(
        kern, name=name, grid=(m // tm, n // tn, nk),
        in_specs=[a_spec, b_spec], out_specs=pl.BlockSpec((tm, tn), lambda i, j, kk: (i, j)),
        out_shape=jax.ShapeDtypeStruct((m, n), out_dtype),
        compiler_params=_cparams(("parallel", "parallel", "arbitrary")),
    )(a, b)


def _full_spec(p):
    nd = p.ndim
    return pl.BlockSpec(p.shape, lambda *_, nd=nd: (0,) * nd)


def _rowwise(name, f, ins, params, out_widths, tile=256, out_dtype=f32):
    rows = ins[0][0].shape[0]
    tile = _pick(rows, tile)
    n_x = len(ins) + len(params)

    def kern(*refs):
        for o_ref, r in zip(refs[n_x:], f(*[r[...] for r in refs[:n_x]])):
            o_ref[...] = r.astype(o_ref.dtype)

    in_specs = [pl.BlockSpec((tile, w), lambda i, c=c: (i, c)) for (_, w, c) in ins] + [_full_spec(p) for p in params]
    return pl.pallas_call(
        kern, name=name, grid=(rows // tile,), in_specs=in_specs,
        out_specs=[pl.BlockSpec((tile, w), lambda i: (i, 0)) for w in out_widths],
        out_shape=[jax.ShapeDtypeStruct((rows, w), out_dtype) for w in out_widths],
        compiler_params=_cparams(("parallel",)),
    )(*[a for (a, _, _) in ins], *params)


def _rowwise_bwd(name, f, ins, params, douts, want, tile=256):
    rows = ins[0][0].shape[0]
    tile = _pick(rows, tile)
    n_in, n_p = len(ins), len(params)
    parts = [p for d in douts for p in d]
    n_x, n_d = n_in + n_p, len(parts)

    def kern(*refs):
        xs = [r[...] for r in refs[:n_x]]
        d_refs, o_refs = refs[n_x:n_x + n_d], refs[n_x + n_d:]
        cts, pos = [], 0
        for d in douts:
            acc = d_refs[pos][...]
            for r in d_refs[pos + 1:pos + len(d)]:
                acc = acc + r[...]
            pos += len(d)
            cts.append(acc)
        grads = jax.vjp(f, *xs)[1](tuple(cts))
        k = 0
        for j in range(n_in):
            if want[j]:
                o_refs[k][...] = grads[j].astype(o_refs[k].dtype)
                k += 1
        first = pl.program_id(0) == 0
        for j in range(n_p):
            g, o_ref = grads[n_in + j], o_refs[k + j]

            @pl.when(first)
            def _(o_ref=o_ref, g=g):
                o_ref[...] = g

            @pl.when(jnp.logical_not(first))
            def _(o_ref=o_ref, g=g):
                o_ref[...] += g

    in_specs = ([pl.BlockSpec((tile, w), lambda i, c=c: (i, c)) for (_, w, c) in ins] + [_full_spec(p) for p in params]
                + [pl.BlockSpec((tile, w), lambda i, c=c: (i, c)) for (_, w, c) in parts])
    out_specs, out_shape = [], []
    for j in range(n_in):
        if want[j]:
            out_specs.append(pl.BlockSpec((tile, ins[j][1]), lambda i: (i, 0)))
            out_shape.append(jax.ShapeDtypeStruct((rows, ins[j][1]), want[j]))
    n_g = len(out_specs)
    for p in params:
        out_specs.append(_full_spec(p))
        out_shape.append(jax.ShapeDtypeStruct(p.shape, f32))
    res = pl.pallas_call(
        kern, name=name, grid=(rows // tile,), in_specs=in_specs, out_specs=out_specs, out_shape=out_shape,
        compiler_params=_cparams(("arbitrary",)),
    )(*[a for (a, _, _) in ins], *params, *[a for (a, _, _) in parts])
    return list(res[:n_g]), list(res[n_g:])


_sigmoid = jax.nn.sigmoid
_silu = jax.nn.silu
_softplus = jax.nn.softplus
_log_sigmoid = jax.nn.log_sigmoid


def _f_ln(x, r, g, b):
    t = ALPHA * x + r
    mu = jnp.mean(t, -1, keepdims=True)
    var = jnp.mean(jnp.square(t - mu), -1, keepdims=True)
    return ((t - mu) * lax.rsqrt(var + LN_EPS) * g + b,)


def _f_pre(c1, hs, a_log, dt_bias, w_alpha, b_alpha):
    s = _silu(c1)
    outs = []
    for j in range(3 * DN_HEADS):
        t = s[:, j * DN_DK:(j + 1) * DN_DK]
        if j < 2 * DN_HEADS:
            t = t * lax.rsqrt(jnp.sum(t * t, -1, keepdims=True) + 1e-6)
        if j < DN_HEADS:
            t = t * (DN_DK ** -0.5)
        outs.append(t)
    qkvn = jnp.concatenate(outs, axis=1)
    lane = _iota(hs.shape, 1)
    g = -jnp.exp(a_log) * _softplus(hs + dt_bias)
    beta = _sigmoid(hs)
    gb = jnp.where(lane < DN_HEADS, g, jnp.where(lane < 2 * DN_HEADS, beta, 0.0))
    la = _log_sigmoid(_bdot(hs, w_alpha) + b_alpha) * (1.0 / GLA_TAU)
    lane5 = _iota(la.shape, 1)
    la = jnp.where((lane5 % 128) < GLA_DK, la, 0.0)
    return qkvn, gb, la


def _f_post(o, gate, w):
    outs = []
    for j in range(4):
        t = o[:, j * 128:(j + 1) * 128]
        outs.append(t * lax.rsqrt(jnp.mean(t * t, -1, keepdims=True) + LN_EPS) * w)
    return (jnp.concatenate(outs, axis=1) * _silu(gate),)


def _f_glu(a, g):
    return (a * _sigmoid(g),)


def _f_cfpost(c, bias, g, b):
    t = c + bias
    mu = jnp.mean(t, -1, keepdims=True)
    var = jnp.mean(jnp.square(t - mu), -1, keepdims=True)
    return (_silu((t - mu) * lax.rsqrt(var + LN_EPS) * g + b),)


def _f_gelu(y):
    return (jax.nn.gelu(y),)


def _f_merge(ga, gb_, gc, gd, ya, yb, zv, zg, yd):
    return (_sigmoid(ga) * ya + _sigmoid(gb_) * yb + _sigmoid(gc) * (zv * _sigmoid(zg)) + _sigmoid(gd) * yd,)


def _f_act(a, b):
    return (_silu(a) * b,)


def _conv_tiles(rows, ch):
    return _pick(rows, 256), _pick(ch, 512)


def _conv_fwd(name, x, width, colblk0, w, taps):
    rows = x.shape[0]
    tr, cb = _conv_tiles(rows, width)
    hb = tr // HALO

    def kern(prev_ref, x_ref, w_ref, o_ref, ext):
        i = pl.program_id(1)
        ext[pl.ds(0, HALO), :] = jnp.where(i > 0, _round(prev_ref[...]), 0.0)
        ext[pl.ds(HALO, tr), :] = _round(x_ref[...])
        wv = _round(w_ref[...])
        acc = jnp.zeros((tr, cb), f32)
        for k in range(taps):
            acc = acc + wv[k:k + 1, :] * ext[pl.ds(HALO - taps + 1 + k, tr), :]
        o_ref[...] = acc

    c0 = colblk0 * (width // cb)
    return pl.pallas_call(
        kern, name=name, grid=(width // cb, rows // tr),
        in_specs=[pl.BlockSpec((HALO, cb), lambda c, i: (jnp.maximum(i * hb - 1, 0), c0 + c)),
                  pl.BlockSpec((tr, cb), lambda c, i: (i, c0 + c)),
                  pl.BlockSpec((w.shape[0], cb), lambda c, i: (0, c))],
        out_specs=pl.BlockSpec((tr, cb), lambda c, i: (i, c)),
        out_shape=jax.ShapeDtypeStruct((rows, width), f32),
        scratch_shapes=[pltpu.VMEM((HALO + tr, cb), f32)],
        compiler_params=_cparams(("parallel", "arbitrary")),
    )(x, x, w)


def _conv_bwd(name, x, width, colblk0, w, taps, dy, dx_dtype=f32):
    rows = x.shape[0]
    tr, cb = _conv_tiles(rows, width)
    hb = tr // HALO
    nt = rows // tr
    wr = w.shape[0]

    def kern(prev_ref, x_ref, w_ref, dy_ref, next_ref, dx_ref, dw_ref, ext, dext):
        i = pl.program_id(1)
        ext[pl.ds(0, HALO), :] = jnp.where(i > 0, _round(prev_ref[...]), 0.0)
        ext[pl.ds(HALO, tr), :] = _round(x_ref[...])
        dyv = _round(dy_ref[...])
        dext[pl.ds(0, tr), :] = dyv
        dext[pl.ds(tr, HALO), :] = jnp.where(i < nt - 1, _round(next_ref[...]), 0.0)
        wv = _round(w_ref[...])
        acc = jnp.zeros((tr, cb), f32)
        rows_w = []
        for k in range(taps):
            acc = acc + wv[k:k + 1, :] * dext[pl.ds(taps - 1 - k, tr), :]
            rows_w.append(jnp.sum(dyv * ext[pl.ds(HALO - taps + 1 + k, tr), :], axis=0, keepdims=True))
        dx_ref[...] = acc.astype(dx_ref.dtype)
        if wr > taps:
            rows_w.append(jnp.zeros((wr - taps, cb), f32))
        dwv = jnp.concatenate(rows_w, axis=0)

        @pl.when(i == 0)
        def _():
            dw_ref[...] = dwv

        @pl.when(i > 0)
        def _():
            dw_ref[...] += dwv

    c0 = colblk0 * (width // cb)
    return pl.pallas_call(
        kern, name=name, grid=(width // cb, nt),
        in_specs=[pl.BlockSpec((HALO, cb), lambda c, i: (jnp.maximum(i * hb - 1, 0), c0 + c)),
                  pl.BlockSpec((tr, cb), lambda c, i: (i, c0 + c)),
                  pl.BlockSpec((wr, cb), lambda c, i: (0, c)),
                  pl.BlockSpec((tr, cb), lambda c, i: (i, c)),
                  pl.BlockSpec((HALO, cb), lambda c, i: (jnp.minimum((i + 1) * hb, nt * hb - 1), c))],
        out_specs=[pl.BlockSpec((tr, cb), lambda c, i: (i, c)), pl.BlockSpec((wr, cb), lambda c, i: (0, c))],
        out_shape=[jax.ShapeDtypeStruct((rows, width), dx_dtype), jax.ShapeDtypeStruct((wr, width), f32)],
        scratch_shapes=[pltpu.VMEM((HALO + tr, cb), f32), pltpu.VMEM((HALO + tr, cb), f32)],
        compiler_params=_cparams(("parallel", "arbitrary")),
    )(x, x, w, dy, dy)


def _series_inverse(neg):
    n = neg.shape[0]
    inv = jnp.where(_iota((n, n), 0) == _iota((n, n), 1), 1.0, 0.0) + neg
    p = neg
    for _ in range(5):
        p = _dot(p, p)
        inv = inv + _dot(inv, p)
    return inv


def _inverse_bwd(inv, g):
    return _dot(_dot(inv, g, TN), inv, NT)


@jax.custom_vjp
def _unit_lower_inverse(neg):
    return _series_inverse(neg)


_unit_lower_inverse.defvjp(lambda neg: (_series_inverse(neg),) * 2, lambda inv, g: (_inverse_bwd(inv, g),))


@jax.custom_vjp
def _known_inverse(neg, inv):
    return inv


_known_inverse.defvjp(lambda neg, inv: (inv, inv), lambda inv, g: (_inverse_bwd(inv, g), jnp.zeros_like(inv)))


def _dn_head(state, q, k, v, gc, gl, beta, inv_saved=None):
    c = DN_CHUNK
    ii, jj = _iota((c, c), 0), _iota((c, c), 1)
    causal, strict = ii >= jj, ii > jj
    gcb = jnp.broadcast_to(gc, (c, c))
    decay = jnp.where(causal, jnp.exp(jnp.where(causal, gcb - gcb.T, 0.0)), 0.0)
    kb = k * beta
    neg = -jnp.where(strict, _bdot(kb, k, NT) * decay, 0.0)
    inv = _unit_lower_inverse(neg) if inv_saved is None else _known_inverse(neg, inv_saved)
    egc = jnp.exp(gc)
    u = _dot(inv, v * beta)
    w = _dot(inv, kb * egc)
    intra = _bdot(q, k, NT) * decay
    v_new = u - _bdot(w, state)
    o = _bdot(q * egc, state) + _bdot(intra, v_new)
    new_state = state * jnp.exp(gl) + _bdot(k * jnp.exp(gl - gc), v_new, TN)
    return (o, new_state, inv) if inv_saved is None else (o, new_state)


def _dn_cum(gb):
    c = DN_CHUNK
    tril = jnp.where(_iota((c, c), 0) >= _iota((c, c), 1), 1.0, 0.0)
    return _dot(tril, gb), jnp.sum(gb, axis=0, keepdims=True)


def _dn_fwd(qkvn, gb):
    rows = qkvn.shape[0]
    c, h, d = DN_CHUNK, DN_HEADS, DN_DK
    nc = rows // c

    def kern(qkv_ref, gb_ref, o_ref, st_ref, inv_ref, state):
        @pl.when(pl.program_id(0) == 0)
        def _():
            state[...] = jnp.zeros_like(state)

        gbv = gb_ref[...]
        cum, tot = _dn_cum(gbv)
        for j in range(h):
            st = state[j]
            st_ref[0, j] = st
            o, new, inv = _dn_head(st, qkv_ref[:, j * d:(j + 1) * d], qkv_ref[:, (h + j) * d:(h + j + 1) * d],
                                   qkv_ref[:, (2 * h + j) * d:(2 * h + j + 1) * d],
                                   cum[:, j:j + 1], tot[:, j:j + 1], gbv[:, h + j:h + j + 1])
            o_ref[:, j * d:(j + 1) * d] = o
            inv_ref[0, j] = inv
            state[j] = new

    return pl.pallas_call(
        kern, name="dn_fwd", grid=(nc,),
        in_specs=[pl.BlockSpec((c, 3 * h * d), lambda i: (i, 0)), pl.BlockSpec((c, 128), lambda i: (i, 0))],
        out_specs=[pl.BlockSpec((c, h * d), lambda i: (i, 0)), pl.BlockSpec((1, h, d, d), lambda i: (i, 0, 0, 0)),
                   pl.BlockSpec((1, h, c, c), lambda i: (i, 0, 0, 0))],
        out_shape=[jax.ShapeDtypeStruct((rows, h * d), f32), jax.ShapeDtypeStruct((nc, h, d, d), f32),
                   jax.ShapeDtypeStruct((nc, h, c, c), f32)],
        scratch_shapes=[pltpu.VMEM((h, d, d), f32)],
        compiler_params=_cparams(("arbitrary",)),
    )(qkvn, gb)


def _dn_bwd(qkvn, gb, states, invs, do):
    rows = qkvn.shape[0]
    c, h, d = DN_CHUNK, DN_HEADS, DN_DK
    nc = rows // c

    def kern(qkv_ref, gb_ref, st_ref, inv_ref, do_ref, dqkv_ref, dgb_ref, dstate):
        @pl.when(pl.program_id(0) == 0)
        def _():
            dstate[...] = jnp.zeros_like(dstate)

        gbv = gb_ref[...]
        cum, tot = _dn_cum(gbv)
        lane = _iota((c, 128), 1)
        dcum = jnp.zeros((c, 128), f32)
        dgb = jnp.zeros((c, 128), f32)
        for j in range(h):
            args = (st_ref[0, j], qkv_ref[:, j * d:(j + 1) * d], qkv_ref[:, (h + j) * d:(h + j + 1) * d],
                    qkv_ref[:, (2 * h + j) * d:(2 * h + j + 1) * d],
                    cum[:, j:j + 1], tot[:, j:j + 1], gbv[:, h + j:h + j + 1])
            head = functools.partial(_dn_head, inv_saved=inv_ref[0, j])
            ds, dq, dk, dv, dgc, dgl, dbeta = jax.vjp(head, *args)[1]((do_ref[:, j * d:(j + 1) * d], dstate[j]))
            dstate[j] = ds
            dqkv_ref[:, j * d:(j + 1) * d] = dq
            dqkv_ref[:, (h + j) * d:(h + j + 1) * d] = dk
            dqkv_ref[:, (2 * h + j) * d:(2 * h + j + 1) * d] = dv
            dcum = dcum + jnp.where(lane == j, dgc, 0.0)
            dgb = dgb + jnp.where(lane == j, dgl, 0.0) + jnp.where(lane == h + j, dbeta, 0.0)
        triu = jnp.where(_iota((c, c), 0) <= _iota((c, c), 1), 1.0, 0.0)
        dgb_ref[...] = dgb + _dot(triu, dcum)

    rev = lambda i: (nc - 1 - i, 0)
    return pl.pallas_call(
        kern, name="dn_bwd", grid=(nc,),
        in_specs=[pl.BlockSpec((c, 3 * h * d), rev), pl.BlockSpec((c, 128), rev),
                  pl.BlockSpec((1, h, d, d), lambda i: (nc - 1 - i, 0, 0, 0)),
                  pl.BlockSpec((1, h, c, c), lambda i: (nc - 1 - i, 0, 0, 0)), pl.BlockSpec((c, h * d), rev)],
        out_specs=[pl.BlockSpec((c, 3 * h * d), rev), pl.BlockSpec((c, 128), rev)],
        out_shape=[jax.ShapeDtypeStruct((rows, 3 * h * d), f32), jax.ShapeDtypeStruct((rows, 128), f32)],
        scratch_shapes=[pltpu.VMEM((h, d, d), f32)],
        compiler_params=_cparams(("arbitrary",)),
    )(qkvn, gb, states, invs, do)


def _gla_block(state_t, q, k, v, la):
    c = GLA_CHUNK
    ii, jj = _iota((c, c), 0), _iota((c, c), 1)
    causal = ii >= jj
    tril = jnp.where(causal, 1.0, 0.0)
    outs = []
    for n in range(q.shape[0] // c):
        sl = slice(n * c, (n + 1) * c)
        qn, kn, vn, ln = q[sl] * (GLA_DK ** -0.5), k[sl], v[sl], la[sl]
        gc = _dot(tril, ln)
        gl = jnp.sum(ln, axis=0, keepdims=True)
        q_dec = qn * jnp.exp(gc)
        scores = jnp.where(causal, _bdot(q_dec, kn * jnp.exp(-gc), NT), 0.0)
        outs.append(_bdot(q_dec, state_t, NT) + _bdot(scores, vn))
        state_t = state_t * jnp.exp(gl) + _bdot(vn, kn * jnp.exp(gl - gc), TN)
    return jnp.concatenate(outs, axis=0), state_t


GLA_Q0, GLA_K0, GLA_V0 = 7680 // 512, 8192 // 512, 8704 // 512


def _gla_fwd(hmain, la):
    rows = hmain.shape[0]
    b, h = _pick(rows, GLA_BLOCK), GLA_HEADS
    nb = rows // b

    def kern(q_ref, k_ref, v_ref, la_ref, o_ref, st_ref, state):
        @pl.when(pl.program_id(0) == 0)
        def _():
            state[...] = jnp.zeros_like(state)

        for j in range(h):
            sl = slice(j * 128, (j + 1) * 128)
            st = state[j]
            st_ref[0, j] = st
            o, new = _gla_block(st, q_ref[:, sl], k_ref[:, sl], v_ref[:, sl], la_ref[:, sl])
            o_ref[:, sl] = o
            state[j] = new

    return pl.pallas_call(
        kern, name="gla_fwd", grid=(nb,),
        in_specs=[pl.BlockSpec((b, 512), lambda i: (i, GLA_Q0)), pl.BlockSpec((b, 512), lambda i: (i, GLA_K0)),
                  pl.BlockSpec((b, 512), lambda i: (i, GLA_V0)), pl.BlockSpec((b, 512), lambda i: (i, 0))],
        out_specs=[pl.BlockSpec((b, 512), lambda i: (i, 0)), pl.BlockSpec((1, h, 128, 128), lambda i: (i, 0, 0, 0))],
        out_shape=[jax.ShapeDtypeStruct((rows, h * 128), f32), jax.ShapeDtypeStruct((nb, h, 128, 128), f32)],
        scratch_shapes=[pltpu.VMEM((h, 128, 128), f32)],
        compiler_params=_cparams(("arbitrary",)),
    )(hmain, hmain, hmain, la)


def _gla_bwd(hmain, la, states, do):
    rows = hmain.shape[0]
    b, h = _pick(rows, GLA_BLOCK), GLA_HEADS
    nb = rows // b

    def kern(q_ref, k_ref, v_ref, la_ref, st_ref, do_ref, dq_ref, dk_ref, dv_ref, dla_ref, dstate):
        @pl.when(pl.program_id(0) == 0)
        def _():
            dstate[...] = jnp.zeros_like(dstate)

        for j in range(h):
            sl = slice(j * 128, (j + 1) * 128)
            args = (st_ref[0, j], q_ref[:, sl], k_ref[:, sl], v_ref[:, sl], la_ref[:, sl])
            ds, dq, dk, dv, dla = jax.vjp(_gla_block, *args)[1]((do_ref[:, sl], dstate[j]))
            dstate[j] = ds
            dq_ref[:, sl] = dq.astype(bf16)
            dk_ref[:, sl] = dk.astype(bf16)
            dv_ref[:, sl] = dv.astype(bf16)
            dla_ref[:, sl] = dla

    rev = lambda i: (nb - 1 - i, 0)
    return pl.pallas_call(
        kern, name="gla_bwd", grid=(nb,),
        in_specs=[pl.BlockSpec((b, 512), lambda i: (nb - 1 - i, GLA_Q0)), pl.BlockSpec((b, 512), lambda i: (nb - 1 - i, GLA_K0)),
                  pl.BlockSpec((b, 512), lambda i: (nb - 1 - i, GLA_V0)), pl.BlockSpec((b, 512), rev),
                  pl.BlockSpec((1, h, 128, 128), lambda i: (nb - 1 - i, 0, 0, 0)), pl.BlockSpec((b, 512), rev)],
        out_specs=[pl.BlockSpec((b, 512), rev)] * 4,
        out_shape=[jax.ShapeDtypeStruct((rows, h * 128), bf16)] * 3 + [jax.ShapeDtypeStruct((rows, h * 128), f32)],
        scratch_shapes=[pltpu.VMEM((h, 128, 128), f32)],
        compiler_params=_cparams(("arbitrary",)),
    )(hmain, hmain, hmain, la, states, do)


def _f_s5_params(a_re, a_im, log_dt, b_re, b_im):
    dt = jnp.exp(log_dt)
    mag = jnp.exp(dt * a_re)
    abar_re, abar_im = mag * jnp.cos(dt * a_im), mag * jnp.sin(dt * a_im)
    den = a_re * a_re + a_im * a_im
    nr, ni = abar_re - 1.0, abar_im
    fr, fi = (nr * a_re + ni * a_im) / den, (ni * a_re - nr * a_im) / den
    return abar_re, abar_im, fr[None] * b_re - fi[None] * b_im, fr[None] * b_im + fi[None] * b_re


def _s5_params(a_re, a_im, log_dt, b_re, b_im):
    def kern(*refs):
        for o_ref, r in zip(refs[5:], _f_s5_params(*[r[...] for r in refs[:5]])):
            o_ref[...] = r

    ins = (a_re, a_im, log_dt, b_re, b_im)
    return pl.pallas_call(
        kern, name="s5_params", out_shape=[jax.ShapeDtypeStruct(a_re.shape, f32)] * 2 + [jax.ShapeDtypeStruct(b_re.shape, f32)] * 2,
    )(*ins)


def _s5_params_bwd(a_re, a_im, log_dt, b_re, b_im, d_ar, d_ai, d_br, d_bi):
    def kern(*refs):
        grads = jax.vjp(_f_s5_params, *[r[...] for r in refs[:5]])[1](tuple(r[...] for r in refs[5:9]))
        for o_ref, g in zip(refs[9:], grads):
            o_ref[...] = g

    ins = (a_re, a_im, log_dt, b_re, b_im)
    return pl.pallas_call(
        kern, name="s5_params_bwd", out_shape=[jax.ShapeDtypeStruct(t.shape, f32) for t in ins],
    )(*ins, d_ar, d_ai, d_br, d_bi)


def _cmul(ar, ai, br, bi):
    return ar * br - ai * bi, ar * bi + ai * br


def _s5_scan(xr, xi, ar, ai, reverse):
    t = xr.shape[0]
    row = _iota(xr.shape, 0)
    s = 1
    while s < t:
        if reverse:
            keep = row < t - s
            sr, si = pltpu.roll(xr, t - s, 0), pltpu.roll(xi, t - s, 0)
        else:
            keep = row >= s
            sr, si = pltpu.roll(xr, s, 0), pltpu.roll(xi, s, 0)
        sr, si = jnp.where(keep, sr, 0.0), jnp.where(keep, si, 0.0)
        pr, pi = _cmul(ar, ai, sr, si)
        xr, xi = xr + pr, xi + pi
        ar, ai = _cmul(ar, ai, ar, ai)
        s *= 2
    return xr, xi


def _s5_powers(ar, ai, t, reverse):
    row = _iota((t, ar.shape[1]), 0)
    at = (row == (t - 1 if reverse else 0))
    return _s5_scan(jnp.where(at, ar, 0.0), jnp.where(at, ai, 0.0), ar, ai, reverse)


S5_U0 = 7168 // 128


def _s5_fwd(hmain, abar, bmat_re, bmat_im, cmat_re, cmat_im, dvec):
    rows = hmain.shape[0]
    t = _pick(rows, S5_TILE)
    nt, ns = rows // t, 512

    def kern(u_ref, a_ref, br_ref, bi_ref, cr_ref, ci_ref, d_ref, y_ref, xr_ref, xi_ref, pw, carry):
        ar, ai = a_ref[0, 0], a_ref[1, 0]

        @pl.when(pl.program_id(1) == 0)
        def _():
            pr, pi = _s5_powers(ar, ai, t, False)
            pw[0], pw[1] = pr, pi
            carry[...] = jnp.zeros_like(carry)

        u = u_ref[...]
        xr, xi = _s5_scan(_rdot(u, br_ref[0]), _rdot(u, bi_ref[0]), ar, ai, False)
        cr, ci = carry[0:1, :], carry[1:2, :]
        qr, qi = _cmul(pw[0], pw[1], cr, ci)
        xr, xi = xr + qr, xi + qi
        xr_ref[...] = xr
        xi_ref[...] = xi
        carry[0:1, :] = xr[t - 1:t, :]
        carry[1:2, :] = xi[t - 1:t, :]
        y_ref[...] = _rdot(xr, cr_ref[0]) - _rdot(xi, ci_ref[0]) + d_ref[...] * u

    sb3 = lambda b, i: (b, 0, 0)
    return pl.pallas_call(
        kern, name="s5_fwd", grid=(S5_SB, nt),
        in_specs=[pl.BlockSpec((t, 128), lambda b, i: (i, S5_U0 + b)), pl.BlockSpec((2, 1, 1, ns), lambda b, i: (0, b, 0, 0)),
                  pl.BlockSpec((1, 128, ns), sb3), pl.BlockSpec((1, 128, ns), sb3),
                  pl.BlockSpec((1, ns, 128), sb3), pl.BlockSpec((1, ns, 128), sb3), pl.BlockSpec((1, 128), lambda b, i: (0, b))],
        out_specs=[pl.BlockSpec((t, 128), lambda b, i: (i, b)), pl.BlockSpec((t, ns), lambda b, i: (i, b)),
                   pl.BlockSpec((t, ns), lambda b, i: (i, b))],
        out_shape=[jax.ShapeDtypeStruct((rows, 512), f32), jax.ShapeDtypeStruct((rows, S5_SB * ns), f32),
                   jax.ShapeDtypeStruct((rows, S5_SB * ns), f32)],
        scratch_shapes=[pltpu.VMEM((2, t, ns), f32), pltpu.VMEM((8, ns), f32)],
        compiler_params=_cparams(("parallel", "arbitrary")),
    )(hmain, abar, bmat_re, bmat_im, cmat_re, cmat_im, dvec)


def _s5_bwd(hmain, abar, bmat_re, bmat_im, cmat_re, cmat_im, dvec, x_re, x_im, dy):
    rows = hmain.shape[0]
    t = _pick(rows, S5_TILE)
    nt, ns = rows // t, 512
    t8 = t // 8

    def kern(u_ref, a_ref, br_ref, bi_ref, cr_ref, ci_ref, d_ref, xr_ref, xi_ref, xpr_ref, xpi_ref, dy_ref,
             du_ref, da_ref, dbr_ref, dbi_ref, dcr_ref, dci_ref, dd_ref, pw, carry):
        i = pl.program_id(1)
        ar, ai = a_ref[0, 0], -a_ref[1, 0]

        @pl.when(i == 0)
        def _():
            pr, pi = _s5_powers(ar, ai, t, True)
            pw[0], pw[1] = pr, pi
            carry[...] = jnp.zeros_like(carry)

        u, gy = u_ref[...], dy_ref[...]
        lr, li = _s5_scan(_rdot(gy, cr_ref[0], NT), -_rdot(gy, ci_ref[0], NT), ar, ai, True)
        qr, qi = _cmul(pw[0], pw[1], carry[0:1, :], carry[1:2, :])
        lr, li = lr + qr, li + qi
        carry[0:1, :] = lr[0:1, :]
        carry[1:2, :] = li[0:1, :]
        du_ref[...] = (_rdot(lr, br_ref[0], NT) + _rdot(li, bi_ref[0], NT) + d_ref[...] * gy).astype(bf16)
        xr, xi = xr_ref[...], xi_ref[...]
        row = _iota(xr.shape, 0)
        first_r = jnp.where(i < nt - 1, xpr_ref[7:8, :], 0.0)
        first_i = jnp.where(i < nt - 1, xpi_ref[7:8, :], 0.0)
        xpr = jnp.where(row == 0, first_r, pltpu.roll(xr, 1, 0))
        xpi = jnp.where(row == 0, first_i, pltpu.roll(xi, 1, 0))
        da_r = jnp.sum(lr * xpr + li * xpi, axis=0, keepdims=True)
        da_i = jnp.sum(li * xpr - lr * xpi, axis=0, keepdims=True)
        upd = [(da_ref.at[0, 0], da_r), (da_ref.at[1, 0], da_i),
               (dbr_ref.at[0], _rdot(u, lr, TN)), (dbi_ref.at[0], _rdot(u, li, TN)),
               (dcr_ref.at[0], _rdot(xr, gy, TN)), (dci_ref.at[0], -_rdot(xi, gy, TN)),
               (dd_ref, jnp.sum(gy * u, axis=0, keepdims=True))]

        @pl.when(i == 0)
        def _():
            for ref, val in upd:
                ref[...] = val

        @pl.when(i > 0)
        def _():
            for ref, val in upd:
                ref[...] += val

    sb3 = lambda b, i: (b, 0, 0)
    rev = lambda b, i: (nt - 1 - i, b)
    prev8 = lambda b, i: (jnp.maximum((nt - 1 - i) * t8 - 1, 0), b)
    return pl.pallas_call(
        kern, name="s5_bwd", grid=(S5_SB, nt),
        in_specs=[pl.BlockSpec((t, 128), lambda b, i: (nt - 1 - i, S5_U0 + b)), pl.BlockSpec((2, 1, 1, ns), lambda b, i: (0, b, 0, 0)),
                  pl.BlockSpec((1, 128, ns), sb3), pl.BlockSpec((1, 128, ns), sb3),
                  pl.BlockSpec((1, ns, 128), sb3), pl.BlockSpec((1, ns, 128), sb3), pl.BlockSpec((1, 128), lambda b, i: (0, b)),
                  pl.BlockSpec((t, ns), rev), pl.BlockSpec((t, ns), rev), pl.BlockSpec((8, ns), prev8), pl.BlockSpec((8, ns), prev8),
                  pl.BlockSpec((t, 128), rev)],
        out_specs=[pl.BlockSpec((t, 128), rev), pl.BlockSpec((2, 1, 1, ns), lambda b, i: (0, b, 0, 0)),
                   pl.BlockSpec((1, 128, ns), sb3), pl.BlockSpec((1, 128, ns), sb3),
                   pl.BlockSpec((1, ns, 128), sb3), pl.BlockSpec((1, ns, 128), sb3), pl.BlockSpec((1, 128), lambda b, i: (0, b))],
        out_shape=[jax.ShapeDtypeStruct((rows, 512), bf16), jax.ShapeDtypeStruct((2, S5_SB, 1, ns), f32),
                   jax.ShapeDtypeStruct((S5_SB, 128, ns), f32), jax.ShapeDtypeStruct((S5_SB, 128, ns), f32),
                   jax.ShapeDtypeStruct((S5_SB, ns, 128), f32), jax.ShapeDtypeStruct((S5_SB, ns, 128), f32),
                   jax.ShapeDtypeStruct((1, 512), f32)],
        scratch_shapes=[pltpu.VMEM((2, t, ns), f32), pltpu.VMEM((8, ns), f32)],
        compiler_params=_cparams(("parallel", "arbitrary")),
    )(hmain, abar, bmat_re, bmat_im, cmat_re, cmat_im, dvec, x_re, x_im, x_re, x_im, dy)


def _loss_head(y, target):
    rows, feat = y.shape
    tile = _pick(rows, 256)

    def kern(y_ref, t_ref, l_ref, dy_ref):
        e = y_ref[...] - t_ref[...]
        dy_ref[...] = e * (1.0 / feat)
        part = jnp.broadcast_to(0.5 * jnp.sum(e * e) * (1.0 / feat), l_ref.shape)

        @pl.when(pl.program_id(0) == 0)
        def _():
            l_ref[...] = part

        @pl.when(pl.program_id(0) > 0)
        def _():
            l_ref[...] += part

    return pl.pallas_call(
        kern, name="loss_head", grid=(rows // tile,),
        in_specs=[pl.BlockSpec((tile, feat), lambda i: (i, 0))] * 2,
        out_specs=[pl.BlockSpec((8, 128), lambda i: (0, 0)), pl.BlockSpec((tile, feat), lambda i: (i, 0))],
        out_shape=[jax.ShapeDtypeStruct((8, 128), f32), jax.ShapeDtypeStruct((rows, feat), f32)],
        compiler_params=_cparams(("arbitrary",)),
    )(y, target)


def _sum_parts(name, parts, out_dtype=f32):
    rows, cols = parts[0].shape
    tile = _pick(rows, 512)

    def kern(*refs):
        acc = refs[0][...].astype(f32)
        for r in refs[1:-1]:
            acc = acc + r[...].astype(f32)
        refs[-1][...] = acc.astype(out_dtype)

    return pl.pallas_call(
        kern, name=name, grid=(rows // tile,),
        in_specs=[pl.BlockSpec((tile, cols), lambda i: (i, 0))] * len(parts),
        out_specs=pl.BlockSpec((tile, cols), lambda i: (i, 0)),
        out_shape=jax.ShapeDtypeStruct((rows, cols), out_dtype),
        compiler_params=_cparams(("parallel",)),
    )(*parts)


ADAMW_BLOCK_BYTES = 1536 * 1024


def _adamw(name, w, m, v, g):
    rows, cols = w.shape
    tile = _pick(rows, 512)
    while tile > 8 and tile * cols * 4 > ADAMW_BLOCK_BYTES and rows % (tile // 2) == 0:
        tile //= 2
    c1, c2 = 1.0 / (1.0 - ADAM_B1 ** ADAM_STEP), 1.0 / (1.0 - ADAM_B2 ** ADAM_STEP)

    def kern(w_ref, m_ref, v_ref, g_ref, d_ref, nm_ref, nv_ref):
        gv = g_ref[...]
        nm = ADAM_B1 * m_ref[...] + (1.0 - ADAM_B1) * gv
        nv = ADAM_B2 * v_ref[...] + (1.0 - ADAM_B2) * (gv * gv)
        nm_ref[...] = nm
        nv_ref[...] = nv
        d_ref[...] = -ADAM_LR * ((nm * c1) / (jnp.sqrt(nv * c2) + ADAM_EPS) + ADAM_WD * w_ref[...])

    spec = pl.BlockSpec((tile, cols), lambda i: (i, 0))
    return pl.pallas_call(
        kern, name=name, grid=(rows // tile,), in_specs=[spec] * 4, out_specs=[spec] * 3,
        out_shape=[jax.ShapeDtypeStruct((rows, cols), f32)] * 3,
        compiler_params=_cparams(("parallel",)),
    )(w, m, v, g)


MESH = pl.DeviceIdType.MESH
HBM_SPEC = pl.BlockSpec(memory_space=pltpu.HBM)


def _other_chips(x, y):
    return [(1 - x, y), (x, 1 - y), (1 - x, 1 - y)]


def _gather_chips(big, small):
    rows = big.shape[0]
    half = rows // 2

    def kern(big_ref, small_ref, bout, sout, ici_send, ici_recv, d2d_send, d2d_recv, small_send, small_recv, local_sems):
        x, y, c = lax.axis_index("x"), lax.axis_index("y"), lax.axis_index("c")
        me, sibling = 2 * x + y, (x, y, 1 - c)
        chips = _other_chips(x, y)
        slots = [2 * chip[0] + chip[1] for chip in chips]
        mine, theirs = pl.ds(c * half, half), pl.ds((1 - c) * half, half)
        local = [pltpu.make_async_copy(big_ref, bout.at[me], local_sems.at[0]),
                 pltpu.make_async_copy(small_ref, sout.at[me], local_sems.at[1])]
        for cp in local:
            cp.start()
        sends = [pltpu.make_async_remote_copy(
            src_ref=big_ref.at[mine], dst_ref=bout.at[me, mine], send_sem=ici_send.at[k], recv_sem=ici_recv.at[k],
            device_id=(*chip, c), device_id_type=MESH) for k, chip in enumerate(chips)]
        sends += [pltpu.make_async_remote_copy(
            src_ref=small_ref, dst_ref=sout.at[me], send_sem=small_send.at[k], recv_sem=small_recv.at[k],
            device_id=(*chip, c), device_id_type=MESH) for k, chip in enumerate(chips)]
        for cp in sends:
            cp.start()
        for k, chip in enumerate(chips):
            pltpu.make_async_remote_copy(
                src_ref=big_ref.at[mine], dst_ref=bout.at[slots[k], mine], send_sem=ici_send.at[k], recv_sem=ici_recv.at[k],
                device_id=(*chip, c), device_id_type=MESH).wait_recv()
            passed = pltpu.make_async_remote_copy(
                src_ref=bout.at[slots[k], mine], dst_ref=bout.at[slots[k], mine], send_sem=d2d_send.at[k],
                recv_sem=d2d_recv.at[k], device_id=sibling, device_id_type=MESH)
            passed.start()
            sends.append(passed)
        for k, chip in enumerate(chips):
            pltpu.make_async_remote_copy(
                src_ref=bout.at[slots[k], theirs], dst_ref=bout.at[slots[k], theirs], send_sem=d2d_send.at[k],
                recv_sem=d2d_recv.at[k], device_id=sibling, device_id_type=MESH).wait_recv()
            pltpu.make_async_remote_copy(
                src_ref=small_ref, dst_ref=sout.at[slots[k]], send_sem=small_send.at[k], recv_sem=small_recv.at[k],
                device_id=(*chip, c), device_id_type=MESH).wait_recv()
        for cp in sends:
            cp.wait_send()
        for cp in local:
            cp.wait()

    return pl.pallas_call(
        kern, name="gather_chips", in_specs=[HBM_SPEC] * 2, out_specs=[HBM_SPEC] * 2,
        out_shape=[jax.ShapeDtypeStruct((N_CHIPS,) + big.shape, big.dtype), jax.ShapeDtypeStruct((N_CHIPS,) + small.shape, small.dtype)],
        scratch_shapes=[pltpu.SemaphoreType.DMA((3,))] * 6 + [pltpu.SemaphoreType.DMA((2,))],
    )(big, small)


def _join_halves(part):
    half, cols = part.shape

    def kern(p_ref, out_ref, send_sem, recv_sem, local_sem):
        x, y, c = lax.axis_index("x"), lax.axis_index("y"), lax.axis_index("c")
        mine, theirs = pl.ds(c * half, half), pl.ds((1 - c) * half, half)
        local = pltpu.make_async_copy(p_ref, out_ref.at[mine], local_sem)
        local.start()
        cp = pltpu.make_async_remote_copy(src_ref=p_ref, dst_ref=out_ref.at[mine], send_sem=send_sem, recv_sem=recv_sem,
                                          device_id=(x, y, 1 - c), device_id_type=MESH)
        cp.start()
        pltpu.make_async_remote_copy(src_ref=p_ref, dst_ref=out_ref.at[theirs], send_sem=send_sem, recv_sem=recv_sem,
                                     device_id=(x, y, 1 - c), device_id_type=MESH).wait_recv()
        cp.wait_send()
        local.wait()

    return pl.pallas_call(
        kern, name="join_halves", in_specs=[HBM_SPEC], out_specs=HBM_SPEC,
        out_shape=jax.ShapeDtypeStruct((2 * half, cols), part.dtype),
        scratch_shapes=[pltpu.SemaphoreType.DMA, pltpu.SemaphoreType.DMA, pltpu.SemaphoreType.DMA],
    )(part)


def _scatter_chips(pieces):
    _, rows, cols = pieces.shape

    def kern(p_ref, out_ref, send_sems, recv_sems):
        x, y, c = lax.axis_index("x"), lax.axis_index("y"), lax.axis_index("c")
        chips = _other_chips(x, y)
        sends = [pltpu.make_async_remote_copy(
            src_ref=p_ref.at[2 * chip[0] + chip[1]], dst_ref=out_ref.at[k], send_sem=send_sems.at[k],
            recv_sem=recv_sems.at[k], device_id=(*chip, c), device_id_type=MESH) for k, chip in enumerate(chips)]
        for cp in sends:
            cp.start()
        for cp in sends:
            cp.wait()

    return pl.pallas_call(
        kern, name="scatter_chips", in_specs=[HBM_SPEC], out_specs=HBM_SPEC,
        out_shape=jax.ShapeDtypeStruct((3, rows, cols), pieces.dtype),
        scratch_shapes=[pltpu.SemaphoreType.DMA((3,)), pltpu.SemaphoreType.DMA((3,))],
    )(pieces)


def _swap_sibling(buf):
    def kern(b_ref, out_ref, send_sem, recv_sem):
        x, y, c = lax.axis_index("x"), lax.axis_index("y"), lax.axis_index("c")
        cp = pltpu.make_async_remote_copy(src_ref=b_ref, dst_ref=out_ref, send_sem=send_sem, recv_sem=recv_sem,
                                          device_id=(x, y, 1 - c), device_id_type=MESH)
        cp.start()
        cp.wait()

    return pl.pallas_call(
        kern, name="swap_sibling", in_specs=[HBM_SPEC], out_specs=HBM_SPEC,
        out_shape=jax.ShapeDtypeStruct(buf.shape, buf.dtype),
        scratch_shapes=[pltpu.SemaphoreType.DMA, pltpu.SemaphoreType.DMA],
    )(buf)


def _gather_all(buf):
    rows, cols = buf.shape

    def kern(b_ref, out_ref, send_sems, recv_sems, local_sem):
        x, y, c = lax.axis_index("x"), lax.axis_index("y"), lax.axis_index("c")
        me = 4 * x + 2 * y + c
        local = pltpu.make_async_copy(b_ref, out_ref.at[me], local_sem)
        local.start()
        peers = []
        for k in range(1, N_DEV):
            fx, fy, fc = (k >> 2) & 1, (k >> 1) & 1, k & 1
            peers.append((x ^ fx, y ^ fy, c ^ fc))
        sends = [pltpu.make_async_remote_copy(
            src_ref=b_ref, dst_ref=out_ref.at[me], send_sem=send_sems.at[k], recv_sem=recv_sems.at[k],
            device_id=peer, device_id_type=MESH) for k, peer in enumerate(peers)]
        for cp in sends:
            cp.start()
        for k, peer in enumerate(peers):
            pltpu.make_async_remote_copy(
                src_ref=b_ref, dst_ref=out_ref.at[4 * peer[0] + 2 * peer[1] + peer[2]], send_sem=send_sems.at[k],
                recv_sem=recv_sems.at[k], device_id=peer, device_id_type=MESH).wait_recv()
        for cp in sends:
            cp.wait_send()
        local.wait()

    return pl.pallas_call(
        kern, name="gather_all", in_specs=[HBM_SPEC], out_specs=HBM_SPEC,
        out_shape=jax.ShapeDtypeStruct((N_DEV, rows, cols), buf.dtype),
        scratch_shapes=[pltpu.SemaphoreType.DMA((N_DEV - 1,)), pltpu.SemaphoreType.DMA((N_DEV - 1,)), pltpu.SemaphoreType.DMA],
    )(buf)


def _sum_slots(name, buf):
    n, rows, cols = buf.shape
    tile = _pick(rows, 256)

    def kern(b_ref, o_ref):
        acc = b_ref[0]
        for k in range(1, n):
            acc = acc + b_ref[k]
        o_ref[...] = acc

    return pl.pallas_call(
        kern, name=name, grid=(rows // tile,),
        in_specs=[pl.BlockSpec((n, tile, cols), lambda i: (0, i, 0))], out_specs=pl.BlockSpec((tile, cols), lambda i: (i, 0)),
        out_shape=jax.ShapeDtypeStruct((rows, cols), f32), compiler_params=_cparams(("parallel",)),
    )(buf)


def _pad_heads(t):
    r = t.shape[0]
    return jnp.pad(t.reshape(r, GLA_HEADS, GLA_DK), ((0, 0), (0, 0), (0, 128 - GLA_DK))).reshape(r, GLA_HEADS * 128)


def _unpad_heads(t):
    r = t.shape[0]
    return t.reshape(r, GLA_HEADS, 128)[:, :, :GLA_DK].reshape(r, GLA_HEADS * GLA_DK)


def _blockdiag(t):
    _, r, c = t.shape
    eye = jnp.eye(8, dtype=t.dtype).reshape(1, 8, 1, 8, 1)
    return (t.reshape(S5_SB, 8, r, 1, c) * eye).reshape(S5_SB, 8 * r, 8 * c)


def _blockdiag_extract(m, r, c):
    m5 = m.reshape(S5_SB, 8, r, 8, c)
    return jnp.stack([m5[:, g, :, g, :] for g in range(8)], axis=1).reshape(S5_GROUPS, r, c)


def _row(v, width=None):
    v = v[None]
    return v if width is None else jnp.pad(v, ((0, 0), (0, width - v.shape[1])))


def _layer_operands(p):
    w_in = p['w_in']

    def seg(n):
        return w_in[:, IN_ORIG[n][0]:IN_ORIG[n][1]]

    o = dict(p)
    o['wm'] = jnp.concatenate([seg('dn_qkv'), seg('dn_gate'), seg('cf'), seg('gates'), seg('s5'), _pad_heads(seg('gla_q')),
                               _pad_heads(seg('gla_k')), seg('gla_v'), seg('gla_g')], axis=1)
    o['ws'] = jnp.pad(jnp.concatenate([seg('dn_a'), seg('dn_b'), seg('gla_lr')], axis=1), ((0, 0), (0, WS_COLS - 24)))
    o['dn_conv8'] = jnp.pad(p['dn_conv'], ((0, 4), (0, 0)))
    o['a_log_r'] = _row(p['dn_a_log'], 128)
    o['dt_bias_r'] = _row(p['dn_dt_bias'], 128)
    o['dn_norm_r'] = _row(p['dn_norm'])
    o['cf_dw32'] = jnp.pad(p['cf_dw'], ((0, 1), (0, 0)))
    o['cf_bias_r'], o['cf_g_r'], o['cf_b_r'] = _row(p['cf_dw_bias']), _row(p['cf_ln_g']), _row(p['cf_ln_b'])
    o['w_alpha_p'] = jnp.pad(_pad_heads(p['gla_w_alpha']), ((8, 128 - 24), (0, 0)))
    o['b_alpha_r'] = _pad_heads(_row(p['gla_b_alpha']))
    o['gla_norm_r'] = _row(p['gla_norm'])
    o['ln1_g_r'], o['ln1_b_r'], o['ln2_g_r'], o['ln2_b_r'] = (_row(p[n]) for n in ('ln1_g', 'ln1_b', 'ln2_g', 'ln2_b'))
    o['ffn_conv8'] = jnp.pad(p['ffn_conv'], ((0, 5), (0, 0)))
    o['s5_in'] = (p['s5_a_re'], p['s5_a_im'], p['s5_log_dt'][:, None],
                  p['s5_b_re'].transpose(2, 0, 1), p['s5_b_im'].transpose(2, 0, 1))
    abar_re, abar_im, bbar_re, bbar_im = _s5_params(*o['s5_in'])
    o['abar'] = jnp.stack([abar_re, abar_im]).reshape(2, S5_SB, 1, 512)
    o['bmat_re'], o['bmat_im'] = _blockdiag(bbar_re.transpose(1, 0, 2)), _blockdiag(bbar_im.transpose(1, 0, 2))
    o['cmat_re'], o['cmat_im'] = _blockdiag(p['s5_c_re'].transpose(0, 2, 1)), _blockdiag(p['s5_c_im'].transpose(0, 2, 1))
    o['dvec'] = _row(p['s5_d'])
    return o


def _whole(a):
    return (a, a.shape[1], 0)


def _merge_ins(h, s):
    return [(h, 1024, 3), (h, 1024, 4), (h, 1024, 5), (h, 1024, 6), (s['y_a'], 1024, 0), (s['y_b'], 1024, 0),
            (s['zz'], 1024, 0), (s['zz'], 1024, 1), (s['y_d'], 1024, 0)]


def _layer_fwd(x, o):
    s = {}
    h = s['h'] = _mm(x, o['wm'], 'nn', 'mm_h')
    hs = s['hs'] = _mm(x, o['ws'], 'nn', 'mm_hs')
    s['c1'] = _conv_fwd('conv_dn', h, 1536, 0, o['dn_conv8'], 4)
    s['qkvn'], s['gb'], s['la'] = _rowwise('pre', _f_pre, [_whole(s['c1']), _whole(hs)],
                                           [o['a_log_r'], o['dt_bias_r'], o['w_alpha_p'], o['b_alpha_r']], [1536, 128, 512])
    s['o_dn'], s['st_dn'], s['inv_dn'] = _dn_fwd(s['qkvn'], s['gb'])
    (s['on_dn'],) = _rowwise('post_dn', _f_post, [_whole(s['o_dn']), (h, 512, 3)], [o['dn_norm_r']], [512], out_dtype=bf16)
    s['y_a'] = _mm(s['on_dn'], o['w_br_dn'], 'nn', 'mm_br')
    (s['cfp'],) = _rowwise('glu_cf', _f_glu, [(h, 512, 4), (h, 512, 5)], [], [512], out_dtype=bf16)
    s['cc'] = _conv_fwd('conv_cf', s['cfp'], 512, 0, o['cf_dw32'], 31)
    (s['cfo'],) = _rowwise('post_cf', _f_cfpost, [_whole(s['cc'])], [o['cf_bias_r'], o['cf_g_r'], o['cf_b_r']], [512], out_dtype=bf16)
    s['y_b'] = _mm(s['cfo'], o['w_br_cf'], 'nn', 'mm_br')
    s['ys5'], s['xr'], s['xi'] = _s5_fwd(h, o['abar'], o['bmat_re'], o['bmat_im'], o['cmat_re'], o['cmat_im'], o['dvec'])
    (s['z'],) = _rowwise('gelu', _f_gelu, [_whole(s['ys5'])], [], [512], out_dtype=bf16)
    s['zz'] = _mm(s['z'], o['w_br_s5'], 'nn', 'mm_br_s5')
    s['o_gla'], s['st_gla'] = _gla_fwd(h, s['la'])
    (s['on_gla'],) = _rowwise('post_gla', _f_post, [_whole(s['o_gla']), (h, 512, 18)], [o['gla_norm_r']], [512], out_dtype=bf16)
    s['y_d'] = _mm(s['on_gla'], o['w_br_gla'], 'nn', 'mm_br')
    (s['merged'],) = _rowwise('merge', _f_merge, _merge_ins(h, s), [], [1024], tile=128, out_dtype=bf16)
    s['mix'] = _mm(s['merged'], o['w_o'], 'nn', 'mm_o')
    (s['x1'],) = _rowwise('ln', _f_ln, [_whole(x), _whole(s['mix'])], [o['ln1_g_r'], o['ln1_b_r']], [1024])
    s['up'] = _mm(s['x1'], o['w_up'], 'nn', 'mm_up', out_dtype=bf16)
    s['u'] = _conv_fwd('conv_ffn', s['up'], 2 * D_FF, 0, o['ffn_conv8'], 3)
    (s['act'],) = _rowwise('act', _f_act, [(s['u'], D_FF, 0), (s['u'], D_FF, 1)], [], [D_FF], out_dtype=bf16)
    s['ffn'] = _mm(s['act'], o['w_down'], 'nn', 'mm_down')
    (x2,) = _rowwise('ln', _f_ln, [_whole(s['x1']), _whole(s['ffn'])], [o['ln2_g_r'], o['ln2_b_r']], [1024])
    return x2, s


def _layer_bwd(x, o, s, dparts):
    h, g = s['h'], {}
    (dx1_a, dffn), (g['ln2_g'], g['ln2_b']) = _rowwise_bwd(
        'ln_bwd', _f_ln, [_whole(s['x1']), _whole(s['ffn'])], [o['ln2_g_r'], o['ln2_b_r']], [[_whole(d) for d in dparts]], [f32, bf16])
    dact = _mm(dffn, o['w_down'], 'nt', 'mm_down_dx')
    g['w_down'] = _mm(s['act'], dffn, 'tn', 'mm_down_dw')
    (du_a, du_b), _ = _rowwise_bwd('act_bwd', _f_act, [(s['u'], D_FF, 0), (s['u'], D_FF, 1)], [], [[_whole(dact)]], [bf16, bf16])
    dup, dw = _conv_bwd('conv_ffn_bwd', s['up'], 2 * D_FF, 0, o['ffn_conv8'], 3, jnp.concatenate([du_a, du_b], axis=1), dx_dtype=bf16)
    g['ffn_conv'] = dw[:3]
    dx1_b = _mm(dup, o['w_up'], 'nt', 'mm_up_dx')
    g['w_up'] = _mm(s['x1'], dup, 'tn', 'mm_up_dw')
    (dx_a, dmix), (g['ln1_g'], g['ln1_b']) = _rowwise_bwd(
        'ln_bwd', _f_ln, [_whole(x), _whole(s['mix'])], [o['ln1_g_r'], o['ln1_b_r']], [[_whole(dx1_a), _whole(dx1_b)]], [f32, bf16])
    dmerged = _mm(dmix, o['w_o'], 'nt', 'mm_o_dx')
    g['w_o'] = _mm(s['merged'], dmix, 'tn', 'mm_o_dw')
    (dga, dgb_, dgc, dgd, dya, dyb, dzv, dzg, dyd), _ = _rowwise_bwd(
        'merge_bwd', _f_merge, _merge_ins(h, s), [], [[_whole(dmerged)]], [bf16] * 9, tile=128)
    dzz = jnp.concatenate([dzv, dzg], axis=1)
    don = _mm(dya, o['w_br_dn'], 'nt', 'mm_br_dx')
    g['w_br_dn'] = _mm(s['on_dn'], dya, 'tn', 'mm_br_dw')
    (do_dn, dgate_dn), (g['dn_norm'],) = _rowwise_bwd(
        'post_bwd', _f_post, [_whole(s['o_dn']), (h, 512, 3)], [o['dn_norm_r']], [[_whole(don)]], [f32, bf16])
    dqkvn, dgb = _dn_bwd(s['qkvn'], s['gb'], s['st_dn'], s['inv_dn'], do_dn)
    don = _mm(dyd, o['w_br_gla'], 'nt', 'mm_br_dx')
    g['w_br_gla'] = _mm(s['on_gla'], dyd, 'tn', 'mm_br_dw')
    (do_gla, dgate_gla), (g['gla_norm'],) = _rowwise_bwd(
        'post_bwd', _f_post, [_whole(s['o_gla']), (h, 512, 18)], [o['gla_norm_r']], [[_whole(don)]], [f32, bf16])
    dq_gla, dk_gla, dv_gla, dla = _gla_bwd(h, s['la'], s['st_gla'], do_gla)
    (dc1, dhs), (d_alog, d_dtb, d_walpha, d_balpha) = _rowwise_bwd(
        'pre_bwd', _f_pre, [_whole(s['c1']), _whole(s['hs'])], [o['a_log_r'], o['dt_bias_r'], o['w_alpha_p'], o['b_alpha_r']],
        [[_whole(dqkvn)], [_whole(dgb)], [_whole(dla)]], [bf16, bf16])
    g['dn_a_log'], g['dn_dt_bias'] = d_alog[0, :DN_HEADS], d_dtb[0, :DN_HEADS]
    g['gla_w_alpha'], g['gla_b_alpha'] = _unpad_heads(d_walpha[8:24]), _unpad_heads(d_balpha)[0]
    d_dnqkv, dw = _conv_bwd('conv_dn_bwd', h, 1536, 0, o['dn_conv8'], 4, dc1, dx_dtype=bf16)
    g['dn_conv'] = dw[:4]
    dcfo = _mm(dyb, o['w_br_cf'], 'nt', 'mm_br_dx')
    g['w_br_cf'] = _mm(s['cfo'], dyb, 'tn', 'mm_br_dw')
    (dcc,), (g['cf_dw_bias'], g['cf_ln_g'], g['cf_ln_b']) = _rowwise_bwd(
        'post_cf_bwd', _f_cfpost, [_whole(s['cc'])], [o['cf_bias_r'], o['cf_g_r'], o['cf_b_r']], [[_whole(dcfo)]], [bf16])
    dcfp, dw = _conv_bwd('conv_cf_bwd', s['cfp'], 512, 0, o['cf_dw32'], 31, dcc)
    g['cf_dw'] = dw[:31]
    (dcf_a, dcf_g), _ = _rowwise_bwd('glu_bwd', _f_glu, [(h, 512, 4), (h, 512, 5)], [], [[_whole(dcfp)]], [bf16, bf16])
    dz = _mm(dzz, o['w_br_s5'], 'nt', 'mm_br_s5_dx')
    g['w_br_s5'] = _mm(s['z'], dzz, 'tn', 'mm_br_s5_dw')
    (dys5,), _ = _rowwise_bwd('gelu_bwd', _f_gelu, [_whole(s['ys5'])], [], [[_whole(dz)]], [f32])
    du_s5, d_abar, dbm_re, dbm_im, dcm_re, dcm_im, d_dvec = _s5_bwd(
        h, o['abar'], o['bmat_re'], o['bmat_im'], o['cmat_re'], o['cmat_im'], o['dvec'], s['xr'], s['xi'], dys5)
    d_bbar = [_blockdiag_extract(m, S5_GROUP, S5_STATE).transpose(1, 0, 2) for m in (dbm_re, dbm_im)]
    da_re, da_im, dlog_dt, db_re, db_im = _s5_params_bwd(
        *o['s5_in'], d_abar[0].reshape(S5_GROUPS, S5_STATE), d_abar[1].reshape(S5_GROUPS, S5_STATE), *d_bbar)
    g['s5_a_re'], g['s5_a_im'], g['s5_log_dt'] = da_re, da_im, dlog_dt[:, 0]
    g['s5_b_re'], g['s5_b_im'] = db_re.transpose(1, 2, 0), db_im.transpose(1, 2, 0)
    g['s5_c_re'], g['s5_c_im'] = (_blockdiag_extract(m, S5_STATE, S5_GROUP).transpose(0, 2, 1) for m in (dcm_re, dcm_im))
    g['s5_d'] = d_dvec[0]
    for n in ('dn_norm', 'gla_norm', 'cf_dw_bias', 'cf_ln_g', 'cf_ln_b', 'ln1_g', 'ln1_b', 'ln2_g', 'ln2_b'):
        g[n] = g[n][0]
    dh = jnp.concatenate([d_dnqkv, dgate_dn, dcf_a, dcf_g, dga, dgb_, dgc, dgd, du_s5, dq_gla, dk_gla, dv_gla, dgate_gla], axis=1)
    dwm = _mm(x, dh, 'tn', 'mm_h_dw')
    dws = _mm(x, dhs, 'tn', 'mm_hs_dw')
    g['w_in'] = jnp.concatenate([
        dwm[:, 0:1536], dws[:, 0:8], dwm[:, 1536:2048], dwm[:, 2048:3072], dwm[:, 7168:7680], _unpad_heads(dwm[:, 7680:8192]),
        _unpad_heads(dwm[:, 8192:8704]), dwm[:, 8704:9216], dwm[:, 9216:9728], dws[:, 8:24], dwm[:, 3072:7168]], axis=1)
    return [dx_a, _mm(dh, o['wm'], 'nt', 'mm_h_dx'), _mm(dhs, o['ws'], 'nt', 'mm_hs_dx')], g


WEIGHTS = ('w_in', 'dn_conv', 'dn_a_log', 'dn_dt_bias', 'dn_norm', 'w_br_dn', 'cf_dw', 'cf_dw_bias', 'cf_ln_g', 'cf_ln_b',
           'w_br_cf', 's5_a_re', 's5_a_im', 's5_log_dt', 's5_b_re', 's5_b_im', 's5_c_re', 's5_c_im', 's5_d', 'w_br_s5',
           'gla_w_alpha', 'gla_b_alpha', 'gla_norm', 'w_br_gla', 'w_o', 'ln1_g', 'ln1_b', 'w_up', 'ffn_conv', 'w_down',
           'ln2_g', 'ln2_b')
LARGE = ('w_in', 'w_br_dn', 'w_br_cf', 'w_br_s5', 'w_br_gla', 'w_o', 'w_up', 'w_down')
SHARD_AXIS = dict(w_in=2, w_br_dn=2, w_br_cf=2, w_br_s5=2, w_br_gla=2, w_o=1, w_up=2, w_down=1,
                  dn_conv=2, cf_dw=2, gla_w_alpha=2, ffn_conv=2)
SMALL = tuple(n for n in WEIGHTS if n not in LARGE)
SMALL_SHARDED = tuple(n for n in SMALL if n in SHARD_AXIS)


def _local_step(x, target, full):
    ops, saved, xs = [], [], [x]
    for l in range(DEPTH):
        o = _layer_operands({n: full[n][l] for n in WEIGHTS})
        y, s = _layer_fwd(xs[-1], o)
        ops.append(o)
        saved.append(s)
        xs.append(y)
    loss, dy = _loss_head(xs[-1], target)
    dparts, grads = [dy], [None] * DEPTH
    for l in reversed(range(DEPTH)):
        dparts, grads[l] = _layer_bwd(xs[l], ops[l], saved[l], dparts)
    grad_x = _sum_parts('sum_dx', dparts)
    return loss[0, 0], grad_x, {n: jnp.stack([grads[l][n] for l in range(DEPTH)]) for n in WEIGHTS}


def _pack(arrs, rows, dtype=f32):
    flat = jnp.concatenate([a.reshape(-1).astype(dtype) for a in arrs])
    return jnp.pad(flat, (0, rows * 1024 - flat.shape[0])).reshape(rows, 1024)


def _unpack(buf, shapes):
    flat, out, pos = buf.reshape(-1), [], 0
    for shp in shapes:
        n = 1
        for d in shp:
            n *= d
        out.append(flat[pos:pos + n].reshape(shp))
        pos += n
    return out


def _rows_for(shapes, mult):
    n = 0
    for shp in shapes:
        k = 1
        for d in shp:
            k *= d
        n += k
    rows = -(-n // 1024)
    return -(-rows // mult) * mult


def _shard(a, axis, chip):
    size = a.shape[axis] // N_CHIPS
    return lax.dynamic_slice_in_dim(a, chip * size, size, axis)


def kernel(x, w_in, dn_conv, dn_a_log, dn_dt_bias, dn_norm, w_br_dn, cf_dw, cf_dw_bias, cf_ln_g, cf_ln_b, w_br_cf, s5_a_re, s5_a_im, s5_log_dt, s5_b_re, s5_b_im, s5_c_re, s5_c_im, s5_d, w_br_s5, gla_w_alpha, gla_b_alpha, gla_norm, w_br_gla, w_o, ln1_g, ln1_b, w_up, ffn_conv, w_down, ln2_g, ln2_b, loss_target, m_w_in, m_dn_conv, m_dn_a_log, m_dn_dt_bias, m_dn_norm, m_w_br_dn, m_cf_dw, m_cf_dw_bias, m_cf_ln_g, m_cf_ln_b, m_w_br_cf, m_s5_a_re, m_s5_a_im, m_s5_log_dt, m_s5_b_re, m_s5_b_im, m_s5_c_re, m_s5_c_im, m_s5_d, m_w_br_s5, m_gla_w_alpha, m_gla_b_alpha, m_gla_norm, m_w_br_gla, m_w_o, m_ln1_g, m_ln1_b, m_w_up, m_ffn_conv, m_w_down, m_ln2_g, m_ln2_b, v_w_in, v_dn_conv, v_dn_a_log, v_dn_dt_bias, v_dn_norm, v_w_br_dn, v_cf_dw, v_cf_dw_bias, v_cf_ln_g, v_cf_ln_b, v_w_br_cf, v_s5_a_re, v_s5_a_im, v_s5_log_dt, v_s5_b_re, v_s5_b_im, v_s5_c_re, v_s5_c_im, v_s5_d, v_w_br_s5, v_gla_w_alpha, v_gla_b_alpha, v_gla_norm, v_w_br_gla, v_w_o, v_ln1_g, v_ln1_b, v_w_up, v_ffn_conv, v_w_down, v_ln2_g, v_ln2_b):
    env = locals()
    w = {n: env[n] for n in WEIGHTS}
    m = {n: env['m_' + n] for n in WEIGHTS}
    v = {n: env['v_' + n] for n in WEIGHTS}
    chip = 2 * lax.axis_index("x") + lax.axis_index("y")

    large_shapes = [w[n].shape for n in LARGE]
    ssh_shapes = [w[n].shape for n in SMALL_SHARDED]
    large_rows, ssh_rows = _rows_for(large_shapes, 512), _rows_for(ssh_shapes, 8)
    got_large, got_ssh = _gather_chips(_pack([w[n] for n in LARGE], large_rows, bf16),
                                       _pack([w[n] for n in SMALL_SHARDED], ssh_rows))
    full = {n: w[n] for n in SMALL if n not in SHARD_AXIS}
    per_chip = [dict(zip(LARGE + SMALL_SHARDED, _unpack(got_large[k], large_shapes) + _unpack(got_ssh[k], ssh_shapes)))
                for k in range(N_CHIPS)]
    for n in LARGE + SMALL_SHARDED:
        full[n] = jnp.concatenate([per_chip[k][n] for k in range(N_CHIPS)], axis=SHARD_AXIS[n])

    loss, grad_x, g = _local_step(x[0], loss_target[0], full)
    loss = lax.psum(loss, ("x", "y", "c"))

    def piece(k):
        size = lambda n: g[n].shape[SHARD_AXIS[n]] // N_CHIPS
        return _pack([lax.slice_in_dim(g[n], k * size(n), (k + 1) * size(n), axis=SHARD_AXIS[n]) for n in LARGE], large_rows, bf16)

    half = large_rows // 2
    core = lax.axis_index("c")
    pieces = jnp.stack([piece(k) for k in range(N_CHIPS)])
    p_mine = lax.dynamic_slice_in_dim(pieces, core * half, half, axis=1).reshape(N_CHIPS * half, 1024)
    p_theirs = lax.dynamic_slice_in_dim(pieces, (1 - core) * half, half, axis=1).reshape(N_CHIPS * half, 1024)
    pair = _sum_parts('sum_pair', [p_mine, _swap_sibling(p_theirs)], bf16).reshape(N_CHIPS, half, 1024)
    from_chips = _scatter_chips(pair)
    own = lax.dynamic_index_in_dim(pair, chip, 0, keepdims=False)
    chip_sum = _join_halves(_sum_parts('sum_chips', [own, from_chips[0], from_chips[1], from_chips[2]]))
    res = {0: dict(zip(LARGE, _unpack(chip_sum, large_shapes))), 1: {}, 2: {}, 3: {}}
    for n in LARGE:
        two_d = (w[n].shape[0] * w[n].shape[1], w[n].shape[2])
        upd = _adamw('adamw_large', w[n].reshape(two_d), m[n].reshape(two_d), v[n].reshape(two_d), res[0][n].reshape(two_d))
        for kind in range(3):
            res[kind + 1][n] = upd[kind].reshape(w[n].shape)

    small_full_shapes = [g[n].shape for n in SMALL]
    small_rows = _rows_for(small_full_shapes, 8)
    small_sum = _sum_slots('sum_devices', _gather_all(_pack([g[n] for n in SMALL], small_rows)))
    gs = dict(zip(SMALL, _unpack(small_sum, small_full_shapes)))
    for n in SMALL_SHARDED:
        gs[n] = _shard(gs[n], SHARD_AXIS[n], chip)
    small_shapes = [w[n].shape for n in SMALL]
    upd_rows = _rows_for(small_shapes, 8)
    upd = _adamw('adamw_small', _pack([w[n] for n in SMALL], upd_rows), _pack([m[n] for n in SMALL], upd_rows),
                 _pack([v[n] for n in SMALL], upd_rows), _pack([gs[n] for n in SMALL], upd_rows))
    res[0].update(gs)
    for kind in range(3):
        res[kind + 1].update(zip(SMALL, _unpack(upd[kind], small_shapes)))
    return (loss, grad_x[None], *[res[kind][n] for kind in range(4) for n in WEIGHTS])
```

```python
import functools

import jax
import jax.numpy as jnp
from jax import lax
from jax.experimental import pallas as pl
from jax.experimental.pallas import tpu as pltpu

f32 = jnp.float32
bf16 = jnp.bfloat16
HI = lax.Precision.HIGHEST

D_MODEL = 1024
DEPTH = 4
DN_HEADS, DN_DK, DN_CHUNK = 4, 128, 64
GLA_HEADS, GLA_DK, GLA_CHUNK, GLA_TAU = 4, 64, 16, 16.0
GLA_BLOCK = 128
S5_GROUPS, S5_GROUP, S5_STATE = 32, 16, 64
S5_SB = 4
S5_TILE = 256
D_FF = 2816
LN_EPS = 1e-5
ALPHA = (2.0 * DEPTH) ** 0.25
ADAM_LR, ADAM_B1, ADAM_B2, ADAM_EPS, ADAM_WD, ADAM_STEP = 0.001, 0.9, 0.999, 1e-08, 0.01, 10

VMEM_LIMIT_V7X = 56 * 1024 * 1024
MM_A_BLOCK_BYTES = 8 * 1024 * 1024
HALO = 32
N_CHIPS = 4
N_DEV = 8

IN_ORIG = dict(dn_qkv=(0, 1536), dn_a=(1536, 1540), dn_b=(1540, 1544), dn_gate=(1544, 2056), cf=(2056, 3080),
               s5=(3080, 3592), gla_q=(3592, 3848), gla_k=(3848, 4104), gla_v=(4104, 4616), gla_g=(4616, 5128),
               gla_lr=(5128, 5144), gates=(5144, 9240))
IN_COLS = 9240
WM_COLS = 9728
WS_COLS = 128


def _cparams(sem):
    return pltpu.CompilerParams(dimension_semantics=sem, vmem_limit_bytes=VMEM_LIMIT_V7X)


def _pick(dim, pref):
    for t in (pref, 512, 256, 128, 64, 32, 16, 8):
        if t <= pref and dim % t == 0:
            return t
    return dim


NN = (((1,), (0,)), ((), ()))
NT = (((1,), (1,)), ((), ()))
TN = (((0,), (0,)), ((), ()))


def _dot(a, b, dims=NN):
    return lax.dot_general(a, b, dims, precision=HI, preferred_element_type=f32)


def _round(a):
    return a.astype(bf16).astype(f32)


def _rdot(a, b, dims=NN):
    return lax.dot_general(a.astype(bf16), b.astype(bf16), dims, preferred_element_type=f32)


@functools.partial(jax.custom_vjp, nondiff_argnums=(2,))
def _bdot(a, b, dims=NN):
    return _rdot(a, b, dims)


def _bdot_fwd(a, b, dims):
    return _rdot(a, b, dims), (a, b)


def _bdot_bwd(dims, res, g):
    a, b = res
    if dims == NN:
        return _rdot(g, b, NT), _rdot(a, g, TN)
    if dims == NT:
        return _rdot(g, b, NN), _rdot(g, a, TN)
    assert dims == TN
    return _rdot(b, g, NT), _rdot(a, g, NN)


_bdot.defvjp(_bdot_fwd, _bdot_bwd)


def _iota(shape, axis):
    return lax.broadcasted_iota(jnp.int32, shape, axis)


def _mm(a, b, mode, name, tn=512, out_dtype=f32):
    if mode == 'nn':
        (m, k), n = a.shape, b.shape[1]
    elif mode == 'nt':
        (m, k), n = a.shape, b.shape[0]
    else:
        (k, m), n = a.shape, b.shape[1]
    tm, tn = _pick(m, 512 if mode == 'tn' else 1024), _pick(n, tn)
    nk = 1
    while (k // nk) * tm * a.dtype.itemsize > MM_A_BLOCK_BYTES or k % nk or (k // nk) % 128:
        nk += 1
    tk = k // nk
    assert nk == 1 or out_dtype == f32
    if mode == 'nn':
        dims = NN
        a_spec = pl.BlockSpec((tm, tk), lambda i, j, kk: (i, kk))
        b_spec = pl.BlockSpec((tk, tn), lambda i, j, kk: (kk, j))
    elif mode == 'nt':
        dims = NT
        a_spec = pl.BlockSpec((tm, tk), lambda i, j, kk: (i, kk))
        b_spec = pl.BlockSpec((tn, tk), lambda i, j, kk: (j, kk))
    else:
        dims = TN
        a_spec = pl.BlockSpec((tk, tm), lambda i, j, kk: (kk, i))
        b_spec = pl.BlockSpec((tk, tn), lambda i, j, kk: (kk, j))

    def kern(a_ref, b_ref, o_ref):
        p = lax.dot_general(a_ref[...].astype(bf16), b_ref[...].astype(bf16), dims, preferred_element_type=f32)
        if nk == 1:
            o_ref[...] = p.astype(o_ref.dtype)
        else:
            kk = pl.program_id(2)

            @pl.when(kk == 0)
            def _():
                o_ref[...] = p

            @pl.when(kk > 0)
            def _():
                o_ref[...] += p

    return pl.pallas_call(
        kern, name=name, grid=(m // tm, n // tn, nk),
        in_specs=[a_spec, b_spec], out_specs=pl.BlockSpec((tm, tn), lambda i, j, kk: (i, j)),
        out_shape=jax.ShapeDtypeStruct((m, n), out_dtype),
        compiler_params=_cparams(("parallel", "parallel", "arbitrary")),
    )(a, b)


def _full_spec(p):
    nd = p.ndim
    return pl.BlockSpec(p.shape, lambda *_, nd=nd: (0,) * nd)


def _rowwise(name, f, ins, params, out_widths, tile=256, out_dtype=f32):
    rows = ins[0][0].shape[0]
    tile = _pick(rows, tile)
    n_x = len(ins) + len(params)

    def kern(*refs):
        for o_ref, r in zip(refs[n_x:], f(*[r[...] for r in refs[:n_x]])):
            o_ref[...] = r.astype(o_ref.dtype)

    in_specs = [pl.BlockSpec((tile, w), lambda i, c=c: (i, c)) for (_, w, c) in ins] + [_full_spec(p) for p in params]
    return pl.pallas_call(
        kern, name=name, grid=(rows // tile,), in_specs=in_specs,
        out_specs=[pl.BlockSpec((tile, w), lambda i: (i, 0)) for w in out_widths],
        out_shape=[jax.ShapeDtypeStruct((rows, w), out_dtype) for w in out_widths],
        compiler_params=_cparams(("parallel",)),
    )(*[a for (a, _, _) in ins], *params)


def _rowwise_bwd(name, f, ins, params, douts, want, tile=256):
    rows = ins[0][0].shape[0]
    tile = _pick(rows, tile)
    n_in, n_p = len(ins), len(params)
    parts = [p for d in douts for p in d]
    n_x, n_d = n_in + n_p, len(parts)

    def kern(*refs):
        xs = [r[...] for r in refs[:n_x]]
        d_refs, o_refs = refs[n_x:n_x + n_d], refs[n_x + n_d:]
        cts, pos = [], 0
        for d in douts:
            acc = d_refs[pos][...]
            for r in d_refs[pos + 1:pos + len(d)]:
                acc = acc + r[...]
            pos += len(d)
            cts.append(acc)
        grads = jax.vjp(f, *xs)[1](tuple(cts))
        k = 0
        for j in range(n_in):
            if want[j]:
                o_refs[k][...] = grads[j].astype(o_refs[k].dtype)
                k += 1
        first = pl.program_id(0) == 0
        for j in range(n_p):
            g, o_ref = grads[n_in + j], o_refs[k + j]

            @pl.when(first)
            def _(o_ref=o_ref, g=g):
                o_ref[...] = g

            @pl.when(jnp.logical_not(first))
            def _(o_ref=o_ref, g=g):
                o_ref[...] += g

    in_specs = ([pl.BlockSpec((tile, w), lambda i, c=c: (i, c)) for (_, w, c) in ins] + [_full_spec(p) for p in params]
                + [pl.BlockSpec((tile, w), lambda i, c=c: (i, c)) for (_, w, c) in parts])
    out_specs, out_shape = [], []
    for j in range(n_in):
        if want[j]:
            out_specs.append(pl.BlockSpec((tile, ins[j][1]), lambda i: (i, 0)))
            out_shape.append(jax.ShapeDtypeStruct((rows, ins[j][1]), want[j]))
    n_g = len(out_specs)
    for p in params:
        out_specs.append(_full_spec(p))
        out_shape.append(jax.ShapeDtypeStruct(p.shape, f32))
    res = pl.pallas_call(
        kern, name=name, grid=(rows // tile,), in_specs=in_specs, out_specs=out_specs, out_shape=out_shape,
        compiler_params=_cparams(("arbitrary",)),
    )(*[a for (a, _, _) in ins], *params, *[a for (a, _, _) in parts])
    return list(res[:n_g]), list(res[n_g:])


_sigmoid = jax.nn.sigmoid
_silu = jax.nn.silu
_softplus = jax.nn.softplus
_log_sigmoid = jax.nn.log_sigmoid


def _f_ln(x, r, g, b):
    t = ALPHA * x + r
    mu = jnp.mean(t, -1, keepdims=True)
    var = jnp.mean(jnp.square(t - mu), -1, keepdims=True)
    return ((t - mu) * lax.rsqrt(var + LN_EPS) * g + b,)


def _f_pre(c1, hs, a_log, dt_bias, w_alpha, b_alpha):
    s = _silu(c1)
    outs = []
    for j in range(3 * DN_HEADS):
        t = s[:, j * DN_DK:(j + 1) * DN_DK]
        if j < 2 * DN_HEADS:
            t = t * lax.rsqrt(jnp.sum(t * t, -1, keepdims=True) + 1e-6)
        if j < DN_HEADS:
            t = t * (DN_DK ** -0.5)
        outs.append(t)
    qkvn = jnp.concatenate(outs, axis=1)
    lane = _iota(hs.shape, 1)
    g = -jnp.exp(a_log) * _softplus(hs + dt_bias)
    beta = _sigmoid(hs)
    gb = jnp.where(lane < DN_HEADS, g, jnp.where(lane < 2 * DN_HEADS, beta, 0.0))
    la = _log_sigmoid(_bdot(hs, w_alpha) + b_alpha) * (1.0 / GLA_TAU)
    lane5 = _iota(la.shape, 1)
    la = jnp.where((lane5 % 128) < GLA_DK, la, 0.0)
    return qkvn, gb, la


def _f_post(o, gate, w):
    outs = []
    for j in range(4):
        t = o[:, j * 128:(j + 1) * 128]
        outs.append(t * lax.rsqrt(jnp.mean(t * t, -1, keepdims=True) + LN_EPS) * w)
    return (jnp.concatenate(outs, axis=1) * _silu(gate),)


def _f_glu(a, g):
    return (a * _sigmoid(g),)


def _f_cfpost(c, bias, g, b):
    t = c + bias
    mu = jnp.mean(t, -1, keepdims=True)
    var = jnp.mean(jnp.square(t - mu), -1, keepdims=True)
    return (_silu((t - mu) * lax.rsqrt(var + LN_EPS) * g + b),)


def _f_gelu(y):
    return (jax.nn.gelu(y),)


def _f_merge(ga, gb_, gc, gd, ya, yb, zv, zg, yd):
    return (_sigmoid(ga) * ya + _sigmoid(gb_) * yb + _sigmoid(gc) * (zv * _sigmoid(zg)) + _sigmoid(gd) * yd,)


def _f_act(a, b):
    return (_silu(a) * b,)


def _conv_tiles(rows, ch):
    return _pick(rows, 256), _pick(ch, 512)


def _conv_fwd(name, x, width, colblk0, w, taps):
    rows = x.shape[0]
    tr, cb = _conv_tiles(rows, width)
    hb = tr // HALO

    def kern(prev_ref, x_ref, w_ref, o_ref, ext):
        i = pl.program_id(1)
        ext[pl.ds(0, HALO), :] = jnp.where(i > 0, _round(prev_ref[...]), 0.0)
        ext[pl.ds(HALO, tr), :] = _round(x_ref[...])
        wv = _round(w_ref[...])
        acc = jnp.zeros((tr, cb), f32)
        for k in range(taps):
            acc = acc + wv[k:k + 1, :] * ext[pl.ds(HALO - taps + 1 + k, tr), :]
        o_ref[...] = acc

    c0 = colblk0 * (width // cb)
    return pl.pallas_call(
        kern, name=name, grid=(width // cb, rows // tr),
        in_specs=[pl.BlockSpec((HALO, cb), lambda c, i: (jnp.maximum(i * hb - 1, 0), c0 + c)),
                  pl.BlockSpec((tr, cb), lambda c, i: (i, c0 + c)),
                  pl.BlockSpec((w.shape[0], cb), lambda c, i: (0, c))],
        out_specs=pl.BlockSpec((tr, cb), lambda c, i: (i, c)),
        out_shape=jax.ShapeDtypeStruct((rows, width), f32),
        scratch_shapes=[pltpu.VMEM((HALO + tr, cb), f32)],
        compiler_params=_cparams(("parallel", "arbitrary")),
    )(x, x, w)


def _conv_bwd(name, x, width, colblk0, w, taps, dy, dx_dtype=f32):
    rows = x.shape[0]
    tr, cb = _conv_tiles(rows, width)
    hb = tr // HALO
    nt = rows // tr
    wr = w.shape[0]

    def kern(prev_ref, x_ref, w_ref, dy_ref, next_ref, dx_ref, dw_ref, ext, dext):
        i = pl.program_id(1)
        ext[pl.ds(0, HALO), :] = jnp.where(i > 0, _round(prev_ref[...]), 0.0)
        ext[pl.ds(HALO, tr), :] = _round(x_ref[...])
        dyv = _round(dy_ref[...])
        dext[pl.ds(0, tr), :] = dyv
        dext[pl.ds(tr, HALO), :] = jnp.where(i < nt - 1, _round(next_ref[...]), 0.0)
        wv = _round(w_ref[...])
        acc = jnp.zeros((tr, cb), f32)
        rows_w = []
        for k in range(taps):
            acc = acc + wv[k:k + 1, :] * dext[pl.ds(taps - 1 - k, tr), :]
            rows_w.append(jnp.sum(dyv * ext[pl.ds(HALO - taps + 1 + k, tr), :], axis=0, keepdims=True))
        dx_ref[...] = acc.astype(dx_ref.dtype)
        if wr > taps:
            rows_w.append(jnp.zeros((wr - taps, cb), f32))
        dwv = jnp.concatenate(rows_w, axis=0)

        @pl.when(i == 0)
        def _():
            dw_ref[...] = dwv

        @pl.when(i > 0)
        def _():
            dw_ref[...] += dwv

    c0 = colblk0 * (width // cb)
    return pl.pallas_call(
        kern, name=name, grid=(width // cb, nt),
        in_specs=[pl.BlockSpec((HALO, cb), lambda c, i: (jnp.maximum(i * hb - 1, 0), c0 + c)),
                  pl.BlockSpec((tr, cb), lambda c, i: (i, c0 + c)),
                  pl.BlockSpec((wr, cb), lambda c, i: (0, c)),
                  pl.BlockSpec((tr, cb), lambda c, i: (i, c)),
                  pl.BlockSpec((HALO, cb), lambda c, i: (jnp.minimum((i + 1) * hb, nt * hb - 1), c))],
        out_specs=[pl.BlockSpec((tr, cb), lambda c, i: (i, c)), pl.BlockSpec((wr, cb), lambda c, i: (0, c))],
        out_shape=[jax.ShapeDtypeStruct((rows, width), dx_dtype), jax.ShapeDtypeStruct((wr, width), f32)],
        scratch_shapes=[pltpu.VMEM((HALO + tr, cb), f32), pltpu.VMEM((HALO + tr, cb), f32)],
        compiler_params=_cparams(("parallel", "arbitrary")),
    )(x, x, w, dy, dy)


def _series_inverse(neg):
    n = neg.shape[0]
    inv = jnp.where(_iota((n, n), 0) == _iota((n, n), 1), 1.0, 0.0) + neg
    p = neg
    for _ in range(5):
        p = _dot(p, p)
        inv = inv + _dot(inv, p)
    return inv


def _inverse_bwd(inv, g):
    return _dot(_dot(inv, g, TN), inv, NT)


@jax.custom_vjp
def _unit_lower_inverse(neg):
    return _series_inverse(neg)


_unit_lower_inverse.defvjp(lambda neg: (_series_inverse(neg),) * 2, lambda inv, g: (_inverse_bwd(inv, g),))


@jax.custom_vjp
def _known_inverse(neg, inv):
    return inv


_known_inverse.defvjp(lambda neg, inv: (inv, inv), lambda inv, g: (_inverse_bwd(inv, g), jnp.zeros_like(inv)))


def _dn_head(state, q, k, v, gc, gl, beta, inv_saved=None):
    c = DN_CHUNK
    ii, jj = _iota((c, c), 0), _iota((c, c), 1)
    causal, strict = ii >= jj, ii > jj
    gcb = jnp.broadcast_to(gc, (c, c))
    decay = jnp.where(causal, jnp.exp(jnp.where(causal, gcb - gcb.T, 0.0)), 0.0)
    kb = k * beta
    neg = -jnp.where(strict, _bdot(kb, k, NT) * decay, 0.0)
    inv = _unit_lower_inverse(neg) if inv_saved is None else _known_inverse(neg, inv_saved)
    egc = jnp.exp(gc)
    u = _dot(inv, v * beta)
    w = _dot(inv, kb * egc)
    intra = _bdot(q, k, NT) * decay
    v_new = u - _bdot(w, state)
    o = _bdot(q * egc, state) + _bdot(intra, v_new)
    new_state = state * jnp.exp(gl) + _bdot(k * jnp.exp(gl - gc), v_new, TN)
    return (o, new_state, inv) if inv_saved is None else (o, new_state)


def _dn_cum(gb):
    c = DN_CHUNK
    tril = jnp.where(_iota((c, c), 0) >= _iota((c, c), 1), 1.0, 0.0)
    return _dot(tril, gb), jnp.sum(gb, axis=0, keepdims=True)


def _dn_fwd(qkvn, gb):
    rows = qkvn.shape[0]
    c, h, d = DN_CHUNK, DN_HEADS, DN_DK
    nc = rows // c

    def kern(qkv_ref, gb_ref, o_ref, st_ref, inv_ref, state):
        @pl.when(pl.program_id(0) == 0)
        def _():
            state[...] = jnp.zeros_like(state)

        gbv = gb_ref[...]
        cum, tot = _dn_cum(gbv)
        for j in range(h):
            st = state[j]
            st_ref[0, j] = st
            o, new, inv = _dn_head(st, qkv_ref[:, j * d:(j + 1) * d], qkv_ref[:, (h + j) * d:(h + j + 1) * d],
                                   qkv_ref[:, (2 * h + j) * d:(2 * h + j + 1) * d],
                                   cum[:, j:j + 1], tot[:, j:j + 1], gbv[:, h + j:h + j + 1])
            o_ref[:, j * d:(j + 1) * d] = o
            inv_ref[0, j] = inv
            state[j] = new

    return pl.pallas_call(
        kern, name="dn_fwd", grid=(nc,),
        in_specs=[pl.BlockSpec((c, 3 * h * d), lambda i: (i, 0)), pl.BlockSpec((c, 128), lambda i: (i, 0))],
        out_specs=[pl.BlockSpec((c, h * d), lambda i: (i, 0)), pl.BlockSpec((1, h, d, d), lambda i: (i, 0, 0, 0)),
                   pl.BlockSpec((1, h, c, c), lambda i: (i, 0, 0, 0))],
        out_shape=[jax.ShapeDtypeStruct((rows, h * d), f32), jax.ShapeDtypeStruct((nc, h, d, d), f32),
                   jax.ShapeDtypeStruct((nc, h, c, c), f32)],
        scratch_shapes=[pltpu.VMEM((h, d, d), f32)],
        compiler_params=_cparams(("arbitrary",)),
    )(qkvn, gb)


def _dn_bwd(qkvn, gb, states, invs, do):
    rows = qkvn.shape[0]
    c, h, d = DN_CHUNK, DN_HEADS, DN_DK
    nc = rows // c

    def kern(qkv_ref, gb_ref, st_ref, inv_ref, do_ref, dqkv_ref, dgb_ref, dstate):
        @pl.when(pl.program_id(0) == 0)
        def _():
            dstate[...] = jnp.zeros_like(dstate)

        gbv = gb_ref[...]
        cum, tot = _dn_cum(gbv)
        lane = _iota((c, 128), 1)
        dcum = jnp.zeros((c, 128), f32)
        dgb = jnp.zeros((c, 128), f32)
        for j in range(h):
            args = (st_ref[0, j], qkv_ref[:, j * d:(j + 1) * d], qkv_ref[:, (h + j) * d:(h + j + 1) * d],
                    qkv_ref[:, (2 * h + j) * d:(2 * h + j + 1) * d],
                    cum[:, j:j + 1], tot[:, j:j + 1], gbv[:, h + j:h + j + 1])
            head = functools.partial(_dn_head, inv_saved=inv_ref[0, j])
            ds, dq, dk, dv, dgc, dgl, dbeta = jax.vjp(head, *args)[1]((do_ref[:, j * d:(j + 1) * d], dstate[j]))
            dstate[j] = ds
            dqkv_ref[:, j * d:(j + 1) * d] = dq
            dqkv_ref[:, (h + j) * d:(h + j + 1) * d] = dk
            dqkv_ref[:, (2 * h + j) * d:(2 * h + j + 1) * d] = dv
            dcum = dcum + jnp.where(lane == j, dgc, 0.0)
            dgb = dgb + jnp.where(lane == j, dgl, 0.0) + jnp.where(lane == h + j, dbeta, 0.0)
        triu = jnp.where(_iota((c, c), 0) <= _iota((c, c), 1), 1.0, 0.0)
        dgb_ref[...] = dgb + _dot(triu, dcum)

    rev = lambda i: (nc - 1 - i, 0)
    return pl.pallas_call(
        kern, name="dn_bwd", grid=(nc,),
        in_specs=[pl.BlockSpec((c, 3 * h * d), rev), pl.BlockSpec((c, 128), rev),
                  pl.BlockSpec((1, h, d, d), lambda i: (nc - 1 - i, 0, 0, 0)),
                  pl.BlockSpec((1, h, c, c), lambda i: (nc - 1 - i, 0, 0, 0)), pl.BlockSpec((c, h * d), rev)],
        out_specs=[pl.BlockSpec((c, 3 * h * d), rev), pl.BlockSpec((c, 128), rev)],
        out_shape=[jax.ShapeDtypeStruct((rows, 3 * h * d), f32), jax.ShapeDtypeStruct((rows, 128), f32)],
        scratch_shapes=[pltpu.VMEM((h, d, d), f32)],
        compiler_params=_cparams(("arbitrary",)),
    )(qkvn, gb, states, invs, do)


def _gla_block(state_t, q, k, v, la):
    c = GLA_CHUNK
    ii, jj = _iota((c, c), 0), _iota((c, c), 1)
    causal = ii >= jj
    tril = jnp.where(causal, 1.0, 0.0)
    outs = []
    for n in range(q.shape[0] // c):
        sl = slice(n * c, (n + 1) * c)
        qn, kn, vn, ln = q[sl] * (GLA_DK ** -0.5), k[sl], v[sl], la[sl]
        gc = _dot(tril, ln)
        gl = jnp.sum(ln, axis=0, keepdims=True)
        q_dec = qn * jnp.exp(gc)
        scores = jnp.where(causal, _bdot(q_dec, kn * jnp.exp(-gc), NT), 0.0)
        outs.append(_bdot(q_dec, state_t, NT) + _bdot(scores, vn))
        state_t = state_t * jnp.exp(gl) + _bdot(vn, kn * jnp.exp(gl - gc), TN)
    return jnp.concatenate(outs, axis=0), state_t


GLA_Q0, GLA_K0, GLA_V0 = 7680 // 512, 8192 // 512, 8704 // 512


def _gla_fwd(hmain, la):
    rows = hmain.shape[0]
    b, h = _pick(rows, GLA_BLOCK), GLA_HEADS
    nb = rows // b

    def kern(q_ref, k_ref, v_ref, la_ref, o_ref, st_ref, state):
        @pl.when(pl.program_id(0) == 0)
        def _():
            state[...] = jnp.zeros_like(state)

        for j in range(h):
            sl = slice(j * 128, (j + 1) * 128)
            st = state[j]
            st_ref[0, j] = st
            o, new = _gla_block(st, q_ref[:, sl], k_ref[:, sl], v_ref[:, sl], la_ref[:, sl])
            o_ref[:, sl] = o
            state[j] = new

    return pl.pallas_call(
        kern, name="gla_fwd", grid=(nb,),
        in_specs=[pl.BlockSpec((b, 512), lambda i: (i, GLA_Q0)), pl.BlockSpec((b, 512), lambda i: (i, GLA_K0)),
                  pl.BlockSpec((b, 512), lambda i: (i, GLA_V0)), pl.BlockSpec((b, 512), lambda i: (i, 0))],
        out_specs=[pl.BlockSpec((b, 512), lambda i: (i, 0)), pl.BlockSpec((1, h, 128, 128), lambda i: (i, 0, 0, 0))],
        out_shape=[jax.ShapeDtypeStruct((rows, h * 128), f32), jax.ShapeDtypeStruct((nb, h, 128, 128), f32)],
        scratch_shapes=[pltpu.VMEM((h, 128, 128), f32)],
        compiler_params=_cparams(("arbitrary",)),
    )(hmain, hmain, hmain, la)


def _gla_bwd(hmain, la, states, do):
    rows = hmain.shape[0]
    b, h = _pick(rows, GLA_BLOCK), GLA_HEADS
    nb = rows // b

    def kern(q_ref, k_ref, v_ref, la_ref, st_ref, do_ref, dq_ref, dk_ref, dv_ref, dla_ref, dstate):
        @pl.when(pl.program_id(0) == 0)
        def _():
            dstate[...] = jnp.zeros_like(dstate)

        for j in range(h):
            sl = slice(j * 128, (j + 1) * 128)
            args = (st_ref[0, j], q_ref[:, sl], k_ref[:, sl], v_ref[:, sl], la_ref[:, sl])
            ds, dq, dk, dv, dla = jax.vjp(_gla_block, *args)[1]((do_ref[:, sl], dstate[j]))
            dstate[j] = ds
            dq_ref[:, sl] = dq.astype(bf16)
            dk_ref[:, sl] = dk.astype(bf16)
            dv_ref[:, sl] = dv.astype(bf16)
            dla_ref[:, sl] = dla

    rev = lambda i: (nb - 1 - i, 0)
    return pl.pallas_call(
        kern, name="gla_bwd", grid=(nb,),
        in_specs=[pl.BlockSpec((b, 512), lambda i: (nb - 1 - i, GLA_Q0)), pl.BlockSpec((b, 512), lambda i: (nb - 1 - i, GLA_K0)),
                  pl.BlockSpec((b, 512), lambda i: (nb - 1 - i, GLA_V0)), pl.BlockSpec((b, 512), rev),
                  pl.BlockSpec((1, h, 128, 128), lambda i: (nb - 1 - i, 0, 0, 0)), pl.BlockSpec((b, 512), rev)],
        out_specs=[pl.BlockSpec((b, 512), rev)] * 4,
        out_shape=[jax.ShapeDtypeStruct((rows, h * 128), bf16)] * 3 + [jax.ShapeDtypeStruct((rows, h * 128), f32)],
        scratch_shapes=[pltpu.VMEM((h, 128, 128), f32)],
        compiler_params=_cparams(("arbitrary",)),
    )(hmain, hmain, hmain, la, states, do)


def _f_s5_params(a_re, a_im, log_dt, b_re, b_im):
    dt = jnp.exp(log_dt)
    mag = jnp.exp(dt * a_re)
    abar_re, abar_im = mag * jnp.cos(dt * a_im), mag * jnp.sin(dt * a_im)
    den = a_re * a_re + a_im * a_im
    nr, ni = abar_re - 1.0, abar_im
    fr, fi = (nr * a_re + ni * a_im) / den, (ni * a_re - nr * a_im) / den
    return abar_re, abar_im, fr[None] * b_re - fi[None] * b_im, fr[None] * b_im + fi[None] * b_re


def _s5_params(a_re, a_im, log_dt, b_re, b_im):
    def kern(*refs):
        for o_ref, r in zip(refs[5:], _f_s5_params(*[r[...] for r in refs[:5]])):
            o_ref[...] = r

    ins = (a_re, a_im, log_dt, b_re, b_im)
    return pl.pallas_call(
        kern, name="s5_params", out_shape=[jax.ShapeDtypeStruct(a_re.shape, f32)] * 2 + [jax.ShapeDtypeStruct(b_re.shape, f32)] * 2,
    )(*ins)


def _s5_params_bwd(a_re, a_im, log_dt, b_re, b_im, d_ar, d_ai, d_br, d_bi):
    def kern(*refs):
        grads = jax.vjp(_f_s5_params, *[r[...] for r in refs[:5]])[1](tuple(r[...] for r in refs[5:9]))
        for o_ref, g in zip(refs[9:], grads):
            o_ref[...] = g

    ins = (a_re, a_im, log_dt, b_re, b_im)
    return pl.pallas_call(
        kern, name="s5_params_bwd", out_shape=[jax.ShapeDtypeStruct(t.shape, f32) for t in ins],
    )(*ins, d_ar, d_ai, d_br, d_bi)


def _cmul(ar, ai, br, bi):
    return ar * br - ai * bi, ar * bi + ai * br


def _s5_scan(xr, xi, ar, ai, reverse):
    t = xr.shape[0]
    row = _iota(xr.shape, 0)
    s = 1
    while s < t:
        if reverse:
            keep = row < t - s
            sr, si = pltpu.roll(xr, t - s, 0), pltpu.roll(xi, t - s, 0)
        else:
            keep = row >= s
            sr, si = pltpu.roll(xr, s, 0), pltpu.roll(xi, s, 0)
        sr, si = jnp.where(keep, sr, 0.0), jnp.where(keep, si, 0.0)
        pr, pi = _cmul(ar, ai, sr, si)
        xr, xi = xr + pr, xi + pi
        ar, ai = _cmul(ar, ai, ar, ai)
        s *= 2
    return xr, xi


def _s5_powers(ar, ai, t, reverse):
    row = _iota((t, ar.shape[1]), 0)
    at = (row == (t - 1 if reverse else 0))
    return _s5_scan(jnp.where(at, ar, 0.0), jnp.where(at, ai, 0.0), ar, ai, reverse)


S5_U0 = 7168 // 128


def _s5_fwd(hmain, abar, bmat_re, bmat_im, cmat_re, cmat_im, dvec):
    rows = hmain.shape[0]
    t = _pick(rows, S5_TILE)
    nt, ns = rows // t, 512

    def kern(u_ref, a_ref, br_ref, bi_ref, cr_ref, ci_ref, d_ref, y_ref, xr_ref, xi_ref, pw, carry):
        ar, ai = a_ref[0, 0], a_ref[1, 0]

        @pl.when(pl.program_id(1) == 0)
        def _():
            pr, pi = _s5_powers(ar, ai, t, False)
            pw[0], pw[1] = pr, pi
            carry[...] = jnp.zeros_like(carry)

        u = u_ref[...]
        xr, xi = _s5_scan(_rdot(u, br_ref[0]), _rdot(u, bi_ref[0]), ar, ai, False)
        cr, ci = carry[0:1, :], carry[1:2, :]
        qr, qi = _cmul(pw[0], pw[1], cr, ci)
        xr, xi = xr + qr, xi + qi
        xr_ref[...] = xr
        xi_ref[...] = xi
        carry[0:1, :] = xr[t - 1:t, :]
        carry[1:2, :] = xi[t - 1:t, :]
        y_ref[...] = _rdot(xr, cr_ref[0]) - _rdot(xi, ci_ref[0]) + d_ref[...] * u

    sb3 = lambda b, i: (b, 0, 0)
    return pl.pallas_call(
        kern, name="s5_fwd", grid=(S5_SB, nt),
        in_specs=[pl.BlockSpec((t, 128), lambda b, i: (i, S5_U0 + b)), pl.BlockSpec((2, 1, 1, ns), lambda b, i: (0, b, 0, 0)),
                  pl.BlockSpec((1, 128, ns), sb3), pl.BlockSpec((1, 128, ns), sb3),
                  pl.BlockSpec((1, ns, 128), sb3), pl.BlockSpec((1, ns, 128), sb3), pl.BlockSpec((1, 128), lambda b, i: (0, b))],
        out_specs=[pl.BlockSpec((t, 128), lambda b, i: (i, b)), pl.BlockSpec((t, ns), lambda b, i: (i, b)),
                   pl.BlockSpec((t, ns), lambda b, i: (i, b))],
        out_shape=[jax.ShapeDtypeStruct((rows, 512), f32), jax.ShapeDtypeStruct((rows, S5_SB * ns), f32),
                   jax.ShapeDtypeStruct((rows, S5_SB * ns), f32)],
        scratch_shapes=[pltpu.VMEM((2, t, ns), f32), pltpu.VMEM((8, ns), f32)],
        compiler_params=_cparams(("parallel", "arbitrary")),
    )(hmain, abar, bmat_re, bmat_im, cmat_re, cmat_im, dvec)


def _s5_bwd(hmain, abar, bmat_re, bmat_im, cmat_re, cmat_im, dvec, x_re, x_im, dy):
    rows = hmain.shape[0]
    t = _pick(rows, S5_TILE)
    nt, ns = rows // t, 512
    t8 = t // 8

    def kern(u_ref, a_ref, br_ref, bi_ref, cr_ref, ci_ref, d_ref, xr_ref, xi_ref, xpr_ref, xpi_ref, dy_ref,
             du_ref, da_ref, dbr_ref, dbi_ref, dcr_ref, dci_ref, dd_ref, pw, carry):
        i = pl.program_id(1)
        ar, ai = a_ref[0, 0], -a_ref[1, 0]

        @pl.when(i == 0)
        def _():
            pr, pi = _s5_powers(ar, ai, t, True)
            pw[0], pw[1] = pr, pi
            carry[...] = jnp.zeros_like(carry)

        u, gy = u_ref[...], dy_ref[...]
        lr, li = _s5_scan(_rdot(gy, cr_ref[0], NT), -_rdot(gy, ci_ref[0], NT), ar, ai, True)
        qr, qi = _cmul(pw[0], pw[1], carry[0:1, :], carry[1:2, :])
        lr, li = lr + qr, li + qi
        carry[0:1, :] = lr[0:1, :]
        carry[1:2, :] = li[0:1, :]
        du_ref[...] = (_rdot(lr, br_ref[0], NT) + _rdot(li, bi_ref[0], NT) + d_ref[...] * gy).astype(bf16)
        xr, xi = xr_ref[...], xi_ref[...]
        row = _iota(xr.shape, 0)
        first_r = jnp.where(i < nt - 1, xpr_ref[7:8, :], 0.0)
        first_i = jnp.where(i < nt - 1, xpi_ref[7:8, :], 0.0)
        xpr = jnp.where(row == 0, first_r, pltpu.roll(xr, 1, 0))
        xpi = jnp.where(row == 0, first_i, pltpu.roll(xi, 1, 0))
        da_r = jnp.sum(lr * xpr + li * xpi, axis=0, keepdims=True)
        da_i = jnp.sum(li * xpr - lr * xpi, axis=0, keepdims=True)
        upd = [(da_ref.at[0, 0], da_r), (da_ref.at[1, 0], da_i),
               (dbr_ref.at[0], _rdot(u, lr, TN)), (dbi_ref.at[0], _rdot(u, li, TN)),
               (dcr_ref.at[0], _rdot(xr, gy, TN)), (dci_ref.at[0], -_rdot(xi, gy, TN)),
               (dd_ref, jnp.sum(gy * u, axis=0, keepdims=True))]

        @pl.when(i == 0)
        def _():
            for ref, val in upd:
                ref[...] = val

        @pl.when(i > 0)
        def _():
            for ref, val in upd:
                ref[...] += val

    sb3 = lambda b, i: (b, 0, 0)
    rev = lambda b, i: (nt - 1 - i, b)
    prev8 = lambda b, i: (jnp.maximum((nt - 1 - i) * t8 - 1, 0), b)
    return pl.pallas_call(
        kern, name="s5_bwd", grid=(S5_SB, nt),
        in_specs=[pl.BlockSpec((t, 128), lambda b, i: (nt - 1 - i, S5_U0 + b)), pl.BlockSpec((2, 1, 1, ns), lambda b, i: (0, b, 0, 0)),
                  pl.BlockSpec((1, 128, ns), sb3), pl.BlockSpec((1, 128, ns), sb3),
                  pl.BlockSpec((1, ns, 128), sb3), pl.BlockSpec((1, ns, 128), sb3), pl.BlockSpec((1, 128), lambda b, i: (0, b)),
                  pl.BlockSpec((t, ns), rev), pl.BlockSpec((t, ns), rev), pl.BlockSpec((8, ns), prev8), pl.BlockSpec((8, ns), prev8),
                  pl.BlockSpec((t, 128), rev)],
        out_specs=[pl.BlockSpec((t, 128), rev), pl.BlockSpec((2, 1, 1, ns), lambda b, i: (0, b, 0, 0)),
                   pl.BlockSpec((1, 128, ns), sb3), pl.BlockSpec((1, 128, ns), sb3),
                   pl.BlockSpec((1, ns, 128), sb3), pl.BlockSpec((1, ns, 128), sb3), pl.BlockSpec((1, 128), lambda b, i: (0, b))],
        out_shape=[jax.ShapeDtypeStruct((rows, 512), bf16), jax.ShapeDtypeStruct((2, S5_SB, 1, ns), f32),
                   jax.ShapeDtypeStruct((S5_SB, 128, ns), f32), jax.ShapeDtypeStruct((S5_SB, 128, ns), f32),
                   jax.ShapeDtypeStruct((S5_SB, ns, 128), f32), jax.ShapeDtypeStruct((S5_SB, ns, 128), f32),
                   jax.ShapeDtypeStruct((1, 512), f32)],
        scratch_shapes=[pltpu.VMEM((2, t, ns), f32), pltpu.VMEM((8, ns), f32)],
        compiler_params=_cparams(("parallel", "arbitrary")),
    )(hmain, abar, bmat_re, bmat_im, cmat_re, cmat_im, dvec, x_re, x_im, x_re, x_im, dy)


def _loss_head(y, target):
    rows, feat = y.shape
    tile = _pick(rows, 256)

    def kern(y_ref, t_ref, l_ref, dy_ref):
        e = y_ref[...] - t_ref[...]
        dy_ref[...] = e * (1.0 / feat)
        part = jnp.broadcast_to(0.5 * jnp.sum(e * e) * (1.0 / feat), l_ref.shape)

        @pl.when(pl.program_id(0) == 0)
        def _():
            l_ref[...] = part

        @pl.when(pl.program_id(0) > 0)
        def _():
            l_ref[...] += part

    return pl.pallas_call(
        kern, name="loss_head", grid=(rows // tile,),
        in_specs=[pl.BlockSpec((tile, feat), lambda i: (i, 0))] * 2,
        out_specs=[pl.BlockSpec((8, 128), lambda i: (0, 0)), pl.BlockSpec((tile, feat), lambda i: (i, 0))],
        out_shape=[jax.ShapeDtypeStruct((8, 128), f32), jax.ShapeDtypeStruct((rows, feat), f32)],
        compiler_params=_cparams(("arbitrary",)),
    )(y, target)


def _sum_parts(name, parts, out_dtype=f32):
    rows, cols = parts[0].shape
    tile = _pick(rows, 512)

    def kern(*refs):
        acc = refs[0][...].astype(f32)
        for r in refs[1:-1]:
            acc = acc + r[...].astype(f32)
        refs[-1][...] = acc.astype(out_dtype)

    return pl.pallas_call(
        kern, name=name, grid=(rows // tile,),
        in_specs=[pl.BlockSpec((tile, cols), lambda i: (i, 0))] * len(parts),
        out_specs=pl.BlockSpec((tile, cols), lambda i: (i, 0)),
        out_shape=jax.ShapeDtypeStruct((rows, cols), out_dtype),
        compiler_params=_cparams(("parallel",)),
    )(*parts)


ADAMW_BLOCK_BYTES = 1536 * 1024


def _adamw(name, w, m, v, g):
    rows, cols = w.shape
    tile = _pick(rows, 512)
    while tile > 8 and tile * cols * 4 > ADAMW_BLOCK_BYTES and rows % (tile // 2) == 0:
        tile //= 2
    c1, c2 = 1.0 / (1.0 - ADAM_B1 ** ADAM_STEP), 1.0 / (1.0 - ADAM_B2 ** ADAM_STEP)

    def kern(w_ref, m_ref, v_ref, g_ref, d_ref, nm_ref, nv_ref):
        gv = g_ref[...]
        nm = ADAM_B1 * m_ref[...] + (1.0 - ADAM_B1) * gv
        nv = ADAM_B2 * v_ref[...] + (1.0 - ADAM_B2) * (gv * gv)
        nm_ref[...] = nm
        nv_ref[...] = nv
        d_ref[...] = -ADAM_LR * ((nm * c1) / (jnp.sqrt(nv * c2) + ADAM_EPS) + ADAM_WD * w_ref[...])

    spec = pl.BlockSpec((tile, cols), lambda i: (i, 0))
    return pl.pallas_call(
        kern, name=name, grid=(rows // tile,), in_specs=[spec] * 4, out_specs=[spec] * 3,
        out_shape=[jax.ShapeDtypeStruct((rows, cols), f32)] * 3,
        compiler_params=_cparams(("parallel",)),
    )(w, m, v, g)


MESH = pl.DeviceIdType.MESH
HBM_SPEC = pl.BlockSpec(memory_space=pltpu.HBM)


def _other_chips(x, y):
    return [(1 - x, y), (x, 1 - y), (1 - x, 1 - y)]


def _gather_chips(big, small):
    rows, cols = big.shape
    half = rows // 2

    def kern(big_ref, small_ref, bout, sout, ici_send, ici_recv, d2d_send, d2d_recv, small_send, small_recv, local_sems):
        x, y, c = lax.axis_index("x"), lax.axis_index("y"), lax.axis_index("c")
        me, sibling = 2 * x + y, (x, y, 1 - c)
        chips = _other_chips(x, y)
        slots = [2 * chip[0] + chip[1] for chip in chips]
        local = [pltpu.make_async_copy(big_ref, bout.at[me], local_sems.at[0]),
                 pltpu.make_async_copy(small_ref, sout.at[me], local_sems.at[1])]
        for cp in local:
            cp.start()
        sends = [pltpu.make_async_remote_copy(
            src_ref=big_ref.at[c], dst_ref=bout.at[me, c], send_sem=ici_send.at[k], recv_sem=ici_recv.at[k],
            device_id=(*chip, c), device_id_type=MESH) for k, chip in enumerate(chips)]
        sends += [pltpu.make_async_remote_copy(
            src_ref=small_ref, dst_ref=sout.at[me], send_sem=small_send.at[k], recv_sem=small_recv.at[k],
            device_id=(*chip, c), device_id_type=MESH) for k, chip in enumerate(chips)]
        for cp in sends:
            cp.start()
        for k, chip in enumerate(chips):
            pltpu.make_async_remote_copy(
                src_ref=big_ref.at[c], dst_ref=bout.at[slots[k], c], send_sem=ici_send.at[k], recv_sem=ici_recv.at[k],
                device_id=(*chip, c), device_id_type=MESH).wait_recv()
            passed = pltpu.make_async_remote_copy(
                src_ref=bout.at[slots[k], c], dst_ref=bout.at[slots[k], c], send_sem=d2d_send.at[k],
                recv_sem=d2d_recv.at[k], device_id=sibling, device_id_type=MESH)
            passed.start()
            sends.append(passed)
        for k, chip in enumerate(chips):
            pltpu.make_async_remote_copy(
                src_ref=bout.at[slots[k], 1 - c], dst_ref=bout.at[slots[k], 1 - c], send_sem=d2d_send.at[k],
                recv_sem=d2d_recv.at[k], device_id=sibling, device_id_type=MESH).wait_recv()
            pltpu.make_async_remote_copy(
                src_ref=small_ref, dst_ref=sout.at[slots[k]], send_sem=small_send.at[k], recv_sem=small_recv.at[k],
                device_id=(*chip, c), device_id_type=MESH).wait_recv()
        for cp in sends:
            cp.wait_send()
        for cp in local:
            cp.wait()

    got_big, got_small = pl.pallas_call(
        kern, name="gather_chips", in_specs=[HBM_SPEC] * 2, out_specs=[HBM_SPEC] * 2,
        out_shape=[jax.ShapeDtypeStruct((N_CHIPS, 2, half, cols), big.dtype), jax.ShapeDtypeStruct((N_CHIPS,) + small.shape, small.dtype)],
        scratch_shapes=[pltpu.SemaphoreType.DMA((3,))] * 6 + [pltpu.SemaphoreType.DMA((2,))],
    )(big.reshape(2, half, cols), small)
    return got_big.reshape(N_CHIPS, rows, cols), got_small


def _join_halves(part):
    half, cols = part.shape

    def kern(p_ref, out_ref, send_sem, recv_sem, local_sem):
        x, y, c = lax.axis_index("x"), lax.axis_index("y"), lax.axis_index("c")
        local = pltpu.make_async_copy(p_ref, out_ref.at[c], local_sem)
        local.start()
        cp = pltpu.make_async_remote_copy(src_ref=p_ref, dst_ref=out_ref.at[c], send_sem=send_sem, recv_sem=recv_sem,
                                          device_id=(x, y, 1 - c), device_id_type=MESH)
        cp.start()
        pltpu.make_async_remote_copy(src_ref=p_ref, dst_ref=out_ref.at[1 - c], send_sem=send_sem, recv_sem=recv_sem,
                                     device_id=(x, y, 1 - c), device_id_type=MESH).wait_recv()
        cp.wait_send()
        local.wait()

    return pl.pallas_call(
        kern, name="join_halves", in_specs=[HBM_SPEC], out_specs=HBM_SPEC,
        out_shape=jax.ShapeDtypeStruct((2, half, cols), part.dtype),
        scratch_shapes=[pltpu.SemaphoreType.DMA, pltpu.SemaphoreType.DMA, pltpu.SemaphoreType.DMA],
    )(part).reshape(2 * half, cols)


def _scatter_chips(pieces):
    _, rows, cols = pieces.shape

    def kern(p_ref, out_ref, send_sems, recv_sems):
        x, y, c = lax.axis_index("x"), lax.axis_index("y"), lax.axis_index("c")
        chips = _other_chips(x, y)
        sends = [pltpu.make_async_remote_copy(
            src_ref=p_ref.at[2 * chip[0] + chip[1]], dst_ref=out_ref.at[k], send_sem=send_sems.at[k],
            recv_sem=recv_sems.at[k], device_id=(*chip, c), device_id_type=MESH) for k, chip in enumerate(chips)]
        for cp in sends:
            cp.start()
        for cp in sends:
            cp.wait()

    return pl.pallas_call(
        kern, name="scatter_chips", in_specs=[HBM_SPEC], out_specs=HBM_SPEC,
        out_shape=jax.ShapeDtypeStruct((3, rows, cols), pieces.dtype),
        scratch_shapes=[pltpu.SemaphoreType.DMA((3,)), pltpu.SemaphoreType.DMA((3,))],
    )(pieces)


def _swap_sibling(buf):
    def kern(b_ref, out_ref, send_sem, recv_sem):
        x, y, c = lax.axis_index("x"), lax.axis_index("y"), lax.axis_index("c")
        cp = pltpu.make_async_remote_copy(src_ref=b_ref, dst_ref=out_ref, send_sem=send_sem, recv_sem=recv_sem,
                                          device_id=(x, y, 1 - c), device_id_type=MESH)
        cp.start()
        cp.wait()

    return pl.pallas_call(
        kern, name="swap_sibling", in_specs=[HBM_SPEC], out_specs=HBM_SPEC,
        out_shape=jax.ShapeDtypeStruct(buf.shape, buf.dtype),
        scratch_shapes=[pltpu.SemaphoreType.DMA, pltpu.SemaphoreType.DMA],
    )(buf)


def _gather_all(buf):
    rows, cols = buf.shape

    def kern(b_ref, out_ref, send_sems, recv_sems, local_sem):
        x, y, c = lax.axis_index("x"), lax.axis_index("y"), lax.axis_index("c")
        me = 4 * x + 2 * y + c
        local = pltpu.make_async_copy(b_ref, out_ref.at[me], local_sem)
        local.start()
        peers = []
        for k in range(1, N_DEV):
            fx, fy, fc = (k >> 2) & 1, (k >> 1) & 1, k & 1
            peers.append((x ^ fx, y ^ fy, c ^ fc))
        sends = [pltpu.make_async_remote_copy(
            src_ref=b_ref, dst_ref=out_ref.at[me], send_sem=send_sems.at[k], recv_sem=recv_sems.at[k],
            device_id=peer, device_id_type=MESH) for k, peer in enumerate(peers)]
        for cp in sends:
            cp.start()
        for k, peer in enumerate(peers):
            pltpu.make_async_remote_copy(
                src_ref=b_ref, dst_ref=out_ref.at[4 * peer[0] + 2 * peer[1] + peer[2]], send_sem=send_sems.at[k],
                recv_sem=recv_sems.at[k], device_id=peer, device_id_type=MESH).wait_recv()
        for cp in sends:
            cp.wait_send()
        local.wait()

    return pl.pallas_call(
        kern, name="gather_all", in_specs=[HBM_SPEC], out_specs=HBM_SPEC,
        out_shape=jax.ShapeDtypeStruct((N_DEV, rows, cols), buf.dtype),
        scratch_shapes=[pltpu.SemaphoreType.DMA((N_DEV - 1,)), pltpu.SemaphoreType.DMA((N_DEV - 1,)), pltpu.SemaphoreType.DMA],
    )(buf)


def _sum_slots(name, buf):
    n, rows, cols = buf.shape
    tile = _pick(rows, 256)

    def kern(b_ref, o_ref):
        acc = b_ref[0]
        for k in range(1, n):
            acc = acc + b_ref[k]
        o_ref[...] = acc

    return pl.pallas_call(
        kern, name=name, grid=(rows // tile,),
        in_specs=[pl.BlockSpec((n, tile, cols), lambda i: (0, i, 0))], out_specs=pl.BlockSpec((tile, cols), lambda i: (i, 0)),
        out_shape=jax.ShapeDtypeStruct((rows, cols), f32), compiler_params=_cparams(("parallel",)),
    )(buf)


def _pad_heads(t):
    r = t.shape[0]
    return jnp.pad(t.reshape(r, GLA_HEADS, GLA_DK), ((0, 0), (0, 0), (0, 128 - GLA_DK))).reshape(r, GLA_HEADS * 128)


def _unpad_heads(t):
    r = t.shape[0]
    return t.reshape(r, GLA_HEADS, 128)[:, :, :GLA_DK].reshape(r, GLA_HEADS * GLA_DK)


def _blockdiag(t):
    _, r, c = t.shape
    eye = jnp.eye(8, dtype=t.dtype).reshape(1, 8, 1, 8, 1)
    return (t.reshape(S5_SB, 8, r, 1, c) * eye).reshape(S5_SB, 8 * r, 8 * c)


def _blockdiag_extract(m, r, c):
    m5 = m.reshape(S5_SB, 8, r, 8, c)
    return jnp.stack([m5[:, g, :, g, :] for g in range(8)], axis=1).reshape(S5_GROUPS, r, c)


def _row(v, width=None):
    v = v[None]
    return v if width is None else jnp.pad(v, ((0, 0), (0, width - v.shape[1])))


def _layer_operands(p):
    w_in = p['w_in']

    def seg(n):
        return w_in[:, IN_ORIG[n][0]:IN_ORIG[n][1]]

    o = dict(p)
    o['wm'] = jnp.concatenate([seg('dn_qkv'), seg('dn_gate'), seg('cf'), seg('gates'), seg('s5'), _pad_heads(seg('gla_q')),
                               _pad_heads(seg('gla_k')), seg('gla_v'), seg('gla_g')], axis=1)
    o['ws'] = jnp.pad(jnp.concatenate([seg('dn_a'), seg('dn_b'), seg('gla_lr')], axis=1), ((0, 0), (0, WS_COLS - 24)))
    o['dn_conv8'] = jnp.pad(p['dn_conv'], ((0, 4), (0, 0)))
    o['a_log_r'] = _row(p['dn_a_log'], 128)
    o['dt_bias_r'] = _row(p['dn_dt_bias'], 128)
    o['dn_norm_r'] = _row(p['dn_norm'])
    o['cf_dw32'] = jnp.pad(p['cf_dw'], ((0, 1), (0, 0)))
    o['cf_bias_r'], o['cf_g_r'], o['cf_b_r'] = _row(p['cf_dw_bias']), _row(p['cf_ln_g']), _row(p['cf_ln_b'])
    o['w_alpha_p'] = jnp.pad(_pad_heads(p['gla_w_alpha']), ((8, 128 - 24), (0, 0)))
    o['b_alpha_r'] = _pad_heads(_row(p['gla_b_alpha']))
    o['gla_norm_r'] = _row(p['gla_norm'])
    o['ln1_g_r'], o['ln1_b_r'], o['ln2_g_r'], o['ln2_b_r'] = (_row(p[n]) for n in ('ln1_g', 'ln1_b', 'ln2_g', 'ln2_b'))
    o['ffn_conv8'] = jnp.pad(p['ffn_conv'], ((0, 5), (0, 0)))
    o['s5_in'] = (p['s5_a_re'], p['s5_a_im'], p['s5_log_dt'][:, None],
                  p['s5_b_re'].transpose(2, 0, 1), p['s5_b_im'].transpose(2, 0, 1))
    abar_re, abar_im, bbar_re, bbar_im = _s5_params(*o['s5_in'])
    o['abar'] = jnp.stack([abar_re, abar_im]).reshape(2, S5_SB, 1, 512)
    o['bmat_re'], o['bmat_im'] = _blockdiag(bbar_re.transpose(1, 0, 2)), _blockdiag(bbar_im.transpose(1, 0, 2))
    o['cmat_re'], o['cmat_im'] = _blockdiag(p['s5_c_re'].transpose(0, 2, 1)), _blockdiag(p['s5_c_im'].transpose(0, 2, 1))
    o['dvec'] = _row(p['s5_d'])
    return o


def _whole(a):
    return (a, a.shape[1], 0)


def _merge_ins(h, s):
    return [(h, 1024, 3), (h, 1024, 4), (h, 1024, 5), (h, 1024, 6), (s['y_a'], 1024, 0), (s['y_b'], 1024, 0),
            (s['zz'], 1024, 0), (s['zz'], 1024, 1), (s['y_d'], 1024, 0)]


def _layer_fwd(x, o):
    s = {}
    h = s['h'] = _mm(x, o['wm'], 'nn', 'mm_h')
    hs = s['hs'] = _mm(x, o['ws'], 'nn', 'mm_hs')
    s['c1'] = _conv_fwd('conv_dn', h, 1536, 0, o['dn_conv8'], 4)
    s['qkvn'], s['gb'], s['la'] = _rowwise('pre', _f_pre, [_whole(s['c1']), _whole(hs)],
                                           [o['a_log_r'], o['dt_bias_r'], o['w_alpha_p'], o['b_alpha_r']], [1536, 128, 512])
    s['o_dn'], s['st_dn'], s['inv_dn'] = _dn_fwd(s['qkvn'], s['gb'])
    (s['on_dn'],) = _rowwise('post_dn', _f_post, [_whole(s['o_dn']), (h, 512, 3)], [o['dn_norm_r']], [512], out_dtype=bf16)
    s['y_a'] = _mm(s['on_dn'], o['w_br_dn'], 'nn', 'mm_br')
    (s['cfp'],) = _rowwise('glu_cf', _f_glu, [(h, 512, 4), (h, 512, 5)], [], [512], out_dtype=bf16)
    s['cc'] = _conv_fwd('conv_cf', s['cfp'], 512, 0, o['cf_dw32'], 31)
    (s['cfo'],) = _rowwise('post_cf', _f_cfpost, [_whole(s['cc'])], [o['cf_bias_r'], o['cf_g_r'], o['cf_b_r']], [512], out_dtype=bf16)
    s['y_b'] = _mm(s['cfo'], o['w_br_cf'], 'nn', 'mm_br')
    s['ys5'], s['xr'], s['xi'] = _s5_fwd(h, o['abar'], o['bmat_re'], o['bmat_im'], o['cmat_re'], o['cmat_im'], o['dvec'])
    (s['z'],) = _rowwise('gelu', _f_gelu, [_whole(s['ys5'])], [], [512], out_dtype=bf16)
    s['zz'] = _mm(s['z'], o['w_br_s5'], 'nn', 'mm_br_s5')
    s['o_gla'], s['st_gla'] = _gla_fwd(h, s['la'])
    (s['on_gla'],) = _rowwise('post_gla', _f_post, [_whole(s['o_gla']), (h, 512, 18)], [o['gla_norm_r']], [512], out_dtype=bf16)
    s['y_d'] = _mm(s['on_gla'], o['w_br_gla'], 'nn', 'mm_br')
    (s['merged'],) = _rowwise('merge', _f_merge, _merge_ins(h, s), [], [1024], tile=128, out_dtype=bf16)
    s['mix'] = _mm(s['merged'], o['w_o'], 'nn', 'mm_o')
    (s['x1'],) = _rowwise('ln', _f_ln, [_whole(x), _whole(s['mix'])], [o['ln1_g_r'], o['ln1_b_r']], [1024])
    s['up'] = _mm(s['x1'], o['w_up'], 'nn', 'mm_up', out_dtype=bf16)
    s['u'] = _conv_fwd('conv_ffn', s['up'], 2 * D_FF, 0, o['ffn_conv8'], 3)
    (s['act'],) = _rowwise('act', _f_act, [(s['u'], D_FF, 0), (s['u'], D_FF, 1)], [], [D_FF], out_dtype=bf16)
    s['ffn'] = _mm(s['act'], o['w_down'], 'nn', 'mm_down')
    (x2,) = _rowwise('ln', _f_ln, [_whole(s['x1']), _whole(s['ffn'])], [o['ln2_g_r'], o['ln2_b_r']], [1024])
    return x2, s


def _layer_bwd(x, o, s, dparts):
    h, g = s['h'], {}
    (dx1_a, dffn), (g['ln2_g'], g['ln2_b']) = _rowwise_bwd(
        'ln_bwd', _f_ln, [_whole(s['x1']), _whole(s['ffn'])], [o['ln2_g_r'], o['ln2_b_r']], [[_whole(d) for d in dparts]], [f32, bf16])
    dact = _mm(dffn, o['w_down'], 'nt', 'mm_down_dx')
    g['w_down'] = _mm(s['act'], dffn, 'tn', 'mm_down_dw', out_dtype=bf16)
    (du_a, du_b), _ = _rowwise_bwd('act_bwd', _f_act, [(s['u'], D_FF, 0), (s['u'], D_FF, 1)], [], [[_whole(dact)]], [bf16, bf16])
    dup, dw = _conv_bwd('conv_ffn_bwd', s['up'], 2 * D_FF, 0, o['ffn_conv8'], 3, jnp.concatenate([du_a, du_b], axis=1), dx_dtype=bf16)
    g['ffn_conv'] = dw[:3]
    dx1_b = _mm(dup, o['w_up'], 'nt', 'mm_up_dx')
    g['w_up'] = _mm(s['x1'], dup, 'tn', 'mm_up_dw', out_dtype=bf16)
    (dx_a, dmix), (g['ln1_g'], g['ln1_b']) = _rowwise_bwd(
        'ln_bwd', _f_ln, [_whole(x), _whole(s['mix'])], [o['ln1_g_r'], o['ln1_b_r']], [[_whole(dx1_a), _whole(dx1_b)]], [f32, bf16])
    dmerged = _mm(dmix, o['w_o'], 'nt', 'mm_o_dx')
    g['w_o'] = _mm(s['merged'], dmix, 'tn', 'mm_o_dw', out_dtype=bf16)
    (dga, dgb_, dgc, dgd, dya, dyb, dzv, dzg, dyd), _ = _rowwise_bwd(
        'merge_bwd', _f_merge, _merge_ins(h, s), [], [[_whole(dmerged)]], [bf16] * 9, tile=128)
    dzz = jnp.concatenate([dzv, dzg], axis=1)
    don = _mm(dya, o['w_br_dn'], 'nt', 'mm_br_dx')
    g['w_br_dn'] = _mm(s['on_dn'], dya, 'tn', 'mm_br_dw', out_dtype=bf16)
    (do_dn, dgate_dn), (g['dn_norm'],) = _rowwise_bwd(
        'post_bwd', _f_post, [_whole(s['o_dn']), (h, 512, 3)], [o['dn_norm_r']], [[_whole(don)]], [f32, bf16])
    dqkvn, dgb = _dn_bwd(s['qkvn'], s['gb'], s['st_dn'], s['inv_dn'], do_dn)
    don = _mm(dyd, o['w_br_gla'], 'nt', 'mm_br_dx')
    g['w_br_gla'] = _mm(s['on_gla'], dyd, 'tn', 'mm_br_dw', out_dtype=bf16)
    (do_gla, dgate_gla), (g['gla_norm'],) = _rowwise_bwd(
        'post_bwd', _f_post, [_whole(s['o_gla']), (h, 512, 18)], [o['gla_norm_r']], [[_whole(don)]], [f32, bf16])
    dq_gla, dk_gla, dv_gla, dla = _gla_bwd(h, s['la'], s['st_gla'], do_gla)
    (dc1, dhs), (d_alog, d_dtb, d_walpha, d_balpha) = _rowwise_bwd(
        'pre_bwd', _f_pre, [_whole(s['c1']), _whole(s['hs'])], [o['a_log_r'], o['dt_bias_r'], o['w_alpha_p'], o['b_alpha_r']],
        [[_whole(dqkvn)], [_whole(dgb)], [_whole(dla)]], [bf16, bf16])
    g['dn_a_log'], g['dn_dt_bias'] = d_alog[0, :DN_HEADS], d_dtb[0, :DN_HEADS]
    g['gla_w_alpha'], g['gla_b_alpha'] = _unpad_heads(d_walpha[8:24]), _unpad_heads(d_balpha)[0]
    d_dnqkv, dw = _conv_bwd('conv_dn_bwd', h, 1536, 0, o['dn_conv8'], 4, dc1, dx_dtype=bf16)
    g['dn_conv'] = dw[:4]
    dcfo = _mm(dyb, o['w_br_cf'], 'nt', 'mm_br_dx')
    g['w_br_cf'] = _mm(s['cfo'], dyb, 'tn', 'mm_br_dw', out_dtype=bf16)
    (dcc,), (g['cf_dw_bias'], g['cf_ln_g'], g['cf_ln_b']) = _rowwise_bwd(
        'post_cf_bwd', _f_cfpost, [_whole(s['cc'])], [o['cf_bias_r'], o['cf_g_r'], o['cf_b_r']], [[_whole(dcfo)]], [bf16])
    dcfp, dw = _conv_bwd('conv_cf_bwd', s['cfp'], 512, 0, o['cf_dw32'], 31, dcc)
    g['cf_dw'] = dw[:31]
    (dcf_a, dcf_g), _ = _rowwise_bwd('glu_bwd', _f_glu, [(h, 512, 4), (h, 512, 5)], [], [[_whole(dcfp)]], [bf16, bf16])
    dz = _mm(dzz, o['w_br_s5'], 'nt', 'mm_br_s5_dx')
    g['w_br_s5'] = _mm(s['z'], dzz, 'tn', 'mm_br_s5_dw', out_dtype=bf16)
    (dys5,), _ = _rowwise_bwd('gelu_bwd', _f_gelu, [_whole(s['ys5'])], [], [[_whole(dz)]], [f32])
    du_s5, d_abar, dbm_re, dbm_im, dcm_re, dcm_im, d_dvec = _s5_bwd(
        h, o['abar'], o['bmat_re'], o['bmat_im'], o['cmat_re'], o['cmat_im'], o['dvec'], s['xr'], s['xi'], dys5)
    d_bbar = [_blockdiag_extract(m, S5_GROUP, S5_STATE).transpose(1, 0, 2) for m in (dbm_re, dbm_im)]
    da_re, da_im, dlog_dt, db_re, db_im = _s5_params_bwd(
        *o['s5_in'], d_abar[0].reshape(S5_GROUPS, S5_STATE), d_abar[1].reshape(S5_GROUPS, S5_STATE), *d_bbar)
    g['s5_a_re'], g['s5_a_im'], g['s5_log_dt'] = da_re, da_im, dlog_dt[:, 0]
    g['s5_b_re'], g['s5_b_im'] = db_re.transpose(1, 2, 0), db_im.transpose(1, 2, 0)
    g['s5_c_re'], g['s5_c_im'] = (_blockdiag_extract(m, S5_STATE, S5_GROUP).transpose(0, 2, 1) for m in (dcm_re, dcm_im))
    g['s5_d'] = d_dvec[0]
    for n in ('dn_norm', 'gla_norm', 'cf_dw_bias', 'cf_ln_g', 'cf_ln_b', 'ln1_g', 'ln1_b', 'ln2_g', 'ln2_b'):
        g[n] = g[n][0]
    dh = jnp.concatenate([d_dnqkv, dgate_dn, dcf_a, dcf_g, dga, dgb_, dgc, dgd, du_s5, dq_gla, dk_gla, dv_gla, dgate_gla], axis=1)
    dwm = _mm(x, dh, 'tn', 'mm_h_dw', out_dtype=bf16)
    dws = _mm(x, dhs, 'tn', 'mm_hs_dw', out_dtype=bf16)
    g['w_in'] = jnp.concatenate([
        dwm[:, 0:1536], dws[:, 0:8], dwm[:, 1536:2048], dwm[:, 2048:3072], dwm[:, 7168:7680], _unpad_heads(dwm[:, 7680:8192]),
        _unpad_heads(dwm[:, 8192:8704]), dwm[:, 8704:9216], dwm[:, 9216:9728], dws[:, 8:24], dwm[:, 3072:7168]], axis=1)
    return [dx_a, _mm(dh, o['wm'], 'nt', 'mm_h_dx'), _mm(dhs, o['ws'], 'nt', 'mm_hs_dx')], g


WEIGHTS = ('w_in', 'dn_conv', 'dn_a_log', 'dn_dt_bias', 'dn_norm', 'w_br_dn', 'cf_dw', 'cf_dw_bias', 'cf_ln_g', 'cf_ln_b',
           'w_br_cf', 's5_a_re', 's5_a_im', 's5_log_dt', 's5_b_re', 's5_b_im', 's5_c_re', 's5_c_im', 's5_d', 'w_br_s5',
           'gla_w_alpha', 'gla_b_alpha', 'gla_norm', 'w_br_gla', 'w_o', 'ln1_g', 'ln1_b', 'w_up', 'ffn_conv', 'w_down',
           'ln2_g', 'ln2_b')
LARGE = ('w_in', 'w_br_dn', 'w_br_cf', 'w_br_s5', 'w_br_gla', 'w_o', 'w_up', 'w_down')
SHARD_AXIS = dict(w_in=2, w_br_dn=2, w_br_cf=2, w_br_s5=2, w_br_gla=2, w_o=1, w_up=2, w_down=1,
                  dn_conv=2, cf_dw=2, gla_w_alpha=2, ffn_conv=2)
SMALL = tuple(n for n in WEIGHTS if n not in LARGE)
SMALL_SHARDED = tuple(n for n in SMALL if n in SHARD_AXIS)


def _local_step(x, target, full):
    ops, saved, xs = [], [], [x]
    for l in range(DEPTH):
        o = _layer_operands({n: full[n][l] for n in WEIGHTS})
        y, s = _layer_fwd(xs[-1], o)
        ops.append(o)
        saved.append(s)
        xs.append(y)
    loss, dy = _loss_head(xs[-1], target)
    dparts, grads = [dy], [None] * DEPTH
    for l in reversed(range(DEPTH)):
        dparts, grads[l] = _layer_bwd(xs[l], ops[l], saved[l], dparts)
    grad_x = _sum_parts('sum_dx', dparts)
    small = {n: jnp.stack([grads[l][n] for l in range(DEPTH)]) for n in SMALL}
    return loss[0, 0], grad_x, small, {n: [grads[l][n] for l in range(DEPTH)] for n in LARGE}


def _pack(arrs, rows, dtype=f32):
    flat = jnp.concatenate([a.reshape(-1).astype(dtype) for a in arrs])
    return jnp.pad(flat, (0, rows * 1024 - flat.shape[0])).reshape(rows, 1024)


def _unpack(buf, shapes):
    flat, out, pos = buf.reshape(-1), [], 0
    for shp in shapes:
        n = 1
        for d in shp:
            n *= d
        out.append(flat[pos:pos + n].reshape(shp))
        pos += n
    return out


def _rows_for(shapes, mult):
    n = 0
    for shp in shapes:
        k = 1
        for d in shp:
            k *= d
        n += k
    rows = -(-n // 1024)
    return -(-rows // mult) * mult


def _shard(a, axis, chip):
    size = a.shape[axis] // N_CHIPS
    return lax.dynamic_slice_in_dim(a, chip * size, size, axis)


def kernel(x, w_in, dn_conv, dn_a_log, dn_dt_bias, dn_norm, w_br_dn, cf_dw, cf_dw_bias, cf_ln_g, cf_ln_b, w_br_cf, s5_a_re, s5_a_im, s5_log_dt, s5_b_re, s5_b_im, s5_c_re, s5_c_im, s5_d, w_br_s5, gla_w_alpha, gla_b_alpha, gla_norm, w_br_gla, w_o, ln1_g, ln1_b, w_up, ffn_conv, w_down, ln2_g, ln2_b, loss_target, m_w_in, m_dn_conv, m_dn_a_log, m_dn_dt_bias, m_dn_norm, m_w_br_dn, m_cf_dw, m_cf_dw_bias, m_cf_ln_g, m_cf_ln_b, m_w_br_cf, m_s5_a_re, m_s5_a_im, m_s5_log_dt, m_s5_b_re, m_s5_b_im, m_s5_c_re, m_s5_c_im, m_s5_d, m_w_br_s5, m_gla_w_alpha, m_gla_b_alpha, m_gla_norm, m_w_br_gla, m_w_o, m_ln1_g, m_ln1_b, m_w_up, m_ffn_conv, m_w_down, m_ln2_g, m_ln2_b, v_w_in, v_dn_conv, v_dn_a_log, v_dn_dt_bias, v_dn_norm, v_w_br_dn, v_cf_dw, v_cf_dw_bias, v_cf_ln_g, v_cf_ln_b, v_w_br_cf, v_s5_a_re, v_s5_a_im, v_s5_log_dt, v_s5_b_re, v_s5_b_im, v_s5_c_re, v_s5_c_im, v_s5_d, v_w_br_s5, v_gla_w_alpha, v_gla_b_alpha, v_gla_norm, v_w_br_gla, v_w_o, v_ln1_g, v_ln1_b, v_w_up, v_ffn_conv, v_w_down, v_ln2_g, v_ln2_b):
    env = locals()
    w = {n: env[n] for n in WEIGHTS}
    m = {n: env['m_' + n] for n in WEIGHTS}
    v = {n: env['v_' + n] for n in WEIGHTS}
    chip = 2 * lax.axis_index("x") + lax.axis_index("y")

    large_shapes = [w[n].shape for n in LARGE]
    ssh_shapes = [w[n].shape for n in SMALL_SHARDED]
    large_rows, ssh_rows = _rows_for(large_shapes, 512), _rows_for(ssh_shapes, 8)
    got_large, got_ssh = _gather_chips(_pack([w[n] for n in LARGE], large_rows, bf16),
                                       _pack([w[n] for n in SMALL_SHARDED], ssh_rows))
    full = {n: w[n] for n in SMALL if n not in SHARD_AXIS}
    per_chip = [dict(zip(LARGE + SMALL_SHARDED, _unpack(got_large[k], large_shapes) + _unpack(got_ssh[k], ssh_shapes)))
                for k in range(N_CHIPS)]
    for n in LARGE + SMALL_SHARDED:
        full[n] = jnp.concatenate([per_chip[k][n] for k in range(N_CHIPS)], axis=SHARD_AXIS[n])

    loss, grad_x, g, g_large = _local_step(x[0], loss_target[0], full)
    loss = lax.psum(loss, ("x", "y", "c"))

    def piece(k):
        parts = []
        for n in LARGE:
            axis = SHARD_AXIS[n] - 1
            size = g_large[n][0].shape[axis] // N_CHIPS
            parts += [lax.slice_in_dim(layer, k * size, (k + 1) * size, axis=axis) for layer in g_large[n]]
        return _pack(parts, large_rows, bf16)

    half = large_rows // 2
    core = lax.axis_index("c")
    pieces = jnp.stack([piece(k) for k in range(N_CHIPS)])
    p_mine = lax.dynamic_slice_in_dim(pieces, core * half, half, axis=1).reshape(N_CHIPS * half, 1024)
    p_theirs = lax.dynamic_slice_in_dim(pieces, (1 - core) * half, half, axis=1).reshape(N_CHIPS * half, 1024)
    pair = _sum_parts('sum_pair', [p_mine, _swap_sibling(p_theirs)], bf16).reshape(N_CHIPS, half, 1024)
    from_chips = _scatter_chips(pair)
    own = lax.dynamic_index_in_dim(pair, chip, 0, keepdims=False)
    chip_sum = _join_halves(_sum_parts('sum_chips', [own, from_chips[0], from_chips[1], from_chips[2]]))
    res = {0: dict(zip(LARGE, _unpack(chip_sum, large_shapes))), 1: {}, 2: {}, 3: {}}
    for n in LARGE:
        two_d = (w[n].shape[0] * w[n].shape[1], w[n].shape[2])
        upd = _adamw('adamw_large', w[n].reshape(two_d), m[n].reshape(two_d), v[n].reshape(two_d), res[0][n].reshape(two_d))
        for kind in range(3):
            res[kind + 1][n] = upd[kind].reshape(w[n].shape)

    small_full_shapes = [g[n].shape for n in SMALL]
    small_rows = _rows_for(small_full_shapes, 8)
    small_sum = _sum_slots('sum_devices', _gather_all(_pack([g[n] for n in SMALL], small_rows)))
    gs = dict(zip(SMALL, _unpack(small_sum, small_full_shapes)))
    for n in SMALL_SHARDED:
        gs[n] = _shard(gs[n], SHARD_AXIS[n], chip)
    small_shapes = [w[n].shape for n in SMALL]
    upd_rows = _rows_for(small_shapes, 8)
    upd = _adamw('adamw_small', _pack([w[n] for n in SMALL], upd_rows), _pack([m[n] for n in SMALL], upd_rows),
                 _pack([v[n] for n in SMALL], upd_rows), _pack([gs[n] for n in SMALL], upd_rows))
    res[0].update(gs)
    for kind in range(3):
        res[kind + 1].update(zip(SMALL, _unpack(upd[kind], small_shapes)))
    return (loss, grad_x[None], *[res[kind][n] for kind in range(4) for n in WEIGHTS])
```

```python
import functools

import jax
import jax.numpy as jnp
from jax import lax
from jax.experimental import pallas as pl
from jax.experimental.pallas import tpu as pltpu

f32 = jnp.float32
bf16 = jnp.bfloat16
HI = lax.Precision.HIGHEST

D_MODEL = 1024
DEPTH = 4
DN_HEADS, DN_DK, DN_CHUNK = 4, 128, 64
GLA_HEADS, GLA_DK, GLA_CHUNK, GLA_TAU = 4, 64, 16, 16.0
GLA_BLOCK = 128
S5_GROUPS, S5_GROUP, S5_STATE = 32, 16, 64
S5_SB = 4
S5_TILE = 256
D_FF = 2816
LN_EPS = 1e-5
ALPHA = (2.0 * DEPTH) ** 0.25
ADAM_LR, ADAM_B1, ADAM_B2, ADAM_EPS, ADAM_WD, ADAM_STEP = 0.001, 0.9, 0.999, 1e-08, 0.01, 10

VMEM_LIMIT_V7X = 56 * 1024 * 1024
MM_A_BLOCK_BYTES = 8 * 1024 * 1024
HALO = 32
N_CHIPS = 4
N_DEV = 8

IN_ORIG = dict(dn_qkv=(0, 1536), dn_a=(1536, 1540), dn_b=(1540, 1544), dn_gate=(1544, 2056), cf=(2056, 3080),
               s5=(3080, 3592), gla_q=(3592, 3848), gla_k=(3848, 4104), gla_v=(4104, 4616), gla_g=(4616, 5128),
               gla_lr=(5128, 5144), gates=(5144, 9240))
IN_COLS = 9240
WM_COLS = 9728
WS_COLS = 128


def _cparams(sem):
    return pltpu.CompilerParams(dimension_semantics=sem, vmem_limit_bytes=VMEM_LIMIT_V7X)


def _pick(dim, pref):
    for t in (pref, 512, 256, 128, 64, 32, 16, 8):
        if t <= pref and dim % t == 0:
            return t
    return dim


NN = (((1,), (0,)), ((), ()))
NT = (((1,), (1,)), ((), ()))
TN = (((0,), (0,)), ((), ()))


def _dot(a, b, dims=NN):
    return lax.dot_general(a, b, dims, precision=HI, preferred_element_type=f32)


def _round(a):
    return a.astype(bf16).astype(f32)


def _rdot(a, b, dims=NN):
    return lax.dot_general(a.astype(bf16), b.astype(bf16), dims, preferred_element_type=f32)


@functools.partial(jax.custom_vjp, nondiff_argnums=(2,))
def _bdot(a, b, dims=NN):
    return _rdot(a, b, dims)


def _bdot_fwd(a, b, dims):
    return _rdot(a, b, dims), (a, b)


def _bdot_bwd(dims, res, g):
    a, b = res
    if dims == NN:
        return _rdot(g, b, NT), _rdot(a, g, TN)
    if dims == NT:
        return _rdot(g, b, NN), _rdot(g, a, TN)
    assert dims == TN
    return _rdot(b, g, NT), _rdot(a, g, NN)


_bdot.defvjp(_bdot_fwd, _bdot_bwd)


def _iota(shape, axis):
    return lax.broadcasted_iota(jnp.int32, shape, axis)


def _mm(a, b, mode, name, tn=512, out_dtype=f32):
    if mode == 'nn':
        (m, k), n = a.shape, b.shape[1]
    elif mode == 'nt':
        (m, k), n = a.shape, b.shape[0]
    else:
        (k, m), n = a.shape, b.shape[1]
    tm, tn = _pick(m, 512 if mode == 'tn' else 1024), _pick(n, tn)
    nk = 1
    while (k // nk) * tm * a.dtype.itemsize > MM_A_BLOCK_BYTES or k % nk or (k // nk) % 128:
        nk += 1
    tk = k // nk
    assert nk == 1 or out_dtype == f32
    if mode == 'nn':
        dims = NN
        a_spec = pl.BlockSpec((tm, tk), lambda i, j, kk: (i, kk))
        b_spec = pl.BlockSpec((tk, tn), lambda i, j, kk: (kk, j))
    elif mode == 'nt':
        dims = NT
        a_spec = pl.BlockSpec((tm, tk), lambda i, j, kk: (i, kk))
        b_spec = pl.BlockSpec((tn, tk), lambda i, j, kk: (j, kk))
    else:
        dims = TN
        a_spec = pl.BlockSpec((tk, tm), lambda i, j, kk: (kk, i))
        b_spec = pl.BlockSpec((tk, tn), lambda i, j, kk: (kk, j))

    def kern(a_ref, b_ref, o_ref):
        p = lax.dot_general(a_ref[...].astype(bf16), b_ref[...].astype(bf16), dims, preferred_element_type=f32)
        if nk == 1:
            o_ref[...] = p.astype(o_ref.dtype)
        else:
            kk = pl.program_id(2)

            @pl.when(kk == 0)
            def _():
                o_ref[...] = p

            @pl.when(kk > 0)
            def _():
                o_ref[...] += p

    return pl.pallas_call(
        kern, name=name, grid=(m // tm, n // tn, nk),
        in_specs=[a_spec, b_spec], out_specs=pl.BlockSpec((tm, tn), lambda i, j, kk: (i, j)),
        out_shape=jax.ShapeDtypeStruct((m, n), out_dtype),
        compiler_params=_cparams(("parallel", "parallel", "arbitrary")),
    )(a, b)


def _full_spec(p):
    nd = p.ndim
    return pl.BlockSpec(p.shape, lambda *_, nd=nd: (0,) * nd)


def _rowwise(name, f, ins, params, out_widths, tile=256, out_dtype=f32):
    rows = ins[0][0].shape[0]
    tile = _pick(rows, tile)
    n_x = len(ins) + len(params)

    def kern(*refs):
        for o_ref, r in zip(refs[n_x:], f(*[r[...] for r in refs[:n_x]])):
            o_ref[...] = r.astype(o_ref.dtype)

    in_specs = [pl.BlockSpec((tile, w), lambda i, c=c: (i, c)) for (_, w, c) in ins] + [_full_spec(p) for p in params]
    return pl.pallas_call(
        kern, name=name, grid=(rows // tile,), in_specs=in_specs,
        out_specs=[pl.BlockSpec((tile, w), lambda i: (i, 0)) for w in out_widths],
        out_shape=[jax.ShapeDtypeStruct((rows, w), out_dtype) for w in out_widths],
        compiler_params=_cparams(("parallel",)),
    )(*[a for (a, _, _) in ins], *params)


def _rowwise_bwd(name, f, ins, params, douts, want, tile=256):
    rows = ins[0][0].shape[0]
    tile = _pick(rows, tile)
    n_in, n_p = len(ins), len(params)
    parts = [p for d in douts for p in d]
    n_x, n_d = n_in + n_p, len(parts)

    def kern(*refs):
        xs = [r[...] for r in refs[:n_x]]
        d_refs, o_refs = refs[n_x:n_x + n_d], refs[n_x + n_d:]
        cts, pos = [], 0
        for d in douts:
            acc = d_refs[pos][...]
            for r in d_refs[pos + 1:pos + len(d)]:
                acc = acc + r[...]
            pos += len(d)
            cts.append(acc)
        grads = jax.vjp(f, *xs)[1](tuple(cts))
        k = 0
        for j in range(n_in):
            if want[j]:
                o_refs[k][...] = grads[j].astype(o_refs[k].dtype)
                k += 1
        first = pl.program_id(0) == 0
        for j in range(n_p):
            g, o_ref = grads[n_in + j], o_refs[k + j]

            @pl.when(first)
            def _(o_ref=o_ref, g=g):
                o_ref[...] = g

            @pl.when(jnp.logical_not(first))
            def _(o_ref=o_ref, g=g):
                o_ref[...] += g

    in_specs = ([pl.BlockSpec((tile, w), lambda i, c=c: (i, c)) for (_, w, c) in ins] + [_full_spec(p) for p in params]
                + [pl.BlockSpec((tile, w), lambda i, c=c: (i, c)) for (_, w, c) in parts])
    out_specs, out_shape = [], []
    for j in range(n_in):
        if want[j]:
            out_specs.append(pl.BlockSpec((tile, ins[j][1]), lambda i: (i, 0)))
            out_shape.append(jax.ShapeDtypeStruct((rows, ins[j][1]), want[j]))
    n_g = len(out_specs)
    for p in params:
        out_specs.append(_full_spec(p))
        out_shape.append(jax.ShapeDtypeStruct(p.shape, f32))
    res = pl.pallas_call(
        kern, name=name, grid=(rows // tile,), in_specs=in_specs, out_specs=out_specs, out_shape=out_shape,
        compiler_params=_cparams(("arbitrary",)),
    )(*[a for (a, _, _) in ins], *params, *[a for (a, _, _) in parts])
    return list(res[:n_g]), list(res[n_g:])


_sigmoid = jax.nn.sigmoid
_silu = jax.nn.silu
_softplus = jax.nn.softplus
_log_sigmoid = jax.nn.log_sigmoid


def _f_ln(x, r, g, b):
    t = ALPHA * x + r
    mu = jnp.mean(t, -1, keepdims=True)
    var = jnp.mean(jnp.square(t - mu), -1, keepdims=True)
    return ((t - mu) * lax.rsqrt(var + LN_EPS) * g + b,)


def _f_pre(c1, hs, a_log, dt_bias, w_alpha, b_alpha):
    s = _silu(c1)
    outs = []
    for j in range(3 * DN_HEADS):
        t = s[:, j * DN_DK:(j + 1) * DN_DK]
        if j < 2 * DN_HEADS:
            t = t * lax.rsqrt(jnp.sum(t * t, -1, keepdims=True) + 1e-6)
        if j < DN_HEADS:
            t = t * (DN_DK ** -0.5)
        outs.append(t)
    qkvn = jnp.concatenate(outs, axis=1)
    lane = _iota(hs.shape, 1)
    g = -jnp.exp(a_log) * _softplus(hs + dt_bias)
    beta = _sigmoid(hs)
    gb = jnp.where(lane < DN_HEADS, g, jnp.where(lane < 2 * DN_HEADS, beta, 0.0))
    la = _log_sigmoid(_bdot(hs, w_alpha) + b_alpha) * (1.0 / GLA_TAU)
    lane5 = _iota(la.shape, 1)
    la = jnp.where((lane5 % 128) < GLA_DK, la, 0.0)
    return qkvn, gb, la


def _f_post(o, gate, w):
    outs = []
    for j in range(4):
        t = o[:, j * 128:(j + 1) * 128]
        outs.append(t * lax.rsqrt(jnp.mean(t * t, -1, keepdims=True) + LN_EPS) * w)
    return (jnp.concatenate(outs, axis=1) * _silu(gate),)


def _f_glu(a, g):
    return (a * _sigmoid(g),)


def _f_cfpost(c, bias, g, b):
    t = c + bias
    mu = jnp.mean(t, -1, keepdims=True)
    var = jnp.mean(jnp.square(t - mu), -1, keepdims=True)
    return (_silu((t - mu) * lax.rsqrt(var + LN_EPS) * g + b),)


def _f_gelu(y):
    return (jax.nn.gelu(y),)


def _f_merge(ga, gb_, gc, gd, ya, yb, zv, zg, yd):
    return (_sigmoid(ga) * ya + _sigmoid(gb_) * yb + _sigmoid(gc) * (zv * _sigmoid(zg)) + _sigmoid(gd) * yd,)


def _f_act(a, b):
    return (_silu(a) * b,)


def _conv_tiles(rows, ch):
    return _pick(rows, 256), _pick(ch, 512)


def _conv_fwd(name, x, width, colblk0, w, taps):
    rows = x.shape[0]
    tr, cb = _conv_tiles(rows, width)
    hb = tr // HALO

    def kern(prev_ref, x_ref, w_ref, o_ref, ext):
        i = pl.program_id(1)
        ext[pl.ds(0, HALO), :] = jnp.where(i > 0, _round(prev_ref[...]), 0.0)
        ext[pl.ds(HALO, tr), :] = _round(x_ref[...])
        wv = _round(w_ref[...])
        acc = jnp.zeros((tr, cb), f32)
        for k in range(taps):
            acc = acc + wv[k:k + 1, :] * ext[pl.ds(HALO - taps + 1 + k, tr), :]
        o_ref[...] = acc

    c0 = colblk0 * (width // cb)
    return pl.pallas_call(
        kern, name=name, grid=(width // cb, rows // tr),
        in_specs=[pl.BlockSpec((HALO, cb), lambda c, i: (jnp.maximum(i * hb - 1, 0), c0 + c)),
                  pl.BlockSpec((tr, cb), lambda c, i: (i, c0 + c)),
                  pl.BlockSpec((w.shape[0], cb), lambda c, i: (0, c))],
        out_specs=pl.BlockSpec((tr, cb), lambda c, i: (i, c)),
        out_shape=jax.ShapeDtypeStruct((rows, width), f32),
        scratch_shapes=[pltpu.VMEM((HALO + tr, cb), f32)],
        compiler_params=_cparams(("parallel", "arbitrary")),
    )(x, x, w)


def _conv_bwd(name, x, width, colblk0, w, taps, dy, dx_dtype=f32):
    rows = x.shape[0]
    tr, cb = _conv_tiles(rows, width)
    hb = tr // HALO
    nt = rows // tr
    wr = w.shape[0]

    def kern(prev_ref, x_ref, w_ref, dy_ref, next_ref, dx_ref, dw_ref, ext, dext):
        i = pl.program_id(1)
        ext[pl.ds(0, HALO), :] = jnp.where(i > 0, _round(prev_ref[...]), 0.0)
        ext[pl.ds(HALO, tr), :] = _round(x_ref[...])
        dyv = _round(dy_ref[...])
        dext[pl.ds(0, tr), :] = dyv
        dext[pl.ds(tr, HALO), :] = jnp.where(i < nt - 1, _round(next_ref[...]), 0.0)
        wv = _round(w_ref[...])
        acc = jnp.zeros((tr, cb), f32)
        rows_w = []
        for k in range(taps):
            acc = acc + wv[k:k + 1, :] * dext[pl.ds(taps - 1 - k, tr), :]
            rows_w.append(jnp.sum(dyv * ext[pl.ds(HALO - taps + 1 + k, tr), :], axis=0, keepdims=True))
        dx_ref[...] = acc.astype(dx_ref.dtype)
        if wr > taps:
            rows_w.append(jnp.zeros((wr - taps, cb), f32))
        dwv = jnp.concatenate(rows_w, axis=0)

        @pl.when(i == 0)
        def _():
            dw_ref[...] = dwv

        @pl.when(i > 0)
        def _():
            dw_ref[...] += dwv

    c0 = colblk0 * (width // cb)
    return pl.pallas_call(
        kern, name=name, grid=(width // cb, nt),
        in_specs=[pl.BlockSpec((HALO, cb), lambda c, i: (jnp.maximum(i * hb - 1, 0), c0 + c)),
                  pl.BlockSpec((tr, cb), lambda c, i: (i, c0 + c)),
                  pl.BlockSpec((wr, cb), lambda c, i: (0, c)),
                  pl.BlockSpec((tr, cb), lambda c, i: (i, c)),
                  pl.BlockSpec((HALO, cb), lambda c, i: (jnp.minimum((i + 1) * hb, nt * hb - 1), c))],
        out_specs=[pl.BlockSpec((tr, cb), lambda c, i: (i, c)), pl.BlockSpec((wr, cb), lambda c, i: (0, c))],
        out_shape=[jax.ShapeDtypeStruct((rows, width), dx_dtype), jax.ShapeDtypeStruct((wr, width), f32)],
        scratch_shapes=[pltpu.VMEM((HALO + tr, cb), f32), pltpu.VMEM((HALO + tr, cb), f32)],
        compiler_params=_cparams(("parallel", "arbitrary")),
    )(x, x, w, dy, dy)


def _series_inverse(neg):
    n = neg.shape[0]
    inv = jnp.where(_iota((n, n), 0) == _iota((n, n), 1), 1.0, 0.0) + neg
    p = neg
    for _ in range(5):
        p = _dot(p, p)
        inv = inv + _dot(inv, p)
    return inv


def _inverse_bwd(inv, g):
    return _dot(_dot(inv, g, TN), inv, NT)


@jax.custom_vjp
def _unit_lower_inverse(neg):
    return _series_inverse(neg)


_unit_lower_inverse.defvjp(lambda neg: (_series_inverse(neg),) * 2, lambda inv, g: (_inverse_bwd(inv, g),))


@jax.custom_vjp
def _known_inverse(neg, inv):
    return inv


_known_inverse.defvjp(lambda neg, inv: (inv, inv), lambda inv, g: (_inverse_bwd(inv, g), jnp.zeros_like(inv)))


def _dn_head(state, q, k, v, gc, gl, beta, inv_saved=None):
    c = DN_CHUNK
    ii, jj = _iota((c, c), 0), _iota((c, c), 1)
    causal, strict = ii >= jj, ii > jj
    gcb = jnp.broadcast_to(gc, (c, c))
    decay = jnp.where(causal, jnp.exp(jnp.where(causal, gcb - gcb.T, 0.0)), 0.0)
    kb = k * beta
    neg = -jnp.where(strict, _bdot(kb, k, NT) * decay, 0.0)
    inv = _unit_lower_inverse(neg) if inv_saved is None else _known_inverse(neg, inv_saved)
    egc = jnp.exp(gc)
    u = _dot(inv, v * beta)
    w = _dot(inv, kb * egc)
    intra = _bdot(q, k, NT) * decay
    v_new = u - _bdot(w, state)
    o = _bdot(q * egc, state) + _bdot(intra, v_new)
    new_state = state * jnp.exp(gl) + _bdot(k * jnp.exp(gl - gc), v_new, TN)
    return (o, new_state, inv) if inv_saved is None else (o, new_state)


def _dn_cum(gb):
    c = DN_CHUNK
    tril = jnp.where(_iota((c, c), 0) >= _iota((c, c), 1), 1.0, 0.0)
    return _dot(tril, gb), jnp.sum(gb, axis=0, keepdims=True)


def _dn_fwd(qkvn, gb):
    rows = qkvn.shape[0]
    c, h, d = DN_CHUNK, DN_HEADS, DN_DK
    nc = rows // c

    def kern(qkv_ref, gb_ref, o_ref, st_ref, inv_ref, state):
        @pl.when(pl.program_id(0) == 0)
        def _():
            state[...] = jnp.zeros_like(state)

        gbv = gb_ref[...]
        cum, tot = _dn_cum(gbv)
        for j in range(h):
            st = state[j]
            st_ref[0, j] = st
            o, new, inv = _dn_head(st, qkv_ref[:, j * d:(j + 1) * d], qkv_ref[:, (h + j) * d:(h + j + 1) * d],
                                   qkv_ref[:, (2 * h + j) * d:(2 * h + j + 1) * d],
                                   cum[:, j:j + 1], tot[:, j:j + 1], gbv[:, h + j:h + j + 1])
            o_ref[:, j * d:(j + 1) * d] = o
            inv_ref[0, j] = inv
            state[j] = new

    return pl.pallas_call(
        kern, name="dn_fwd", grid=(nc,),
        in_specs=[pl.BlockSpec((c, 3 * h * d), lambda i: (i, 0)), pl.BlockSpec((c, 128), lambda i: (i, 0))],
        out_specs=[pl.BlockSpec((c, h * d), lambda i: (i, 0)), pl.BlockSpec((1, h, d, d), lambda i: (i, 0, 0, 0)),
                   pl.BlockSpec((1, h, c, c), lambda i: (i, 0, 0, 0))],
        out_shape=[jax.ShapeDtypeStruct((rows, h * d), f32), jax.ShapeDtypeStruct((nc, h, d, d), f32),
                   jax.ShapeDtypeStruct((nc, h, c, c), f32)],
        scratch_shapes=[pltpu.VMEM((h, d, d), f32)],
        compiler_params=_cparams(("arbitrary",)),
    )(qkvn, gb)


def _dn_bwd(qkvn, gb, states, invs, do):
    rows = qkvn.shape[0]
    c, h, d = DN_CHUNK, DN_HEADS, DN_DK
    nc = rows // c

    def kern(qkv_ref, gb_ref, st_ref, inv_ref, do_ref, dqkv_ref, dgb_ref, dstate):
        @pl.when(pl.program_id(0) == 0)
        def _():
            dstate[...] = jnp.zeros_like(dstate)

        gbv = gb_ref[...]
        cum, tot = _dn_cum(gbv)
        lane = _iota((c, 128), 1)
        dcum = jnp.zeros((c, 128), f32)
        dgb = jnp.zeros((c, 128), f32)
        for j in range(h):
            args = (st_ref[0, j], qkv_ref[:, j * d:(j + 1) * d], qkv_ref[:, (h + j) * d:(h + j + 1) * d],
                    qkv_ref[:, (2 * h + j) * d:(2 * h + j + 1) * d],
                    cum[:, j:j + 1], tot[:, j:j + 1], gbv[:, h + j:h + j + 1])
            head = functools.partial(_dn_head, inv_saved=inv_ref[0, j])
            ds, dq, dk, dv, dgc, dgl, dbeta = jax.vjp(head, *args)[1]((do_ref[:, j * d:(j + 1) * d], dstate[j]))
            dstate[j] = ds
            dqkv_ref[:, j * d:(j + 1) * d] = dq
            dqkv_ref[:, (h + j) * d:(h + j + 1) * d] = dk
            dqkv_ref[:, (2 * h + j) * d:(2 * h + j + 1) * d] = dv
            dcum = dcum + jnp.where(lane == j, dgc, 0.0)
            dgb = dgb + jnp.where(lane == j, dgl, 0.0) + jnp.where(lane == h + j, dbeta, 0.0)
        triu = jnp.where(_iota((c, c), 0) <= _iota((c, c), 1), 1.0, 0.0)
        dgb_ref[...] = dgb + _dot(triu, dcum)

    rev = lambda i: (nc - 1 - i, 0)
    return pl.pallas_call(
        kern, name="dn_bwd", grid=(nc,),
        in_specs=[pl.BlockSpec((c, 3 * h * d), rev), pl.BlockSpec((c, 128), rev),
                  pl.BlockSpec((1, h, d, d), lambda i: (nc - 1 - i, 0, 0, 0)),
                  pl.BlockSpec((1, h, c, c), lambda i: (nc - 1 - i, 0, 0, 0)), pl.BlockSpec((c, h * d), rev)],
        out_specs=[pl.BlockSpec((c, 3 * h * d), rev), pl.BlockSpec((c, 128), rev)],
        out_shape=[jax.ShapeDtypeStruct((rows, 3 * h * d), f32), jax.ShapeDtypeStruct((rows, 128), f32)],
        scratch_shapes=[pltpu.VMEM((h, d, d), f32)],
        compiler_params=_cparams(("arbitrary",)),
    )(qkvn, gb, states, invs, do)


def _gla_block(state_t, q, k, v, la):
    c = GLA_CHUNK
    ii, jj = _iota((c, c), 0), _iota((c, c), 1)
    causal = ii >= jj
    tril = jnp.where(causal, 1.0, 0.0)
    outs = []
    for n in range(q.shape[0] // c):
        sl = slice(n * c, (n + 1) * c)
        qn, kn, vn, ln = q[sl] * (GLA_DK ** -0.5), k[sl], v[sl], la[sl]
        gc = _dot(tril, ln)
        gl = jnp.sum(ln, axis=0, keepdims=True)
        q_dec = qn * jnp.exp(gc)
        scores = jnp.where(causal, _bdot(q_dec, kn * jnp.exp(-gc), NT), 0.0)
        outs.append(_bdot(q_dec, state_t, NT) + _bdot(scores, vn))
        state_t = state_t * jnp.exp(gl) + _bdot(vn, kn * jnp.exp(gl - gc), TN)
    return jnp.concatenate(outs, axis=0), state_t


GLA_Q0, GLA_K0, GLA_V0 = 7680 // 512, 8192 // 512, 8704 // 512


def _gla_fwd(hmain, la):
    rows = hmain.shape[0]
    b, h = _pick(rows, GLA_BLOCK), GLA_HEADS
    nb = rows // b

    def kern(q_ref, k_ref, v_ref, la_ref, o_ref, st_ref, state):
        @pl.when(pl.program_id(0) == 0)
        def _():
            state[...] = jnp.zeros_like(state)

        for j in range(h):
            sl = slice(j * 128, (j + 1) * 128)
            st = state[j]
            st_ref[0, j] = st
            o, new = _gla_block(st, q_ref[:, sl], k_ref[:, sl], v_ref[:, sl], la_ref[:, sl])
            o_ref[:, sl] = o
            state[j] = new

    return pl.pallas_call(
        kern, name="gla_fwd", grid=(nb,),
        in_specs=[pl.BlockSpec((b, 512), lambda i: (i, GLA_Q0)), pl.BlockSpec((b, 512), lambda i: (i, GLA_K0)),
                  pl.BlockSpec((b, 512), lambda i: (i, GLA_V0)), pl.BlockSpec((b, 512), lambda i: (i, 0))],
        out_specs=[pl.BlockSpec((b, 512), lambda i: (i, 0)), pl.BlockSpec((1, h, 128, 128), lambda i: (i, 0, 0, 0))],
        out_shape=[jax.ShapeDtypeStruct((rows, h * 128), f32), jax.ShapeDtypeStruct((nb, h, 128, 128), f32)],
        scratch_shapes=[pltpu.VMEM((h, 128, 128), f32)],
        compiler_params=_cparams(("arbitrary",)),
    )(hmain, hmain, hmain, la)


def _gla_bwd(hmain, la, states, do):
    rows = hmain.shape[0]
    b, h = _pick(rows, GLA_BLOCK), GLA_HEADS
    nb = rows // b

    def kern(q_ref, k_ref, v_ref, la_ref, st_ref, do_ref, dq_ref, dk_ref, dv_ref, dla_ref, dstate):
        @pl.when(pl.program_id(0) == 0)
        def _():
            dstate[...] = jnp.zeros_like(dstate)

        for j in range(h):
            sl = slice(j * 128, (j + 1) * 128)
            args = (st_ref[0, j], q_ref[:, sl], k_ref[:, sl], v_ref[:, sl], la_ref[:, sl])
            ds, dq, dk, dv, dla = jax.vjp(_gla_block, *args)[1]((do_ref[:, sl], dstate[j]))
            dstate[j] = ds
            dq_ref[:, sl] = dq.astype(bf16)
            dk_ref[:, sl] = dk.astype(bf16)
            dv_ref[:, sl] = dv.astype(bf16)
            dla_ref[:, sl] = dla

    rev = lambda i: (nb - 1 - i, 0)
    return pl.pallas_call(
        kern, name="gla_bwd", grid=(nb,),
        in_specs=[pl.BlockSpec((b, 512), lambda i: (nb - 1 - i, GLA_Q0)), pl.BlockSpec((b, 512), lambda i: (nb - 1 - i, GLA_K0)),
                  pl.BlockSpec((b, 512), lambda i: (nb - 1 - i, GLA_V0)), pl.BlockSpec((b, 512), rev),
                  pl.BlockSpec((1, h, 128, 128), lambda i: (nb - 1 - i, 0, 0, 0)), pl.BlockSpec((b, 512), rev)],
        out_specs=[pl.BlockSpec((b, 512), rev)] * 4,
        out_shape=[jax.ShapeDtypeStruct((rows, h * 128), bf16)] * 3 + [jax.ShapeDtypeStruct((rows, h * 128), f32)],
        scratch_shapes=[pltpu.VMEM((h, 128, 128), f32)],
        compiler_params=_cparams(("arbitrary",)),
    )(hmain, hmain, hmain, la, states, do)


def _f_s5_params(a_re, a_im, log_dt, b_re, b_im):
    dt = jnp.exp(log_dt)
    mag = jnp.exp(dt * a_re)
    abar_re, abar_im = mag * jnp.cos(dt * a_im), mag * jnp.sin(dt * a_im)
    den = a_re * a_re + a_im * a_im
    nr, ni = abar_re - 1.0, abar_im
    fr, fi = (nr * a_re + ni * a_im) / den, (ni * a_re - nr * a_im) / den
    return abar_re, abar_im, fr[None] * b_re - fi[None] * b_im, fr[None] * b_im + fi[None] * b_re


def _s5_params(a_re, a_im, log_dt, b_re, b_im):
    def kern(*refs):
        for o_ref, r in zip(refs[5:], _f_s5_params(*[r[...] for r in refs[:5]])):
            o_ref[...] = r

    ins = (a_re, a_im, log_dt, b_re, b_im)
    return pl.pallas_call(
        kern, name="s5_params", out_shape=[jax.ShapeDtypeStruct(a_re.shape, f32)] * 2 + [jax.ShapeDtypeStruct(b_re.shape, f32)] * 2,
    )(*ins)


def _s5_params_bwd(a_re, a_im, log_dt, b_re, b_im, d_ar, d_ai, d_br, d_bi):
    def kern(*refs):
        grads = jax.vjp(_f_s5_params, *[r[...] for r in refs[:5]])[1](tuple(r[...] for r in refs[5:9]))
        for o_ref, g in zip(refs[9:], grads):
            o_ref[...] = g

    ins = (a_re, a_im, log_dt, b_re, b_im)
    return pl.pallas_call(
        kern, name="s5_params_bwd", out_shape=[jax.ShapeDtypeStruct(t.shape, f32) for t in ins],
    )(*ins, d_ar, d_ai, d_br, d_bi)


def _cmul(ar, ai, br, bi):
    return ar * br - ai * bi, ar * bi + ai * br


def _s5_scan(xr, xi, ar, ai, reverse):
    t = xr.shape[0]
    row = _iota(xr.shape, 0)
    s = 1
    while s < t:
        if reverse:
            keep = row < t - s
            sr, si = pltpu.roll(xr, t - s, 0), pltpu.roll(xi, t - s, 0)
        else:
            keep = row >= s
            sr, si = pltpu.roll(xr, s, 0), pltpu.roll(xi, s, 0)
        sr, si = jnp.where(keep, sr, 0.0), jnp.where(keep, si, 0.0)
        pr, pi = _cmul(ar, ai, sr, si)
        xr, xi = xr + pr, xi + pi
        ar, ai = _cmul(ar, ai, ar, ai)
        s *= 2
    return xr, xi


def _s5_powers(ar, ai, t, reverse):
    row = _iota((t, ar.shape[1]), 0)
    at = (row == (t - 1 if reverse else 0))
    return _s5_scan(jnp.where(at, ar, 0.0), jnp.where(at, ai, 0.0), ar, ai, reverse)


S5_U0 = 7168 // 128


def _s5_fwd(hmain, abar, bmat_re, bmat_im, cmat_re, cmat_im, dvec):
    rows = hmain.shape[0]
    t = _pick(rows, S5_TILE)
    nt, ns = rows // t, 512

    def kern(u_ref, a_ref, br_ref, bi_ref, cr_ref, ci_ref, d_ref, y_ref, xr_ref, xi_ref, pw, carry):
        ar, ai = a_ref[0, 0], a_ref[1, 0]

        @pl.when(pl.program_id(1) == 0)
        def _():
            pr, pi = _s5_powers(ar, ai, t, False)
            pw[0], pw[1] = pr, pi
            carry[...] = jnp.zeros_like(carry)

        u = u_ref[...]
        xr, xi = _s5_scan(_rdot(u, br_ref[0]), _rdot(u, bi_ref[0]), ar, ai, False)
        cr, ci = carry[0:1, :], carry[1:2, :]
        qr, qi = _cmul(pw[0], pw[1], cr, ci)
        xr, xi = xr + qr, xi + qi
        xr_ref[...] = xr
        xi_ref[...] = xi
        carry[0:1, :] = xr[t - 1:t, :]
        carry[1:2, :] = xi[t - 1:t, :]
        y_ref[...] = _rdot(xr, cr_ref[0]) - _rdot(xi, ci_ref[0]) + d_ref[...] * u

    sb3 = lambda b, i: (b, 0, 0)
    return pl.pallas_call(
        kern, name="s5_fwd", grid=(S5_SB, nt),
        in_specs=[pl.BlockSpec((t, 128), lambda b, i: (i, S5_U0 + b)), pl.BlockSpec((2, 1, 1, ns), lambda b, i: (0, b, 0, 0)),
                  pl.BlockSpec((1, 128, ns), sb3), pl.BlockSpec((1, 128, ns), sb3),
                  pl.BlockSpec((1, ns, 128), sb3), pl.BlockSpec((1, ns, 128), sb3), pl.BlockSpec((1, 128), lambda b, i: (0, b))],
        out_specs=[pl.BlockSpec((t, 128), lambda b, i: (i, b)), pl.BlockSpec((t, ns), lambda b, i: (i, b)),
                   pl.BlockSpec((t, ns), lambda b, i: (i, b))],
        out_shape=[jax.ShapeDtypeStruct((rows, 512), f32), jax.ShapeDtypeStruct((rows, S5_SB * ns), f32),
                   jax.ShapeDtypeStruct((rows, S5_SB * ns), f32)],
        scratch_shapes=[pltpu.VMEM((2, t, ns), f32), pltpu.VMEM((8, ns), f32)],
        compiler_params=_cparams(("parallel", "arbitrary")),
    )(hmain, abar, bmat_re, bmat_im, cmat_re, cmat_im, dvec)


def _s5_bwd(hmain, abar, bmat_re, bmat_im, cmat_re, cmat_im, dvec, x_re, x_im, dy):
    rows = hmain.shape[0]
    t = _pick(rows, S5_TILE)
    nt, ns = rows // t, 512
    t8 = t // 8

    def kern(u_ref, a_ref, br_ref, bi_ref, cr_ref, ci_ref, d_ref, xr_ref, xi_ref, xpr_ref, xpi_ref, dy_ref,
             du_ref, da_ref, dbr_ref, dbi_ref, dcr_ref, dci_ref, dd_ref, pw, carry):
        i = pl.program_id(1)
        ar, ai = a_ref[0, 0], -a_ref[1, 0]

        @pl.when(i == 0)
        def _():
            pr, pi = _s5_powers(ar, ai, t, True)
            pw[0], pw[1] = pr, pi
            carry[...] = jnp.zeros_like(carry)

        u, gy = u_ref[...], dy_ref[...]
        lr, li = _s5_scan(_rdot(gy, cr_ref[0], NT), -_rdot(gy, ci_ref[0], NT), ar, ai, True)
        qr, qi = _cmul(pw[0], pw[1], carry[0:1, :], carry[1:2, :])
        lr, li = lr + qr, li + qi
        carry[0:1, :] = lr[0:1, :]
        carry[1:2, :] = li[0:1, :]
        du_ref[...] = (_rdot(lr, br_ref[0], NT) + _rdot(li, bi_ref[0], NT) + d_ref[...] * gy).astype(bf16)
        xr, xi = xr_ref[...], xi_ref[...]
        row = _iota(xr.shape, 0)
        first_r = jnp.where(i < nt - 1, xpr_ref[7:8, :], 0.0)
        first_i = jnp.where(i < nt - 1, xpi_ref[7:8, :], 0.0)
        xpr = jnp.where(row == 0, first_r, pltpu.roll(xr, 1, 0))
        xpi = jnp.where(row == 0, first_i, pltpu.roll(xi, 1, 0))
        da_r = jnp.sum(lr * xpr + li * xpi, axis=0, keepdims=True)
        da_i = jnp.sum(li * xpr - lr * xpi, axis=0, keepdims=True)
        upd = [(da_ref.at[0, 0], da_r), (da_ref.at[1, 0], da_i),
               (dbr_ref.at[0], _rdot(u, lr, TN)), (dbi_ref.at[0], _rdot(u, li, TN)),
               (dcr_ref.at[0], _rdot(xr, gy, TN)), (dci_ref.at[0], -_rdot(xi, gy, TN)),
               (dd_ref, jnp.sum(gy * u, axis=0, keepdims=True))]

        @pl.when(i == 0)
        def _():
            for ref, val in upd:
                ref[...] = val

        @pl.when(i > 0)
        def _():
            for ref, val in upd:
                ref[...] += val

    sb3 = lambda b, i: (b, 0, 0)
    rev = lambda b, i: (nt - 1 - i, b)
    prev8 = lambda b, i: (jnp.maximum((nt - 1 - i) * t8 - 1, 0), b)
    return pl.pallas_call(
        kern, name="s5_bwd", grid=(S5_SB, nt),
        in_specs=[pl.BlockSpec((t, 128), lambda b, i: (nt - 1 - i, S5_U0 + b)), pl.BlockSpec((2, 1, 1, ns), lambda b, i: (0, b, 0, 0)),
                  pl.BlockSpec((1, 128, ns), sb3), pl.BlockSpec((1, 128, ns), sb3),
                  pl.BlockSpec((1, ns, 128), sb3), pl.BlockSpec((1, ns, 128), sb3), pl.BlockSpec((1, 128), lambda b, i: (0, b)),
                  pl.BlockSpec((t, ns), rev), pl.BlockSpec((t, ns), rev), pl.BlockSpec((8, ns), prev8), pl.BlockSpec((8, ns), prev8),
                  pl.BlockSpec((t, 128), rev)],
        out_specs=[pl.BlockSpec((t, 128), rev), pl.BlockSpec((2, 1, 1, ns), lambda b, i: (0, b, 0, 0)),
                   pl.BlockSpec((1, 128, ns), sb3), pl.BlockSpec((1, 128, ns), sb3),
                   pl.BlockSpec((1, ns, 128), sb3), pl.BlockSpec((1, ns, 128), sb3), pl.BlockSpec((1, 128), lambda b, i: (0, b))],
        out_shape=[jax.ShapeDtypeStruct((rows, 512), bf16), jax.ShapeDtypeStruct((2, S5_SB, 1, ns), f32),
                   jax.ShapeDtypeStruct((S5_SB, 128, ns), f32), jax.ShapeDtypeStruct((S5_SB, 128, ns), f32),
                   jax.ShapeDtypeStruct((S5_SB, ns, 128), f32), jax.ShapeDtypeStruct((S5_SB, ns, 128), f32),
                   jax.ShapeDtypeStruct((1, 512), f32)],
        scratch_shapes=[pltpu.VMEM((2, t, ns), f32), pltpu.VMEM((8, ns), f32)],
        compiler_params=_cparams(("parallel", "arbitrary")),
    )(hmain, abar, bmat_re, bmat_im, cmat_re, cmat_im, dvec, x_re, x_im, x_re, x_im, dy)


def _loss_head(y, target):
    rows, feat = y.shape
    tile = _pick(rows, 256)

    def kern(y_ref, t_ref, l_ref, dy_ref):
        e = y_ref[...] - t_ref[...]
        dy_ref[...] = e * (1.0 / feat)
        part = jnp.broadcast_to(0.5 * jnp.sum(e * e) * (1.0 / feat), l_ref.shape)

        @pl.when(pl.program_id(0) == 0)
        def _():
            l_ref[...] = part

        @pl.when(pl.program_id(0) > 0)
        def _():
            l_ref[...] += part

    return pl.pallas_call(
        kern, name="loss_head", grid=(rows // tile,),
        in_specs=[pl.BlockSpec((tile, feat), lambda i: (i, 0))] * 2,
        out_specs=[pl.BlockSpec((8, 128), lambda i: (0, 0)), pl.BlockSpec((tile, feat), lambda i: (i, 0))],
        out_shape=[jax.ShapeDtypeStruct((8, 128), f32), jax.ShapeDtypeStruct((rows, feat), f32)],
        compiler_params=_cparams(("arbitrary",)),
    )(y, target)


def _sum_parts(name, parts, out_dtype=f32):
    rows, cols = parts[0].shape
    tile = _pick(rows, 512)

    def kern(*refs):
        acc = refs[0][...].astype(f32)
        for r in refs[1:-1]:
            acc = acc + r[...].astype(f32)
        refs[-1][...] = acc.astype(out_dtype)

    return pl.pallas_call(
        kern, name=name, grid=(rows // tile,),
        in_specs=[pl.BlockSpec((tile, cols), lambda i: (i, 0))] * len(parts),
        out_specs=pl.BlockSpec((tile, cols), lambda i: (i, 0)),
        out_shape=jax.ShapeDtypeStruct((rows, cols), out_dtype),
        compiler_params=_cparams(("parallel",)),
    )(*parts)


ADAMW_BLOCK_BYTES = 1536 * 1024


def _adamw(name, w, m, v, g):
    rows, cols = w.shape
    tile = _pick(rows, 512)
    while tile > 8 and tile * cols * 4 > ADAMW_BLOCK_BYTES and rows % (tile // 2) == 0:
        tile //= 2
    c1, c2 = 1.0 / (1.0 - ADAM_B1 ** ADAM_STEP), 1.0 / (1.0 - ADAM_B2 ** ADAM_STEP)

    def kern(w_ref, m_ref, v_ref, g_ref, d_ref, nm_ref, nv_ref):
        gv = g_ref[...]
        nm = ADAM_B1 * m_ref[...] + (1.0 - ADAM_B1) * gv
        nv = ADAM_B2 * v_ref[...] + (1.0 - ADAM_B2) * (gv * gv)
        nm_ref[...] = nm
        nv_ref[...] = nv
        d_ref[...] = -ADAM_LR * ((nm * c1) / (jnp.sqrt(nv * c2) + ADAM_EPS) + ADAM_WD * w_ref[...])

    spec = pl.BlockSpec((tile, cols), lambda i: (i, 0))
    return pl.pallas_call(
        kern, name=name, grid=(rows // tile,), in_specs=[spec] * 4, out_specs=[spec] * 3,
        out_shape=[jax.ShapeDtypeStruct((rows, cols), f32)] * 3,
        compiler_params=_cparams(("parallel",)),
    )(w, m, v, g)


MESH = pl.DeviceIdType.MESH
HBM_SPEC = pl.BlockSpec(memory_space=pltpu.HBM)


def _other_chips(x, y):
    return [(1 - x, y), (x, 1 - y), (1 - x, 1 - y)]


def _gather_chips(big, small):
    rows, cols = big.shape
    half = rows // 2

    def kern(big_ref, small_ref, bout, sout, ici_send, ici_recv, d2d_send, d2d_recv, small_send, small_recv):
        x, y, c = lax.axis_index("x"), lax.axis_index("y"), lax.axis_index("c")
        me, sibling = 2 * x + y, (x, y, 1 - c)
        chips = _other_chips(x, y)
        slots = [2 * chip[0] + chip[1] for chip in chips]
        sends = [pltpu.make_async_remote_copy(
            src_ref=big_ref.at[c], dst_ref=bout.at[me, c], send_sem=ici_send.at[k], recv_sem=ici_recv.at[k],
            device_id=(*chip, c), device_id_type=MESH) for k, chip in enumerate(chips)]
        sends += [pltpu.make_async_remote_copy(
            src_ref=small_ref, dst_ref=sout.at[me], send_sem=small_send.at[k], recv_sem=small_recv.at[k],
            device_id=(*chip, c), device_id_type=MESH) for k, chip in enumerate(chips)]
        for cp in sends:
            cp.start()
        for k, chip in enumerate(chips):
            pltpu.make_async_remote_copy(
                src_ref=big_ref.at[c], dst_ref=bout.at[slots[k], c], send_sem=ici_send.at[k], recv_sem=ici_recv.at[k],
                device_id=(*chip, c), device_id_type=MESH).wait_recv()
            passed = pltpu.make_async_remote_copy(
                src_ref=bout.at[slots[k], c], dst_ref=bout.at[slots[k], c], send_sem=d2d_send.at[k],
                recv_sem=d2d_recv.at[k], device_id=sibling, device_id_type=MESH)
            passed.start()
            sends.append(passed)
        for k, chip in enumerate(chips):
            pltpu.make_async_remote_copy(
                src_ref=bout.at[slots[k], 1 - c], dst_ref=bout.at[slots[k], 1 - c], send_sem=d2d_send.at[k],
                recv_sem=d2d_recv.at[k], device_id=sibling, device_id_type=MESH).wait_recv()
            pltpu.make_async_remote_copy(
                src_ref=small_ref, dst_ref=sout.at[slots[k]], send_sem=small_send.at[k], recv_sem=small_recv.at[k],
                device_id=(*chip, c), device_id_type=MESH).wait_recv()
        for cp in sends:
            cp.wait_send()

    got_big, got_small = pl.pallas_call(
        kern, name="gather_chips", in_specs=[HBM_SPEC] * 2, out_specs=[HBM_SPEC] * 2,
        out_shape=[jax.ShapeDtypeStruct((N_CHIPS, 2, half, cols), big.dtype), jax.ShapeDtypeStruct((N_CHIPS,) + small.shape, small.dtype)],
        scratch_shapes=[pltpu.SemaphoreType.DMA((3,))] * 6,
    )(big.reshape(2, half, cols), small)
    me = 2 * lax.axis_index("x") + lax.axis_index("y")
    got_big = lax.dynamic_update_index_in_dim(got_big.reshape(N_CHIPS, rows, cols), big, me, 0)
    return got_big, lax.dynamic_update_index_in_dim(got_small, small, me, 0)


def _join_halves(part):
    other = _swap_sibling(part)
    first = lax.axis_index("c") == 0
    return jnp.concatenate([jnp.where(first, part, other), jnp.where(first, other, part)], axis=0)


def _scatter_chips(pieces):
    _, rows, cols = pieces.shape

    def kern(p_ref, out_ref, send_sems, recv_sems):
        x, y, c = lax.axis_index("x"), lax.axis_index("y"), lax.axis_index("c")
        chips = _other_chips(x, y)
        sends = [pltpu.make_async_remote_copy(
            src_ref=p_ref.at[2 * chip[0] + chip[1]], dst_ref=out_ref.at[k], send_sem=send_sems.at[k],
            recv_sem=recv_sems.at[k], device_id=(*chip, c), device_id_type=MESH) for k, chip in enumerate(chips)]
        for cp in sends:
            cp.start()
        for cp in sends:
            cp.wait()

    return pl.pallas_call(
        kern, name="scatter_chips", in_specs=[HBM_SPEC], out_specs=HBM_SPEC,
        out_shape=jax.ShapeDtypeStruct((3, rows, cols), pieces.dtype),
        scratch_shapes=[pltpu.SemaphoreType.DMA((3,)), pltpu.SemaphoreType.DMA((3,))],
    )(pieces)


def _swap_sibling(buf):
    def kern(b_ref, out_ref, send_sem, recv_sem):
        x, y, c = lax.axis_index("x"), lax.axis_index("y"), lax.axis_index("c")
        cp = pltpu.make_async_remote_copy(src_ref=b_ref, dst_ref=out_ref, send_sem=send_sem, recv_sem=recv_sem,
                                          device_id=(x, y, 1 - c), device_id_type=MESH)
        cp.start()
        cp.wait()

    return pl.pallas_call(
        kern, name="swap_sibling", in_specs=[HBM_SPEC], out_specs=HBM_SPEC,
        out_shape=jax.ShapeDtypeStruct(buf.shape, buf.dtype),
        scratch_shapes=[pltpu.SemaphoreType.DMA, pltpu.SemaphoreType.DMA],
    )(buf)


def _gather_all(buf):
    rows, cols = buf.shape

    def kern(b_ref, out_ref, send_sems, recv_sems):
        x, y, c = lax.axis_index("x"), lax.axis_index("y"), lax.axis_index("c")
        me = 4 * x + 2 * y + c
        peers = []
        for k in range(1, N_DEV):
            fx, fy, fc = (k >> 2) & 1, (k >> 1) & 1, k & 1
            peers.append((x ^ fx, y ^ fy, c ^ fc))
        sends = [pltpu.make_async_remote_copy(
            src_ref=b_ref, dst_ref=out_ref.at[me], send_sem=send_sems.at[k], recv_sem=recv_sems.at[k],
            device_id=peer, device_id_type=MESH) for k, peer in enumerate(peers)]
        for cp in sends:
            cp.start()
        for k, peer in enumerate(peers):
            pltpu.make_async_remote_copy(
                src_ref=b_ref, dst_ref=out_ref.at[4 * peer[0] + 2 * peer[1] + peer[2]], send_sem=send_sems.at[k],
                recv_sem=recv_sems.at[k], device_id=peer, device_id_type=MESH).wait_recv()
        for cp in sends:
            cp.wait_send()

    got = pl.pallas_call(
        kern, name="gather_all", in_specs=[HBM_SPEC], out_specs=HBM_SPEC,
        out_shape=jax.ShapeDtypeStruct((N_DEV, rows, cols), buf.dtype),
        scratch_shapes=[pltpu.SemaphoreType.DMA((N_DEV - 1,)), pltpu.SemaphoreType.DMA((N_DEV - 1,))],
    )(buf)
    me = 4 * lax.axis_index("x") + 2 * lax.axis_index("y") + lax.axis_index("c")
    return lax.dynamic_update_index_in_dim(got, buf, me, 0)


def _sum_slots(name, buf):
    n, rows, cols = buf.shape
    tile = _pick(rows, 256)

    def kern(b_ref, o_ref):
        acc = b_ref[0]
        for k in range(1, n):
            acc = acc + b_ref[k]
        o_ref[...] = acc

    return pl.pallas_call(
        kern, name=name, grid=(rows // tile,),
        in_specs=[pl.BlockSpec((n, tile, cols), lambda i: (0, i, 0))], out_specs=pl.BlockSpec((tile, cols), lambda i: (i, 0)),
        out_shape=jax.ShapeDtypeStruct((rows, cols), f32), compiler_params=_cparams(("parallel",)),
    )(buf)


def _pad_heads(t):
    r = t.shape[0]
    return jnp.pad(t.reshape(r, GLA_HEADS, GLA_DK), ((0, 0), (0, 0), (0, 128 - GLA_DK))).reshape(r, GLA_HEADS * 128)


def _unpad_heads(t):
    r = t.shape[0]
    return t.reshape(r, GLA_HEADS, 128)[:, :, :GLA_DK].reshape(r, GLA_HEADS * GLA_DK)


def _blockdiag(t):
    _, r, c = t.shape
    eye = jnp.eye(8, dtype=t.dtype).reshape(1, 8, 1, 8, 1)
    return (t.reshape(S5_SB, 8, r, 1, c) * eye).reshape(S5_SB, 8 * r, 8 * c)


def _blockdiag_extract(m, r, c):
    m5 = m.reshape(S5_SB, 8, r, 8, c)
    return jnp.stack([m5[:, g, :, g, :] for g in range(8)], axis=1).reshape(S5_GROUPS, r, c)


def _row(v, width=None):
    v = v[None]
    return v if width is None else jnp.pad(v, ((0, 0), (0, width - v.shape[1])))


def _layer_operands(p):
    w_in = p['w_in']

    def seg(n):
        return w_in[:, IN_ORIG[n][0]:IN_ORIG[n][1]]

    o = dict(p)
    o['wm'] = jnp.concatenate([seg('dn_qkv'), seg('dn_gate'), seg('cf'), seg('gates'), seg('s5'), _pad_heads(seg('gla_q')),
                               _pad_heads(seg('gla_k')), seg('gla_v'), seg('gla_g')], axis=1)
    o['ws'] = jnp.pad(jnp.concatenate([seg('dn_a'), seg('dn_b'), seg('gla_lr')], axis=1), ((0, 0), (0, WS_COLS - 24)))
    o['dn_conv8'] = jnp.pad(p['dn_conv'], ((0, 4), (0, 0)))
    o['a_log_r'] = _row(p['dn_a_log'], 128)
    o['dt_bias_r'] = _row(p['dn_dt_bias'], 128)
    o['dn_norm_r'] = _row(p['dn_norm'])
    o['cf_dw32'] = jnp.pad(p['cf_dw'], ((0, 1), (0, 0)))
    o['cf_bias_r'], o['cf_g_r'], o['cf_b_r'] = _row(p['cf_dw_bias']), _row(p['cf_ln_g']), _row(p['cf_ln_b'])
    o['w_alpha_p'] = jnp.pad(_pad_heads(p['gla_w_alpha']), ((8, 128 - 24), (0, 0)))
    o['b_alpha_r'] = _pad_heads(_row(p['gla_b_alpha']))
    o['gla_norm_r'] = _row(p['gla_norm'])
    o['ln1_g_r'], o['ln1_b_r'], o['ln2_g_r'], o['ln2_b_r'] = (_row(p[n]) for n in ('ln1_g', 'ln1_b', 'ln2_g', 'ln2_b'))
    o['ffn_conv8'] = jnp.pad(p['ffn_conv'], ((0, 5), (0, 0)))
    o['s5_in'] = (p['s5_a_re'], p['s5_a_im'], p['s5_log_dt'][:, None],
                  p['s5_b_re'].transpose(2, 0, 1), p['s5_b_im'].transpose(2, 0, 1))
    abar_re, abar_im, bbar_re, bbar_im = _s5_params(*o['s5_in'])
    o['abar'] = jnp.stack([abar_re, abar_im]).reshape(2, S5_SB, 1, 512)
    o['bmat_re'], o['bmat_im'] = _blockdiag(bbar_re.transpose(1, 0, 2)), _blockdiag(bbar_im.transpose(1, 0, 2))
    o['cmat_re'], o['cmat_im'] = _blockdiag(p['s5_c_re'].transpose(0, 2, 1)), _blockdiag(p['s5_c_im'].transpose(0, 2, 1))
    o['dvec'] = _row(p['s5_d'])
    return o


def _whole(a):
    return (a, a.shape[1], 0)


def _merge_ins(h, s):
    return [(h, 1024, 3), (h, 1024, 4), (h, 1024, 5), (h, 1024, 6), (s['y_a'], 1024, 0), (s['y_b'], 1024, 0),
            (s['zz'], 1024, 0), (s['zz'], 1024, 1), (s['y_d'], 1024, 0)]


def _layer_fwd(x, o):
    s = {}
    h = s['h'] = _mm(x, o['wm'], 'nn', 'mm_h')
    hs = s['hs'] = _mm(x, o['ws'], 'nn', 'mm_hs')
    s['c1'] = _conv_fwd('conv_dn', h, 1536, 0, o['dn_conv8'], 4)
    s['qkvn'], s['gb'], s['la'] = _rowwise('pre', _f_pre, [_whole(s['c1']), _whole(hs)],
                                           [o['a_log_r'], o['dt_bias_r'], o['w_alpha_p'], o['b_alpha_r']], [1536, 128, 512])
    s['o_dn'], s['st_dn'], s['inv_dn'] = _dn_fwd(s['qkvn'], s['gb'])
    (s['on_dn'],) = _rowwise('post_dn', _f_post, [_whole(s['o_dn']), (h, 512, 3)], [o['dn_norm_r']], [512], out_dtype=bf16)
    s['y_a'] = _mm(s['on_dn'], o['w_br_dn'], 'nn', 'mm_br')
    (s['cfp'],) = _rowwise('glu_cf', _f_glu, [(h, 512, 4), (h, 512, 5)], [], [512], out_dtype=bf16)
    s['cc'] = _conv_fwd('conv_cf', s['cfp'], 512, 0, o['cf_dw32'], 31)
    (s['cfo'],) = _rowwise('post_cf', _f_cfpost, [_whole(s['cc'])], [o['cf_bias_r'], o['cf_g_r'], o['cf_b_r']], [512], out_dtype=bf16)
    s['y_b'] = _mm(s['cfo'], o['w_br_cf'], 'nn', 'mm_br')
    s['ys5'], s['xr'], s['xi'] = _s5_fwd(h, o['abar'], o['bmat_re'], o['bmat_im'], o['cmat_re'], o['cmat_im'], o['dvec'])
    (s['z'],) = _rowwise('gelu', _f_gelu, [_whole(s['ys5'])], [], [512], out_dtype=bf16)
    s['zz'] = _mm(s['z'], o['w_br_s5'], 'nn', 'mm_br_s5')
    s['o_gla'], s['st_gla'] = _gla_fwd(h, s['la'])
    (s['on_gla'],) = _rowwise('post_gla', _f_post, [_whole(s['o_gla']), (h, 512, 18)], [o['gla_norm_r']], [512], out_dtype=bf16)
    s['y_d'] = _mm(s['on_gla'], o['w_br_gla'], 'nn', 'mm_br')
    (s['merged'],) = _rowwise('merge', _f_merge, _merge_ins(h, s), [], [1024], tile=128, out_dtype=bf16)
    s['mix'] = _mm(s['merged'], o['w_o'], 'nn', 'mm_o')
    (s['x1'],) = _rowwise('ln', _f_ln, [_whole(x), _whole(s['mix'])], [o['ln1_g_r'], o['ln1_b_r']], [1024])
    s['up'] = _mm(s['x1'], o['w_up'], 'nn', 'mm_up', out_dtype=bf16)
    s['u'] = _conv_fwd('conv_ffn', s['up'], 2 * D_FF, 0, o['ffn_conv8'], 3)
    (s['act'],) = _rowwise('act', _f_act, [(s['u'], D_FF, 0), (s['u'], D_FF, 1)], [], [D_FF], out_dtype=bf16)
    s['ffn'] = _mm(s['act'], o['w_down'], 'nn', 'mm_down')
    (x2,) = _rowwise('ln', _f_ln, [_whole(s['x1']), _whole(s['ffn'])], [o['ln2_g_r'], o['ln2_b_r']], [1024])
    return x2, s


def _layer_bwd(x, o, s, dparts):
    h, g = s['h'], {}
    (dx1_a, dffn), (g['ln2_g'], g['ln2_b']) = _rowwise_bwd(
        'ln_bwd', _f_ln, [_whole(s['x1']), _whole(s['ffn'])], [o['ln2_g_r'], o['ln2_b_r']], [[_whole(d) for d in dparts]], [f32, bf16])
    dact = _mm(dffn, o['w_down'], 'nt', 'mm_down_dx')
    g['w_down'] = _mm(s['act'], dffn, 'tn', 'mm_down_dw', out_dtype=bf16)
    (du_a, du_b), _ = _rowwise_bwd('act_bwd', _f_act, [(s['u'], D_FF, 0), (s['u'], D_FF, 1)], [], [[_whole(dact)]], [bf16, bf16])
    dup, dw = _conv_bwd('conv_ffn_bwd', s['up'], 2 * D_FF, 0, o['ffn_conv8'], 3, jnp.concatenate([du_a, du_b], axis=1), dx_dtype=bf16)
    g['ffn_conv'] = dw[:3]
    dx1_b = _mm(dup, o['w_up'], 'nt', 'mm_up_dx')
    g['w_up'] = _mm(s['x1'], dup, 'tn', 'mm_up_dw', out_dtype=bf16)
    (dx_a, dmix), (g['ln1_g'], g['ln1_b']) = _rowwise_bwd(
        'ln_bwd', _f_ln, [_whole(x), _whole(s['mix'])], [o['ln1_g_r'], o['ln1_b_r']], [[_whole(dx1_a), _whole(dx1_b)]], [f32, bf16])
    dmerged = _mm(dmix, o['w_o'], 'nt', 'mm_o_dx')
    g['w_o'] = _mm(s['merged'], dmix, 'tn', 'mm_o_dw', out_dtype=bf16)
    (dga, dgb_, dgc, dgd, dya, dyb, dzv, dzg, dyd), _ = _rowwise_bwd(
        'merge_bwd', _f_merge, _merge_ins(h, s), [], [[_whole(dmerged)]], [bf16] * 9, tile=128)
    dzz = jnp.concatenate([dzv, dzg], axis=1)
    don = _mm(dya, o['w_br_dn'], 'nt', 'mm_br_dx')
    g['w_br_dn'] = _mm(s['on_dn'], dya, 'tn', 'mm_br_dw', out_dtype=bf16)
    (do_dn, dgate_dn), (g['dn_norm'],) = _rowwise_bwd(
        'post_bwd', _f_post, [_whole(s['o_dn']), (h, 512, 3)], [o['dn_norm_r']], [[_whole(don)]], [f32, bf16])
    dqkvn, dgb = _dn_bwd(s['qkvn'], s['gb'], s['st_dn'], s['inv_dn'], do_dn)
    don = _mm(dyd, o['w_br_gla'], 'nt', 'mm_br_dx')
    g['w_br_gla'] = _mm(s['on_gla'], dyd, 'tn', 'mm_br_dw', out_dtype=bf16)
    (do_gla, dgate_gla), (g['gla_norm'],) = _rowwise_bwd(
        'post_bwd', _f_post, [_whole(s['o_gla']), (h, 512, 18)], [o['gla_norm_r']], [[_whole(don)]], [f32, bf16])
    dq_gla, dk_gla, dv_gla, dla = _gla_bwd(h, s['la'], s['st_gla'], do_gla)
    (dc1, dhs), (d_alog, d_dtb, d_walpha, d_balpha) = _rowwise_bwd(
        'pre_bwd', _f_pre, [_whole(s['c1']), _whole(s['hs'])], [o['a_log_r'], o['dt_bias_r'], o['w_alpha_p'], o['b_alpha_r']],
        [[_whole(dqkvn)], [_whole(dgb)], [_whole(dla)]], [bf16, bf16])
    g['dn_a_log'], g['dn_dt_bias'] = d_alog[0, :DN_HEADS], d_dtb[0, :DN_HEADS]
    g['gla_w_alpha'], g['gla_b_alpha'] = _unpad_heads(d_walpha[8:24]), _unpad_heads(d_balpha)[0]
    d_dnqkv, dw = _conv_bwd('conv_dn_bwd', h, 1536, 0, o['dn_conv8'], 4, dc1, dx_dtype=bf16)
    g['dn_conv'] = dw[:4]
    dcfo = _mm(dyb, o['w_br_cf'], 'nt', 'mm_br_dx')
    g['w_br_cf'] = _mm(s['cfo'], dyb, 'tn', 'mm_br_dw', out_dtype=bf16)
    (dcc,), (g['cf_dw_bias'], g['cf_ln_g'], g['cf_ln_b']) = _rowwise_bwd(
        'post_cf_bwd', _f_cfpost, [_whole(s['cc'])], [o['cf_bias_r'], o['cf_g_r'], o['cf_b_r']], [[_whole(dcfo)]], [bf16])
    dcfp, dw = _conv_bwd('conv_cf_bwd', s['cfp'], 512, 0, o['cf_dw32'], 31, dcc)
    g['cf_dw'] = dw[:31]
    (dcf_a, dcf_g), _ = _rowwise_bwd('glu_bwd', _f_glu, [(h, 512, 4), (h, 512, 5)], [], [[_whole(dcfp)]], [bf16, bf16])
    dz = _mm(dzz, o['w_br_s5'], 'nt', 'mm_br_s5_dx')
    g['w_br_s5'] = _mm(s['z'], dzz, 'tn', 'mm_br_s5_dw', out_dtype=bf16)
    (dys5,), _ = _rowwise_bwd('gelu_bwd', _f_gelu, [_whole(s['ys5'])], [], [[_whole(dz)]], [f32])
    du_s5, d_abar, dbm_re, dbm_im, dcm_re, dcm_im, d_dvec = _s5_bwd(
        h, o['abar'], o['bmat_re'], o['bmat_im'], o['cmat_re'], o['cmat_im'], o['dvec'], s['xr'], s['xi'], dys5)
    d_bbar = [_blockdiag_extract(m, S5_GROUP, S5_STATE).transpose(1, 0, 2) for m in (dbm_re, dbm_im)]
    da_re, da_im, dlog_dt, db_re, db_im = _s5_params_bwd(
        *o['s5_in'], d_abar[0].reshape(S5_GROUPS, S5_STATE), d_abar[1].reshape(S5_GROUPS, S5_STATE), *d_bbar)
    g['s5_a_re'], g['s5_a_im'], g['s5_log_dt'] = da_re, da_im, dlog_dt[:, 0]
    g['s5_b_re'], g['s5_b_im'] = db_re.transpose(1, 2, 0), db_im.transpose(1, 2, 0)
    g['s5_c_re'], g['s5_c_im'] = (_blockdiag_extract(m, S5_STATE, S5_GROUP).transpose(0, 2, 1) for m in (dcm_re, dcm_im))
    g['s5_d'] = d_dvec[0]
    for n in ('dn_norm', 'gla_norm', 'cf_dw_bias', 'cf_ln_g', 'cf_ln_b', 'ln1_g', 'ln1_b', 'ln2_g', 'ln2_b'):
        g[n] = g[n][0]
    dh = jnp.concatenate([d_dnqkv, dgate_dn, dcf_a, dcf_g, dga, dgb_, dgc, dgd, du_s5, dq_gla, dk_gla, dv_gla, dgate_gla], axis=1)
    dwm = _mm(x, dh, 'tn', 'mm_h_dw', out_dtype=bf16)
    dws = _mm(x, dhs, 'tn', 'mm_hs_dw', out_dtype=bf16)
    g['w_in'] = jnp.concatenate([
        dwm[:, 0:1536], dws[:, 0:8], dwm[:, 1536:2048], dwm[:, 2048:3072], dwm[:, 7168:7680], _unpad_heads(dwm[:, 7680:8192]),
        _unpad_heads(dwm[:, 8192:8704]), dwm[:, 8704:9216], dwm[:, 9216:9728], dws[:, 8:24], dwm[:, 3072:7168]], axis=1)
    return [dx_a, _mm(dh, o['wm'], 'nt', 'mm_h_dx'), _mm(dhs, o['ws'], 'nt', 'mm_hs_dx')], g


WEIGHTS = ('w_in', 'dn_conv', 'dn_a_log', 'dn_dt_bias', 'dn_norm', 'w_br_dn', 'cf_dw', 'cf_dw_bias', 'cf_ln_g', 'cf_ln_b',
           'w_br_cf', 's5_a_re', 's5_a_im', 's5_log_dt', 's5_b_re', 's5_b_im', 's5_c_re', 's5_c_im', 's5_d', 'w_br_s5',
           'gla_w_alpha', 'gla_b_alpha', 'gla_norm', 'w_br_gla', 'w_o', 'ln1_g', 'ln1_b', 'w_up', 'ffn_conv', 'w_down',
           'ln2_g', 'ln2_b')
LARGE = ('w_in', 'w_br_dn', 'w_br_cf', 'w_br_s5', 'w_br_gla', 'w_o', 'w_up', 'w_down')
SHARD_AXIS = dict(w_in=2, w_br_dn=2, w_br_cf=2, w_br_s5=2, w_br_gla=2, w_o=1, w_up=2, w_down=1,
                  dn_conv=2, cf_dw=2, gla_w_alpha=2, ffn_conv=2)
SMALL = tuple(n for n in WEIGHTS if n not in LARGE)
SMALL_SHARDED = tuple(n for n in SMALL if n in SHARD_AXIS)


def _local_step(x, target, full):
    ops, saved, xs = [], [], [x]
    for l in range(DEPTH):
        o = _layer_operands({n: full[n][l] for n in WEIGHTS})
        y, s = _layer_fwd(xs[-1], o)
        ops.append(o)
        saved.append(s)
        xs.append(y)
    loss, dy = _loss_head(xs[-1], target)
    dparts, grads = [dy], [None] * DEPTH
    for l in reversed(range(DEPTH)):
        dparts, grads[l] = _layer_bwd(xs[l], ops[l], saved[l], dparts)
    grad_x = _sum_parts('sum_dx', dparts)
    small = {n: jnp.stack([grads[l][n] for l in range(DEPTH)]) for n in SMALL}
    return loss[0, 0], grad_x, small, {n: [grads[l][n] for l in range(DEPTH)] for n in LARGE}


def _pack(arrs, rows, dtype=f32):
    flat = jnp.concatenate([a.reshape(-1).astype(dtype) for a in arrs])
    return jnp.pad(flat, (0, rows * 1024 - flat.shape[0])).reshape(rows, 1024)


def _unpack(buf, shapes):
    flat, out, pos = buf.reshape(-1), [], 0
    for shp in shapes:
        n = 1
        for d in shp:
            n *= d
        out.append(flat[pos:pos + n].reshape(shp))
        pos += n
    return out


def _rows_for(shapes, mult):
    n = 0
    for shp in shapes:
        k = 1
        for d in shp:
            k *= d
        n += k
    rows = -(-n // 1024)
    return -(-rows // mult) * mult


def _shard(a, axis, chip):
    size = a.shape[axis] // N_CHIPS
    return lax.dynamic_slice_in_dim(a, chip * size, size, axis)


def kernel(x, w_in, dn_conv, dn_a_log, dn_dt_bias, dn_norm, w_br_dn, cf_dw, cf_dw_bias, cf_ln_g, cf_ln_b, w_br_cf, s5_a_re, s5_a_im, s5_log_dt, s5_b_re, s5_b_im, s5_c_re, s5_c_im, s5_d, w_br_s5, gla_w_alpha, gla_b_alpha, gla_norm, w_br_gla, w_o, ln1_g, ln1_b, w_up, ffn_conv, w_down, ln2_g, ln2_b, loss_target, m_w_in, m_dn_conv, m_dn_a_log, m_dn_dt_bias, m_dn_norm, m_w_br_dn, m_cf_dw, m_cf_dw_bias, m_cf_ln_g, m_cf_ln_b, m_w_br_cf, m_s5_a_re, m_s5_a_im, m_s5_log_dt, m_s5_b_re, m_s5_b_im, m_s5_c_re, m_s5_c_im, m_s5_d, m_w_br_s5, m_gla_w_alpha, m_gla_b_alpha, m_gla_norm, m_w_br_gla, m_w_o, m_ln1_g, m_ln1_b, m_w_up, m_ffn_conv, m_w_down, m_ln2_g, m_ln2_b, v_w_in, v_dn_conv, v_dn_a_log, v_dn_dt_bias, v_dn_norm, v_w_br_dn, v_cf_dw, v_cf_dw_bias, v_cf_ln_g, v_cf_ln_b, v_w_br_cf, v_s5_a_re, v_s5_a_im, v_s5_log_dt, v_s5_b_re, v_s5_b_im, v_s5_c_re, v_s5_c_im, v_s5_d, v_w_br_s5, v_gla_w_alpha, v_gla_b_alpha, v_gla_norm, v_w_br_gla, v_w_o, v_ln1_g, v_ln1_b, v_w_up, v_ffn_conv, v_w_down, v_ln2_g, v_ln2_b):
    env = locals()
    w = {n: env[n] for n in WEIGHTS}
    m = {n: env['m_' + n] for n in WEIGHTS}
    v = {n: env['v_' + n] for n in WEIGHTS}
    chip = 2 * lax.axis_index("x") + lax.axis_index("y")

    large_shapes = [w[n].shape for n in LARGE]
    ssh_shapes = [w[n].shape for n in SMALL_SHARDED]
    large_rows, ssh_rows = _rows_for(large_shapes, 512), _rows_for(ssh_shapes, 8)
    got_large, got_ssh = _gather_chips(_pack([w[n] for n in LARGE], large_rows, bf16),
                                       _pack([w[n] for n in SMALL_SHARDED], ssh_rows))
    full = {n: w[n] for n in SMALL if n not in SHARD_AXIS}
    per_chip = [dict(zip(LARGE + SMALL_SHARDED, _unpack(got_large[k], large_shapes) + _unpack(got_ssh[k], ssh_shapes)))
                for k in range(N_CHIPS)]
    for n in LARGE + SMALL_SHARDED:
        full[n] = jnp.concatenate([per_chip[k][n] for k in range(N_CHIPS)], axis=SHARD_AXIS[n])

    loss, grad_x, g, g_large = _local_step(x[0], loss_target[0], full)
    loss = lax.psum(loss, ("x", "y", "c"))

    def piece(k):
        parts = []
        for n in LARGE:
            axis = SHARD_AXIS[n] - 1
            size = g_large[n][0].shape[axis] // N_CHIPS
            parts += [lax.slice_in_dim(layer, k * size, (k + 1) * size, axis=axis) for layer in g_large[n]]
        return _pack(parts, large_rows, bf16)

    half = large_rows // 2
    core = lax.axis_index("c")
    pieces = jnp.stack([piece(k) for k in range(N_CHIPS)])
    p_mine = lax.dynamic_slice_in_dim(pieces, core * half, half, axis=1).reshape(N_CHIPS * half, 1024)
    p_theirs = lax.dynamic_slice_in_dim(pieces, (1 - core) * half, half, axis=1).reshape(N_CHIPS * half, 1024)
    pair = _sum_parts('sum_pair', [p_mine, _swap_sibling(p_theirs)], bf16).reshape(N_CHIPS, half, 1024)
    from_chips = _scatter_chips(pair)
    own = lax.dynamic_index_in_dim(pair, chip, 0, keepdims=False)
    chip_sum = _join_halves(_sum_parts('sum_chips', [own, from_chips[0], from_chips[1], from_chips[2]]))
    res = {0: dict(zip(LARGE, _unpack(chip_sum, large_shapes))), 1: {}, 2: {}, 3: {}}
    for n in LARGE:
        two_d = (w[n].shape[0] * w[n].shape[1], w[n].shape[2])
        upd = _adamw('adamw_large', w[n].reshape(two_d), m[n].reshape(two_d), v[n].reshape(two_d), res[0][n].reshape(two_d))
        for kind in range(3):
            res[kind + 1][n] = upd[kind].reshape(w[n].shape)

    small_full_shapes = [g[n].shape for n in SMALL]
    small_rows = _rows_for(small_full_shapes, 8)
    small_sum = _sum_slots('sum_devices', _gather_all(_pack([g[n] for n in SMALL], small_rows)))
    gs = dict(zip(SMALL, _unpack(small_sum, small_full_shapes)))
    for n in SMALL_SHARDED:
        gs[n] = _shard(gs[n], SHARD_AXIS[n], chip)
    small_shapes = [w[n].shape for n in SMALL]
    upd_rows = _rows_for(small_shapes, 8)
    upd = _adamw('adamw_small', _pack([w[n] for n in SMALL], upd_rows), _pack([m[n] for n in SMALL], upd_rows),
                 _pack([v[n] for n in SMALL], upd_rows), _pack([gs[n] for n in SMALL], upd_rows))
    res[0].update(gs)
    for kind in range(3):
        res[kind + 1].update(zip(SMALL, _unpack(upd[kind], small_shapes)))
    return (loss, grad_x[None], *[res[kind][n] for kind in range(4) for n in WEIGHTS])
```

```python
import functools

import jax
import jax.numpy as jnp
from jax import lax
from jax.experimental import pallas as pl
from jax.experimental.pallas import tpu as pltpu

f32 = jnp.float32
bf16 = jnp.bfloat16
HI = lax.Precision.HIGHEST

D_MODEL = 1024
DEPTH = 4
DN_HEADS, DN_DK, DN_CHUNK = 4, 128, 64
DN_STEP_CHUNKS = 2
GLA_HEADS, GLA_DK, GLA_CHUNK, GLA_TAU = 4, 64, 16, 16.0
GLA_BLOCK = 128
S5_GROUPS, S5_GROUP, S5_STATE = 32, 16, 64
S5_SB = 4
S5_TILE = 256
D_FF = 2816
LN_EPS = 1e-5
ALPHA = (2.0 * DEPTH) ** 0.25
ADAM_LR, ADAM_B1, ADAM_B2, ADAM_EPS, ADAM_WD, ADAM_STEP = 0.001, 0.9, 0.999, 1e-08, 0.01, 10

VMEM_LIMIT_V7X = 56 * 1024 * 1024
MM_A_BLOCK_BYTES = 8 * 1024 * 1024
HALO = 32
N_CHIPS = 4
N_DEV = 8

IN_ORIG = dict(dn_qkv=(0, 1536), dn_a=(1536, 1540), dn_b=(1540, 1544), dn_gate=(1544, 2056), cf=(2056, 3080),
               s5=(3080, 3592), gla_q=(3592, 3848), gla_k=(3848, 4104), gla_v=(4104, 4616), gla_g=(4616, 5128),
               gla_lr=(5128, 5144), gates=(5144, 9240))
IN_COLS = 9240
WM_COLS = 9728
WS_COLS = 128


def _cparams(sem):
    return pltpu.CompilerParams(dimension_semantics=sem, vmem_limit_bytes=VMEM_LIMIT_V7X)


def _pick(dim, pref):
    for t in (pref, 512, 256, 128, 64, 32, 16, 8):
        if t <= pref and dim % t == 0:
            return t
    return dim


NN = (((1,), (0,)), ((), ()))
NT = (((1,), (1,)), ((), ()))
TN = (((0,), (0,)), ((), ()))


def _dot(a, b, dims=NN):
    return lax.dot_general(a, b, dims, precision=HI, preferred_element_type=f32)


def _round(a):
    return a.astype(bf16).astype(f32)


def _rdot(a, b, dims=NN):
    return lax.dot_general(a.astype(bf16), b.astype(bf16), dims, preferred_element_type=f32)


@functools.partial(jax.custom_vjp, nondiff_argnums=(2,))
def _bdot(a, b, dims=NN):
    return _rdot(a, b, dims)


def _bdot_fwd(a, b, dims):
    return _rdot(a, b, dims), (a, b)


def _bdot_bwd(dims, res, g):
    a, b = res
    if dims == NN:
        return _rdot(g, b, NT), _rdot(a, g, TN)
    if dims == NT:
        return _rdot(g, b, NN), _rdot(g, a, TN)
    assert dims == TN
    return _rdot(b, g, NT), _rdot(a, g, NN)


_bdot.defvjp(_bdot_fwd, _bdot_bwd)


def _iota(shape, axis):
    return lax.broadcasted_iota(jnp.int32, shape, axis)


def _mm(a, b, mode, name, tn=512, out_dtype=f32):
    if mode == 'nn':
        (m, k), n = a.shape, b.shape[1]
    elif mode == 'nt':
        (m, k), n = a.shape, b.shape[0]
    else:
        (k, m), n = a.shape, b.shape[1]
    tm, tn = _pick(m, 512 if mode == 'tn' else 1024), _pick(n, tn)
    nk = 1
    while (k // nk) * tm * a.dtype.itemsize > MM_A_BLOCK_BYTES or k % nk or (k // nk) % 128:
        nk += 1
    tk = k // nk
    assert nk == 1 or out_dtype == f32
    if mode == 'nn':
        dims = NN
        a_spec = pl.BlockSpec((tm, tk), lambda i, j, kk: (i, kk))
        b_spec = pl.BlockSpec((tk, tn), lambda i, j, kk: (kk, j))
    elif mode == 'nt':
        dims = NT
        a_spec = pl.BlockSpec((tm, tk), lambda i, j, kk: (i, kk))
        b_spec = pl.BlockSpec((tn, tk), lambda i, j, kk: (j, kk))
    else:
        dims = TN
        a_spec = pl.BlockSpec((tk, tm), lambda i, j, kk: (kk, i))
        b_spec = pl.BlockSpec((tk, tn), lambda i, j, kk: (kk, j))

    def kern(a_ref, b_ref, o_ref):
        p = lax.dot_general(a_ref[...].astype(bf16), b_ref[...].astype(bf16), dims, preferred_element_type=f32)
        if nk == 1:
            o_ref[...] = p.astype(o_ref.dtype)
        else:
            kk = pl.program_id(2)

            @pl.when(kk == 0)
            def _():
                o_ref[...] = p

            @pl.when(kk > 0)
            def _():
                o_ref[...] += p

    return pl.pallas_call(
        kern, name=name, grid=(m // tm, n // tn, nk),
        in_specs=[a_spec, b_spec], out_specs=pl.BlockSpec((tm, tn), lambda i, j, kk: (i, j)),
        out_shape=jax.ShapeDtypeStruct((m, n), out_dtype),
        compiler_params=_cparams(("parallel", "parallel", "arbitrary")),
    )(a, b)


def _full_spec(p):
    nd = p.ndim
    return pl.BlockSpec(p.shape, lambda *_, nd=nd: (0,) * nd)


def _rowwise(name, f, ins, params, out_widths, tile=256, out_dtype=f32):
    rows = ins[0][0].shape[0]
    tile = _pick(rows, tile)
    n_x = len(ins) + len(params)

    def kern(*refs):
        for o_ref, r in zip(refs[n_x:], f(*[r[...] for r in refs[:n_x]])):
            o_ref[...] = r.astype(o_ref.dtype)

    in_specs = [pl.BlockSpec((tile, w), lambda i, c=c: (i, c)) for (_, w, c) in ins] + [_full_spec(p) for p in params]
    return pl.pallas_call(
        kern, name=name, grid=(rows // tile,), in_specs=in_specs,
        out_specs=[pl.BlockSpec((tile, w), lambda i: (i, 0)) for w in out_widths],
        out_shape=[jax.ShapeDtypeStruct((rows, w), out_dtype) for w in out_widths],
        compiler_params=_cparams(("parallel",)),
    )(*[a for (a, _, _) in ins], *params)


def _rowwise_bwd(name, f, ins, params, douts, want, tile=256):
    rows = ins[0][0].shape[0]
    tile = _pick(rows, tile)
    n_in, n_p = len(ins), len(params)
    parts = [p for d in douts for p in d]
    n_x, n_d = n_in + n_p, len(parts)

    def kern(*refs):
        xs = [r[...] for r in refs[:n_x]]
        d_refs, o_refs = refs[n_x:n_x + n_d], refs[n_x + n_d:]
        cts, pos = [], 0
        for d in douts:
            acc = d_refs[pos][...]
            for r in d_refs[pos + 1:pos + len(d)]:
                acc = acc + r[...]
            pos += len(d)
            cts.append(acc)
        grads = jax.vjp(f, *xs)[1](tuple(cts))
        k = 0
        for j in range(n_in):
            if want[j]:
                o_refs[k][...] = grads[j].astype(o_refs[k].dtype)
                k += 1
        first = pl.program_id(0) == 0
        for j in range(n_p):
            g, o_ref = grads[n_in + j], o_refs[k + j]

            @pl.when(first)
            def _(o_ref=o_ref, g=g):
                o_ref[...] = g

            @pl.when(jnp.logical_not(first))
            def _(o_ref=o_ref, g=g):
                o_ref[...] += g

    in_specs = ([pl.BlockSpec((tile, w), lambda i, c=c: (i, c)) for (_, w, c) in ins] + [_full_spec(p) for p in params]
                + [pl.BlockSpec((tile, w), lambda i, c=c: (i, c)) for (_, w, c) in parts])
    out_specs, out_shape = [], []
    for j in range(n_in):
        if want[j]:
            out_specs.append(pl.BlockSpec((tile, ins[j][1]), lambda i: (i, 0)))
            out_shape.append(jax.ShapeDtypeStruct((rows, ins[j][1]), want[j]))
    n_g = len(out_specs)
    for p in params:
        out_specs.append(_full_spec(p))
        out_shape.append(jax.ShapeDtypeStruct(p.shape, f32))
    res = pl.pallas_call(
        kern, name=name, grid=(rows // tile,), in_specs=in_specs, out_specs=out_specs, out_shape=out_shape,
        compiler_params=_cparams(("arbitrary",)),
    )(*[a for (a, _, _) in ins], *params, *[a for (a, _, _) in parts])
    return list(res[:n_g]), list(res[n_g:])


_sigmoid = jax.nn.sigmoid
_silu = jax.nn.silu
_softplus = jax.nn.softplus
_log_sigmoid = jax.nn.log_sigmoid


def _f_ln(x, r, g, b):
    t = ALPHA * x + r
    mu = jnp.mean(t, -1, keepdims=True)
    var = jnp.mean(jnp.square(t - mu), -1, keepdims=True)
    return ((t - mu) * lax.rsqrt(var + LN_EPS) * g + b,)


def _f_pre(c1, hs, a_log, dt_bias, w_alpha, b_alpha):
    s = _silu(c1)
    outs = []
    for j in range(3 * DN_HEADS):
        t = s[:, j * DN_DK:(j + 1) * DN_DK]
        if j < 2 * DN_HEADS:
            t = t * lax.rsqrt(jnp.sum(t * t, -1, keepdims=True) + 1e-6)
        if j < DN_HEADS:
            t = t * (DN_DK ** -0.5)
        outs.append(t)
    qkvn = jnp.concatenate(outs, axis=1)
    lane = _iota(hs.shape, 1)
    g = -jnp.exp(a_log) * _softplus(hs + dt_bias)
    beta = _sigmoid(hs)
    gb = jnp.where(lane < DN_HEADS, g, jnp.where(lane < 2 * DN_HEADS, beta, 0.0))
    la = _log_sigmoid(_bdot(hs, w_alpha) + b_alpha) * (1.0 / GLA_TAU)
    lane5 = _iota(la.shape, 1)
    la = jnp.where((lane5 % 128) < GLA_DK, la, 0.0)
    return qkvn, gb, la


def _f_post(o, gate, w):
    outs = []
    for j in range(4):
        t = o[:, j * 128:(j + 1) * 128]
        outs.append(t * lax.rsqrt(jnp.mean(t * t, -1, keepdims=True) + LN_EPS) * w)
    return (jnp.concatenate(outs, axis=1) * _silu(gate),)


def _f_glu(a, g):
    return (a * _sigmoid(g),)


def _f_cfpost(c, bias, g, b):
    t = c + bias
    mu = jnp.mean(t, -1, keepdims=True)
    var = jnp.mean(jnp.square(t - mu), -1, keepdims=True)
    return (_silu((t - mu) * lax.rsqrt(var + LN_EPS) * g + b),)


def _f_gelu(y):
    return (jax.nn.gelu(y),)


def _f_merge(ga, gb_, gc, gd, ya, yb, zv, zg, yd):
    return (_sigmoid(ga) * ya + _sigmoid(gb_) * yb + _sigmoid(gc) * (zv * _sigmoid(zg)) + _sigmoid(gd) * yd,)


def _f_act(a, b):
    return (_silu(a) * b,)


def _conv_tiles(rows, ch):
    return _pick(rows, 256), _pick(ch, 512)


def _conv_fwd(name, x, width, colblk0, w, taps):
    rows = x.shape[0]
    tr, cb = _conv_tiles(rows, width)
    hb = tr // HALO

    def kern(prev_ref, x_ref, w_ref, o_ref, ext):
        i = pl.program_id(1)
        ext[pl.ds(0, HALO), :] = jnp.where(i > 0, _round(prev_ref[...]), 0.0)
        ext[pl.ds(HALO, tr), :] = _round(x_ref[...])
        wv = _round(w_ref[...])
        acc = jnp.zeros((tr, cb), f32)
        for k in range(taps):
            acc = acc + wv[k:k + 1, :] * ext[pl.ds(HALO - taps + 1 + k, tr), :]
        o_ref[...] = acc

    c0 = colblk0 * (width // cb)
    return pl.pallas_call(
        kern, name=name, grid=(width // cb, rows // tr),
        in_specs=[pl.BlockSpec((HALO, cb), lambda c, i: (jnp.maximum(i * hb - 1, 0), c0 + c)),
                  pl.BlockSpec((tr, cb), lambda c, i: (i, c0 + c)),
                  pl.BlockSpec((w.shape[0], cb), lambda c, i: (0, c))],
        out_specs=pl.BlockSpec((tr, cb), lambda c, i: (i, c)),
        out_shape=jax.ShapeDtypeStruct((rows, width), f32),
        scratch_shapes=[pltpu.VMEM((HALO + tr, cb), f32)],
        compiler_params=_cparams(("parallel", "arbitrary")),
    )(x, x, w)


def _conv_bwd(name, x, width, colblk0, w, taps, dy, dx_dtype=f32):
    rows = x.shape[0]
    tr, cb = _conv_tiles(rows, width)
    hb = tr // HALO
    nt = rows // tr
    wr = w.shape[0]

    def kern(prev_ref, x_ref, w_ref, dy_ref, next_ref, dx_ref, dw_ref, ext, dext):
        i = pl.program_id(1)
        ext[pl.ds(0, HALO), :] = jnp.where(i > 0, _round(prev_ref[...]), 0.0)
        ext[pl.ds(HALO, tr), :] = _round(x_ref[...])
        dyv = _round(dy_ref[...])
        dext[pl.ds(0, tr), :] = dyv
        dext[pl.ds(tr, HALO), :] = jnp.where(i < nt - 1, _round(next_ref[...]), 0.0)
        wv = _round(w_ref[...])
        acc = jnp.zeros((tr, cb), f32)
        rows_w = []
        for k in range(taps):
            acc = acc + wv[k:k + 1, :] * dext[pl.ds(taps - 1 - k, tr), :]
            rows_w.append(jnp.sum(dyv * ext[pl.ds(HALO - taps + 1 + k, tr), :], axis=0, keepdims=True))
        dx_ref[...] = acc.astype(dx_ref.dtype)
        if wr > taps:
            rows_w.append(jnp.zeros((wr - taps, cb), f32))
        dwv = jnp.concatenate(rows_w, axis=0)

        @pl.when(i == 0)
        def _():
            dw_ref[...] = dwv

        @pl.when(i > 0)
        def _():
            dw_ref[...] += dwv

    c0 = colblk0 * (width // cb)
    return pl.pallas_call(
        kern, name=name, grid=(width // cb, nt),
        in_specs=[pl.BlockSpec((HALO, cb), lambda c, i: (jnp.maximum(i * hb - 1, 0), c0 + c)),
                  pl.BlockSpec((tr, cb), lambda c, i: (i, c0 + c)),
                  pl.BlockSpec((wr, cb), lambda c, i: (0, c)),
                  pl.BlockSpec((tr, cb), lambda c, i: (i, c)),
                  pl.BlockSpec((HALO, cb), lambda c, i: (jnp.minimum((i + 1) * hb, nt * hb - 1), c))],
        out_specs=[pl.BlockSpec((tr, cb), lambda c, i: (i, c)), pl.BlockSpec((wr, cb), lambda c, i: (0, c))],
        out_shape=[jax.ShapeDtypeStruct((rows, width), dx_dtype), jax.ShapeDtypeStruct((wr, width), f32)],
        scratch_shapes=[pltpu.VMEM((HALO + tr, cb), f32), pltpu.VMEM((HALO + tr, cb), f32)],
        compiler_params=_cparams(("parallel", "arbitrary")),
    )(x, x, w, dy, dy)


def _series_inverse(neg):
    n = neg.shape[0]
    inv = jnp.where(_iota((n, n), 0) == _iota((n, n), 1), 1.0, 0.0) + neg
    p = neg
    for _ in range(5):
        p = _dot(p, p)
        inv = inv + _dot(inv, p)
    return inv


def _inverse_bwd(inv, g):
    return _dot(_dot(inv, g, TN), inv, NT)


@jax.custom_vjp
def _unit_lower_inverse(neg):
    return _series_inverse(neg)


_unit_lower_inverse.defvjp(lambda neg: (_series_inverse(neg),) * 2, lambda inv, g: (_inverse_bwd(inv, g),))


@jax.custom_vjp
def _known_inverse(neg, inv):
    return inv


_known_inverse.defvjp(lambda neg, inv: (inv, inv), lambda inv, g: (_inverse_bwd(inv, g), jnp.zeros_like(inv)))


def _dn_head(state, q, k, v, gc, gl, beta, inv_saved=None):
    c = DN_CHUNK
    ii, jj = _iota((c, c), 0), _iota((c, c), 1)
    causal, strict = ii >= jj, ii > jj
    gcb = jnp.broadcast_to(gc, (c, c))
    decay = jnp.where(causal, jnp.exp(jnp.where(causal, gcb - gcb.T, 0.0)), 0.0)
    kb = k * beta
    neg = -jnp.where(strict, _bdot(kb, k, NT) * decay, 0.0)
    inv = _unit_lower_inverse(neg) if inv_saved is None else _known_inverse(neg, inv_saved)
    egc = jnp.exp(gc)
    u = _dot(inv, v * beta)
    w = _dot(inv, kb * egc)
    intra = _bdot(q, k, NT) * decay
    v_new = u - _bdot(w, state)
    o = _bdot(q * egc, state) + _bdot(intra, v_new)
    new_state = state * jnp.exp(gl) + _bdot(k * jnp.exp(gl - gc), v_new, TN)
    return (o, new_state, inv) if inv_saved is None else (o, new_state)


def _dn_cum(gb):
    c = DN_CHUNK
    tril = jnp.where(_iota((c, c), 0) >= _iota((c, c), 1), 1.0, 0.0)
    return _dot(tril, gb), jnp.sum(gb, axis=0, keepdims=True)


def _dn_block_head(state, q, k, v, gc, gls, beta, invs=None):
    c = DN_CHUNK
    outs, new_invs = [], []
    for n, gl in enumerate(gls):
        sl = slice(n * c, (n + 1) * c)
        if invs is None:
            o, state, inv = _dn_head(state, q[sl], k[sl], v[sl], gc[sl], gl, beta[sl])
            new_invs.append(inv)
        else:
            o, state = _dn_head(state, q[sl], k[sl], v[sl], gc[sl], gl, beta[sl], inv_saved=invs[n])
        outs.append(o)
    o = jnp.concatenate(outs, axis=0)
    return (o, state, new_invs) if invs is None else (o, state)


def _dn_block_cum(gbv, nbk):
    c = DN_CHUNK
    cums, tots = zip(*[_dn_cum(gbv[n * c:(n + 1) * c]) for n in range(nbk)])
    return jnp.concatenate(cums, axis=0), tots


def _dn_fwd(qkvn, gb):
    rows = qkvn.shape[0]
    c, h, d = DN_CHUNK, DN_HEADS, DN_DK
    nbk = DN_STEP_CHUNKS if rows % (c * DN_STEP_CHUNKS) == 0 else 1
    b, nb = c * nbk, rows // (c * nbk)

    def kern(qkv_ref, gb_ref, o_ref, st_ref, inv_ref, state):
        @pl.when(pl.program_id(0) == 0)
        def _():
            state[...] = jnp.zeros_like(state)

        gbv = gb_ref[...]
        cum, tots = _dn_block_cum(gbv, nbk)
        for j in range(h):
            st = state[j]
            st_ref[0, j] = st
            o, new, invs = _dn_block_head(st, qkv_ref[:, j * d:(j + 1) * d], qkv_ref[:, (h + j) * d:(h + j + 1) * d],
                                          qkv_ref[:, (2 * h + j) * d:(2 * h + j + 1) * d],
                                          cum[:, j:j + 1], [t[:, j:j + 1] for t in tots], gbv[:, h + j:h + j + 1])
            o_ref[:, j * d:(j + 1) * d] = o
            for n in range(nbk):
                inv_ref[0, n * h + j] = invs[n]
            state[j] = new

    return pl.pallas_call(
        kern, name="dn_fwd", grid=(nb,),
        in_specs=[pl.BlockSpec((b, 3 * h * d), lambda i: (i, 0)), pl.BlockSpec((b, 128), lambda i: (i, 0))],
        out_specs=[pl.BlockSpec((b, h * d), lambda i: (i, 0)), pl.BlockSpec((1, h, d, d), lambda i: (i, 0, 0, 0)),
                   pl.BlockSpec((1, nbk * h, c, c), lambda i: (i, 0, 0, 0))],
        out_shape=[jax.ShapeDtypeStruct((rows, h * d), f32), jax.ShapeDtypeStruct((nb, h, d, d), f32),
                   jax.ShapeDtypeStruct((nb, nbk * h, c, c), f32)],
        scratch_shapes=[pltpu.VMEM((h, d, d), f32)],
        compiler_params=_cparams(("arbitrary",)),
    )(qkvn, gb)


def _dn_bwd(qkvn, gb, states, invs, do):
    rows = qkvn.shape[0]
    c, h, d = DN_CHUNK, DN_HEADS, DN_DK
    nb = states.shape[0]
    nbk = rows // (c * nb)
    b = c * nbk

    def kern(qkv_ref, gb_ref, st_ref, inv_ref, do_ref, dqkv_ref, dgb_ref, dstate):
        @pl.when(pl.program_id(0) == 0)
        def _():
            dstate[...] = jnp.zeros_like(dstate)

        gbv = gb_ref[...]
        cum, tots = _dn_block_cum(gbv, nbk)
        lane = _iota((b, 128), 1)
        lane_c = _iota((c, 128), 1)
        dcum = jnp.zeros((b, 128), f32)
        dgb = jnp.zeros((b, 128), f32)
        for j in range(h):
            args = (st_ref[0, j], qkv_ref[:, j * d:(j + 1) * d], qkv_ref[:, (h + j) * d:(h + j + 1) * d],
                    qkv_ref[:, (2 * h + j) * d:(2 * h + j + 1) * d],
                    cum[:, j:j + 1], [t[:, j:j + 1] for t in tots], gbv[:, h + j:h + j + 1])
            head = functools.partial(_dn_block_head, invs=[inv_ref[0, n * h + j] for n in range(nbk)])
            ds, dq, dk, dv, dgc, dgls, dbeta = jax.vjp(head, *args)[1]((do_ref[:, j * d:(j + 1) * d], dstate[j]))
            dstate[j] = ds
            dqkv_ref[:, j * d:(j + 1) * d] = dq
            dqkv_ref[:, (h + j) * d:(h + j + 1) * d] = dk
            dqkv_ref[:, (2 * h + j) * d:(2 * h + j + 1) * d] = dv
            dcum = dcum + jnp.where(lane == j, dgc, 0.0)
            dgb = dgb + jnp.where(lane == h + j, dbeta, 0.0) + jnp.concatenate(
                [jnp.where(lane_c == j, dgl, 0.0) for dgl in dgls], axis=0)
        triu = jnp.where(_iota((c, c), 0) <= _iota((c, c), 1), 1.0, 0.0)
        dgb_ref[...] = dgb + jnp.concatenate([_dot(triu, dcum[n * c:(n + 1) * c]) for n in range(nbk)], axis=0)

    rev = lambda i: (nb - 1 - i, 0)
    rev4 = lambda i: (nb - 1 - i, 0, 0, 0)
    return pl.pallas_call(
        kern, name="dn_bwd", grid=(nb,),
        in_specs=[pl.BlockSpec((b, 3 * h * d), rev), pl.BlockSpec((b, 128), rev),
                  pl.BlockSpec((1, h, d, d), rev4), pl.BlockSpec((1, nbk * h, c, c), rev4), pl.BlockSpec((b, h * d), rev)],
        out_specs=[pl.BlockSpec((b, 3 * h * d), rev), pl.BlockSpec((b, 128), rev)],
        out_shape=[jax.ShapeDtypeStruct((rows, 3 * h * d), f32), jax.ShapeDtypeStruct((rows, 128), f32)],
        scratch_shapes=[pltpu.VMEM((h, d, d), f32)],
        compiler_params=_cparams(("arbitrary",)),
    )(qkvn, gb, states, invs, do)


def _gla_block(state_t, q, k, v, la):
    c = GLA_CHUNK
    ii, jj = _iota((c, c), 0), _iota((c, c), 1)
    causal = ii >= jj
    tril = jnp.where(causal, 1.0, 0.0)
    outs = []
    for n in range(q.shape[0] // c):
        sl = slice(n * c, (n + 1) * c)
        qn, kn, vn, ln = q[sl] * (GLA_DK ** -0.5), k[sl], v[sl], la[sl]
        gc = _dot(tril, ln)
        gl = jnp.sum(ln, axis=0, keepdims=True)
        q_dec = qn * jnp.exp(gc)
        scores = jnp.where(causal, _bdot(q_dec, kn * jnp.exp(-gc), NT), 0.0)
        outs.append(_bdot(q_dec, state_t, NT) + _bdot(scores, vn))
        state_t = state_t * jnp.exp(gl) + _bdot(vn, kn * jnp.exp(gl - gc), TN)
    return jnp.concatenate(outs, axis=0), state_t


GLA_Q0, GLA_K0, GLA_V0 = 7680 // 512, 8192 // 512, 8704 // 512


def _gla_fwd(hmain, la):
    rows = hmain.shape[0]
    b, h = _pick(rows, GLA_BLOCK), GLA_HEADS
    nb = rows // b

    def kern(q_ref, k_ref, v_ref, la_ref, o_ref, st_ref, state):
        @pl.when(pl.program_id(0) == 0)
        def _():
            state[...] = jnp.zeros_like(state)

        for j in range(h):
            sl = slice(j * 128, (j + 1) * 128)
            st = state[j]
            st_ref[0, j] = st
            o, new = _gla_block(st, q_ref[:, sl], k_ref[:, sl], v_ref[:, sl], la_ref[:, sl])
            o_ref[:, sl] = o
            state[j] = new

    return pl.pallas_call(
        kern, name="gla_fwd", grid=(nb,),
        in_specs=[pl.BlockSpec((b, 512), lambda i: (i, GLA_Q0)), pl.BlockSpec((b, 512), lambda i: (i, GLA_K0)),
                  pl.BlockSpec((b, 512), lambda i: (i, GLA_V0)), pl.BlockSpec((b, 512), lambda i: (i, 0))],
        out_specs=[pl.BlockSpec((b, 512), lambda i: (i, 0)), pl.BlockSpec((1, h, 128, 128), lambda i: (i, 0, 0, 0))],
        out_shape=[jax.ShapeDtypeStruct((rows, h * 128), f32), jax.ShapeDtypeStruct((nb, h, 128, 128), f32)],
        scratch_shapes=[pltpu.VMEM((h, 128, 128), f32)],
        compiler_params=_cparams(("arbitrary",)),
    )(hmain, hmain, hmain, la)


def _gla_bwd(hmain, la, states, do):
    rows = hmain.shape[0]
    b, h = _pick(rows, GLA_BLOCK), GLA_HEADS
    nb = rows // b

    def kern(q_ref, k_ref, v_ref, la_ref, st_ref, do_ref, dq_ref, dk_ref, dv_ref, dla_ref, dstate):
        @pl.when(pl.program_id(0) == 0)
        def _():
            dstate[...] = jnp.zeros_like(dstate)

        for j in range(h):
            sl = slice(j * 128, (j + 1) * 128)
            args = (st_ref[0, j], q_ref[:, sl], k_ref[:, sl], v_ref[:, sl], la_ref[:, sl])
            ds, dq, dk, dv, dla = jax.vjp(_gla_block, *args)[1]((do_ref[:, sl], dstate[j]))
            dstate[j] = ds
            dq_ref[:, sl] = dq.astype(bf16)
            dk_ref[:, sl] = dk.astype(bf16)
            dv_ref[:, sl] = dv.astype(bf16)
            dla_ref[:, sl] = dla

    rev = lambda i: (nb - 1 - i, 0)
    return pl.pallas_call(
        kern, name="gla_bwd", grid=(nb,),
        in_specs=[pl.BlockSpec((b, 512), lambda i: (nb - 1 - i, GLA_Q0)), pl.BlockSpec((b, 512), lambda i: (nb - 1 - i, GLA_K0)),
                  pl.BlockSpec((b, 512), lambda i: (nb - 1 - i, GLA_V0)), pl.BlockSpec((b, 512), rev),
                  pl.BlockSpec((1, h, 128, 128), lambda i: (nb - 1 - i, 0, 0, 0)), pl.BlockSpec((b, 512), rev)],
        out_specs=[pl.BlockSpec((b, 512), rev)] * 4,
        out_shape=[jax.ShapeDtypeStruct((rows, h * 128), bf16)] * 3 + [jax.ShapeDtypeStruct((rows, h * 128), f32)],
        scratch_shapes=[pltpu.VMEM((h, 128, 128), f32)],
        compiler_params=_cparams(("arbitrary",)),
    )(hmain, hmain, hmain, la, states, do)


def _f_s5_params(a_re, a_im, log_dt, b_re, b_im):
    dt = jnp.exp(log_dt)
    mag = jnp.exp(dt * a_re)
    abar_re, abar_im = mag * jnp.cos(dt * a_im), mag * jnp.sin(dt * a_im)
    den = a_re * a_re + a_im * a_im
    nr, ni = abar_re - 1.0, abar_im
    fr, fi = (nr * a_re + ni * a_im) / den, (ni * a_re - nr * a_im) / den
    return abar_re, abar_im, fr[None] * b_re - fi[None] * b_im, fr[None] * b_im + fi[None] * b_re


def _s5_params(a_re, a_im, log_dt, b_re, b_im):
    def kern(*refs):
        for o_ref, r in zip(refs[5:], _f_s5_params(*[r[...] for r in refs[:5]])):
            o_ref[...] = r

    ins = (a_re, a_im, log_dt, b_re, b_im)
    return pl.pallas_call(
        kern, name="s5_params", out_shape=[jax.ShapeDtypeStruct(a_re.shape, f32)] * 2 + [jax.ShapeDtypeStruct(b_re.shape, f32)] * 2,
    )(*ins)


def _s5_params_bwd(a_re, a_im, log_dt, b_re, b_im, d_ar, d_ai, d_br, d_bi):
    def kern(*refs):
        grads = jax.vjp(_f_s5_params, *[r[...] for r in refs[:5]])[1](tuple(r[...] for r in refs[5:9]))
        for o_ref, g in zip(refs[9:], grads):
            o_ref[...] = g

    ins = (a_re, a_im, log_dt, b_re, b_im)
    return pl.pallas_call(
        kern, name="s5_params_bwd", out_shape=[jax.ShapeDtypeStruct(t.shape, f32) for t in ins],
    )(*ins, d_ar, d_ai, d_br, d_bi)


def _cmul(ar, ai, br, bi):
    return ar * br - ai * bi, ar * bi + ai * br


def _s5_scan(xr, xi, ar, ai, reverse):
    t = xr.shape[0]
    row = _iota(xr.shape, 0)
    s = 1
    while s < t:
        if reverse:
            keep = row < t - s
            sr, si = pltpu.roll(xr, t - s, 0), pltpu.roll(xi, t - s, 0)
        else:
            keep = row >= s
            sr, si = pltpu.roll(xr, s, 0), pltpu.roll(xi, s, 0)
        sr, si = jnp.where(keep, sr, 0.0), jnp.where(keep, si, 0.0)
        pr, pi = _cmul(ar, ai, sr, si)
        xr, xi = xr + pr, xi + pi
        ar, ai = _cmul(ar, ai, ar, ai)
        s *= 2
    return xr, xi


def _s5_powers(ar, ai, t, reverse):
    row = _iota((t, ar.shape[1]), 0)
    at = (row == (t - 1 if reverse else 0))
    return _s5_scan(jnp.where(at, ar, 0.0), jnp.where(at, ai, 0.0), ar, ai, reverse)


S5_U0 = 7168 // 128


def _s5_fwd(hmain, abar, bmat_re, bmat_im, cmat_re, cmat_im, dvec):
    rows = hmain.shape[0]
    t = _pick(rows, S5_TILE)
    nt, ns = rows // t, 512

    def kern(u_ref, a_ref, br_ref, bi_ref, cr_ref, ci_ref, d_ref, y_ref, xr_ref, xi_ref, pw, carry):
        ar, ai = a_ref[0, 0], a_ref[1, 0]

        @pl.when(pl.program_id(1) == 0)
        def _():
            pr, pi = _s5_powers(ar, ai, t, False)
            pw[0], pw[1] = pr, pi
            carry[...] = jnp.zeros_like(carry)

        u = u_ref[...]
        xr, xi = _s5_scan(_rdot(u, br_ref[0]), _rdot(u, bi_ref[0]), ar, ai, False)
        cr, ci = carry[0:1, :], carry[1:2, :]
        qr, qi = _cmul(pw[0], pw[1], cr, ci)
        xr, xi = xr + qr, xi + qi
        xr_ref[...] = xr
        xi_ref[...] = xi
        carry[0:1, :] = xr[t - 1:t, :]
        carry[1:2, :] = xi[t - 1:t, :]
        y_ref[...] = _rdot(xr, cr_ref[0]) - _rdot(xi, ci_ref[0]) + d_ref[...] * u

    sb3 = lambda b, i: (b, 0, 0)
    return pl.pallas_call(
        kern, name="s5_fwd", grid=(S5_SB, nt),
        in_specs=[pl.BlockSpec((t, 128), lambda b, i: (i, S5_U0 + b)), pl.BlockSpec((2, 1, 1, ns), lambda b, i: (0, b, 0, 0)),
                  pl.BlockSpec((1, 128, ns), sb3), pl.BlockSpec((1, 128, ns), sb3),
                  pl.BlockSpec((1, ns, 128), sb3), pl.BlockSpec((1, ns, 128), sb3), pl.BlockSpec((1, 128), lambda b, i: (0, b))],
        out_specs=[pl.BlockSpec((t, 128), lambda b, i: (i, b)), pl.BlockSpec((t, ns), lambda b, i: (i, b)),
                   pl.BlockSpec((t, ns), lambda b, i: (i, b))],
        out_shape=[jax.ShapeDtypeStruct((rows, 512), f32), jax.ShapeDtypeStruct((rows, S5_SB * ns), f32),
                   jax.ShapeDtypeStruct((rows, S5_SB * ns), f32)],
        scratch_shapes=[pltpu.VMEM((2, t, ns), f32), pltpu.VMEM((8, ns), f32)],
        compiler_params=_cparams(("parallel", "arbitrary")),
    )(hmain, abar, bmat_re, bmat_im, cmat_re, cmat_im, dvec)


def _s5_bwd(hmain, abar, bmat_re, bmat_im, cmat_re, cmat_im, dvec, x_re, x_im, dy):
    rows = hmain.shape[0]
    t = _pick(rows, S5_TILE)
    nt, ns = rows // t, 512
    t8 = t // 8

    def kern(u_ref, a_ref, br_ref, bi_ref, cr_ref, ci_ref, d_ref, xr_ref, xi_ref, xpr_ref, xpi_ref, dy_ref,
             du_ref, da_ref, dbr_ref, dbi_ref, dcr_ref, dci_ref, dd_ref, pw, carry):
        i = pl.program_id(1)
        ar, ai = a_ref[0, 0], -a_ref[1, 0]

        @pl.when(i == 0)
        def _():
            pr, pi = _s5_powers(ar, ai, t, True)
            pw[0], pw[1] = pr, pi
            carry[...] = jnp.zeros_like(carry)

        u, gy = u_ref[...], dy_ref[...]
        lr, li = _s5_scan(_rdot(gy, cr_ref[0], NT), -_rdot(gy, ci_ref[0], NT), ar, ai, True)
        qr, qi = _cmul(pw[0], pw[1], carry[0:1, :], carry[1:2, :])
        lr, li = lr + qr, li + qi
        carry[0:1, :] = lr[0:1, :]
        carry[1:2, :] = li[0:1, :]
        du_ref[...] = (_rdot(lr, br_ref[0], NT) + _rdot(li, bi_ref[0], NT) + d_ref[...] * gy).astype(bf16)
        xr, xi = xr_ref[...], xi_ref[...]
        row = _iota(xr.shape, 0)
        first_r = jnp.where(i < nt - 1, xpr_ref[7:8, :], 0.0)
        first_i = jnp.where(i < nt - 1, xpi_ref[7:8, :], 0.0)
        xpr = jnp.where(row == 0, first_r, pltpu.roll(xr, 1, 0))
        xpi = jnp.where(row == 0, first_i, pltpu.roll(xi, 1, 0))
        da_r = jnp.sum(lr * xpr + li * xpi, axis=0, keepdims=True)
        da_i = jnp.sum(li * xpr - lr * xpi, axis=0, keepdims=True)
        upd = [(da_ref.at[0, 0], da_r), (da_ref.at[1, 0], da_i),
               (dbr_ref.at[0], _rdot(u, lr, TN)), (dbi_ref.at[0], _rdot(u, li, TN)),
               (dcr_ref.at[0], _rdot(xr, gy, TN)), (dci_ref.at[0], -_rdot(xi, gy, TN)),
               (dd_ref, jnp.sum(gy * u, axis=0, keepdims=True))]

        @pl.when(i == 0)
        def _():
            for ref, val in upd:
                ref[...] = val

        @pl.when(i > 0)
        def _():
            for ref, val in upd:
                ref[...] += val

    sb3 = lambda b, i: (b, 0, 0)
    rev = lambda b, i: (nt - 1 - i, b)
    prev8 = lambda b, i: (jnp.maximum((nt - 1 - i) * t8 - 1, 0), b)
    return pl.pallas_call(
        kern, name="s5_bwd", grid=(S5_SB, nt),
        in_specs=[pl.BlockSpec((t, 128), lambda b, i: (nt - 1 - i, S5_U0 + b)), pl.BlockSpec((2, 1, 1, ns), lambda b, i: (0, b, 0, 0)),
                  pl.BlockSpec((1, 128, ns), sb3), pl.BlockSpec((1, 128, ns), sb3),
                  pl.BlockSpec((1, ns, 128), sb3), pl.BlockSpec((1, ns, 128), sb3), pl.BlockSpec((1, 128), lambda b, i: (0, b)),
                  pl.BlockSpec((t, ns), rev), pl.BlockSpec((t, ns), rev), pl.BlockSpec((8, ns), prev8), pl.BlockSpec((8, ns), prev8),
                  pl.BlockSpec((t, 128), rev)],
        out_specs=[pl.BlockSpec((t, 128), rev), pl.BlockSpec((2, 1, 1, ns), lambda b, i: (0, b, 0, 0)),
                   pl.BlockSpec((1, 128, ns), sb3), pl.BlockSpec((1, 128, ns), sb3),
                   pl.BlockSpec((1, ns, 128), sb3), pl.BlockSpec((1, ns, 128), sb3), pl.BlockSpec((1, 128), lambda b, i: (0, b))],
        out_shape=[jax.ShapeDtypeStruct((rows, 512), bf16), jax.ShapeDtypeStruct((2, S5_SB, 1, ns), f32),
                   jax.ShapeDtypeStruct((S5_SB, 128, ns), f32), jax.ShapeDtypeStruct((S5_SB, 128, ns), f32),
                   jax.ShapeDtypeStruct((S5_SB, ns, 128), f32), jax.ShapeDtypeStruct((S5_SB, ns, 128), f32),
                   jax.ShapeDtypeStruct((1, 512), f32)],
        scratch_shapes=[pltpu.VMEM((2, t, ns), f32), pltpu.VMEM((8, ns), f32)],
        compiler_params=_cparams(("parallel", "arbitrary")),
    )(hmain, abar, bmat_re, bmat_im, cmat_re, cmat_im, dvec, x_re, x_im, x_re, x_im, dy)


def _loss_head(y, target):
    rows, feat = y.shape
    tile = _pick(rows, 256)

    def kern(y_ref, t_ref, l_ref, dy_ref):
        e = y_ref[...] - t_ref[...]
        dy_ref[...] = e * (1.0 / feat)
        part = jnp.broadcast_to(0.5 * jnp.sum(e * e) * (1.0 / feat), l_ref.shape)

        @pl.when(pl.program_id(0) == 0)
        def _():
            l_ref[...] = part

        @pl.when(pl.program_id(0) > 0)
        def _():
            l_ref[...] += part

    return pl.pallas_call(
        kern, name="loss_head", grid=(rows // tile,),
        in_specs=[pl.BlockSpec((tile, feat), lambda i: (i, 0))] * 2,
        out_specs=[pl.BlockSpec((8, 128), lambda i: (0, 0)), pl.BlockSpec((tile, feat), lambda i: (i, 0))],
        out_shape=[jax.ShapeDtypeStruct((8, 128), f32), jax.ShapeDtypeStruct((rows, feat), f32)],
        compiler_params=_cparams(("arbitrary",)),
    )(y, target)


def _sum_parts(name, parts, out_dtype=f32):
    rows, cols = parts[0].shape
    tile = _pick(rows, 512)

    def kern(*refs):
        acc = refs[0][...].astype(f32)
        for r in refs[1:-1]:
            acc = acc + r[...].astype(f32)
        refs[-1][...] = acc.astype(out_dtype)

    return pl.pallas_call(
        kern, name=name, grid=(rows // tile,),
        in_specs=[pl.BlockSpec((tile, cols), lambda i: (i, 0))] * len(parts),
        out_specs=pl.BlockSpec((tile, cols), lambda i: (i, 0)),
        out_shape=jax.ShapeDtypeStruct((rows, cols), out_dtype),
        compiler_params=_cparams(("parallel",)),
    )(*parts)


ADAMW_BLOCK_BYTES = 1536 * 1024


def _adamw(name, w, m, v, g):
    rows, cols = w.shape
    tile = _pick(rows, 512)
    while tile > 8 and tile * cols * 4 > ADAMW_BLOCK_BYTES and rows % (tile // 2) == 0:
        tile //= 2
    c1, c2 = 1.0 / (1.0 - ADAM_B1 ** ADAM_STEP), 1.0 / (1.0 - ADAM_B2 ** ADAM_STEP)

    def kern(w_ref, m_ref, v_ref, g_ref, d_ref, nm_ref, nv_ref):
        gv = g_ref[...]
        nm = ADAM_B1 * m_ref[...] + (1.0 - ADAM_B1) * gv
        nv = ADAM_B2 * v_ref[...] + (1.0 - ADAM_B2) * (gv * gv)
        nm_ref[...] = nm
        nv_ref[...] = nv
        d_ref[...] = -ADAM_LR * ((nm * c1) / (jnp.sqrt(nv * c2) + ADAM_EPS) + ADAM_WD * w_ref[...])

    spec = pl.BlockSpec((tile, cols), lambda i: (i, 0))
    return pl.pallas_call(
        kern, name=name, grid=(rows // tile,), in_specs=[spec] * 4, out_specs=[spec] * 3,
        out_shape=[jax.ShapeDtypeStruct((rows, cols), f32)] * 3,
        compiler_params=_cparams(("parallel",)),
    )(w, m, v, g)


MESH = pl.DeviceIdType.MESH
HBM_SPEC = pl.BlockSpec(memory_space=pltpu.HBM)


def _other_chips(x, y):
    return [(1 - x, y), (x, 1 - y), (1 - x, 1 - y)]


def _gather_chips(big, small):
    rows, cols = big.shape
    half = rows // 2

    def kern(big_ref, small_ref, bout, sout, ici_send, ici_recv, d2d_send, d2d_recv, small_send, small_recv):
        x, y, c = lax.axis_index("x"), lax.axis_index("y"), lax.axis_index("c")
        me, sibling = 2 * x + y, (x, y, 1 - c)
        chips = _other_chips(x, y)
        slots = [2 * chip[0] + chip[1] for chip in chips]
        sends = [pltpu.make_async_remote_copy(
            src_ref=big_ref.at[c], dst_ref=bout.at[me, c], send_sem=ici_send.at[k], recv_sem=ici_recv.at[k],
            device_id=(*chip, c), device_id_type=MESH) for k, chip in enumerate(chips)]
        sends += [pltpu.make_async_remote_copy(
            src_ref=small_ref, dst_ref=sout.at[me], send_sem=small_send.at[k], recv_sem=small_recv.at[k],
            device_id=(*chip, c), device_id_type=MESH) for k, chip in enumerate(chips)]
        for cp in sends:
            cp.start()
        for k, chip in enumerate(chips):
            pltpu.make_async_remote_copy(
                src_ref=big_ref.at[c], dst_ref=bout.at[slots[k], c], send_sem=ici_send.at[k], recv_sem=ici_recv.at[k],
                device_id=(*chip, c), device_id_type=MESH).wait_recv()
            passed = pltpu.make_async_remote_copy(
                src_ref=bout.at[slots[k], c], dst_ref=bout.at[slots[k], c], send_sem=d2d_send.at[k],
                recv_sem=d2d_recv.at[k], device_id=sibling, device_id_type=MESH)
            passed.start()
            sends.append(passed)
        for k, chip in enumerate(chips):
            pltpu.make_async_remote_copy(
                src_ref=bout.at[slots[k], 1 - c], dst_ref=bout.at[slots[k], 1 - c], send_sem=d2d_send.at[k],
                recv_sem=d2d_recv.at[k], device_id=sibling, device_id_type=MESH).wait_recv()
            pltpu.make_async_remote_copy(
                src_ref=small_ref, dst_ref=sout.at[slots[k]], send_sem=small_send.at[k], recv_sem=small_recv.at[k],
                device_id=(*chip, c), device_id_type=MESH).wait_recv()
        for cp in sends:
            cp.wait_send()

    got_big, got_small = pl.pallas_call(
        kern, name="gather_chips", in_specs=[HBM_SPEC] * 2, out_specs=[HBM_SPEC] * 2,
        out_shape=[jax.ShapeDtypeStruct((N_CHIPS, 2, half, cols), big.dtype), jax.ShapeDtypeStruct((N_CHIPS,) + small.shape, small.dtype)],
        scratch_shapes=[pltpu.SemaphoreType.DMA((3,))] * 6,
    )(big.reshape(2, half, cols), small)
    me = 2 * lax.axis_index("x") + lax.axis_index("y")
    got_big = lax.dynamic_update_index_in_dim(got_big.reshape(N_CHIPS, rows, cols), big, me, 0)
    return got_big, lax.dynamic_update_index_in_dim(got_small, small, me, 0)


def _join_halves(part):
    other = _swap_sibling(part)
    first = lax.axis_index("c") == 0
    return jnp.concatenate([jnp.where(first, part, other), jnp.where(first, other, part)], axis=0)


def _scatter_chips(pieces):
    _, rows, cols = pieces.shape

    def kern(p_ref, out_ref, send_sems, recv_sems):
        x, y, c = lax.axis_index("x"), lax.axis_index("y"), lax.axis_index("c")
        chips = _other_chips(x, y)
        sends = [pltpu.make_async_remote_copy(
            src_ref=p_ref.at[2 * chip[0] + chip[1]], dst_ref=out_ref.at[k], send_sem=send_sems.at[k],
            recv_sem=recv_sems.at[k], device_id=(*chip, c), device_id_type=MESH) for k, chip in enumerate(chips)]
        for cp in sends:
            cp.start()
        for cp in sends:
            cp.wait()

    return pl.pallas_call(
        kern, name="scatter_chips", in_specs=[HBM_SPEC], out_specs=HBM_SPEC,
        out_shape=jax.ShapeDtypeStruct((3, rows, cols), pieces.dtype),
        scratch_shapes=[pltpu.SemaphoreType.DMA((3,)), pltpu.SemaphoreType.DMA((3,))],
    )(pieces)


def _swap_sibling(buf):
    def kern(b_ref, out_ref, send_sem, recv_sem):
        x, y, c = lax.axis_index("x"), lax.axis_index("y"), lax.axis_index("c")
        cp = pltpu.make_async_remote_copy(src_ref=b_ref, dst_ref=out_ref, send_sem=send_sem, recv_sem=recv_sem,
                                          device_id=(x, y, 1 - c), device_id_type=MESH)
        cp.start()
        cp.wait()

    return pl.pallas_call(
        kern, name="swap_sibling", in_specs=[HBM_SPEC], out_specs=HBM_SPEC,
        out_shape=jax.ShapeDtypeStruct(buf.shape, buf.dtype),
        scratch_shapes=[pltpu.SemaphoreType.DMA, pltpu.SemaphoreType.DMA],
    )(buf)


def _gather_all(buf):
    rows, cols = buf.shape

    def kern(b_ref, out_ref, send_sems, recv_sems):
        x, y, c = lax.axis_index("x"), lax.axis_index("y"), lax.axis_index("c")
        me = 4 * x + 2 * y + c
        peers = []
        for k in range(1, N_DEV):
            fx, fy, fc = (k >> 2) & 1, (k >> 1) & 1, k & 1
            peers.append((x ^ fx, y ^ fy, c ^ fc))
        sends = [pltpu.make_async_remote_copy(
            src_ref=b_ref, dst_ref=out_ref.at[me], send_sem=send_sems.at[k], recv_sem=recv_sems.at[k],
            device_id=peer, device_id_type=MESH) for k, peer in enumerate(peers)]
        for cp in sends:
            cp.start()
        for k, peer in enumerate(peers):
            pltpu.make_async_remote_copy(
                src_ref=b_ref, dst_ref=out_ref.at[4 * peer[0] + 2 * peer[1] + peer[2]], send_sem=send_sems.at[k],
                recv_sem=recv_sems.at[k], device_id=peer, device_id_type=MESH).wait_recv()
        for cp in sends:
            cp.wait_send()

    got = pl.pallas_call(
        kern, name="gather_all", in_specs=[HBM_SPEC], out_specs=HBM_SPEC,
        out_shape=jax.ShapeDtypeStruct((N_DEV, rows, cols), buf.dtype),
        scratch_shapes=[pltpu.SemaphoreType.DMA((N_DEV - 1,)), pltpu.SemaphoreType.DMA((N_DEV - 1,))],
    )(buf)
    me = 4 * lax.axis_index("x") + 2 * lax.axis_index("y") + lax.axis_index("c")
    return lax.dynamic_update_index_in_dim(got, buf, me, 0)


def _sum_slots(name, buf):
    n, rows, cols = buf.shape
    tile = _pick(rows, 256)

    def kern(b_ref, o_ref):
        acc = b_ref[0]
        for k in range(1, n):
            acc = acc + b_ref[k]
        o_ref[...] = acc

    return pl.pallas_call(
        kern, name=name, grid=(rows // tile,),
        in_specs=[pl.BlockSpec((n, tile, cols), lambda i: (0, i, 0))], out_specs=pl.BlockSpec((tile, cols), lambda i: (i, 0)),
        out_shape=jax.ShapeDtypeStruct((rows, cols), f32), compiler_params=_cparams(("parallel",)),
    )(buf)


def _pad_heads(t):
    r = t.shape[0]
    return jnp.pad(t.reshape(r, GLA_HEADS, GLA_DK), ((0, 0), (0, 0), (0, 128 - GLA_DK))).reshape(r, GLA_HEADS * 128)


def _unpad_heads(t):
    r = t.shape[0]
    return t.reshape(r, GLA_HEADS, 128)[:, :, :GLA_DK].reshape(r, GLA_HEADS * GLA_DK)


def _blockdiag(t):
    _, r, c = t.shape
    eye = jnp.eye(8, dtype=t.dtype).reshape(1, 8, 1, 8, 1)
    return (t.reshape(S5_SB, 8, r, 1, c) * eye).reshape(S5_SB, 8 * r, 8 * c)


def _blockdiag_extract(m, r, c):
    m5 = m.reshape(S5_SB, 8, r, 8, c)
    return jnp.stack([m5[:, g, :, g, :] for g in range(8)], axis=1).reshape(S5_GROUPS, r, c)


def _row(v, width=None):
    v = v[None]
    return v if width is None else jnp.pad(v, ((0, 0), (0, width - v.shape[1])))


def _layer_operands(p):
    w_in = p['w_in']

    def seg(n):
        return w_in[:, IN_ORIG[n][0]:IN_ORIG[n][1]]

    o = dict(p)
    o['wm'] = jnp.concatenate([seg('dn_qkv'), seg('dn_gate'), seg('cf'), seg('gates'), seg('s5'), _pad_heads(seg('gla_q')),
                               _pad_heads(seg('gla_k')), seg('gla_v'), seg('gla_g')], axis=1)
    o['ws'] = jnp.pad(jnp.concatenate([seg('dn_a'), seg('dn_b'), seg('gla_lr')], axis=1), ((0, 0), (0, WS_COLS - 24)))
    o['dn_conv8'] = jnp.pad(p['dn_conv'], ((0, 4), (0, 0)))
    o['a_log_r'] = _row(p['dn_a_log'], 128)
    o['dt_bias_r'] = _row(p['dn_dt_bias'], 128)
    o['dn_norm_r'] = _row(p['dn_norm'])
    o['cf_dw32'] = jnp.pad(p['cf_dw'], ((0, 1), (0, 0)))
    o['cf_bias_r'], o['cf_g_r'], o['cf_b_r'] = _row(p['cf_dw_bias']), _row(p['cf_ln_g']), _row(p['cf_ln_b'])
    o['w_alpha_p'] = jnp.pad(_pad_heads(p['gla_w_alpha']), ((8, 128 - 24), (0, 0)))
    o['b_alpha_r'] = _pad_heads(_row(p['gla_b_alpha']))
    o['gla_norm_r'] = _row(p['gla_norm'])
    o['ln1_g_r'], o['ln1_b_r'], o['ln2_g_r'], o['ln2_b_r'] = (_row(p[n]) for n in ('ln1_g', 'ln1_b', 'ln2_g', 'ln2_b'))
    o['ffn_conv8'] = jnp.pad(p['ffn_conv'], ((0, 5), (0, 0)))
    o['s5_in'] = (p['s5_a_re'], p['s5_a_im'], p['s5_log_dt'][:, None],
                  p['s5_b_re'].transpose(2, 0, 1), p['s5_b_im'].transpose(2, 0, 1))
    abar_re, abar_im, bbar_re, bbar_im = _s5_params(*o['s5_in'])
    o['abar'] = jnp.stack([abar_re, abar_im]).reshape(2, S5_SB, 1, 512)
    o['bmat_re'], o['bmat_im'] = _blockdiag(bbar_re.transpose(1, 0, 2)), _blockdiag(bbar_im.transpose(1, 0, 2))
    o['cmat_re'], o['cmat_im'] = _blockdiag(p['s5_c_re'].transpose(0, 2, 1)), _blockdiag(p['s5_c_im'].transpose(0, 2, 1))
    o['dvec'] = _row(p['s5_d'])
    return o


def _whole(a):
    return (a, a.shape[1], 0)


def _merge_ins(h, s):
    return [(h, 1024, 3), (h, 1024, 4), (h, 1024, 5), (h, 1024, 6), (s['y_a'], 1024, 0), (s['y_b'], 1024, 0),
            (s['zz'], 1024, 0), (s['zz'], 1024, 1), (s['y_d'], 1024, 0)]


def _layer_fwd(x, o):
    s = {}
    h = s['h'] = _mm(x, o['wm'], 'nn', 'mm_h')
    hs = s['hs'] = _mm(x, o['ws'], 'nn', 'mm_hs')
    s['c1'] = _conv_fwd('conv_dn', h, 1536, 0, o['dn_conv8'], 4)
    s['qkvn'], s['gb'], s['la'] = _rowwise('pre', _f_pre, [_whole(s['c1']), _whole(hs)],
                                           [o['a_log_r'], o['dt_bias_r'], o['w_alpha_p'], o['b_alpha_r']], [1536, 128, 512])
    s['o_dn'], s['st_dn'], s['inv_dn'] = _dn_fwd(s['qkvn'], s['gb'])
    (s['on_dn'],) = _rowwise('post_dn', _f_post, [_whole(s['o_dn']), (h, 512, 3)], [o['dn_norm_r']], [512], out_dtype=bf16)
    s['y_a'] = _mm(s['on_dn'], o['w_br_dn'], 'nn', 'mm_br')
    (s['cfp'],) = _rowwise('glu_cf', _f_glu, [(h, 512, 4), (h, 512, 5)], [], [512], out_dtype=bf16)
    s['cc'] = _conv_fwd('conv_cf', s['cfp'], 512, 0, o['cf_dw32'], 31)
    (s['cfo'],) = _rowwise('post_cf', _f_cfpost, [_whole(s['cc'])], [o['cf_bias_r'], o['cf_g_r'], o['cf_b_r']], [512], out_dtype=bf16)
    s['y_b'] = _mm(s['cfo'], o['w_br_cf'], 'nn', 'mm_br')
    s['ys5'], s['xr'], s['xi'] = _s5_fwd(h, o['abar'], o['bmat_re'], o['bmat_im'], o['cmat_re'], o['cmat_im'], o['dvec'])
    (s['z'],) = _rowwise('gelu', _f_gelu, [_whole(s['ys5'])], [], [512], out_dtype=bf16)
    s['zz'] = _mm(s['z'], o['w_br_s5'], 'nn', 'mm_br_s5')
    s['o_gla'], s['st_gla'] = _gla_fwd(h, s['la'])
    (s['on_gla'],) = _rowwise('post_gla', _f_post, [_whole(s['o_gla']), (h, 512, 18)], [o['gla_norm_r']], [512], out_dtype=bf16)
    s['y_d'] = _mm(s['on_gla'], o['w_br_gla'], 'nn', 'mm_br')
    (s['merged'],) = _rowwise('merge', _f_merge, _merge_ins(h, s), [], [1024], tile=128, out_dtype=bf16)
    s['mix'] = _mm(s['merged'], o['w_o'], 'nn', 'mm_o')
    (s['x1'],) = _rowwise('ln', _f_ln, [_whole(x), _whole(s['mix'])], [o['ln1_g_r'], o['ln1_b_r']], [1024])
    s['up'] = _mm(s['x1'], o['w_up'], 'nn', 'mm_up', out_dtype=bf16)
    s['u'] = _conv_fwd('conv_ffn', s['up'], 2 * D_FF, 0, o['ffn_conv8'], 3)
    (s['act'],) = _rowwise('act', _f_act, [(s['u'], D_FF, 0), (s['u'], D_FF, 1)], [], [D_FF], out_dtype=bf16)
    s['ffn'] = _mm(s['act'], o['w_down'], 'nn', 'mm_down')
    (x2,) = _rowwise('ln', _f_ln, [_whole(s['x1']), _whole(s['ffn'])], [o['ln2_g_r'], o['ln2_b_r']], [1024])
    return x2, s


def _layer_bwd(x, o, s, dparts):
    h, g = s['h'], {}
    (dx1_a, dffn), (g['ln2_g'], g['ln2_b']) = _rowwise_bwd(
        'ln_bwd', _f_ln, [_whole(s['x1']), _whole(s['ffn'])], [o['ln2_g_r'], o['ln2_b_r']], [[_whole(d) for d in dparts]], [f32, bf16])
    dact = _mm(dffn, o['w_down'], 'nt', 'mm_down_dx')
    g['w_down'] = _mm(s['act'], dffn, 'tn', 'mm_down_dw', out_dtype=bf16)
    (du_a, du_b), _ = _rowwise_bwd('act_bwd', _f_act, [(s['u'], D_FF, 0), (s['u'], D_FF, 1)], [], [[_whole(dact)]], [bf16, bf16])
    dup, dw = _conv_bwd('conv_ffn_bwd', s['up'], 2 * D_FF, 0, o['ffn_conv8'], 3, jnp.concatenate([du_a, du_b], axis=1), dx_dtype=bf16)
    g['ffn_conv'] = dw[:3]
    dx1_b = _mm(dup, o['w_up'], 'nt', 'mm_up_dx')
    g['w_up'] = _mm(s['x1'], dup, 'tn', 'mm_up_dw', out_dtype=bf16)
    (dx_a, dmix), (g['ln1_g'], g['ln1_b']) = _rowwise_bwd(
        'ln_bwd', _f_ln, [_whole(x), _whole(s['mix'])], [o['ln1_g_r'], o['ln1_b_r']], [[_whole(dx1_a), _whole(dx1_b)]], [f32, bf16])
    dmerged = _mm(dmix, o['w_o'], 'nt', 'mm_o_dx')
    g['w_o'] = _mm(s['merged'], dmix, 'tn', 'mm_o_dw', out_dtype=bf16)
    (dga, dgb_, dgc, dgd, dya, dyb, dzv, dzg, dyd), _ = _rowwise_bwd(
        'merge_bwd', _f_merge, _merge_ins(h, s), [], [[_whole(dmerged)]], [bf16] * 9, tile=128)
    dzz = jnp.concatenate([dzv, dzg], axis=1)
    don = _mm(dya, o['w_br_dn'], 'nt', 'mm_br_dx')
    g['w_br_dn'] = _mm(s['on_dn'], dya, 'tn', 'mm_br_dw', out_dtype=bf16)
    (do_dn, dgate_dn), (g['dn_norm'],) = _rowwise_bwd(
        'post_bwd', _f_post, [_whole(s['o_dn']), (h, 512, 3)], [o['dn_norm_r']], [[_whole(don)]], [f32, bf16])
    dqkvn, dgb = _dn_bwd(s['qkvn'], s['gb'], s['st_dn'], s['inv_dn'], do_dn)
    don = _mm(dyd, o['w_br_gla'], 'nt', 'mm_br_dx')
    g['w_br_gla'] = _mm(s['on_gla'], dyd, 'tn', 'mm_br_dw', out_dtype=bf16)
    (do_gla, dgate_gla), (g['gla_norm'],) = _rowwise_bwd(
        'post_bwd', _f_post, [_whole(s['o_gla']), (h, 512, 18)], [o['gla_norm_r']], [[_whole(don)]], [f32, bf16])
    dq_gla, dk_gla, dv_gla, dla = _gla_bwd(h, s['la'], s['st_gla'], do_gla)
    (dc1, dhs), (d_alog, d_dtb, d_walpha, d_balpha) = _rowwise_bwd(
        'pre_bwd', _f_pre, [_whole(s['c1']), _whole(s['hs'])], [o['a_log_r'], o['dt_bias_r'], o['w_alpha_p'], o['b_alpha_r']],
        [[_whole(dqkvn)], [_whole(dgb)], [_whole(dla)]], [bf16, bf16])
    g['dn_a_log'], g['dn_dt_bias'] = d_alog[0, :DN_HEADS], d_dtb[0, :DN_HEADS]
    g['gla_w_alpha'], g['gla_b_alpha'] = _unpad_heads(d_walpha[8:24]), _unpad_heads(d_balpha)[0]
    d_dnqkv, dw = _conv_bwd('conv_dn_bwd', h, 1536, 0, o['dn_conv8'], 4, dc1, dx_dtype=bf16)
    g['dn_conv'] = dw[:4]
    dcfo = _mm(dyb, o['w_br_cf'], 'nt', 'mm_br_dx')
    g['w_br_cf'] = _mm(s['cfo'], dyb, 'tn', 'mm_br_dw', out_dtype=bf16)
    (dcc,), (g['cf_dw_bias'], g['cf_ln_g'], g['cf_ln_b']) = _rowwise_bwd(
        'post_cf_bwd', _f_cfpost, [_whole(s['cc'])], [o['cf_bias_r'], o['cf_g_r'], o['cf_b_r']], [[_whole(dcfo)]], [bf16])
    dcfp, dw = _conv_bwd('conv_cf_bwd', s['cfp'], 512, 0, o['cf_dw32'], 31, dcc)
    g['cf_dw'] = dw[:31]
    (dcf_a, dcf_g), _ = _rowwise_bwd('glu_bwd', _f_glu, [(h, 512, 4), (h, 512, 5)], [], [[_whole(dcfp)]], [bf16, bf16])
    dz = _mm(dzz, o['w_br_s5'], 'nt', 'mm_br_s5_dx')
    g['w_br_s5'] = _mm(s['z'], dzz, 'tn', 'mm_br_s5_dw', out_dtype=bf16)
    (dys5,), _ = _rowwise_bwd('gelu_bwd', _f_gelu, [_whole(s['ys5'])], [], [[_whole(dz)]], [f32])
    du_s5, d_abar, dbm_re, dbm_im, dcm_re, dcm_im, d_dvec = _s5_bwd(
        h, o['abar'], o['bmat_re'], o['bmat_im'], o['cmat_re'], o['cmat_im'], o['dvec'], s['xr'], s['xi'], dys5)
    d_bbar = [_blockdiag_extract(m, S5_GROUP, S5_STATE).transpose(1, 0, 2) for m in (dbm_re, dbm_im)]
    da_re, da_im, dlog_dt, db_re, db_im = _s5_params_bwd(
        *o['s5_in'], d_abar[0].reshape(S5_GROUPS, S5_STATE), d_abar[1].reshape(S5_GROUPS, S5_STATE), *d_bbar)
    g['s5_a_re'], g['s5_a_im'], g['s5_log_dt'] = da_re, da_im, dlog_dt[:, 0]
    g['s5_b_re'], g['s5_b_im'] = db_re.transpose(1, 2, 0), db_im.transpose(1, 2, 0)
    g['s5_c_re'], g['s5_c_im'] = (_blockdiag_extract(m, S5_STATE, S5_GROUP).transpose(0, 2, 1) for m in (dcm_re, dcm_im))
    g['s5_d'] = d_dvec[0]
    for n in ('dn_norm', 'gla_norm', 'cf_dw_bias', 'cf_ln_g', 'cf_ln_b', 'ln1_g', 'ln1_b', 'ln2_g', 'ln2_b'):
        g[n] = g[n][0]
    dh = jnp.concatenate([d_dnqkv, dgate_dn, dcf_a, dcf_g, dga, dgb_, dgc, dgd, du_s5, dq_gla, dk_gla, dv_gla, dgate_gla], axis=1)
    dwm = _mm(x, dh, 'tn', 'mm_h_dw', out_dtype=bf16)
    dws = _mm(x, dhs, 'tn', 'mm_hs_dw', out_dtype=bf16)
    g['w_in'] = jnp.concatenate([
        dwm[:, 0:1536], dws[:, 0:8], dwm[:, 1536:2048], dwm[:, 2048:3072], dwm[:, 7168:7680], _unpad_heads(dwm[:, 7680:8192]),
        _unpad_heads(dwm[:, 8192:8704]), dwm[:, 8704:9216], dwm[:, 9216:9728], dws[:, 8:24], dwm[:, 3072:7168]], axis=1)
    return [dx_a, _mm(dh, o['wm'], 'nt', 'mm_h_dx'), _mm(dhs, o['ws'], 'nt', 'mm_hs_dx')], g


WEIGHTS = ('w_in', 'dn_conv', 'dn_a_log', 'dn_dt_bias', 'dn_norm', 'w_br_dn', 'cf_dw', 'cf_dw_bias', 'cf_ln_g', 'cf_ln_b',
           'w_br_cf', 's5_a_re', 's5_a_im', 's5_log_dt', 's5_b_re', 's5_b_im', 's5_c_re', 's5_c_im', 's5_d', 'w_br_s5',
           'gla_w_alpha', 'gla_b_alpha', 'gla_norm', 'w_br_gla', 'w_o', 'ln1_g', 'ln1_b', 'w_up', 'ffn_conv', 'w_down',
           'ln2_g', 'ln2_b')
LARGE = ('w_in', 'w_br_dn', 'w_br_cf', 'w_br_s5', 'w_br_gla', 'w_o', 'w_up', 'w_down')
SHARD_AXIS = dict(w_in=2, w_br_dn=2, w_br_cf=2, w_br_s5=2, w_br_gla=2, w_o=1, w_up=2, w_down=1,
                  dn_conv=2, cf_dw=2, gla_w_alpha=2, ffn_conv=2)
SMALL = tuple(n for n in WEIGHTS if n not in LARGE)
SMALL_SHARDED = tuple(n for n in SMALL if n in SHARD_AXIS)


def _local_step(x, target, full):
    ops, saved, xs = [], [], [x]
    for l in range(DEPTH):
        o = _layer_operands({n: full[n][l] for n in WEIGHTS})
        y, s = _layer_fwd(xs[-1], o)
        ops.append(o)
        saved.append(s)
        xs.append(y)
    loss, dy = _loss_head(xs[-1], target)
    dparts, grads = [dy], [None] * DEPTH
    for l in reversed(range(DEPTH)):
        dparts, grads[l] = _layer_bwd(xs[l], ops[l], saved[l], dparts)
    grad_x = _sum_parts('sum_dx', dparts)
    small = {n: jnp.stack([grads[l][n] for l in range(DEPTH)]) for n in SMALL}
    return loss[0, 0], grad_x, small, {n: [grads[l][n] for l in range(DEPTH)] for n in LARGE}


def _pack(arrs, rows, dtype=f32):
    flat = jnp.concatenate([a.reshape(-1).astype(dtype) for a in arrs])
    return jnp.pad(flat, (0, rows * 1024 - flat.shape[0])).reshape(rows, 1024)


def _unpack(buf, shapes):
    flat, out, pos = buf.reshape(-1), [], 0
    for shp in shapes:
        n = 1
        for d in shp:
            n *= d
        out.append(flat[pos:pos + n].reshape(shp))
        pos += n
    return out


def _rows_for(shapes, mult):
    n = 0
    for shp in shapes:
        k = 1
        for d in shp:
            k *= d
        n += k
    rows = -(-n // 1024)
    return -(-rows // mult) * mult


def _shard(a, axis, chip):
    size = a.shape[axis] // N_CHIPS
    return lax.dynamic_slice_in_dim(a, chip * size, size, axis)


def kernel(x, w_in, dn_conv, dn_a_log, dn_dt_bias, dn_norm, w_br_dn, cf_dw, cf_dw_bias, cf_ln_g, cf_ln_b, w_br_cf, s5_a_re, s5_a_im, s5_log_dt, s5_b_re, s5_b_im, s5_c_re, s5_c_im, s5_d, w_br_s5, gla_w_alpha, gla_b_alpha, gla_norm, w_br_gla, w_o, ln1_g, ln1_b, w_up, ffn_conv, w_down, ln2_g, ln2_b, loss_target, m_w_in, m_dn_conv, m_dn_a_log, m_dn_dt_bias, m_dn_norm, m_w_br_dn, m_cf_dw, m_cf_dw_bias, m_cf_ln_g, m_cf_ln_b, m_w_br_cf, m_s5_a_re, m_s5_a_im, m_s5_log_dt, m_s5_b_re, m_s5_b_im, m_s5_c_re, m_s5_c_im, m_s5_d, m_w_br_s5, m_gla_w_alpha, m_gla_b_alpha, m_gla_norm, m_w_br_gla, m_w_o, m_ln1_g, m_ln1_b, m_w_up, m_ffn_conv, m_w_down, m_ln2_g, m_ln2_b, v_w_in, v_dn_conv, v_dn_a_log, v_dn_dt_bias, v_dn_norm, v_w_br_dn, v_cf_dw, v_cf_dw_bias, v_cf_ln_g, v_cf_ln_b, v_w_br_cf, v_s5_a_re, v_s5_a_im, v_s5_log_dt, v_s5_b_re, v_s5_b_im, v_s5_c_re, v_s5_c_im, v_s5_d, v_w_br_s5, v_gla_w_alpha, v_gla_b_alpha, v_gla_norm, v_w_br_gla, v_w_o, v_ln1_g, v_ln1_b, v_w_up, v_ffn_conv, v_w_down, v_ln2_g, v_ln2_b):
    env = locals()
    w = {n: env[n] for n in WEIGHTS}
    m = {n: env['m_' + n] for n in WEIGHTS}
    v = {n: env['v_' + n] for n in WEIGHTS}
    chip = 2 * lax.axis_index("x") + lax.axis_index("y")

    large_shapes = [w[n].shape for n in LARGE]
    ssh_shapes = [w[n].shape for n in SMALL_SHARDED]
    large_rows, ssh_rows = _rows_for(large_shapes, 512), _rows_for(ssh_shapes, 8)
    got_large, got_ssh = _gather_chips(_pack([w[n] for n in LARGE], large_rows, bf16),
                                       _pack([w[n] for n in SMALL_SHARDED], ssh_rows))
    full = {n: w[n] for n in SMALL if n not in SHARD_AXIS}
    per_chip = [dict(zip(LARGE + SMALL_SHARDED, _unpack(got_large[k], large_shapes) + _unpack(got_ssh[k], ssh_shapes)))
                for k in range(N_CHIPS)]
    for n in LARGE + SMALL_SHARDED:
        full[n] = jnp.concatenate([per_chip[k][n] for k in range(N_CHIPS)], axis=SHARD_AXIS[n])

    loss, grad_x, g, g_large = _local_step(x[0], loss_target[0], full)
    loss = lax.psum(loss, ("x", "y", "c"))

    def piece(k):
        parts = []
        for n in LARGE:
            axis = SHARD_AXIS[n] - 1
            size = g_large[n][0].shape[axis] // N_CHIPS
            parts += [lax.slice_in_dim(layer, k * size, (k + 1) * size, axis=axis) for layer in g_large[n]]
        return _pack(parts, large_rows, bf16)

    half = large_rows // 2
    core = lax.axis_index("c")
    pieces = jnp.stack([piece(k) for k in range(N_CHIPS)])
    p_mine = lax.dynamic_slice_in_dim(pieces, core * half, half, axis=1).reshape(N_CHIPS * half, 1024)
    p_theirs = lax.dynamic_slice_in_dim(pieces, (1 - core) * half, half, axis=1).reshape(N_CHIPS * half, 1024)
    pair = _sum_parts('sum_pair', [p_mine, _swap_sibling(p_theirs)], bf16).reshape(N_CHIPS, half, 1024)
    from_chips = _scatter_chips(pair)
    own = lax.dynamic_index_in_dim(pair, chip, 0, keepdims=False)
    chip_sum = _join_halves(_sum_parts('sum_chips', [own, from_chips[0], from_chips[1], from_chips[2]]))
    res = {0: dict(zip(LARGE, _unpack(chip_sum, large_shapes))), 1: {}, 2: {}, 3: {}}
    for n in LARGE:
        two_d = (w[n].shape[0] * w[n].shape[1], w[n].shape[2])
        upd = _adamw('adamw_large', w[n].reshape(two_d), m[n].reshape(two_d), v[n].reshape(two_d), res[0][n].reshape(two_d))
        for kind in range(3):
            res[kind + 1][n] = upd[kind].reshape(w[n].shape)

    small_full_shapes = [g[n].shape for n in SMALL]
    small_rows = _rows_for(small_full_shapes, 8)
    small_sum = _sum_slots('sum_devices', _gather_all(_pack([g[n] for n in SMALL], small_rows)))
    gs = dict(zip(SMALL, _unpack(small_sum, small_full_shapes)))
    for n in SMALL_SHARDED:
        gs[n] = _shard(gs[n], SHARD_AXIS[n], chip)
    small_shapes = [w[n].shape for n in SMALL]
    upd_rows = _rows_for(small_shapes, 8)
    upd = _adamw('adamw_small', _pack([w[n] for n in SMALL], upd_rows), _pack([m[n] for n in SMALL], upd_rows),
                 _pack([v[n] for n in SMALL], upd_rows), _pack([gs[n] for n in SMALL], upd_rows))
    res[0].update(gs)
    for kind in range(3):
        res[kind + 1].update(zip(SMALL, _unpack(upd[kind], small_shapes)))
    return (loss, grad_x[None], *[res[kind][n] for kind in range(4) for n in WEIGHTS])
```

```python
import functools

import jax
import jax.numpy as jnp
from jax import lax
from jax.experimental import pallas as pl
from jax.experimental.pallas import tpu as pltpu

f32 = jnp.float32
bf16 = jnp.bfloat16
HI = lax.Precision.HIGHEST

D_MODEL = 1024
DEPTH = 4
DN_HEADS, DN_DK, DN_CHUNK = 4, 128, 64
DN_STEP_CHUNKS = 2
GLA_HEADS, GLA_DK, GLA_CHUNK, GLA_TAU = 4, 64, 16, 16.0
GLA_BLOCK = 128
S5_GROUPS, S5_GROUP, S5_STATE = 32, 16, 64
S5_SB = 4
S5_TILE = 256
D_FF = 2816
LN_EPS = 1e-5
ALPHA = (2.0 * DEPTH) ** 0.25
ADAM_LR, ADAM_B1, ADAM_B2, ADAM_EPS, ADAM_WD, ADAM_STEP = 0.001, 0.9, 0.999, 1e-08, 0.01, 10

VMEM_LIMIT_V7X = 56 * 1024 * 1024
MM_A_BLOCK_BYTES = 8 * 1024 * 1024
HALO = 32
N_CHIPS = 4
N_DEV = 8

IN_ORIG = dict(dn_qkv=(0, 1536), dn_a=(1536, 1540), dn_b=(1540, 1544), dn_gate=(1544, 2056), cf=(2056, 3080),
               s5=(3080, 3592), gla_q=(3592, 3848), gla_k=(3848, 4104), gla_v=(4104, 4616), gla_g=(4616, 5128),
               gla_lr=(5128, 5144), gates=(5144, 9240))
IN_COLS = 9240
WM_COLS = 9728
WS_COLS = 128


def _cparams(sem):
    return pltpu.CompilerParams(dimension_semantics=sem, vmem_limit_bytes=VMEM_LIMIT_V7X)


def _pick(dim, pref):
    for t in (pref, 512, 256, 128, 64, 32, 16, 8):
        if t <= pref and dim % t == 0:
            return t
    return dim


NN = (((1,), (0,)), ((), ()))
NT = (((1,), (1,)), ((), ()))
TN = (((0,), (0,)), ((), ()))


def _dot(a, b, dims=NN):
    return lax.dot_general(a, b, dims, precision=HI, preferred_element_type=f32)


def _round(a):
    return a.astype(bf16).astype(f32)


def _rdot(a, b, dims=NN):
    return lax.dot_general(a.astype(bf16), b.astype(bf16), dims, preferred_element_type=f32)


@functools.partial(jax.custom_vjp, nondiff_argnums=(2,))
def _bdot(a, b, dims=NN):
    return _rdot(a, b, dims)


def _bdot_fwd(a, b, dims):
    return _rdot(a, b, dims), (a, b)


def _bdot_bwd(dims, res, g):
    a, b = res
    if dims == NN:
        return _rdot(g, b, NT), _rdot(a, g, TN)
    if dims == NT:
        return _rdot(g, b, NN), _rdot(g, a, TN)
    assert dims == TN
    return _rdot(b, g, NT), _rdot(a, g, NN)


_bdot.defvjp(_bdot_fwd, _bdot_bwd)


def _iota(shape, axis):
    return lax.broadcasted_iota(jnp.int32, shape, axis)


def _mm(a, b, mode, name, tn=512, out_dtype=f32):
    if mode == 'nn':
        (m, k), n = a.shape, b.shape[1]
    elif mode == 'nt':
        (m, k), n = a.shape, b.shape[0]
    else:
        (k, m), n = a.shape, b.shape[1]
    tm, tn = _pick(m, 512 if mode == 'tn' else 1024), _pick(n, tn)
    nk = 1
    while (k // nk) * tm * a.dtype.itemsize > MM_A_BLOCK_BYTES or k % nk or (k // nk) % 128:
        nk += 1
    tk = k // nk
    assert nk == 1 or out_dtype == f32
    if mode == 'nn':
        dims = NN
        a_spec = pl.BlockSpec((tm, tk), lambda i, j, kk: (i, kk))
        b_spec = pl.BlockSpec((tk, tn), lambda i, j, kk: (kk, j))
    elif mode == 'nt':
        dims = NT
        a_spec = pl.BlockSpec((tm, tk), lambda i, j, kk: (i, kk))
        b_spec = pl.BlockSpec((tn, tk), lambda i, j, kk: (j, kk))
    else:
        dims = TN
        a_spec = pl.BlockSpec((tk, tm), lambda i, j, kk: (kk, i))
        b_spec = pl.BlockSpec((tk, tn), lambda i, j, kk: (kk, j))

    def kern(a_ref, b_ref, o_ref):
        p = lax.dot_general(a_ref[...].astype(bf16), b_ref[...].astype(bf16), dims, preferred_element_type=f32)
        if nk == 1:
            o_ref[...] = p.astype(o_ref.dtype)
        else:
            kk = pl.program_id(2)

            @pl.when(kk == 0)
            def _():
                o_ref[...] = p

            @pl.when(kk > 0)
            def _():
                o_ref[...] += p

    return pl.pallas_call(
        kern, name=name, grid=(m // tm, n // tn, nk),
        in_specs=[a_spec, b_spec], out_specs=pl.BlockSpec((tm, tn), lambda i, j, kk: (i, j)),
        out_shape=jax.ShapeDtypeStruct((m, n), out_dtype),
        compiler_params=_cparams(("parallel", "parallel", "arbitrary")),
    )(a, b)


def _full_spec(p):
    nd = p.ndim
    return pl.BlockSpec(p.shape, lambda *_, nd=nd: (0,) * nd)


def _rowwise(name, f, ins, params, out_widths, tile=256, out_dtype=f32):
    rows = ins[0][0].shape[0]
    tile = _pick(rows, tile)
    n_x = len(ins) + len(params)

    def kern(*refs):
        for o_ref, r in zip(refs[n_x:], f(*[r[...] for r in refs[:n_x]])):
            o_ref[...] = r.astype(o_ref.dtype)

    in_specs = [pl.BlockSpec((tile, w), lambda i, c=c: (i, c)) for (_, w, c) in ins] + [_full_spec(p) for p in params]
    return pl.pallas_call(
        kern, name=name, grid=(rows // tile,), in_specs=in_specs,
        out_specs=[pl.BlockSpec((tile, w), lambda i: (i, 0)) for w in out_widths],
        out_shape=[jax.ShapeDtypeStruct((rows, w), out_dtype) for w in out_widths],
        compiler_params=_cparams(("parallel",)),
    )(*[a for (a, _, _) in ins], *params)


def _rowwise_bwd(name, f, ins, params, douts, want, tile=256):
    rows = ins[0][0].shape[0]
    tile = _pick(rows, tile)
    n_in, n_p = len(ins), len(params)
    parts = [p for d in douts for p in d]
    n_x, n_d = n_in + n_p, len(parts)

    def kern(*refs):
        xs = [r[...] for r in refs[:n_x]]
        d_refs, o_refs = refs[n_x:n_x + n_d], refs[n_x + n_d:]
        cts, pos = [], 0
        for d in douts:
            acc = d_refs[pos][...]
            for r in d_refs[pos + 1:pos + len(d)]:
                acc = acc + r[...]
            pos += len(d)
            cts.append(acc)
        grads = jax.vjp(f, *xs)[1](tuple(cts))
        k = 0
        for j in range(n_in):
            if want[j]:
                o_refs[k][...] = grads[j].astype(o_refs[k].dtype)
                k += 1
        first = pl.program_id(0) == 0
        for j in range(n_p):
            g, o_ref = grads[n_in + j], o_refs[k + j]

            @pl.when(first)
            def _(o_ref=o_ref, g=g):
                o_ref[...] = g

            @pl.when(jnp.logical_not(first))
            def _(o_ref=o_ref, g=g):
                o_ref[...] += g

    in_specs = ([pl.BlockSpec((tile, w), lambda i, c=c: (i, c)) for (_, w, c) in ins] + [_full_spec(p) for p in params]
                + [pl.BlockSpec((tile, w), lambda i, c=c: (i, c)) for (_, w, c) in parts])
    out_specs, out_shape = [], []
    for j in range(n_in):
        if want[j]:
            out_specs.append(pl.BlockSpec((tile, ins[j][1]), lambda i: (i, 0)))
            out_shape.append(jax.ShapeDtypeStruct((rows, ins[j][1]), want[j]))
    n_g = len(out_specs)
    for p in params:
        out_specs.append(_full_spec(p))
        out_shape.append(jax.ShapeDtypeStruct(p.shape, f32))
    res = pl.pallas_call(
        kern, name=name, grid=(rows // tile,), in_specs=in_specs, out_specs=out_specs, out_shape=out_shape,
        compiler_params=_cparams(("arbitrary",)),
    )(*[a for (a, _, _) in ins], *params, *[a for (a, _, _) in parts])
    return list(res[:n_g]), list(res[n_g:])


_sigmoid = jax.nn.sigmoid
_silu = jax.nn.silu
_softplus = jax.nn.softplus
_log_sigmoid = jax.nn.log_sigmoid


def _f_ln(x, r, g, b):
    t = ALPHA * x + r
    mu = jnp.mean(t, -1, keepdims=True)
    var = jnp.mean(jnp.square(t - mu), -1, keepdims=True)
    return ((t - mu) * lax.rsqrt(var + LN_EPS) * g + b,)


def _f_pre(c1, hs, a_log, dt_bias, w_alpha, b_alpha):
    s = _silu(c1)
    outs = []
    for j in range(3 * DN_HEADS):
        t = s[:, j * DN_DK:(j + 1) * DN_DK]
        if j < 2 * DN_HEADS:
            t = t * lax.rsqrt(jnp.sum(t * t, -1, keepdims=True) + 1e-6)
        if j < DN_HEADS:
            t = t * (DN_DK ** -0.5)
        outs.append(t)
    qkvn = jnp.concatenate(outs, axis=1)
    lane = _iota(hs.shape, 1)
    g = -jnp.exp(a_log) * _softplus(hs + dt_bias)
    beta = _sigmoid(hs)
    gb = jnp.where(lane < DN_HEADS, g, jnp.where(lane < 2 * DN_HEADS, beta, 0.0))
    la = _log_sigmoid(_bdot(hs, w_alpha) + b_alpha) * (1.0 / GLA_TAU)
    lane5 = _iota(la.shape, 1)
    la = jnp.where((lane5 % 128) < GLA_DK, la, 0.0)
    return qkvn, gb, la


def _f_post(o, gate, w):
    outs = []
    for j in range(4):
        t = o[:, j * 128:(j + 1) * 128]
        outs.append(t * lax.rsqrt(jnp.mean(t * t, -1, keepdims=True) + LN_EPS) * w)
    return (jnp.concatenate(outs, axis=1) * _silu(gate),)


def _f_glu(a, g):
    return (a * _sigmoid(g),)


def _f_cfpost(c, bias, g, b):
    t = c + bias
    mu = jnp.mean(t, -1, keepdims=True)
    var = jnp.mean(jnp.square(t - mu), -1, keepdims=True)
    return (_silu((t - mu) * lax.rsqrt(var + LN_EPS) * g + b),)


def _f_gelu(y):
    return (jax.nn.gelu(y),)


def _f_merge(ga, gb_, gc, gd, ya, yb, zv, zg, yd):
    return (_sigmoid(ga) * ya + _sigmoid(gb_) * yb + _sigmoid(gc) * (zv * _sigmoid(zg)) + _sigmoid(gd) * yd,)


def _f_act(a, b):
    return (_silu(a) * b,)


def _conv_tiles(rows, ch):
    return _pick(rows, 256), _pick(ch, 512)


def _conv_fwd(name, x, width, colblk0, w, taps):
    rows = x.shape[0]
    tr, cb = _conv_tiles(rows, width)
    hb = tr // HALO

    def kern(prev_ref, x_ref, w_ref, o_ref, ext):
        i = pl.program_id(1)
        ext[pl.ds(0, HALO), :] = jnp.where(i > 0, _round(prev_ref[...]), 0.0)
        ext[pl.ds(HALO, tr), :] = _round(x_ref[...])
        wv = _round(w_ref[...])
        acc = jnp.zeros((tr, cb), f32)
        for k in range(taps):
            acc = acc + wv[k:k + 1, :] * ext[pl.ds(HALO - taps + 1 + k, tr), :]
        o_ref[...] = acc

    c0 = colblk0 * (width // cb)
    return pl.pallas_call(
        kern, name=name, grid=(width // cb, rows // tr),
        in_specs=[pl.BlockSpec((HALO, cb), lambda c, i: (jnp.maximum(i * hb - 1, 0), c0 + c)),
                  pl.BlockSpec((tr, cb), lambda c, i: (i, c0 + c)),
                  pl.BlockSpec((w.shape[0], cb), lambda c, i: (0, c))],
        out_specs=pl.BlockSpec((tr, cb), lambda c, i: (i, c)),
        out_shape=jax.ShapeDtypeStruct((rows, width), f32),
        scratch_shapes=[pltpu.VMEM((HALO + tr, cb), f32)],
        compiler_params=_cparams(("parallel", "arbitrary")),
    )(x, x, w)


def _conv_bwd(name, x, width, colblk0, w, taps, dy, dx_dtype=f32):
    rows = x.shape[0]
    tr, cb = _conv_tiles(rows, width)
    hb = tr // HALO
    nt = rows // tr
    wr = w.shape[0]

    def kern(prev_ref, x_ref, w_ref, dy_ref, next_ref, dx_ref, dw_ref, ext, dext):
        i = pl.program_id(1)
        ext[pl.ds(0, HALO), :] = jnp.where(i > 0, _round(prev_ref[...]), 0.0)
        ext[pl.ds(HALO, tr), :] = _round(x_ref[...])
        dyv = _round(dy_ref[...])
        dext[pl.ds(0, tr), :] = dyv
        dext[pl.ds(tr, HALO), :] = jnp.where(i < nt - 1, _round(next_ref[...]), 0.0)
        wv = _round(w_ref[...])
        acc = jnp.zeros((tr, cb), f32)
        rows_w = []
        for k in range(taps):
            acc = acc + wv[k:k + 1, :] * dext[pl.ds(taps - 1 - k, tr), :]
            rows_w.append(jnp.sum(dyv * ext[pl.ds(HALO - taps + 1 + k, tr), :], axis=0, keepdims=True))
        dx_ref[...] = acc.astype(dx_ref.dtype)
        if wr > taps:
            rows_w.append(jnp.zeros((wr - taps, cb), f32))
        dwv = jnp.concatenate(rows_w, axis=0)

        @pl.when(i == 0)
        def _():
            dw_ref[...] = dwv

        @pl.when(i > 0)
        def _():
            dw_ref[...] += dwv

    c0 = colblk0 * (width // cb)
    return pl.pallas_call(
        kern, name=name, grid=(width // cb, nt),
        in_specs=[pl.BlockSpec((HALO, cb), lambda c, i: (jnp.maximum(i * hb - 1, 0), c0 + c)),
                  pl.BlockSpec((tr, cb), lambda c, i: (i, c0 + c)),
                  pl.BlockSpec((wr, cb), lambda c, i: (0, c)),
                  pl.BlockSpec((tr, cb), lambda c, i: (i, c)),
                  pl.BlockSpec((HALO, cb), lambda c, i: (jnp.minimum((i + 1) * hb, nt * hb - 1), c))],
        out_specs=[pl.BlockSpec((tr, cb), lambda c, i: (i, c)), pl.BlockSpec((wr, cb), lambda c, i: (0, c))],
        out_shape=[jax.ShapeDtypeStruct((rows, width), dx_dtype), jax.ShapeDtypeStruct((wr, width), f32)],
        scratch_shapes=[pltpu.VMEM((HALO + tr, cb), f32), pltpu.VMEM((HALO + tr, cb), f32)],
        compiler_params=_cparams(("parallel", "arbitrary")),
    )(x, x, w, dy, dy)


def _series_inverse(neg):
    n = neg.shape[0]
    inv = jnp.where(_iota((n, n), 0) == _iota((n, n), 1), 1.0, 0.0) + neg
    p = neg
    for _ in range(5):
        p = _dot(p, p)
        inv = inv + _dot(inv, p)
    return inv


def _inverse_bwd(inv, g):
    return _dot(_dot(inv, g, TN), inv, NT)


@jax.custom_vjp
def _unit_lower_inverse(neg):
    return _series_inverse(neg)


_unit_lower_inverse.defvjp(lambda neg: (_series_inverse(neg),) * 2, lambda inv, g: (_inverse_bwd(inv, g),))


@jax.custom_vjp
def _known_inverse(neg, inv):
    return inv


_known_inverse.defvjp(lambda neg, inv: (inv, inv), lambda inv, g: (_inverse_bwd(inv, g), jnp.zeros_like(inv)))


def _dn_head(state, q, k, v, gc, gl, beta, inv_saved=None):
    c = DN_CHUNK
    ii, jj = _iota((c, c), 0), _iota((c, c), 1)
    causal, strict = ii >= jj, ii > jj
    gcb = jnp.broadcast_to(gc, (c, c))
    decay = jnp.where(causal, jnp.exp(jnp.where(causal, gcb - gcb.T, 0.0)), 0.0)
    kb = k * beta
    neg = -jnp.where(strict, _bdot(kb, k, NT) * decay, 0.0)
    inv = _unit_lower_inverse(neg) if inv_saved is None else _known_inverse(neg, inv_saved)
    egc = jnp.exp(gc)
    u = _dot(inv, v * beta)
    w = _dot(inv, kb * egc)
    intra = _bdot(q, k, NT) * decay
    v_new = u - _bdot(w, state)
    o = _bdot(q * egc, state) + _bdot(intra, v_new)
    new_state = state * jnp.exp(gl) + _bdot(k * jnp.exp(gl - gc), v_new, TN)
    return (o, new_state, inv) if inv_saved is None else (o, new_state)


def _dn_cum(gb):
    c = DN_CHUNK
    tril = jnp.where(_iota((c, c), 0) >= _iota((c, c), 1), 1.0, 0.0)
    return _dot(tril, gb), jnp.sum(gb, axis=0, keepdims=True)


def _dn_block_head(state, q, k, v, gc, gls, beta, invs=None):
    c = DN_CHUNK
    outs, new_invs = [], []
    for n, gl in enumerate(gls):
        sl = slice(n * c, (n + 1) * c)
        if invs is None:
            o, state, inv = _dn_head(state, q[sl], k[sl], v[sl], gc[sl], gl, beta[sl])
            new_invs.append(inv)
        else:
            o, state = _dn_head(state, q[sl], k[sl], v[sl], gc[sl], gl, beta[sl], inv_saved=invs[n])
        outs.append(o)
    o = jnp.concatenate(outs, axis=0)
    return (o, state, new_invs) if invs is None else (o, state)


def _dn_block_cum(gbv, nbk):
    c = DN_CHUNK
    cums, tots = zip(*[_dn_cum(gbv[n * c:(n + 1) * c]) for n in range(nbk)])
    return jnp.concatenate(cums, axis=0), tots


def _dn_fwd(qkvn, gb):
    rows = qkvn.shape[0]
    c, h, d = DN_CHUNK, DN_HEADS, DN_DK
    nbk = DN_STEP_CHUNKS if rows % (c * DN_STEP_CHUNKS) == 0 else 1
    b, nb = c * nbk, rows // (c * nbk)

    def kern(qkv_ref, gb_ref, o_ref, st_ref, inv_ref, state):
        @pl.when(pl.program_id(0) == 0)
        def _():
            state[...] = jnp.zeros_like(state)

        gbv = gb_ref[...]
        cum, tots = _dn_block_cum(gbv, nbk)
        for j in range(h):
            st = state[j]
            st_ref[0, j] = st
            o, new, invs = _dn_block_head(st, qkv_ref[:, j * d:(j + 1) * d], qkv_ref[:, (h + j) * d:(h + j + 1) * d],
                                          qkv_ref[:, (2 * h + j) * d:(2 * h + j + 1) * d],
                                          cum[:, j:j + 1], [t[:, j:j + 1] for t in tots], gbv[:, h + j:h + j + 1])
            o_ref[:, j * d:(j + 1) * d] = o
            for n in range(nbk):
                inv_ref[0, n * h + j] = invs[n]
            state[j] = new

    return pl.pallas_call(
        kern, name="dn_fwd", grid=(nb,),
        in_specs=[pl.BlockSpec((b, 3 * h * d), lambda i: (i, 0)), pl.BlockSpec((b, 128), lambda i: (i, 0))],
        out_specs=[pl.BlockSpec((b, h * d), lambda i: (i, 0)), pl.BlockSpec((1, h, d, d), lambda i: (i, 0, 0, 0)),
                   pl.BlockSpec((1, nbk * h, c, c), lambda i: (i, 0, 0, 0))],
        out_shape=[jax.ShapeDtypeStruct((rows, h * d), f32), jax.ShapeDtypeStruct((nb, h, d, d), f32),
                   jax.ShapeDtypeStruct((nb, nbk * h, c, c), f32)],
        scratch_shapes=[pltpu.VMEM((h, d, d), f32)],
        compiler_params=_cparams(("arbitrary",)),
    )(qkvn, gb)


def _dn_bwd(qkvn, gb, states, invs, do):
    rows = qkvn.shape[0]
    c, h, d = DN_CHUNK, DN_HEADS, DN_DK
    nb = states.shape[0]
    nbk = rows // (c * nb)
    b = c * nbk

    def kern(qkv_ref, gb_ref, st_ref, inv_ref, do_ref, dqkv_ref, dgb_ref, dstate):
        @pl.when(pl.program_id(0) == 0)
        def _():
            dstate[...] = jnp.zeros_like(dstate)

        gbv = gb_ref[...]
        cum, tots = _dn_block_cum(gbv, nbk)
        lane = _iota((b, 128), 1)
        lane_c = _iota((c, 128), 1)
        dcum = jnp.zeros((b, 128), f32)
        dgb = jnp.zeros((b, 128), f32)
        for j in range(h):
            args = (st_ref[0, j], qkv_ref[:, j * d:(j + 1) * d], qkv_ref[:, (h + j) * d:(h + j + 1) * d],
                    qkv_ref[:, (2 * h + j) * d:(2 * h + j + 1) * d],
                    cum[:, j:j + 1], [t[:, j:j + 1] for t in tots], gbv[:, h + j:h + j + 1])
            head = functools.partial(_dn_block_head, invs=[inv_ref[0, n * h + j] for n in range(nbk)])
            ds, dq, dk, dv, dgc, dgls, dbeta = jax.vjp(head, *args)[1]((do_ref[:, j * d:(j + 1) * d], dstate[j]))
            dstate[j] = ds
            dqkv_ref[:, j * d:(j + 1) * d] = dq
            dqkv_ref[:, (h + j) * d:(h + j + 1) * d] = dk
            dqkv_ref[:, (2 * h + j) * d:(2 * h + j + 1) * d] = dv
            dcum = dcum + jnp.where(lane == j, dgc, 0.0)
            dgb = dgb + jnp.where(lane == h + j, dbeta, 0.0) + jnp.concatenate(
                [jnp.where(lane_c == j, dgl, 0.0) for dgl in dgls], axis=0)
        triu = jnp.where(_iota((c, c), 0) <= _iota((c, c), 1), 1.0, 0.0)
        dgb_ref[...] = dgb + jnp.concatenate([_dot(triu, dcum[n * c:(n + 1) * c]) for n in range(nbk)], axis=0)

    rev = lambda i: (nb - 1 - i, 0)
    rev4 = lambda i: (nb - 1 - i, 0, 0, 0)
    return pl.pallas_call(
        kern, name="dn_bwd", grid=(nb,),
        in_specs=[pl.BlockSpec((b, 3 * h * d), rev), pl.BlockSpec((b, 128), rev),
                  pl.BlockSpec((1, h, d, d), rev4), pl.BlockSpec((1, nbk * h, c, c), rev4), pl.BlockSpec((b, h * d), rev)],
        out_specs=[pl.BlockSpec((b, 3 * h * d), rev), pl.BlockSpec((b, 128), rev)],
        out_shape=[jax.ShapeDtypeStruct((rows, 3 * h * d), f32), jax.ShapeDtypeStruct((rows, 128), f32)],
        scratch_shapes=[pltpu.VMEM((h, d, d), f32)],
        compiler_params=_cparams(("arbitrary",)),
    )(qkvn, gb, states, invs, do)


def _gla_block(state_t, q, k, v, la):
    c, b = GLA_CHUNK, q.shape[0]
    ii, jj = _iota((b, b), 0), _iota((b, b), 1)
    causal = (ii >= jj) & (ii - jj < c) & (jnp.bitwise_and(ii, c - 1) >= jnp.bitwise_and(jj, c - 1))
    gc = _dot(jnp.where(causal, 1.0, 0.0), la)
    q_dec = q * (GLA_DK ** -0.5) * jnp.exp(gc)
    scores = jnp.where(causal, _bdot(q_dec, k * jnp.exp(-gc), NT), 0.0)
    intra = _bdot(scores, v)
    outs = []
    for n in range(b // c):
        sl = slice(n * c, (n + 1) * c)
        gl = gc[(n + 1) * c - 1:(n + 1) * c, :]
        outs.append(_bdot(q_dec[sl], state_t, NT))
        state_t = state_t * jnp.exp(gl) + _bdot(v[sl], k[sl] * jnp.exp(gl - gc[sl]), TN)
    return jnp.concatenate(outs, axis=0) + intra, state_t


GLA_Q0, GLA_K0, GLA_V0 = 7680 // 512, 8192 // 512, 8704 // 512


def _gla_fwd(hmain, la):
    rows = hmain.shape[0]
    b, h = _pick(rows, GLA_BLOCK), GLA_HEADS
    nb = rows // b

    def kern(q_ref, k_ref, v_ref, la_ref, o_ref, st_ref, state):
        @pl.when(pl.program_id(0) == 0)
        def _():
            state[...] = jnp.zeros_like(state)

        for j in range(h):
            sl = slice(j * 128, (j + 1) * 128)
            st = state[j]
            st_ref[0, j] = st
            o, new = _gla_block(st, q_ref[:, sl], k_ref[:, sl], v_ref[:, sl], la_ref[:, sl])
            o_ref[:, sl] = o
            state[j] = new

    return pl.pallas_call(
        kern, name="gla_fwd", grid=(nb,),
        in_specs=[pl.BlockSpec((b, 512), lambda i: (i, GLA_Q0)), pl.BlockSpec((b, 512), lambda i: (i, GLA_K0)),
                  pl.BlockSpec((b, 512), lambda i: (i, GLA_V0)), pl.BlockSpec((b, 512), lambda i: (i, 0))],
        out_specs=[pl.BlockSpec((b, 512), lambda i: (i, 0)), pl.BlockSpec((1, h, 128, 128), lambda i: (i, 0, 0, 0))],
        out_shape=[jax.ShapeDtypeStruct((rows, h * 128), f32), jax.ShapeDtypeStruct((nb, h, 128, 128), f32)],
        scratch_shapes=[pltpu.VMEM((h, 128, 128), f32)],
        compiler_params=_cparams(("arbitrary",)),
    )(hmain, hmain, hmain, la)


def _gla_bwd(hmain, la, states, do):
    rows = hmain.shape[0]
    b, h = _pick(rows, GLA_BLOCK), GLA_HEADS
    nb = rows // b

    def kern(q_ref, k_ref, v_ref, la_ref, st_ref, do_ref, dq_ref, dk_ref, dv_ref, dla_ref, dstate):
        @pl.when(pl.program_id(0) == 0)
        def _():
            dstate[...] = jnp.zeros_like(dstate)

        for j in range(h):
            sl = slice(j * 128, (j + 1) * 128)
            args = (st_ref[0, j], q_ref[:, sl], k_ref[:, sl], v_ref[:, sl], la_ref[:, sl])
            ds, dq, dk, dv, dla = jax.vjp(_gla_block, *args)[1]((do_ref[:, sl], dstate[j]))
            dstate[j] = ds
            dq_ref[:, sl] = dq.astype(bf16)
            dk_ref[:, sl] = dk.astype(bf16)
            dv_ref[:, sl] = dv.astype(bf16)
            dla_ref[:, sl] = dla

    rev = lambda i: (nb - 1 - i, 0)
    return pl.pallas_call(
        kern, name="gla_bwd", grid=(nb,),
        in_specs=[pl.BlockSpec((b, 512), lambda i: (nb - 1 - i, GLA_Q0)), pl.BlockSpec((b, 512), lambda i: (nb - 1 - i, GLA_K0)),
                  pl.BlockSpec((b, 512), lambda i: (nb - 1 - i, GLA_V0)), pl.BlockSpec((b, 512), rev),
                  pl.BlockSpec((1, h, 128, 128), lambda i: (nb - 1 - i, 0, 0, 0)), pl.BlockSpec((b, 512), rev)],
        out_specs=[pl.BlockSpec((b, 512), rev)] * 4,
        out_shape=[jax.ShapeDtypeStruct((rows, h * 128), bf16)] * 3 + [jax.ShapeDtypeStruct((rows, h * 128), f32)],
        scratch_shapes=[pltpu.VMEM((h, 128, 128), f32)],
        compiler_params=_cparams(("arbitrary",)),
    )(hmain, hmain, hmain, la, states, do)


def _f_s5_params(a_re, a_im, log_dt, b_re, b_im):
    dt = jnp.exp(log_dt)
    mag = jnp.exp(dt * a_re)
    abar_re, abar_im = mag * jnp.cos(dt * a_im), mag * jnp.sin(dt * a_im)
    den = a_re * a_re + a_im * a_im
    nr, ni = abar_re - 1.0, abar_im
    fr, fi = (nr * a_re + ni * a_im) / den, (ni * a_re - nr * a_im) / den
    return abar_re, abar_im, fr[None] * b_re - fi[None] * b_im, fr[None] * b_im + fi[None] * b_re


def _s5_params(a_re, a_im, log_dt, b_re, b_im):
    def kern(*refs):
        for o_ref, r in zip(refs[5:], _f_s5_params(*[r[...] for r in refs[:5]])):
            o_ref[...] = r

    ins = (a_re, a_im, log_dt, b_re, b_im)
    return pl.pallas_call(
        kern, name="s5_params", out_shape=[jax.ShapeDtypeStruct(a_re.shape, f32)] * 2 + [jax.ShapeDtypeStruct(b_re.shape, f32)] * 2,
    )(*ins)


def _s5_params_bwd(a_re, a_im, log_dt, b_re, b_im, d_ar, d_ai, d_br, d_bi):
    def kern(*refs):
        grads = jax.vjp(_f_s5_params, *[r[...] for r in refs[:5]])[1](tuple(r[...] for r in refs[5:9]))
        for o_ref, g in zip(refs[9:], grads):
            o_ref[...] = g

    ins = (a_re, a_im, log_dt, b_re, b_im)
    return pl.pallas_call(
        kern, name="s5_params_bwd", out_shape=[jax.ShapeDtypeStruct(t.shape, f32) for t in ins],
    )(*ins, d_ar, d_ai, d_br, d_bi)


def _cmul(ar, ai, br, bi):
    return ar * br - ai * bi, ar * bi + ai * br


def _s5_scan(xr, xi, ar, ai, reverse):
    t = xr.shape[0]
    row = _iota(xr.shape, 0)
    s = 1
    while s < t:
        if reverse:
            keep = row < t - s
            sr, si = pltpu.roll(xr, t - s, 0), pltpu.roll(xi, t - s, 0)
        else:
            keep = row >= s
            sr, si = pltpu.roll(xr, s, 0), pltpu.roll(xi, s, 0)
        sr, si = jnp.where(keep, sr, 0.0), jnp.where(keep, si, 0.0)
        pr, pi = _cmul(ar, ai, sr, si)
        xr, xi = xr + pr, xi + pi
        ar, ai = _cmul(ar, ai, ar, ai)
        s *= 2
    return xr, xi


def _s5_powers(ar, ai, t, reverse):
    row = _iota((t, ar.shape[1]), 0)
    at = (row == (t - 1 if reverse else 0))
    return _s5_scan(jnp.where(at, ar, 0.0), jnp.where(at, ai, 0.0), ar, ai, reverse)


S5_U0 = 7168 // 128


def _s5_fwd(hmain, abar, bmat_re, bmat_im, cmat_re, cmat_im, dvec):
    rows = hmain.shape[0]
    t = _pick(rows, S5_TILE)
    nt, ns = rows // t, 512

    def kern(u_ref, a_ref, br_ref, bi_ref, cr_ref, ci_ref, d_ref, y_ref, xr_ref, xi_ref, pw, carry):
        ar, ai = a_ref[0, 0], a_ref[1, 0]

        @pl.when(pl.program_id(1) == 0)
        def _():
            pr, pi = _s5_powers(ar, ai, t, False)
            pw[0], pw[1] = pr, pi
            carry[...] = jnp.zeros_like(carry)

        u = u_ref[...]
        xr, xi = _s5_scan(_rdot(u, br_ref[0]), _rdot(u, bi_ref[0]), ar, ai, False)
        cr, ci = carry[0:1, :], carry[1:2, :]
        qr, qi = _cmul(pw[0], pw[1], cr, ci)
        xr, xi = xr + qr, xi + qi
        xr_ref[...] = xr
        xi_ref[...] = xi
        carry[0:1, :] = xr[t - 1:t, :]
        carry[1:2, :] = xi[t - 1:t, :]
        y_ref[...] = _rdot(xr, cr_ref[0]) - _rdot(xi, ci_ref[0]) + d_ref[...] * u

    sb3 = lambda b, i: (b, 0, 0)
    return pl.pallas_call(
        kern, name="s5_fwd", grid=(S5_SB, nt),
        in_specs=[pl.BlockSpec((t, 128), lambda b, i: (i, S5_U0 + b)), pl.BlockSpec((2, 1, 1, ns), lambda b, i: (0, b, 0, 0)),
                  pl.BlockSpec((1, 128, ns), sb3), pl.BlockSpec((1, 128, ns), sb3),
                  pl.BlockSpec((1, ns, 128), sb3), pl.BlockSpec((1, ns, 128), sb3), pl.BlockSpec((1, 128), lambda b, i: (0, b))],
        out_specs=[pl.BlockSpec((t, 128), lambda b, i: (i, b)), pl.BlockSpec((t, ns), lambda b, i: (i, b)),
                   pl.BlockSpec((t, ns), lambda b, i: (i, b))],
        out_shape=[jax.ShapeDtypeStruct((rows, 512), f32), jax.ShapeDtypeStruct((rows, S5_SB * ns), f32),
                   jax.ShapeDtypeStruct((rows, S5_SB * ns), f32)],
        scratch_shapes=[pltpu.VMEM((2, t, ns), f32), pltpu.VMEM((8, ns), f32)],
        compiler_params=_cparams(("parallel", "arbitrary")),
    )(hmain, abar, bmat_re, bmat_im, cmat_re, cmat_im, dvec)


def _s5_bwd(hmain, abar, bmat_re, bmat_im, cmat_re, cmat_im, dvec, x_re, x_im, dy):
    rows = hmain.shape[0]
    t = _pick(rows, S5_TILE)
    nt, ns = rows // t, 512
    t8 = t // 8

    def kern(u_ref, a_ref, br_ref, bi_ref, cr_ref, ci_ref, d_ref, xr_ref, xi_ref, xpr_ref, xpi_ref, dy_ref,
             du_ref, da_ref, dbr_ref, dbi_ref, dcr_ref, dci_ref, dd_ref, pw, carry):
        i = pl.program_id(1)
        ar, ai = a_ref[0, 0], -a_ref[1, 0]

        @pl.when(i == 0)
        def _():
            pr, pi = _s5_powers(ar, ai, t, True)
            pw[0], pw[1] = pr, pi
            carry[...] = jnp.zeros_like(carry)

        u, gy = u_ref[...], dy_ref[...]
        lr, li = _s5_scan(_rdot(gy, cr_ref[0], NT), -_rdot(gy, ci_ref[0], NT), ar, ai, True)
        qr, qi = _cmul(pw[0], pw[1], carry[0:1, :], carry[1:2, :])
        lr, li = lr + qr, li + qi
        carry[0:1, :] = lr[0:1, :]
        carry[1:2, :] = li[0:1, :]
        du_ref[...] = (_rdot(lr, br_ref[0], NT) + _rdot(li, bi_ref[0], NT) + d_ref[...] * gy).astype(bf16)
        xr, xi = xr_ref[...], xi_ref[...]
        row = _iota(xr.shape, 0)
        first_r = jnp.where(i < nt - 1, xpr_ref[7:8, :], 0.0)
        first_i = jnp.where(i < nt - 1, xpi_ref[7:8, :], 0.0)
        xpr = jnp.where(row == 0, first_r, pltpu.roll(xr, 1, 0))
        xpi = jnp.where(row == 0, first_i, pltpu.roll(xi, 1, 0))
        da_r = jnp.sum(lr * xpr + li * xpi, axis=0, keepdims=True)
        da_i = jnp.sum(li * xpr - lr * xpi, axis=0, keepdims=True)
        upd = [(da_ref.at[0, 0], da_r), (da_ref.at[1, 0], da_i),
               (dbr_ref.at[0], _rdot(u, lr, TN)), (dbi_ref.at[0], _rdot(u, li, TN)),
               (dcr_ref.at[0], _rdot(xr, gy, TN)), (dci_ref.at[0], -_rdot(xi, gy, TN)),
               (dd_ref, jnp.sum(gy * u, axis=0, keepdims=True))]

        @pl.when(i == 0)
        def _():
            for ref, val in upd:
                ref[...] = val

        @pl.when(i > 0)
        def _():
            for ref, val in upd:
                ref[...] += val

    sb3 = lambda b, i: (b, 0, 0)
    rev = lambda b, i: (nt - 1 - i, b)
    prev8 = lambda b, i: (jnp.maximum((nt - 1 - i) * t8 - 1, 0), b)
    return pl.pallas_call(
        kern, name="s5_bwd", grid=(S5_SB, nt),
        in_specs=[pl.BlockSpec((t, 128), lambda b, i: (nt - 1 - i, S5_U0 + b)), pl.BlockSpec((2, 1, 1, ns), lambda b, i: (0, b, 0, 0)),
                  pl.BlockSpec((1, 128, ns), sb3), pl.BlockSpec((1, 128, ns), sb3),
                  pl.BlockSpec((1, ns, 128), sb3), pl.BlockSpec((1, ns, 128), sb3), pl.BlockSpec((1, 128), lambda b, i: (0, b)),
                  pl.BlockSpec((t, ns), rev), pl.BlockSpec((t, ns), rev), pl.BlockSpec((8, ns), prev8), pl.BlockSpec((8, ns), prev8),
                  pl.BlockSpec((t, 128), rev)],
        out_specs=[pl.BlockSpec((t, 128), rev), pl.BlockSpec((2, 1, 1, ns), lambda b, i: (0, b, 0, 0)),
                   pl.BlockSpec((1, 128, ns), sb3), pl.BlockSpec((1, 128, ns), sb3),
                   pl.BlockSpec((1, ns, 128), sb3), pl.BlockSpec((1, ns, 128), sb3), pl.BlockSpec((1, 128), lambda b, i: (0, b))],
        out_shape=[jax.ShapeDtypeStruct((rows, 512), bf16), jax.ShapeDtypeStruct((2, S5_SB, 1, ns), f32),
                   jax.ShapeDtypeStruct((S5_SB, 128, ns), f32), jax.ShapeDtypeStruct((S5_SB, 128, ns), f32),
                   jax.ShapeDtypeStruct((S5_SB, ns, 128), f32), jax.ShapeDtypeStruct((S5_SB, ns, 128), f32),
                   jax.ShapeDtypeStruct((1, 512), f32)],
        scratch_shapes=[pltpu.VMEM((2, t, ns), f32), pltpu.VMEM((8, ns), f32)],
        compiler_params=_cparams(("parallel", "arbitrary")),
    )(hmain, abar, bmat_re, bmat_im, cmat_re, cmat_im, dvec, x_re, x_im, x_re, x_im, dy)


def _loss_head(y, target):
    rows, feat = y.shape
    tile = _pick(rows, 256)

    def kern(y_ref, t_ref, l_ref, dy_ref):
        e = y_ref[...] - t_ref[...]
        dy_ref[...] = e * (1.0 / feat)
        part = jnp.broadcast_to(0.5 * jnp.sum(e * e) * (1.0 / feat), l_ref.shape)

        @pl.when(pl.program_id(0) == 0)
        def _():
            l_ref[...] = part

        @pl.when(pl.program_id(0) > 0)
        def _():
            l_ref[...] += part

    return pl.pallas_call(
        kern, name="loss_head", grid=(rows // tile,),
        in_specs=[pl.BlockSpec((tile, feat), lambda i: (i, 0))] * 2,
        out_specs=[pl.BlockSpec((8, 128), lambda i: (0, 0)), pl.BlockSpec((tile, feat), lambda i: (i, 0))],
        out_shape=[jax.ShapeDtypeStruct((8, 128), f32), jax.ShapeDtypeStruct((rows, feat), f32)],
        compiler_params=_cparams(("arbitrary",)),
    )(y, target)


def _sum_parts(name, parts, out_dtype=f32):
    rows, cols = parts[0].shape
    tile = _pick(rows, 512)

    def kern(*refs):
        acc = refs[0][...].astype(f32)
        for r in refs[1:-1]:
            acc = acc + r[...].astype(f32)
        refs[-1][...] = acc.astype(out_dtype)

    return pl.pallas_call(
        kern, name=name, grid=(rows // tile,),
        in_specs=[pl.BlockSpec((tile, cols), lambda i: (i, 0))] * len(parts),
        out_specs=pl.BlockSpec((tile, cols), lambda i: (i, 0)),
        out_shape=jax.ShapeDtypeStruct((rows, cols), out_dtype),
        compiler_params=_cparams(("parallel",)),
    )(*parts)


ADAMW_BLOCK_BYTES = 1536 * 1024


def _adamw(name, w, m, v, g):
    rows, cols = w.shape
    tile = _pick(rows, 512)
    while tile > 8 and tile * cols * 4 > ADAMW_BLOCK_BYTES and rows % (tile // 2) == 0:
        tile //= 2
    c1, c2 = 1.0 / (1.0 - ADAM_B1 ** ADAM_STEP), 1.0 / (1.0 - ADAM_B2 ** ADAM_STEP)

    def kern(w_ref, m_ref, v_ref, g_ref, d_ref, nm_ref, nv_ref):
        gv = g_ref[...]
        nm = ADAM_B1 * m_ref[...] + (1.0 - ADAM_B1) * gv
        nv = ADAM_B2 * v_ref[...] + (1.0 - ADAM_B2) * (gv * gv)
        nm_ref[...] = nm
        nv_ref[...] = nv
        d_ref[...] = -ADAM_LR * ((nm * c1) / (jnp.sqrt(nv * c2) + ADAM_EPS) + ADAM_WD * w_ref[...])

    spec = pl.BlockSpec((tile, cols), lambda i: (i, 0))
    return pl.pallas_call(
        kern, name=name, grid=(rows // tile,), in_specs=[spec] * 4, out_specs=[spec] * 3,
        out_shape=[jax.ShapeDtypeStruct((rows, cols), f32)] * 3,
        compiler_params=_cparams(("parallel",)),
    )(w, m, v, g)


MESH = pl.DeviceIdType.MESH
HBM_SPEC = pl.BlockSpec(memory_space=pltpu.HBM)


def _other_chips(x, y):
    return [(1 - x, y), (x, 1 - y), (1 - x, 1 - y)]


def _gather_chips(big, small):
    rows, cols = big.shape
    half = rows // 2

    def kern(big_ref, small_ref, bout, sout, ici_send, ici_recv, d2d_send, d2d_recv, small_send, small_recv):
        x, y, c = lax.axis_index("x"), lax.axis_index("y"), lax.axis_index("c")
        me, sibling = 2 * x + y, (x, y, 1 - c)
        chips = _other_chips(x, y)
        slots = [2 * chip[0] + chip[1] for chip in chips]
        sends = [pltpu.make_async_remote_copy(
            src_ref=big_ref.at[c], dst_ref=bout.at[me, c], send_sem=ici_send.at[k], recv_sem=ici_recv.at[k],
            device_id=(*chip, c), device_id_type=MESH) for k, chip in enumerate(chips)]
        sends += [pltpu.make_async_remote_copy(
            src_ref=small_ref, dst_ref=sout.at[me], send_sem=small_send.at[k], recv_sem=small_recv.at[k],
            device_id=(*chip, c), device_id_type=MESH) for k, chip in enumerate(chips)]
        for cp in sends:
            cp.start()
        for k, chip in enumerate(chips):
            pltpu.make_async_remote_copy(
                src_ref=big_ref.at[c], dst_ref=bout.at[slots[k], c], send_sem=ici_send.at[k], recv_sem=ici_recv.at[k],
                device_id=(*chip, c), device_id_type=MESH).wait_recv()
            passed = pltpu.make_async_remote_copy(
                src_ref=bout.at[slots[k], c], dst_ref=bout.at[slots[k], c], send_sem=d2d_send.at[k],
                recv_sem=d2d_recv.at[k], device_id=sibling, device_id_type=MESH)
            passed.start()
            sends.append(passed)
        for k, chip in enumerate(chips):
            pltpu.make_async_remote_copy(
                src_ref=bout.at[slots[k], 1 - c], dst_ref=bout.at[slots[k], 1 - c], send_sem=d2d_send.at[k],
                recv_sem=d2d_recv.at[k], device_id=sibling, device_id_type=MESH).wait_recv()
            pltpu.make_async_remote_copy(
                src_ref=small_ref, dst_ref=sout.at[slots[k]], send_sem=small_send.at[k], recv_sem=small_recv.at[k],
                device_id=(*chip, c), device_id_type=MESH).wait_recv()
        for cp in sends:
            cp.wait_send()

    got_big, got_small = pl.pallas_call(
        kern, name="gather_chips", in_specs=[HBM_SPEC] * 2, out_specs=[HBM_SPEC] * 2,
        out_shape=[jax.ShapeDtypeStruct((N_CHIPS, 2, half, cols), big.dtype), jax.ShapeDtypeStruct((N_CHIPS,) + small.shape, small.dtype)],
        scratch_shapes=[pltpu.SemaphoreType.DMA((3,))] * 6,
    )(big.reshape(2, half, cols), small)
    me = 2 * lax.axis_index("x") + lax.axis_index("y")
    got_big = lax.dynamic_update_index_in_dim(got_big.reshape(N_CHIPS, rows, cols), big, me, 0)
    return got_big, lax.dynamic_update_index_in_dim(got_small, small, me, 0)


def _join_halves(part):
    other = _swap_sibling(part)
    first = lax.axis_index("c") == 0
    return jnp.concatenate([jnp.where(first, part, other), jnp.where(first, other, part)], axis=0)


def _scatter_chips(pieces):
    _, rows, cols = pieces.shape

    def kern(p_ref, out_ref, send_sems, recv_sems):
        x, y, c = lax.axis_index("x"), lax.axis_index("y"), lax.axis_index("c")
        chips = _other_chips(x, y)
        sends = [pltpu.make_async_remote_copy(
            src_ref=p_ref.at[2 * chip[0] + chip[1]], dst_ref=out_ref.at[k], send_sem=send_sems.at[k],
            recv_sem=recv_sems.at[k], device_id=(*chip, c), device_id_type=MESH) for k, chip in enumerate(chips)]
        for cp in sends:
            cp.start()
        for cp in sends:
            cp.wait()

    return pl.pallas_call(
        kern, name="scatter_chips", in_specs=[HBM_SPEC], out_specs=HBM_SPEC,
        out_shape=jax.ShapeDtypeStruct((3, rows, cols), pieces.dtype),
        scratch_shapes=[pltpu.SemaphoreType.DMA((3,)), pltpu.SemaphoreType.DMA((3,))],
    )(pieces)


def _swap_sibling(buf):
    def kern(b_ref, out_ref, send_sem, recv_sem):
        x, y, c = lax.axis_index("x"), lax.axis_index("y"), lax.axis_index("c")
        cp = pltpu.make_async_remote_copy(src_ref=b_ref, dst_ref=out_ref, send_sem=send_sem, recv_sem=recv_sem,
                                          device_id=(x, y, 1 - c), device_id_type=MESH)
        cp.start()
        cp.wait()

    return pl.pallas_call(
        kern, name="swap_sibling", in_specs=[HBM_SPEC], out_specs=HBM_SPEC,
        out_shape=jax.ShapeDtypeStruct(buf.shape, buf.dtype),
        scratch_shapes=[pltpu.SemaphoreType.DMA, pltpu.SemaphoreType.DMA],
    )(buf)


def _gather_all(buf):
    rows, cols = buf.shape

    def kern(b_ref, out_ref, send_sems, recv_sems):
        x, y, c = lax.axis_index("x"), lax.axis_index("y"), lax.axis_index("c")
        me = 4 * x + 2 * y + c
        peers = []
        for k in range(1, N_DEV):
            fx, fy, fc = (k >> 2) & 1, (k >> 1) & 1, k & 1
            peers.append((x ^ fx, y ^ fy, c ^ fc))
        sends = [pltpu.make_async_remote_copy(
            src_ref=b_ref, dst_ref=out_ref.at[me], send_sem=send_sems.at[k], recv_sem=recv_sems.at[k],
            device_id=peer, device_id_type=MESH) for k, peer in enumerate(peers)]
        for cp in sends:
            cp.start()
        for k, peer in enumerate(peers):
            pltpu.make_async_remote_copy(
                src_ref=b_ref, dst_ref=out_ref.at[4 * peer[0] + 2 * peer[1] + peer[2]], send_sem=send_sems.at[k],
                recv_sem=recv_sems.at[k], device_id=peer, device_id_type=MESH).wait_recv()
        for cp in sends:
            cp.wait_send()

    got = pl.pallas_call(
        kern, name="gather_all", in_specs=[HBM_SPEC], out_specs=HBM_SPEC,
        out_shape=jax.ShapeDtypeStruct((N_DEV, rows, cols), buf.dtype),
        scratch_shapes=[pltpu.SemaphoreType.DMA((N_DEV - 1,)), pltpu.SemaphoreType.DMA((N_DEV - 1,))],
    )(buf)
    me = 4 * lax.axis_index("x") + 2 * lax.axis_index("y") + lax.axis_index("c")
    return lax.dynamic_update_index_in_dim(got, buf, me, 0)


def _sum_slots(name, buf):
    n, rows, cols = buf.shape
    tile = _pick(rows, 256)

    def kern(b_ref, o_ref):
        acc = b_ref[0]
        for k in range(1, n):
            acc = acc + b_ref[k]
        o_ref[...] = acc

    return pl.pallas_call(
        kern, name=name, grid=(rows // tile,),
        in_specs=[pl.BlockSpec((n, tile, cols), lambda i: (0, i, 0))], out_specs=pl.BlockSpec((tile, cols), lambda i: (i, 0)),
        out_shape=jax.ShapeDtypeStruct((rows, cols), f32), compiler_params=_cparams(("parallel",)),
    )(buf)


def _pad_heads(t):
    r = t.shape[0]
    return jnp.pad(t.reshape(r, GLA_HEADS, GLA_DK), ((0, 0), (0, 0), (0, 128 - GLA_DK))).reshape(r, GLA_HEADS * 128)


def _unpad_heads(t):
    r = t.shape[0]
    return t.reshape(r, GLA_HEADS, 128)[:, :, :GLA_DK].reshape(r, GLA_HEADS * GLA_DK)


def _blockdiag(t):
    _, r, c = t.shape
    eye = jnp.eye(8, dtype=t.dtype).reshape(1, 8, 1, 8, 1)
    return (t.reshape(S5_SB, 8, r, 1, c) * eye).reshape(S5_SB, 8 * r, 8 * c)


def _blockdiag_extract(m, r, c):
    m5 = m.reshape(S5_SB, 8, r, 8, c)
    return jnp.stack([m5[:, g, :, g, :] for g in range(8)], axis=1).reshape(S5_GROUPS, r, c)


def _row(v, width=None):
    v = v[None]
    return v if width is None else jnp.pad(v, ((0, 0), (0, width - v.shape[1])))


def _layer_operands(p):
    w_in = p['w_in']

    def seg(n):
        return w_in[:, IN_ORIG[n][0]:IN_ORIG[n][1]]

    o = dict(p)
    o['wm'] = jnp.concatenate([seg('dn_qkv'), seg('dn_gate'), seg('cf'), seg('gates'), seg('s5'), _pad_heads(seg('gla_q')),
                               _pad_heads(seg('gla_k')), seg('gla_v'), seg('gla_g')], axis=1)
    o['ws'] = jnp.pad(jnp.concatenate([seg('dn_a'), seg('dn_b'), seg('gla_lr')], axis=1), ((0, 0), (0, WS_COLS - 24)))
    o['dn_conv8'] = jnp.pad(p['dn_conv'], ((0, 4), (0, 0)))
    o['a_log_r'] = _row(p['dn_a_log'], 128)
    o['dt_bias_r'] = _row(p['dn_dt_bias'], 128)
    o['dn_norm_r'] = _row(p['dn_norm'])
    o['cf_dw32'] = jnp.pad(p['cf_dw'], ((0, 1), (0, 0)))
    o['cf_bias_r'], o['cf_g_r'], o['cf_b_r'] = _row(p['cf_dw_bias']), _row(p['cf_ln_g']), _row(p['cf_ln_b'])
    o['w_alpha_p'] = jnp.pad(_pad_heads(p['gla_w_alpha']), ((8, 128 - 24), (0, 0)))
    o['b_alpha_r'] = _pad_heads(_row(p['gla_b_alpha']))
    o['gla_norm_r'] = _row(p['gla_norm'])
    o['ln1_g_r'], o['ln1_b_r'], o['ln2_g_r'], o['ln2_b_r'] = (_row(p[n]) for n in ('ln1_g', 'ln1_b', 'ln2_g', 'ln2_b'))
    o['ffn_conv8'] = jnp.pad(p['ffn_conv'], ((0, 5), (0, 0)))
    o['s5_in'] = (p['s5_a_re'], p['s5_a_im'], p['s5_log_dt'][:, None],
                  p['s5_b_re'].transpose(2, 0, 1), p['s5_b_im'].transpose(2, 0, 1))
    abar_re, abar_im, bbar_re, bbar_im = _s5_params(*o['s5_in'])
    o['abar'] = jnp.stack([abar_re, abar_im]).reshape(2, S5_SB, 1, 512)
    o['bmat_re'], o['bmat_im'] = _blockdiag(bbar_re.transpose(1, 0, 2)), _blockdiag(bbar_im.transpose(1, 0, 2))
    o['cmat_re'], o['cmat_im'] = _blockdiag(p['s5_c_re'].transpose(0, 2, 1)), _blockdiag(p['s5_c_im'].transpose(0, 2, 1))
    o['dvec'] = _row(p['s5_d'])
    return o


def _whole(a):
    return (a, a.shape[1], 0)


def _merge_ins(h, s):
    return [(h, 1024, 3), (h, 1024, 4), (h, 1024, 5), (h, 1024, 6), (s['y_a'], 1024, 0), (s['y_b'], 1024, 0),
            (s['zz'], 1024, 0), (s['zz'], 1024, 1), (s['y_d'], 1024, 0)]


def _layer_fwd(x, o):
    s = {}
    h = s['h'] = _mm(x, o['wm'], 'nn', 'mm_h')
    hs = s['hs'] = _mm(x, o['ws'], 'nn', 'mm_hs')
    s['c1'] = _conv_fwd('conv_dn', h, 1536, 0, o['dn_conv8'], 4)
    s['qkvn'], s['gb'], s['la'] = _rowwise('pre', _f_pre, [_whole(s['c1']), _whole(hs)],
                                           [o['a_log_r'], o['dt_bias_r'], o['w_alpha_p'], o['b_alpha_r']], [1536, 128, 512])
    s['o_dn'], s['st_dn'], s['inv_dn'] = _dn_fwd(s['qkvn'], s['gb'])
    (s['on_dn'],) = _rowwise('post_dn', _f_post, [_whole(s['o_dn']), (h, 512, 3)], [o['dn_norm_r']], [512], out_dtype=bf16)
    s['y_a'] = _mm(s['on_dn'], o['w_br_dn'], 'nn', 'mm_br')
    (s['cfp'],) = _rowwise('glu_cf', _f_glu, [(h, 512, 4), (h, 512, 5)], [], [512], out_dtype=bf16)
    s['cc'] = _conv_fwd('conv_cf', s['cfp'], 512, 0, o['cf_dw32'], 31)
    (s['cfo'],) = _rowwise('post_cf', _f_cfpost, [_whole(s['cc'])], [o['cf_bias_r'], o['cf_g_r'], o['cf_b_r']], [512], out_dtype=bf16)
    s['y_b'] = _mm(s['cfo'], o['w_br_cf'], 'nn', 'mm_br')
    s['ys5'], s['xr'], s['xi'] = _s5_fwd(h, o['abar'], o['bmat_re'], o['bmat_im'], o['cmat_re'], o['cmat_im'], o['dvec'])
    (s['z'],) = _rowwise('gelu', _f_gelu, [_whole(s['ys5'])], [], [512], out_dtype=bf16)
    s['zz'] = _mm(s['z'], o['w_br_s5'], 'nn', 'mm_br_s5')
    s['o_gla'], s['st_gla'] = _gla_fwd(h, s['la'])
    (s['on_gla'],) = _rowwise('post_gla', _f_post, [_whole(s['o_gla']), (h, 512, 18)], [o['gla_norm_r']], [512], out_dtype=bf16)
    s['y_d'] = _mm(s['on_gla'], o['w_br_gla'], 'nn', 'mm_br')
    (s['merged'],) = _rowwise('merge', _f_merge, _merge_ins(h, s), [], [1024], tile=128, out_dtype=bf16)
    s['mix'] = _mm(s['merged'], o['w_o'], 'nn', 'mm_o')
    (s['x1'],) = _rowwise('ln', _f_ln, [_whole(x), _whole(s['mix'])], [o['ln1_g_r'], o['ln1_b_r']], [1024])
    s['up'] = _mm(s['x1'], o['w_up'], 'nn', 'mm_up', out_dtype=bf16)
    s['u'] = _conv_fwd('conv_ffn', s['up'], 2 * D_FF, 0, o['ffn_conv8'], 3)
    (s['act'],) = _rowwise('act', _f_act, [(s['u'], D_FF, 0), (s['u'], D_FF, 1)], [], [D_FF], out_dtype=bf16)
    s['ffn'] = _mm(s['act'], o['w_down'], 'nn', 'mm_down')
    (x2,) = _rowwise('ln', _f_ln, [_whole(s['x1']), _whole(s['ffn'])], [o['ln2_g_r'], o['ln2_b_r']], [1024])
    return x2, s


def _layer_bwd(x, o, s, dparts):
    h, g = s['h'], {}
    (dx1_a, dffn), (g['ln2_g'], g['ln2_b']) = _rowwise_bwd(
        'ln_bwd', _f_ln, [_whole(s['x1']), _whole(s['ffn'])], [o['ln2_g_r'], o['ln2_b_r']], [[_whole(d) for d in dparts]], [f32, bf16])
    dact = _mm(dffn, o['w_down'], 'nt', 'mm_down_dx')
    g['w_down'] = _mm(s['act'], dffn, 'tn', 'mm_down_dw', out_dtype=bf16)
    (du_a, du_b), _ = _rowwise_bwd('act_bwd', _f_act, [(s['u'], D_FF, 0), (s['u'], D_FF, 1)], [], [[_whole(dact)]], [bf16, bf16])
    dup, dw = _conv_bwd('conv_ffn_bwd', s['up'], 2 * D_FF, 0, o['ffn_conv8'], 3, jnp.concatenate([du_a, du_b], axis=1), dx_dtype=bf16)
    g['ffn_conv'] = dw[:3]
    dx1_b = _mm(dup, o['w_up'], 'nt', 'mm_up_dx')
    g['w_up'] = _mm(s['x1'], dup, 'tn', 'mm_up_dw', out_dtype=bf16)
    (dx_a, dmix), (g['ln1_g'], g['ln1_b']) = _rowwise_bwd(
        'ln_bwd', _f_ln, [_whole(x), _whole(s['mix'])], [o['ln1_g_r'], o['ln1_b_r']], [[_whole(dx1_a), _whole(dx1_b)]], [f32, bf16])
    dmerged = _mm(dmix, o['w_o'], 'nt', 'mm_o_dx')
    g['w_o'] = _mm(s['merged'], dmix, 'tn', 'mm_o_dw', out_dtype=bf16)
    (dga, dgb_, dgc, dgd, dya, dyb, dzv, dzg, dyd), _ = _rowwise_bwd(
        'merge_bwd', _f_merge, _merge_ins(h, s), [], [[_whole(dmerged)]], [bf16] * 9, tile=128)
    dzz = jnp.concatenate([dzv, dzg], axis=1)
    don = _mm(dya, o['w_br_dn'], 'nt', 'mm_br_dx')
    g['w_br_dn'] = _mm(s['on_dn'], dya, 'tn', 'mm_br_dw', out_dtype=bf16)
    (do_dn, dgate_dn), (g['dn_norm'],) = _rowwise_bwd(
        'post_bwd', _f_post, [_whole(s['o_dn']), (h, 512, 3)], [o['dn_norm_r']], [[_whole(don)]], [f32, bf16])
    dqkvn, dgb = _dn_bwd(s['qkvn'], s['gb'], s['st_dn'], s['inv_dn'], do_dn)
    don = _mm(dyd, o['w_br_gla'], 'nt', 'mm_br_dx')
    g['w_br_gla'] = _mm(s['on_gla'], dyd, 'tn', 'mm_br_dw', out_dtype=bf16)
    (do_gla, dgate_gla), (g['gla_norm'],) = _rowwise_bwd(
        'post_bwd', _f_post, [_whole(s['o_gla']), (h, 512, 18)], [o['gla_norm_r']], [[_whole(don)]], [f32, bf16])
    dq_gla, dk_gla, dv_gla, dla = _gla_bwd(h, s['la'], s['st_gla'], do_gla)
    (dc1, dhs), (d_alog, d_dtb, d_walpha, d_balpha) = _rowwise_bwd(
        'pre_bwd', _f_pre, [_whole(s['c1']), _whole(s['hs'])], [o['a_log_r'], o['dt_bias_r'], o['w_alpha_p'], o['b_alpha_r']],
        [[_whole(dqkvn)], [_whole(dgb)], [_whole(dla)]], [bf16, bf16])
    g['dn_a_log'], g['dn_dt_bias'] = d_alog[0, :DN_HEADS], d_dtb[0, :DN_HEADS]
    g['gla_w_alpha'], g['gla_b_alpha'] = _unpad_heads(d_walpha[8:24]), _unpad_heads(d_balpha)[0]
    d_dnqkv, dw = _conv_bwd('conv_dn_bwd', h, 1536, 0, o['dn_conv8'], 4, dc1, dx_dtype=bf16)
    g['dn_conv'] = dw[:4]
    dcfo = _mm(dyb, o['w_br_cf'], 'nt', 'mm_br_dx')
    g['w_br_cf'] = _mm(s['cfo'], dyb, 'tn', 'mm_br_dw', out_dtype=bf16)
    (dcc,), (g['cf_dw_bias'], g['cf_ln_g'], g['cf_ln_b']) = _rowwise_bwd(
        'post_cf_bwd', _f_cfpost, [_whole(s['cc'])], [o['cf_bias_r'], o['cf_g_r'], o['cf_b_r']], [[_whole(dcfo)]], [bf16])
    dcfp, dw = _conv_bwd('conv_cf_bwd', s['cfp'], 512, 0, o['cf_dw32'], 31, dcc)
    g['cf_dw'] = dw[:31]
    (dcf_a, dcf_g), _ = _rowwise_bwd('glu_bwd', _f_glu, [(h, 512, 4), (h, 512, 5)], [], [[_whole(dcfp)]], [bf16, bf16])
    dz = _mm(dzz, o['w_br_s5'], 'nt', 'mm_br_s5_dx')
    g['w_br_s5'] = _mm(s['z'], dzz, 'tn', 'mm_br_s5_dw', out_dtype=bf16)
    (dys5,), _ = _rowwise_bwd('gelu_bwd', _f_gelu, [_whole(s['ys5'])], [], [[_whole(dz)]], [f32])
    du_s5, d_abar, dbm_re, dbm_im, dcm_re, dcm_im, d_dvec = _s5_bwd(
        h, o['abar'], o['bmat_re'], o['bmat_im'], o['cmat_re'], o['cmat_im'], o['dvec'], s['xr'], s['xi'], dys5)
    d_bbar = [_blockdiag_extract(m, S5_GROUP, S5_STATE).transpose(1, 0, 2) for m in (dbm_re, dbm_im)]
    da_re, da_im, dlog_dt, db_re, db_im = _s5_params_bwd(
        *o['s5_in'], d_abar[0].reshape(S5_GROUPS, S5_STATE), d_abar[1].reshape(S5_GROUPS, S5_STATE), *d_bbar)
    g['s5_a_re'], g['s5_a_im'], g['s5_log_dt'] = da_re, da_im, dlog_dt[:, 0]
    g['s5_b_re'], g['s5_b_im'] = db_re.transpose(1, 2, 0), db_im.transpose(1, 2, 0)
    g['s5_c_re'], g['s5_c_im'] = (_blockdiag_extract(m, S5_STATE, S5_GROUP).transpose(0, 2, 1) for m in (dcm_re, dcm_im))
    g['s5_d'] = d_dvec[0]
    for n in ('dn_norm', 'gla_norm', 'cf_dw_bias', 'cf_ln_g', 'cf_ln_b', 'ln1_g', 'ln1_b', 'ln2_g', 'ln2_b'):
        g[n] = g[n][0]
    dh = jnp.concatenate([d_dnqkv, dgate_dn, dcf_a, dcf_g, dga, dgb_, dgc, dgd, du_s5, dq_gla, dk_gla, dv_gla, dgate_gla], axis=1)
    dwm = _mm(x, dh, 'tn', 'mm_h_dw', out_dtype=bf16)
    dws = _mm(x, dhs, 'tn', 'mm_hs_dw', out_dtype=bf16)
    g['w_in'] = jnp.concatenate([
        dwm[:, 0:1536], dws[:, 0:8], dwm[:, 1536:2048], dwm[:, 2048:3072], dwm[:, 7168:7680], _unpad_heads(dwm[:, 7680:8192]),
        _unpad_heads(dwm[:, 8192:8704]), dwm[:, 8704:9216], dwm[:, 9216:9728], dws[:, 8:24], dwm[:, 3072:7168]], axis=1)
    return [dx_a, _mm(dh, o['wm'], 'nt', 'mm_h_dx'), _mm(dhs, o['ws'], 'nt', 'mm_hs_dx')], g


WEIGHTS = ('w_in', 'dn_conv', 'dn_a_log', 'dn_dt_bias', 'dn_norm', 'w_br_dn', 'cf_dw', 'cf_dw_bias', 'cf_ln_g', 'cf_ln_b',
           'w_br_cf', 's5_a_re', 's5_a_im', 's5_log_dt', 's5_b_re', 's5_b_im', 's5_c_re', 's5_c_im', 's5_d', 'w_br_s5',
           'gla_w_alpha', 'gla_b_alpha', 'gla_norm', 'w_br_gla', 'w_o', 'ln1_g', 'ln1_b', 'w_up', 'ffn_conv', 'w_down',
           'ln2_g', 'ln2_b')
LARGE = ('w_in', 'w_br_dn', 'w_br_cf', 'w_br_s5', 'w_br_gla', 'w_o', 'w_up', 'w_down')
SHARD_AXIS = dict(w_in=2, w_br_dn=2, w_br_cf=2, w_br_s5=2, w_br_gla=2, w_o=1, w_up=2, w_down=1,
                  dn_conv=2, cf_dw=2, gla_w_alpha=2, ffn_conv=2)
SMALL = tuple(n for n in WEIGHTS if n not in LARGE)
SMALL_SHARDED = tuple(n for n in SMALL if n in SHARD_AXIS)


def _local_step(x, target, full):
    ops, saved, xs = [], [], [x]
    for l in range(DEPTH):
        o = _layer_operands({n: full[n][l] for n in WEIGHTS})
        y, s = _layer_fwd(xs[-1], o)
        ops.append(o)
        saved.append(s)
        xs.append(y)
    loss, dy = _loss_head(xs[-1], target)
    dparts, grads = [dy], [None] * DEPTH
    for l in reversed(range(DEPTH)):
        dparts, grads[l] = _layer_bwd(xs[l], ops[l], saved[l], dparts)
    grad_x = _sum_parts('sum_dx', dparts)
    small = {n: jnp.stack([grads[l][n] for l in range(DEPTH)]) for n in SMALL}
    return loss[0, 0], grad_x, small, {n: [grads[l][n] for l in range(DEPTH)] for n in LARGE}


def _pack(arrs, rows, dtype=f32):
    flat = jnp.concatenate([a.reshape(-1).astype(dtype) for a in arrs])
    return jnp.pad(flat, (0, rows * 1024 - flat.shape[0])).reshape(rows, 1024)


def _unpack(buf, shapes):
    flat, out, pos = buf.reshape(-1), [], 0
    for shp in shapes:
        n = 1
        for d in shp:
            n *= d
        out.append(flat[pos:pos + n].reshape(shp))
        pos += n
    return out


def _rows_for(shapes, mult):
    n = 0
    for shp in shapes:
        k = 1
        for d in shp:
            k *= d
        n += k
    rows = -(-n // 1024)
    return -(-rows // mult) * mult


def _shard(a, axis, chip):
    size = a.shape[axis] // N_CHIPS
    return lax.dynamic_slice_in_dim(a, chip * size, size, axis)


def kernel(x, w_in, dn_conv, dn_a_log, dn_dt_bias, dn_norm, w_br_dn, cf_dw, cf_dw_bias, cf_ln_g, cf_ln_b, w_br_cf, s5_a_re, s5_a_im, s5_log_dt, s5_b_re, s5_b_im, s5_c_re, s5_c_im, s5_d, w_br_s5, gla_w_alpha, gla_b_alpha, gla_norm, w_br_gla, w_o, ln1_g, ln1_b, w_up, ffn_conv, w_down, ln2_g, ln2_b, loss_target, m_w_in, m_dn_conv, m_dn_a_log, m_dn_dt_bias, m_dn_norm, m_w_br_dn, m_cf_dw, m_cf_dw_bias, m_cf_ln_g, m_cf_ln_b, m_w_br_cf, m_s5_a_re, m_s5_a_im, m_s5_log_dt, m_s5_b_re, m_s5_b_im, m_s5_c_re, m_s5_c_im, m_s5_d, m_w_br_s5, m_gla_w_alpha, m_gla_b_alpha, m_gla_norm, m_w_br_gla, m_w_o, m_ln1_g, m_ln1_b, m_w_up, m_ffn_conv, m_w_down, m_ln2_g, m_ln2_b, v_w_in, v_dn_conv, v_dn_a_log, v_dn_dt_bias, v_dn_norm, v_w_br_dn, v_cf_dw, v_cf_dw_bias, v_cf_ln_g, v_cf_ln_b, v_w_br_cf, v_s5_a_re, v_s5_a_im, v_s5_log_dt, v_s5_b_re, v_s5_b_im, v_s5_c_re, v_s5_c_im, v_s5_d, v_w_br_s5, v_gla_w_alpha, v_gla_b_alpha, v_gla_norm, v_w_br_gla, v_w_o, v_ln1_g, v_ln1_b, v_w_up, v_ffn_conv, v_w_down, v_ln2_g, v_ln2_b):
    env = locals()
    w = {n: env[n] for n in WEIGHTS}
    m = {n: env['m_' + n] for n in WEIGHTS}
    v = {n: env['v_' + n] for n in WEIGHTS}
    chip = 2 * lax.axis_index("x") + lax.axis_index("y")

    large_shapes = [w[n].shape for n in LARGE]
    ssh_shapes = [w[n].shape for n in SMALL_SHARDED]
    large_rows, ssh_rows = _rows_for(large_shapes, 512), _rows_for(ssh_shapes, 8)
    got_large, got_ssh = _gather_chips(_pack([w[n] for n in LARGE], large_rows, bf16),
                                       _pack([w[n] for n in SMALL_SHARDED], ssh_rows))
    full = {n: w[n] for n in SMALL if n not in SHARD_AXIS}
    per_chip = [dict(zip(LARGE + SMALL_SHARDED, _unpack(got_large[k], large_shapes) + _unpack(got_ssh[k], ssh_shapes)))
                for k in range(N_CHIPS)]
    for n in LARGE + SMALL_SHARDED:
        full[n] = jnp.concatenate([per_chip[k][n] for k in range(N_CHIPS)], axis=SHARD_AXIS[n])

    loss, grad_x, g, g_large = _local_step(x[0], loss_target[0], full)
    loss = lax.psum(loss, ("x", "y", "c"))

    def piece(k):
        parts = []
        for n in LARGE:
            axis = SHARD_AXIS[n] - 1
            size = g_large[n][0].shape[axis] // N_CHIPS
            parts += [lax.slice_in_dim(layer, k * size, (k + 1) * size, axis=axis) for layer in g_large[n]]
        return _pack(parts, large_rows, bf16)

    half = large_rows // 2
    core = lax.axis_index("c")
    pieces = jnp.stack([piece(k) for k in range(N_CHIPS)])
    p_mine = lax.dynamic_slice_in_dim(pieces, core * half, half, axis=1).reshape(N_CHIPS * half, 1024)
    p_theirs = lax.dynamic_slice_in_dim(pieces, (1 - core) * half, half, axis=1).reshape(N_CHIPS * half, 1024)
    pair = _sum_parts('sum_pair', [p_mine, _swap_sibling(p_theirs)], bf16).reshape(N_CHIPS, half, 1024)
    from_chips = _scatter_chips(pair)
    own = lax.dynamic_index_in_dim(pair, chip, 0, keepdims=False)
    chip_sum = _join_halves(_sum_parts('sum_chips', [own, from_chips[0], from_chips[1], from_chips[2]]))
    res = {0: dict(zip(LARGE, _unpack(chip_sum, large_shapes))), 1: {}, 2: {}, 3: {}}
    for n in LARGE:
        two_d = (w[n].shape[0] * w[n].shape[1], w[n].shape[2])
        upd = _adamw('adamw_large', w[n].reshape(two_d), m[n].reshape(two_d), v[n].reshape(two_d), res[0][n].reshape(two_d))
        for kind in range(3):
            res[kind + 1][n] = upd[kind].reshape(w[n].shape)

    small_full_shapes = [g[n].shape for n in SMALL]
    small_rows = _rows_for(small_full_shapes, 8)
    small_sum = _sum_slots('sum_devices', _gather_all(_pack([g[n] for n in SMALL], small_rows)))
    gs = dict(zip(SMALL, _unpack(small_sum, small_full_shapes)))
    for n in SMALL_SHARDED:
        gs[n] = _shard(gs[n], SHARD_AXIS[n], chip)
    small_shapes = [w[n].shape for n in SMALL]
    upd_rows = _rows_for(small_shapes, 8)
    upd = _adamw('adamw_small', _pack([w[n] for n in SMALL], upd_rows), _pack([m[n] for n in SMALL], upd_rows),
                 _pack([v[n] for n in SMALL], upd_rows), _pack([gs[n] for n in SMALL], upd_rows))
    res[0].update(gs)
    for kind in range(3):
        res[kind + 1].update(zip(SMALL, _unpack(upd[kind], small_shapes)))
    return (loss, grad_x[None], *[res[kind][n] for kind in range(4) for n in WEIGHTS])
```

```python
import functools

import jax
import jax.numpy as jnp
from jax import lax
from jax.experimental import pallas as pl
from jax.experimental.pallas import tpu as pltpu

f32 = jnp.float32
bf16 = jnp.bfloat16
HI = lax.Precision.HIGHEST

D_MODEL = 1024
DEPTH = 4
DN_HEADS, DN_DK, DN_CHUNK = 4, 128, 64
DN_STEP_CHUNKS = 2
GLA_HEADS, GLA_DK, GLA_CHUNK, GLA_TAU = 4, 64, 16, 16.0
GLA_BLOCK = 128
S5_GROUPS, S5_GROUP, S5_STATE = 32, 16, 64
S5_SB = 4
S5_TILE = 256
D_FF = 2816
LN_EPS = 1e-5
ALPHA = (2.0 * DEPTH) ** 0.25
ADAM_LR, ADAM_B1, ADAM_B2, ADAM_EPS, ADAM_WD, ADAM_STEP = 0.001, 0.9, 0.999, 1e-08, 0.01, 10

VMEM_LIMIT_V7X = 56 * 1024 * 1024
MM_A_BLOCK_BYTES = 8 * 1024 * 1024
MM_B_BLOCK_BYTES = 6 * 1024 * 1024
MM_OUT_BLOCK_BYTES = 10 * 1024 * 1024
MM_TN_MAX = 2560
HALO = 32
N_CHIPS = 4
N_DEV = 8

IN_ORIG = dict(dn_qkv=(0, 1536), dn_a=(1536, 1540), dn_b=(1540, 1544), dn_gate=(1544, 2056), cf=(2056, 3080),
               s5=(3080, 3592), gla_q=(3592, 3848), gla_k=(3848, 4104), gla_v=(4104, 4616), gla_g=(4616, 5128),
               gla_lr=(5128, 5144), gates=(5144, 9240))
IN_COLS = 9240
WM_COLS = 9728
WS_COLS = 128


def _cparams(sem):
    return pltpu.CompilerParams(dimension_semantics=sem, vmem_limit_bytes=VMEM_LIMIT_V7X)


def _pick(dim, pref):
    for t in (pref, 512, 256, 128, 64, 32, 16, 8):
        if t <= pref and dim % t == 0:
            return t
    return dim


NN = (((1,), (0,)), ((), ()))
NT = (((1,), (1,)), ((), ()))
TN = (((0,), (0,)), ((), ()))


def _dot(a, b, dims=NN):
    return lax.dot_general(a, b, dims, precision=HI, preferred_element_type=f32)


def _round(a):
    return a.astype(bf16).astype(f32)


def _rdot(a, b, dims=NN):
    return lax.dot_general(a.astype(bf16), b.astype(bf16), dims, preferred_element_type=f32)


@functools.partial(jax.custom_vjp, nondiff_argnums=(2,))
def _bdot(a, b, dims=NN):
    return _rdot(a, b, dims)


def _bdot_fwd(a, b, dims):
    return _rdot(a, b, dims), (a, b)


def _bdot_bwd(dims, res, g):
    a, b = res
    if dims == NN:
        return _rdot(g, b, NT), _rdot(a, g, TN)
    if dims == NT:
        return _rdot(g, b, NN), _rdot(g, a, TN)
    assert dims == TN
    return _rdot(b, g, NT), _rdot(a, g, NN)


_bdot.defvjp(_bdot_fwd, _bdot_bwd)


def _iota(shape, axis):
    return lax.broadcasted_iota(jnp.int32, shape, axis)


def _mm(a, b, mode, name, out_dtype=f32):
    if mode == 'nn':
        (m, k), n = a.shape, b.shape[1]
    elif mode == 'nt':
        (m, k), n = a.shape, b.shape[0]
    else:
        (k, m), n = a.shape, b.shape[1]
    tm = _pick(m, 512 if mode == 'tn' else 1024)
    nk = 1
    while (k // nk) * tm * a.dtype.itemsize > MM_A_BLOCK_BYTES or k % nk or (k // nk) % 128:
        nk += 1
    tk = k // nk
    assert nk == 1 or out_dtype == f32
    tn = 128
    for cand in range(128, min(n, MM_TN_MAX) + 1, 128):
        if n % cand == 0 and tk * cand * b.dtype.itemsize <= MM_B_BLOCK_BYTES and tm * cand * 4 <= MM_OUT_BLOCK_BYTES:
            tn = cand
    if mode == 'nn':
        dims = NN
        a_spec = pl.BlockSpec((tm, tk), lambda i, j, kk: (i, kk))
        b_spec = pl.BlockSpec((tk, tn), lambda i, j, kk: (kk, j))
    elif mode == 'nt':
        dims = NT
        a_spec = pl.BlockSpec((tm, tk), lambda i, j, kk: (i, kk))
        b_spec = pl.BlockSpec((tn, tk), lambda i, j, kk: (j, kk))
    else:
        dims = TN
        a_spec = pl.BlockSpec((tk, tm), lambda i, j, kk: (kk, i))
        b_spec = pl.BlockSpec((tk, tn), lambda i, j, kk: (kk, j))

    def kern(a_ref, b_ref, o_ref):
        p = lax.dot_general(a_ref[...].astype(bf16), b_ref[...].astype(bf16), dims, preferred_element_type=f32)
        if nk == 1:
            o_ref[...] = p.astype(o_ref.dtype)
        else:
            kk = pl.program_id(2)

            @pl.when(kk == 0)
            def _():
                o_ref[...] = p

            @pl.when(kk > 0)
            def _():
                o_ref[...] += p

    return pl.pallas_call(
        kern, name=name, grid=(m // tm, n // tn, nk),
        in_specs=[a_spec, b_spec], out_specs=pl.BlockSpec((tm, tn), lambda i, j, kk: (i, j)),
        out_shape=jax.ShapeDtypeStruct((m, n), out_dtype),
        compiler_params=_cparams(("parallel", "parallel", "arbitrary")),
    )(a, b)


def _full_spec(p):
    nd = p.ndim
    return pl.BlockSpec(p.shape, lambda *_, nd=nd: (0,) * nd)


def _rowwise(name, f, ins, params, out_widths, tile=256, out_dtype=f32):
    rows = ins[0][0].shape[0]
    tile = _pick(rows, tile)
    n_x = len(ins) + len(params)

    def kern(*refs):
        for o_ref, r in zip(refs[n_x:], f(*[r[...] for r in refs[:n_x]])):
            o_ref[...] = r.astype(o_ref.dtype)

    in_specs = [pl.BlockSpec((tile, w), lambda i, c=c: (i, c)) for (_, w, c) in ins] + [_full_spec(p) for p in params]
    return pl.pallas_call(
        kern, name=name, grid=(rows // tile,), in_specs=in_specs,
        out_specs=[pl.BlockSpec((tile, w), lambda i: (i, 0)) for w in out_widths],
        out_shape=[jax.ShapeDtypeStruct((rows, w), out_dtype) for w in out_widths],
        compiler_params=_cparams(("parallel",)),
    )(*[a for (a, _, _) in ins], *params)


def _rowwise_bwd(name, f, ins, params, douts, want, tile=256):
    rows = ins[0][0].shape[0]
    tile = _pick(rows, tile)
    n_in, n_p = len(ins), len(params)
    parts = [p for d in douts for p in d]
    n_x, n_d = n_in + n_p, len(parts)

    def kern(*refs):
        xs = [r[...] for r in refs[:n_x]]
        d_refs, o_refs = refs[n_x:n_x + n_d], refs[n_x + n_d:]
        cts, pos = [], 0
        for d in douts:
            acc = d_refs[pos][...]
            for r in d_refs[pos + 1:pos + len(d)]:
                acc = acc + r[...]
            pos += len(d)
            cts.append(acc)
        grads = jax.vjp(f, *xs)[1](tuple(cts))
        k = 0
        for j in range(n_in):
            if want[j]:
                o_refs[k][...] = grads[j].astype(o_refs[k].dtype)
                k += 1
        first = pl.program_id(0) == 0
        for j in range(n_p):
            g, o_ref = grads[n_in + j], o_refs[k + j]

            @pl.when(first)
            def _(o_ref=o_ref, g=g):
                o_ref[...] = g

            @pl.when(jnp.logical_not(first))
            def _(o_ref=o_ref, g=g):
                o_ref[...] += g

    in_specs = ([pl.BlockSpec((tile, w), lambda i, c=c: (i, c)) for (_, w, c) in ins] + [_full_spec(p) for p in params]
                + [pl.BlockSpec((tile, w), lambda i, c=c: (i, c)) for (_, w, c) in parts])
    out_specs, out_shape = [], []
    for j in range(n_in):
        if want[j]:
            out_specs.append(pl.BlockSpec((tile, ins[j][1]), lambda i: (i, 0)))
            out_shape.append(jax.ShapeDtypeStruct((rows, ins[j][1]), want[j]))
    n_g = len(out_specs)
    for p in params:
        out_specs.append(_full_spec(p))
        out_shape.append(jax.ShapeDtypeStruct(p.shape, f32))
    res = pl.pallas_call(
        kern, name=name, grid=(rows // tile,), in_specs=in_specs, out_specs=out_specs, out_shape=out_shape,
        compiler_params=_cparams(("arbitrary",)),
    )(*[a for (a, _, _) in ins], *params, *[a for (a, _, _) in parts])
    return list(res[:n_g]), list(res[n_g:])


_sigmoid = jax.nn.sigmoid
_silu = jax.nn.silu
_softplus = jax.nn.softplus
_log_sigmoid = jax.nn.log_sigmoid


def _f_ln(x, r, g, b):
    t = ALPHA * x + r
    mu = jnp.mean(t, -1, keepdims=True)
    var = jnp.mean(jnp.square(t - mu), -1, keepdims=True)
    return ((t - mu) * lax.rsqrt(var + LN_EPS) * g + b,)


def _f_pre(c1, hs, a_log, dt_bias, w_alpha, b_alpha):
    s = _silu(c1)
    outs = []
    for j in range(3 * DN_HEADS):
        t = s[:, j * DN_DK:(j + 1) * DN_DK]
        if j < 2 * DN_HEADS:
            t = t * lax.rsqrt(jnp.sum(t * t, -1, keepdims=True) + 1e-6)
        if j < DN_HEADS:
            t = t * (DN_DK ** -0.5)
        outs.append(t)
    qkvn = jnp.concatenate(outs, axis=1)
    lane = _iota(hs.shape, 1)
    g = -jnp.exp(a_log) * _softplus(hs + dt_bias)
    beta = _sigmoid(hs)
    gb = jnp.where(lane < DN_HEADS, g, jnp.where(lane < 2 * DN_HEADS, beta, 0.0))
    la = _log_sigmoid(_bdot(hs, w_alpha) + b_alpha) * (1.0 / GLA_TAU)
    lane5 = _iota(la.shape, 1)
    la = jnp.where((lane5 % 128) < GLA_DK, la, 0.0)
    return qkvn, gb, la


def _f_post(o, gate, w):
    outs = []
    for j in range(4):
        t = o[:, j * 128:(j + 1) * 128]
        outs.append(t * lax.rsqrt(jnp.mean(t * t, -1, keepdims=True) + LN_EPS) * w)
    return (jnp.concatenate(outs, axis=1) * _silu(gate),)


def _f_glu(a, g):
    return (a * _sigmoid(g),)


def _f_cfpost(c, bias, g, b):
    t = c + bias
    mu = jnp.mean(t, -1, keepdims=True)
    var = jnp.mean(jnp.square(t - mu), -1, keepdims=True)
    return (_silu((t - mu) * lax.rsqrt(var + LN_EPS) * g + b),)


def _f_gelu(y):
    return (jax.nn.gelu(y),)


def _f_merge(ga, gb_, gc, gd, ya, yb, zv, zg, yd):
    return (_sigmoid(ga) * ya + _sigmoid(gb_) * yb + _sigmoid(gc) * (zv * _sigmoid(zg)) + _sigmoid(gd) * yd,)


def _f_act(a, b):
    return (_silu(a) * b,)


def _conv_tiles(rows, ch):
    return _pick(rows, 256), _pick(ch, 512)


def _conv_fwd(name, x, width, colblk0, w, taps):
    rows = x.shape[0]
    tr, cb = _conv_tiles(rows, width)
    hb = tr // HALO

    def kern(prev_ref, x_ref, w_ref, o_ref, ext):
        i = pl.program_id(1)
        ext[pl.ds(0, HALO), :] = jnp.where(i > 0, _round(prev_ref[...]), 0.0)
        ext[pl.ds(HALO, tr), :] = _round(x_ref[...])
        wv = _round(w_ref[...])
        acc = jnp.zeros((tr, cb), f32)
        for k in range(taps):
            acc = acc + wv[k:k + 1, :] * ext[pl.ds(HALO - taps + 1 + k, tr), :]
        o_ref[...] = acc

    c0 = colblk0 * (width // cb)
    return pl.pallas_call(
        kern, name=name, grid=(width // cb, rows // tr),
        in_specs=[pl.BlockSpec((HALO, cb), lambda c, i: (jnp.maximum(i * hb - 1, 0), c0 + c)),
                  pl.BlockSpec((tr, cb), lambda c, i: (i, c0 + c)),
                  pl.BlockSpec((w.shape[0], cb), lambda c, i: (0, c))],
        out_specs=pl.BlockSpec((tr, cb), lambda c, i: (i, c)),
        out_shape=jax.ShapeDtypeStruct((rows, width), f32),
        scratch_shapes=[pltpu.VMEM((HALO + tr, cb), f32)],
        compiler_params=_cparams(("parallel", "arbitrary")),
    )(x, x, w)


def _conv_bwd(name, x, width, colblk0, w, taps, dy, dx_dtype=f32):
    rows = x.shape[0]
    tr, cb = _conv_tiles(rows, width)
    hb = tr // HALO
    nt = rows // tr
    wr = w.shape[0]

    def kern(prev_ref, x_ref, w_ref, dy_ref, next_ref, dx_ref, dw_ref, ext, dext):
        i = pl.program_id(1)
        ext[pl.ds(0, HALO), :] = jnp.where(i > 0, _round(prev_ref[...]), 0.0)
        ext[pl.ds(HALO, tr), :] = _round(x_ref[...])
        dyv = _round(dy_ref[...])
        dext[pl.ds(0, tr), :] = dyv
        dext[pl.ds(tr, HALO), :] = jnp.where(i < nt - 1, _round(next_ref[...]), 0.0)
        wv = _round(w_ref[...])
        acc = jnp.zeros((tr, cb), f32)
        rows_w = []
        for k in range(taps):
            acc = acc + wv[k:k + 1, :] * dext[pl.ds(taps - 1 - k, tr), :]
            rows_w.append(jnp.sum(dyv * ext[pl.ds(HALO - taps + 1 + k, tr), :], axis=0, keepdims=True))
        dx_ref[...] = acc.astype(dx_ref.dtype)
        if wr > taps:
            rows_w.append(jnp.zeros((wr - taps, cb), f32))
        dwv = jnp.concatenate(rows_w, axis=0)

        @pl.when(i == 0)
        def _():
            dw_ref[...] = dwv

        @pl.when(i > 0)
        def _():
            dw_ref[...] += dwv

    c0 = colblk0 * (width // cb)
    return pl.pallas_call(
        kern, name=name, grid=(width // cb, nt),
        in_specs=[pl.BlockSpec((HALO, cb), lambda c, i: (jnp.maximum(i * hb - 1, 0), c0 + c)),
                  pl.BlockSpec((tr, cb), lambda c, i: (i, c0 + c)),
                  pl.BlockSpec((wr, cb), lambda c, i: (0, c)),
                  pl.BlockSpec((tr, cb), lambda c, i: (i, c)),
                  pl.BlockSpec((HALO, cb), lambda c, i: (jnp.minimum((i + 1) * hb, nt * hb - 1), c))],
        out_specs=[pl.BlockSpec((tr, cb), lambda c, i: (i, c)), pl.BlockSpec((wr, cb), lambda c, i: (0, c))],
        out_shape=[jax.ShapeDtypeStruct((rows, width), dx_dtype), jax.ShapeDtypeStruct((wr, width), f32)],
        scratch_shapes=[pltpu.VMEM((HALO + tr, cb), f32), pltpu.VMEM((HALO + tr, cb), f32)],
        compiler_params=_cparams(("parallel", "arbitrary")),
    )(x, x, w, dy, dy)


def _series_inverse(neg):
    n = neg.shape[0]
    inv = jnp.where(_iota((n, n), 0) == _iota((n, n), 1), 1.0, 0.0) + neg
    p = neg
    for _ in range(5):
        p = _dot(p, p)
        inv = inv + _dot(inv, p)
    return inv


def _inverse_bwd(inv, g):
    return _dot(_dot(inv, g, TN), inv, NT)


@jax.custom_vjp
def _unit_lower_inverse(neg):
    return _series_inverse(neg)


_unit_lower_inverse.defvjp(lambda neg: (_series_inverse(neg),) * 2, lambda inv, g: (_inverse_bwd(inv, g),))


@jax.custom_vjp
def _known_inverse(neg, inv):
    return inv


_known_inverse.defvjp(lambda neg, inv: (inv, inv), lambda inv, g: (_inverse_bwd(inv, g), jnp.zeros_like(inv)))


def _dn_head(state, q, k, v, gc, gl, beta, inv_saved=None):
    c = DN_CHUNK
    ii, jj = _iota((c, c), 0), _iota((c, c), 1)
    causal, strict = ii >= jj, ii > jj
    gcb = jnp.broadcast_to(gc, (c, c))
    decay = jnp.where(causal, jnp.exp(jnp.where(causal, gcb - gcb.T, 0.0)), 0.0)
    kb = k * beta
    neg = -jnp.where(strict, _bdot(kb, k, NT) * decay, 0.0)
    inv = _unit_lower_inverse(neg) if inv_saved is None else _known_inverse(neg, inv_saved)
    egc = jnp.exp(gc)
    u = _dot(inv, v * beta)
    w = _dot(inv, kb * egc)
    intra = _bdot(q, k, NT) * decay
    v_new = u - _bdot(w, state)
    o = _bdot(q * egc, state) + _bdot(intra, v_new)
    new_state = state * jnp.exp(gl) + _bdot(k * jnp.exp(gl - gc), v_new, TN)
    return (o, new_state, inv) if inv_saved is None else (o, new_state)


def _dn_cum(gb):
    c = DN_CHUNK
    tril = jnp.where(_iota((c, c), 0) >= _iota((c, c), 1), 1.0, 0.0)
    return _dot(tril, gb), jnp.sum(gb, axis=0, keepdims=True)


def _dn_block_head(state, q, k, v, gc, gls, beta, invs=None):
    c = DN_CHUNK
    outs, new_invs = [], []
    for n, gl in enumerate(gls):
        sl = slice(n * c, (n + 1) * c)
        if invs is None:
            o, state, inv = _dn_head(state, q[sl], k[sl], v[sl], gc[sl], gl, beta[sl])
            new_invs.append(inv)
        else:
            o, state = _dn_head(state, q[sl], k[sl], v[sl], gc[sl], gl, beta[sl], inv_saved=invs[n])
        outs.append(o)
    o = jnp.concatenate(outs, axis=0)
    return (o, state, new_invs) if invs is None else (o, state)


def _dn_block_cum(gbv, nbk):
    c = DN_CHUNK
    cums, tots = zip(*[_dn_cum(gbv[n * c:(n + 1) * c]) for n in range(nbk)])
    return jnp.concatenate(cums, axis=0), tots


def _dn_fwd(qkvn, gb):
    rows = qkvn.shape[0]
    c, h, d = DN_CHUNK, DN_HEADS, DN_DK
    nbk = DN_STEP_CHUNKS if rows % (c * DN_STEP_CHUNKS) == 0 else 1
    b, nb = c * nbk, rows // (c * nbk)

    def kern(qkv_ref, gb_ref, o_ref, st_ref, inv_ref, state):
        @pl.when(pl.program_id(0) == 0)
        def _():
            state[...] = jnp.zeros_like(state)

        gbv = gb_ref[...]
        cum, tots = _dn_block_cum(gbv, nbk)
        for j in range(h):
            st = state[j]
            st_ref[0, j] = st
            o, new, invs = _dn_block_head(st, qkv_ref[:, j * d:(j + 1) * d], qkv_ref[:, (h + j) * d:(h + j + 1) * d],
                                          qkv_ref[:, (2 * h + j) * d:(2 * h + j + 1) * d],
                                          cum[:, j:j + 1], [t[:, j:j + 1] for t in tots], gbv[:, h + j:h + j + 1])
            o_ref[:, j * d:(j + 1) * d] = o
            for n in range(nbk):
                inv_ref[0, n * h + j] = invs[n]
            state[j] = new

    return pl.pallas_call(
        kern, name="dn_fwd", grid=(nb,),
        in_specs=[pl.BlockSpec((b, 3 * h * d), lambda i: (i, 0)), pl.BlockSpec((b, 128), lambda i: (i, 0))],
        out_specs=[pl.BlockSpec((b, h * d), lambda i: (i, 0)), pl.BlockSpec((1, h, d, d), lambda i: (i, 0, 0, 0)),
                   pl.BlockSpec((1, nbk * h, c, c), lambda i: (i, 0, 0, 0))],
        out_shape=[jax.ShapeDtypeStruct((rows, h * d), f32), jax.ShapeDtypeStruct((nb, h, d, d), f32),
                   jax.ShapeDtypeStruct((nb, nbk * h, c, c), f32)],
        scratch_shapes=[pltpu.VMEM((h, d, d), f32)],
        compiler_params=_cparams(("arbitrary",)),
    )(qkvn, gb)


def _dn_bwd(qkvn, gb, states, invs, do):
    rows = qkvn.shape[0]
    c, h, d = DN_CHUNK, DN_HEADS, DN_DK
    nb = states.shape[0]
    nbk = rows // (c * nb)
    b = c * nbk

    def kern(qkv_ref, gb_ref, st_ref, inv_ref, do_ref, dqkv_ref, dgb_ref, dstate):
        @pl.when(pl.program_id(0) == 0)
        def _():
            dstate[...] = jnp.zeros_like(dstate)

        gbv = gb_ref[...]
        cum, tots = _dn_block_cum(gbv, nbk)
        lane = _iota((b, 128), 1)
        lane_c = _iota((c, 128), 1)
        dcum = jnp.zeros((b, 128), f32)
        dgb = jnp.zeros((b, 128), f32)
        for j in range(h):
            args = (st_ref[0, j], qkv_ref[:, j * d:(j + 1) * d], qkv_ref[:, (h + j) * d:(h + j + 1) * d],
                    qkv_ref[:, (2 * h + j) * d:(2 * h + j + 1) * d],
                    cum[:, j:j + 1], [t[:, j:j + 1] for t in tots], gbv[:, h + j:h + j + 1])
            head = functools.partial(_dn_block_head, invs=[inv_ref[0, n * h + j] for n in range(nbk)])
            ds, dq, dk, dv, dgc, dgls, dbeta = jax.vjp(head, *args)[1]((do_ref[:, j * d:(j + 1) * d], dstate[j]))
            dstate[j] = ds
            dqkv_ref[:, j * d:(j + 1) * d] = dq
            dqkv_ref[:, (h + j) * d:(h + j + 1) * d] = dk
            dqkv_ref[:, (2 * h + j) * d:(2 * h + j + 1) * d] = dv
            dcum = dcum + jnp.where(lane == j, dgc, 0.0)
            dgb = dgb + jnp.where(lane == h + j, dbeta, 0.0) + jnp.concatenate(
                [jnp.where(lane_c == j, dgl, 0.0) for dgl in dgls], axis=0)
        triu = jnp.where(_iota((c, c), 0) <= _iota((c, c), 1), 1.0, 0.0)
        dgb_ref[...] = dgb + jnp.concatenate([_dot(triu, dcum[n * c:(n + 1) * c]) for n in range(nbk)], axis=0)

    rev = lambda i: (nb - 1 - i, 0)
    rev4 = lambda i: (nb - 1 - i, 0, 0, 0)
    return pl.pallas_call(
        kern, name="dn_bwd", grid=(nb,),
        in_specs=[pl.BlockSpec((b, 3 * h * d), rev), pl.BlockSpec((b, 128), rev),
                  pl.BlockSpec((1, h, d, d), rev4), pl.BlockSpec((1, nbk * h, c, c), rev4), pl.BlockSpec((b, h * d), rev)],
        out_specs=[pl.BlockSpec((b, 3 * h * d), rev), pl.BlockSpec((b, 128), rev)],
        out_shape=[jax.ShapeDtypeStruct((rows, 3 * h * d), f32), jax.ShapeDtypeStruct((rows, 128), f32)],
        scratch_shapes=[pltpu.VMEM((h, d, d), f32)],
        compiler_params=_cparams(("arbitrary",)),
    )(qkvn, gb, states, invs, do)


def _gla_block(state_t, q, k, v, la):
    c, b = GLA_CHUNK, q.shape[0]
    ii, jj = _iota((b, b), 0), _iota((b, b), 1)
    causal = (ii >= jj) & (ii - jj < c) & (jnp.bitwise_and(ii, c - 1) >= jnp.bitwise_and(jj, c - 1))
    gc = _dot(jnp.where(causal, 1.0, 0.0), la)
    q_dec = q * (GLA_DK ** -0.5) * jnp.exp(gc)
    scores = jnp.where(causal, _bdot(q_dec, k * jnp.exp(-gc), NT), 0.0)
    intra = _bdot(scores, v)
    outs = []
    for n in range(b // c):
        sl = slice(n * c, (n + 1) * c)
        gl = gc[(n + 1) * c - 1:(n + 1) * c, :]
        outs.append(_bdot(q_dec[sl], state_t, NT))
        state_t = state_t * jnp.exp(gl) + _bdot(v[sl], k[sl] * jnp.exp(gl - gc[sl]), TN)
    return jnp.concatenate(outs, axis=0) + intra, state_t


GLA_Q0, GLA_K0, GLA_V0 = 7680 // 512, 8192 // 512, 8704 // 512


def _gla_fwd(hmain, la):
    rows = hmain.shape[0]
    b, h = _pick(rows, GLA_BLOCK), GLA_HEADS
    nb = rows // b

    def kern(q_ref, k_ref, v_ref, la_ref, o_ref, st_ref, state):
        @pl.when(pl.program_id(0) == 0)
        def _():
            state[...] = jnp.zeros_like(state)

        for j in range(h):
            sl = slice(j * 128, (j + 1) * 128)
            st = state[j]
            st_ref[0, j] = st
            o, new = _gla_block(st, q_ref[:, sl], k_ref[:, sl], v_ref[:, sl], la_ref[:, sl])
            o_ref[:, sl] = o
            state[j] = new

    return pl.pallas_call(
        kern, name="gla_fwd", grid=(nb,),
        in_specs=[pl.BlockSpec((b, 512), lambda i: (i, GLA_Q0)), pl.BlockSpec((b, 512), lambda i: (i, GLA_K0)),
                  pl.BlockSpec((b, 512), lambda i: (i, GLA_V0)), pl.BlockSpec((b, 512), lambda i: (i, 0))],
        out_specs=[pl.BlockSpec((b, 512), lambda i: (i, 0)), pl.BlockSpec((1, h, 128, 128), lambda i: (i, 0, 0, 0))],
        out_shape=[jax.ShapeDtypeStruct((rows, h * 128), f32), jax.ShapeDtypeStruct((nb, h, 128, 128), f32)],
        scratch_shapes=[pltpu.VMEM((h, 128, 128), f32)],
        compiler_params=_cparams(("arbitrary",)),
    )(hmain, hmain, hmain, la)


def _gla_bwd(hmain, la, states, do):
    rows = hmain.shape[0]
    b, h = _pick(rows, GLA_BLOCK), GLA_HEADS
    nb = rows // b

    def kern(q_ref, k_ref, v_ref, la_ref, st_ref, do_ref, dq_ref, dk_ref, dv_ref, dla_ref, dstate):
        @pl.when(pl.program_id(0) == 0)
        def _():
            dstate[...] = jnp.zeros_like(dstate)

        for j in range(h):
            sl = slice(j * 128, (j + 1) * 128)
            args = (st_ref[0, j], q_ref[:, sl], k_ref[:, sl], v_ref[:, sl], la_ref[:, sl])
            ds, dq, dk, dv, dla = jax.vjp(_gla_block, *args)[1]((do_ref[:, sl], dstate[j]))
            dstate[j] = ds
            dq_ref[:, sl] = dq.astype(bf16)
            dk_ref[:, sl] = dk.astype(bf16)
            dv_ref[:, sl] = dv.astype(bf16)
            dla_ref[:, sl] = dla

    rev = lambda i: (nb - 1 - i, 0)
    return pl.pallas_call(
        kern, name="gla_bwd", grid=(nb,),
        in_specs=[pl.BlockSpec((b, 512), lambda i: (nb - 1 - i, GLA_Q0)), pl.BlockSpec((b, 512), lambda i: (nb - 1 - i, GLA_K0)),
                  pl.BlockSpec((b, 512), lambda i: (nb - 1 - i, GLA_V0)), pl.BlockSpec((b, 512), rev),
                  pl.BlockSpec((1, h, 128, 128), lambda i: (nb - 1 - i, 0, 0, 0)), pl.BlockSpec((b, 512), rev)],
        out_specs=[pl.BlockSpec((b, 512), rev)] * 4,
        out_shape=[jax.ShapeDtypeStruct((rows, h * 128), bf16)] * 3 + [jax.ShapeDtypeStruct((rows, h * 128), f32)],
        scratch_shapes=[pltpu.VMEM((h, 128, 128), f32)],
        compiler_params=_cparams(("arbitrary",)),
    )(hmain, hmain, hmain, la, states, do)


def _f_s5_params(a_re, a_im, log_dt, b_re, b_im):
    dt = jnp.exp(log_dt)
    mag = jnp.exp(dt * a_re)
    abar_re, abar_im = mag * jnp.cos(dt * a_im), mag * jnp.sin(dt * a_im)
    den = a_re * a_re + a_im * a_im
    nr, ni = abar_re - 1.0, abar_im
    fr, fi = (nr * a_re + ni * a_im) / den, (ni * a_re - nr * a_im) / den
    return abar_re, abar_im, fr[None] * b_re - fi[None] * b_im, fr[None] * b_im + fi[None] * b_re


def _s5_params(a_re, a_im, log_dt, b_re, b_im):
    def kern(*refs):
        for o_ref, r in zip(refs[5:], _f_s5_params(*[r[...] for r in refs[:5]])):
            o_ref[...] = r

    ins = (a_re, a_im, log_dt, b_re, b_im)
    return pl.pallas_call(
        kern, name="s5_params", out_shape=[jax.ShapeDtypeStruct(a_re.shape, f32)] * 2 + [jax.ShapeDtypeStruct(b_re.shape, f32)] * 2,
    )(*ins)


def _s5_params_bwd(a_re, a_im, log_dt, b_re, b_im, d_ar, d_ai, d_br, d_bi):
    def kern(*refs):
        grads = jax.vjp(_f_s5_params, *[r[...] for r in refs[:5]])[1](tuple(r[...] for r in refs[5:9]))
        for o_ref, g in zip(refs[9:], grads):
            o_ref[...] = g

    ins = (a_re, a_im, log_dt, b_re, b_im)
    return pl.pallas_call(
        kern, name="s5_params_bwd", out_shape=[jax.ShapeDtypeStruct(t.shape, f32) for t in ins],
    )(*ins, d_ar, d_ai, d_br, d_bi)


def _cmul(ar, ai, br, bi):
    return ar * br - ai * bi, ar * bi + ai * br


def _s5_scan(xr, xi, ar, ai, reverse):
    t = xr.shape[0]
    row = _iota(xr.shape, 0)
    s = 1
    while s < t:
        if reverse:
            keep = row < t - s
            sr, si = pltpu.roll(xr, t - s, 0), pltpu.roll(xi, t - s, 0)
        else:
            keep = row >= s
            sr, si = pltpu.roll(xr, s, 0), pltpu.roll(xi, s, 0)
        sr, si = jnp.where(keep, sr, 0.0), jnp.where(keep, si, 0.0)
        pr, pi = _cmul(ar, ai, sr, si)
        xr, xi = xr + pr, xi + pi
        ar, ai = _cmul(ar, ai, ar, ai)
        s *= 2
    return xr, xi


def _s5_powers(ar, ai, t, reverse):
    row = _iota((t, ar.shape[1]), 0)
    at = (row == (t - 1 if reverse else 0))
    return _s5_scan(jnp.where(at, ar, 0.0), jnp.where(at, ai, 0.0), ar, ai, reverse)


S5_U0 = 7168 // 128


def _s5_fwd(hmain, abar, bmat_re, bmat_im, cmat_re, cmat_im, dvec):
    rows = hmain.shape[0]
    t = _pick(rows, S5_TILE)
    nt, ns = rows // t, 512

    def kern(u_ref, a_ref, br_ref, bi_ref, cr_ref, ci_ref, d_ref, y_ref, xr_ref, xi_ref, pw, carry):
        ar, ai = a_ref[0, 0], a_ref[1, 0]

        @pl.when(pl.program_id(1) == 0)
        def _():
            pr, pi = _s5_powers(ar, ai, t, False)
            pw[0], pw[1] = pr, pi
            carry[...] = jnp.zeros_like(carry)

        u = u_ref[...]
        xr, xi = _s5_scan(_rdot(u, br_ref[0]), _rdot(u, bi_ref[0]), ar, ai, False)
        cr, ci = carry[0:1, :], carry[1:2, :]
        qr, qi = _cmul(pw[0], pw[1], cr, ci)
        xr, xi = xr + qr, xi + qi
        xr_ref[...] = xr
        xi_ref[...] = xi
        carry[0:1, :] = xr[t - 1:t, :]
        carry[1:2, :] = xi[t - 1:t, :]
        y_ref[...] = _rdot(xr, cr_ref[0]) - _rdot(xi, ci_ref[0]) + d_ref[...] * u

    sb3 = lambda b, i: (b, 0, 0)
    return pl.pallas_call(
        kern, name="s5_fwd", grid=(S5_SB, nt),
        in_specs=[pl.BlockSpec((t, 128), lambda b, i: (i, S5_U0 + b)), pl.BlockSpec((2, 1, 1, ns), lambda b, i: (0, b, 0, 0)),
                  pl.BlockSpec((1, 128, ns), sb3), pl.BlockSpec((1, 128, ns), sb3),
                  pl.BlockSpec((1, ns, 128), sb3), pl.BlockSpec((1, ns, 128), sb3), pl.BlockSpec((1, 128), lambda b, i: (0, b))],
        out_specs=[pl.BlockSpec((t, 128), lambda b, i: (i, b)), pl.BlockSpec((t, ns), lambda b, i: (i, b)),
                   pl.BlockSpec((t, ns), lambda b, i: (i, b))],
        out_shape=[jax.ShapeDtypeStruct((rows, 512), f32), jax.ShapeDtypeStruct((rows, S5_SB * ns), f32),
                   jax.ShapeDtypeStruct((rows, S5_SB * ns), f32)],
        scratch_shapes=[pltpu.VMEM((2, t, ns), f32), pltpu.VMEM((8, ns), f32)],
        compiler_params=_cparams(("parallel", "arbitrary")),
    )(hmain, abar, bmat_re, bmat_im, cmat_re, cmat_im, dvec)


def _s5_bwd(hmain, abar, bmat_re, bmat_im, cmat_re, cmat_im, dvec, x_re, x_im, dy):
    rows = hmain.shape[0]
    t = _pick(rows, S5_TILE)
    nt, ns = rows // t, 512
    t8 = t // 8

    def kern(u_ref, a_ref, br_ref, bi_ref, cr_ref, ci_ref, d_ref, xr_ref, xi_ref, xpr_ref, xpi_ref, dy_ref,
             du_ref, da_ref, dbr_ref, dbi_ref, dcr_ref, dci_ref, dd_ref, pw, carry):
        i = pl.program_id(1)
        ar, ai = a_ref[0, 0], -a_ref[1, 0]

        @pl.when(i == 0)
        def _():
            pr, pi = _s5_powers(ar, ai, t, True)
            pw[0], pw[1] = pr, pi
            carry[...] = jnp.zeros_like(carry)

        u, gy = u_ref[...], dy_ref[...]
        lr, li = _s5_scan(_rdot(gy, cr_ref[0], NT), -_rdot(gy, ci_ref[0], NT), ar, ai, True)
        qr, qi = _cmul(pw[0], pw[1], carry[0:1, :], carry[1:2, :])
        lr, li = lr + qr, li + qi
        carry[0:1, :] = lr[0:1, :]
        carry[1:2, :] = li[0:1, :]
        du_ref[...] = (_rdot(lr, br_ref[0], NT) + _rdot(li, bi_ref[0], NT) + d_ref[...] * gy).astype(bf16)
        xr, xi = xr_ref[...], xi_ref[...]
        row = _iota(xr.shape, 0)
        first_r = jnp.where(i < nt - 1, xpr_ref[7:8, :], 0.0)
        first_i = jnp.where(i < nt - 1, xpi_ref[7:8, :], 0.0)
        xpr = jnp.where(row == 0, first_r, pltpu.roll(xr, 1, 0))
        xpi = jnp.where(row == 0, first_i, pltpu.roll(xi, 1, 0))
        da_r = jnp.sum(lr * xpr + li * xpi, axis=0, keepdims=True)
        da_i = jnp.sum(li * xpr - lr * xpi, axis=0, keepdims=True)
        upd = [(da_ref.at[0, 0], da_r), (da_ref.at[1, 0], da_i),
               (dbr_ref.at[0], _rdot(u, lr, TN)), (dbi_ref.at[0], _rdot(u, li, TN)),
               (dcr_ref.at[0], _rdot(xr, gy, TN)), (dci_ref.at[0], -_rdot(xi, gy, TN)),
               (dd_ref, jnp.sum(gy * u, axis=0, keepdims=True))]

        @pl.when(i == 0)
        def _():
            for ref, val in upd:
                ref[...] = val

        @pl.when(i > 0)
        def _():
            for ref, val in upd:
                ref[...] += val

    sb3 = lambda b, i: (b, 0, 0)
    rev = lambda b, i: (nt - 1 - i, b)
    prev8 = lambda b, i: (jnp.maximum((nt - 1 - i) * t8 - 1, 0), b)
    return pl.pallas_call(
        kern, name="s5_bwd", grid=(S5_SB, nt),
        in_specs=[pl.BlockSpec((t, 128), lambda b, i: (nt - 1 - i, S5_U0 + b)), pl.BlockSpec((2, 1, 1, ns), lambda b, i: (0, b, 0, 0)),
                  pl.BlockSpec((1, 128, ns), sb3), pl.BlockSpec((1, 128, ns), sb3),
                  pl.BlockSpec((1, ns, 128), sb3), pl.BlockSpec((1, ns, 128), sb3), pl.BlockSpec((1, 128), lambda b, i: (0, b)),
                  pl.BlockSpec((t, ns), rev), pl.BlockSpec((t, ns), rev), pl.BlockSpec((8, ns), prev8), pl.BlockSpec((8, ns), prev8),
                  pl.BlockSpec((t, 128), rev)],
        out_specs=[pl.BlockSpec((t, 128), rev), pl.BlockSpec((2, 1, 1, ns), lambda b, i: (0, b, 0, 0)),
                   pl.BlockSpec((1, 128, ns), sb3), pl.BlockSpec((1, 128, ns), sb3),
                   pl.BlockSpec((1, ns, 128), sb3), pl.BlockSpec((1, ns, 128), sb3), pl.BlockSpec((1, 128), lambda b, i: (0, b))],
        out_shape=[jax.ShapeDtypeStruct((rows, 512), bf16), jax.ShapeDtypeStruct((2, S5_SB, 1, ns), f32),
                   jax.ShapeDtypeStruct((S5_SB, 128, ns), f32), jax.ShapeDtypeStruct((S5_SB, 128, ns), f32),
                   jax.ShapeDtypeStruct((S5_SB, ns, 128), f32), jax.ShapeDtypeStruct((S5_SB, ns, 128), f32),
                   jax.ShapeDtypeStruct((1, 512), f32)],
        scratch_shapes=[pltpu.VMEM((2, t, ns), f32), pltpu.VMEM((8, ns), f32)],
        compiler_params=_cparams(("parallel", "arbitrary")),
    )(hmain, abar, bmat_re, bmat_im, cmat_re, cmat_im, dvec, x_re, x_im, x_re, x_im, dy)


def _loss_head(y, target):
    rows, feat = y.shape
    tile = _pick(rows, 256)

    def kern(y_ref, t_ref, l_ref, dy_ref):
        e = y_ref[...] - t_ref[...]
        dy_ref[...] = e * (1.0 / feat)
        part = jnp.broadcast_to(0.5 * jnp.sum(e * e) * (1.0 / feat), l_ref.shape)

        @pl.when(pl.program_id(0) == 0)
        def _():
            l_ref[...] = part

        @pl.when(pl.program_id(0) > 0)
        def _():
            l_ref[...] += part

    return pl.pallas_call(
        kern, name="loss_head", grid=(rows // tile,),
        in_specs=[pl.BlockSpec((tile, feat), lambda i: (i, 0))] * 2,
        out_specs=[pl.BlockSpec((8, 128), lambda i: (0, 0)), pl.BlockSpec((tile, feat), lambda i: (i, 0))],
        out_shape=[jax.ShapeDtypeStruct((8, 128), f32), jax.ShapeDtypeStruct((rows, feat), f32)],
        compiler_params=_cparams(("arbitrary",)),
    )(y, target)


def _sum_parts(name, parts, out_dtype=f32):
    rows, cols = parts[0].shape
    tile = _pick(rows, 512)

    def kern(*refs):
        acc = refs[0][...].astype(f32)
        for r in refs[1:-1]:
            acc = acc + r[...].astype(f32)
        refs[-1][...] = acc.astype(out_dtype)

    return pl.pallas_call(
        kern, name=name, grid=(rows // tile,),
        in_specs=[pl.BlockSpec((tile, cols), lambda i: (i, 0))] * len(parts),
        out_specs=pl.BlockSpec((tile, cols), lambda i: (i, 0)),
        out_shape=jax.ShapeDtypeStruct((rows, cols), out_dtype),
        compiler_params=_cparams(("parallel",)),
    )(*parts)


ADAMW_BLOCK_BYTES = 1536 * 1024


def _adamw(name, w, m, v, g):
    rows, cols = w.shape
    tile = _pick(rows, 512)
    while tile > 8 and tile * cols * 4 > ADAMW_BLOCK_BYTES and rows % (tile // 2) == 0:
        tile //= 2
    c1, c2 = 1.0 / (1.0 - ADAM_B1 ** ADAM_STEP), 1.0 / (1.0 - ADAM_B2 ** ADAM_STEP)

    def kern(w_ref, m_ref, v_ref, g_ref, d_ref, nm_ref, nv_ref):
        gv = g_ref[...]
        nm = ADAM_B1 * m_ref[...] + (1.0 - ADAM_B1) * gv
        nv = ADAM_B2 * v_ref[...] + (1.0 - ADAM_B2) * (gv * gv)
        nm_ref[...] = nm
        nv_ref[...] = nv
        d_ref[...] = -ADAM_LR * ((nm * c1) / (jnp.sqrt(nv * c2) + ADAM_EPS) + ADAM_WD * w_ref[...])

    spec = pl.BlockSpec((tile, cols), lambda i: (i, 0))
    return pl.pallas_call(
        kern, name=name, grid=(rows // tile,), in_specs=[spec] * 4, out_specs=[spec] * 3,
        out_shape=[jax.ShapeDtypeStruct((rows, cols), f32)] * 3,
        compiler_params=_cparams(("parallel",)),
    )(w, m, v, g)


MESH = pl.DeviceIdType.MESH
HBM_SPEC = pl.BlockSpec(memory_space=pltpu.HBM)


def _other_chips(x, y):
    return [(1 - x, y), (x, 1 - y), (1 - x, 1 - y)]


def _gather_chips(big, small):
    rows, cols = big.shape
    half = rows // 2

    def kern(big_ref, small_ref, bout, sout, ici_send, ici_recv, d2d_send, d2d_recv, small_send, small_recv):
        x, y, c = lax.axis_index("x"), lax.axis_index("y"), lax.axis_index("c")
        me, sibling = 2 * x + y, (x, y, 1 - c)
        chips = _other_chips(x, y)
        slots = [2 * chip[0] + chip[1] for chip in chips]
        sends = [pltpu.make_async_remote_copy(
            src_ref=big_ref.at[c], dst_ref=bout.at[me, c], send_sem=ici_send.at[k], recv_sem=ici_recv.at[k],
            device_id=(*chip, c), device_id_type=MESH) for k, chip in enumerate(chips)]
        sends += [pltpu.make_async_remote_copy(
            src_ref=small_ref, dst_ref=sout.at[me], send_sem=small_send.at[k], recv_sem=small_recv.at[k],
            device_id=(*chip, c), device_id_type=MESH) for k, chip in enumerate(chips)]
        for cp in sends:
            cp.start()
        for k, chip in enumerate(chips):
            pltpu.make_async_remote_copy(
                src_ref=big_ref.at[c], dst_ref=bout.at[slots[k], c], send_sem=ici_send.at[k], recv_sem=ici_recv.at[k],
                device_id=(*chip, c), device_id_type=MESH).wait_recv()
            passed = pltpu.make_async_remote_copy(
                src_ref=bout.at[slots[k], c], dst_ref=bout.at[slots[k], c], send_sem=d2d_send.at[k],
                recv_sem=d2d_recv.at[k], device_id=sibling, device_id_type=MESH)
            passed.start()
            sends.append(passed)
        for k, chip in enumerate(chips):
            pltpu.make_async_remote_copy(
                src_ref=bout.at[slots[k], 1 - c], dst_ref=bout.at[slots[k], 1 - c], send_sem=d2d_send.at[k],
                recv_sem=d2d_recv.at[k], device_id=sibling, device_id_type=MESH).wait_recv()
            pltpu.make_async_remote_copy(
                src_ref=small_ref, dst_ref=sout.at[slots[k]], send_sem=small_send.at[k], recv_sem=small_recv.at[k],
                device_id=(*chip, c), device_id_type=MESH).wait_recv()
        for cp in sends:
            cp.wait_send()

    got_big, got_small = pl.pallas_call(
        kern, name="gather_chips", in_specs=[HBM_SPEC] * 2, out_specs=[HBM_SPEC] * 2,
        out_shape=[jax.ShapeDtypeStruct((N_CHIPS, 2, half, cols), big.dtype), jax.ShapeDtypeStruct((N_CHIPS,) + small.shape, small.dtype)],
        scratch_shapes=[pltpu.SemaphoreType.DMA((3,))] * 6,
    )(big.reshape(2, half, cols), small)
    me = 2 * lax.axis_index("x") + lax.axis_index("y")
    got_big = lax.dynamic_update_index_in_dim(got_big.reshape(N_CHIPS, rows, cols), big, me, 0)
    return got_big, lax.dynamic_update_index_in_dim(got_small, small, me, 0)


def _join_halves(part):
    other = _swap_sibling(part)
    first = lax.axis_index("c") == 0
    return jnp.concatenate([jnp.where(first, part, other), jnp.where(first, other, part)], axis=0)


def _scatter_chips(pieces):
    _, rows, cols = pieces.shape

    def kern(p_ref, out_ref, send_sems, recv_sems):
        x, y, c = lax.axis_index("x"), lax.axis_index("y"), lax.axis_index("c")
        chips = _other_chips(x, y)
        sends = [pltpu.make_async_remote_copy(
            src_ref=p_ref.at[2 * chip[0] + chip[1]], dst_ref=out_ref.at[k], send_sem=send_sems.at[k],
            recv_sem=recv_sems.at[k], device_id=(*chip, c), device_id_type=MESH) for k, chip in enumerate(chips)]
        for cp in sends:
            cp.start()
        for cp in sends:
            cp.wait()

    return pl.pallas_call(
        kern, name="scatter_chips", in_specs=[HBM_SPEC], out_specs=HBM_SPEC,
        out_shape=jax.ShapeDtypeStruct((3, rows, cols), pieces.dtype),
        scratch_shapes=[pltpu.SemaphoreType.DMA((3,)), pltpu.SemaphoreType.DMA((3,))],
    )(pieces)


def _swap_sibling(buf):
    def kern(b_ref, out_ref, send_sem, recv_sem):
        x, y, c = lax.axis_index("x"), lax.axis_index("y"), lax.axis_index("c")
        cp = pltpu.make_async_remote_copy(src_ref=b_ref, dst_ref=out_ref, send_sem=send_sem, recv_sem=recv_sem,
                                          device_id=(x, y, 1 - c), device_id_type=MESH)
        cp.start()
        cp.wait()

    return pl.pallas_call(
        kern, name="swap_sibling", in_specs=[HBM_SPEC], out_specs=HBM_SPEC,
        out_shape=jax.ShapeDtypeStruct(buf.shape, buf.dtype),
        scratch_shapes=[pltpu.SemaphoreType.DMA, pltpu.SemaphoreType.DMA],
    )(buf)


def _gather_all(buf):
    rows, cols = buf.shape

    def kern(b_ref, out_ref, send_sems, recv_sems):
        x, y, c = lax.axis_index("x"), lax.axis_index("y"), lax.axis_index("c")
        me = 4 * x + 2 * y + c
        peers = []
        for k in range(1, N_DEV):
            fx, fy, fc = (k >> 2) & 1, (k >> 1) & 1, k & 1
            peers.append((x ^ fx, y ^ fy, c ^ fc))
        sends = [pltpu.make_async_remote_copy(
            src_ref=b_ref, dst_ref=out_ref.at[me], send_sem=send_sems.at[k], recv_sem=recv_sems.at[k],
            device_id=peer, device_id_type=MESH) for k, peer in enumerate(peers)]
        for cp in sends:
            cp.start()
        for k, peer in enumerate(peers):
            pltpu.make_async_remote_copy(
                src_ref=b_ref, dst_ref=out_ref.at[4 * peer[0] + 2 * peer[1] + peer[2]], send_sem=send_sems.at[k],
                recv_sem=recv_sems.at[k], device_id=peer, device_id_type=MESH).wait_recv()
        for cp in sends:
            cp.wait_send()

    got = pl.pallas_call(
        kern, name="gather_all", in_specs=[HBM_SPEC], out_specs=HBM_SPEC,
        out_shape=jax.ShapeDtypeStruct((N_DEV, rows, cols), buf.dtype),
        scratch_shapes=[pltpu.SemaphoreType.DMA((N_DEV - 1,)), pltpu.SemaphoreType.DMA((N_DEV - 1,))],
    )(buf)
    me = 4 * lax.axis_index("x") + 2 * lax.axis_index("y") + lax.axis_index("c")
    return lax.dynamic_update_index_in_dim(got, buf, me, 0)


def _sum_slots(name, buf):
    n, rows, cols = buf.shape
    tile = _pick(rows, 256)

    def kern(b_ref, o_ref):
        acc = b_ref[0]
        for k in range(1, n):
            acc = acc + b_ref[k]
        o_ref[...] = acc

    return pl.pallas_call(
        kern, name=name, grid=(rows // tile,),
        in_specs=[pl.BlockSpec((n, tile, cols), lambda i: (0, i, 0))], out_specs=pl.BlockSpec((tile, cols), lambda i: (i, 0)),
        out_shape=jax.ShapeDtypeStruct((rows, cols), f32), compiler_params=_cparams(("parallel",)),
    )(buf)


def _pad_heads(t):
    r = t.shape[0]
    return jnp.pad(t.reshape(r, GLA_HEADS, GLA_DK), ((0, 0), (0, 0), (0, 128 - GLA_DK))).reshape(r, GLA_HEADS * 128)


def _unpad_heads(t):
    r = t.shape[0]
    return t.reshape(r, GLA_HEADS, 128)[:, :, :GLA_DK].reshape(r, GLA_HEADS * GLA_DK)


def _blockdiag(t):
    _, r, c = t.shape
    eye = jnp.eye(8, dtype=t.dtype).reshape(1, 8, 1, 8, 1)
    return (t.reshape(S5_SB, 8, r, 1, c) * eye).reshape(S5_SB, 8 * r, 8 * c)


def _blockdiag_extract(m, r, c):
    m5 = m.reshape(S5_SB, 8, r, 8, c)
    return jnp.stack([m5[:, g, :, g, :] for g in range(8)], axis=1).reshape(S5_GROUPS, r, c)


def _row(v, width=None):
    v = v[None]
    return v if width is None else jnp.pad(v, ((0, 0), (0, width - v.shape[1])))


def _layer_operands(p):
    w_in = p['w_in']

    def seg(n):
        return w_in[:, IN_ORIG[n][0]:IN_ORIG[n][1]]

    o = dict(p)
    o['wm'] = jnp.concatenate([seg('dn_qkv'), seg('dn_gate'), seg('cf'), seg('gates'), seg('s5'), _pad_heads(seg('gla_q')),
                               _pad_heads(seg('gla_k')), seg('gla_v'), seg('gla_g')], axis=1)
    o['ws'] = jnp.pad(jnp.concatenate([seg('dn_a'), seg('dn_b'), seg('gla_lr')], axis=1), ((0, 0), (0, WS_COLS - 24)))
    o['dn_conv8'] = jnp.pad(p['dn_conv'], ((0, 4), (0, 0)))
    o['a_log_r'] = _row(p['dn_a_log'], 128)
    o['dt_bias_r'] = _row(p['dn_dt_bias'], 128)
    o['dn_norm_r'] = _row(p['dn_norm'])
    o['cf_dw32'] = jnp.pad(p['cf_dw'], ((0, 1), (0, 0)))
    o['cf_bias_r'], o['cf_g_r'], o['cf_b_r'] = _row(p['cf_dw_bias']), _row(p['cf_ln_g']), _row(p['cf_ln_b'])
    o['w_alpha_p'] = jnp.pad(_pad_heads(p['gla_w_alpha']), ((8, 128 - 24), (0, 0)))
    o['b_alpha_r'] = _pad_heads(_row(p['gla_b_alpha']))
    o['gla_norm_r'] = _row(p['gla_norm'])
    o['ln1_g_r'], o['ln1_b_r'], o['ln2_g_r'], o['ln2_b_r'] = (_row(p[n]) for n in ('ln1_g', 'ln1_b', 'ln2_g', 'ln2_b'))
    o['ffn_conv8'] = jnp.pad(p['ffn_conv'], ((0, 5), (0, 0)))
    o['s5_in'] = (p['s5_a_re'], p['s5_a_im'], p['s5_log_dt'][:, None],
                  p['s5_b_re'].transpose(2, 0, 1), p['s5_b_im'].transpose(2, 0, 1))
    abar_re, abar_im, bbar_re, bbar_im = _s5_params(*o['s5_in'])
    o['abar'] = jnp.stack([abar_re, abar_im]).reshape(2, S5_SB, 1, 512)
    o['bmat_re'], o['bmat_im'] = _blockdiag(bbar_re.transpose(1, 0, 2)), _blockdiag(bbar_im.transpose(1, 0, 2))
    o['cmat_re'], o['cmat_im'] = _blockdiag(p['s5_c_re'].transpose(0, 2, 1)), _blockdiag(p['s5_c_im'].transpose(0, 2, 1))
    o['dvec'] = _row(p['s5_d'])
    return o


def _whole(a):
    return (a, a.shape[1], 0)


def _merge_ins(h, s):
    return [(h, 1024, 3), (h, 1024, 4), (h, 1024, 5), (h, 1024, 6), (s['y_a'], 1024, 0), (s['y_b'], 1024, 0),
            (s['zz'], 1024, 0), (s['zz'], 1024, 1), (s['y_d'], 1024, 0)]


def _layer_fwd(x, o):
    s = {}
    h = s['h'] = _mm(x, o['wm'], 'nn', 'mm_h')
    hs = s['hs'] = _mm(x, o['ws'], 'nn', 'mm_hs')
    s['c1'] = _conv_fwd('conv_dn', h, 1536, 0, o['dn_conv8'], 4)
    s['qkvn'], s['gb'], s['la'] = _rowwise('pre', _f_pre, [_whole(s['c1']), _whole(hs)],
                                           [o['a_log_r'], o['dt_bias_r'], o['w_alpha_p'], o['b_alpha_r']], [1536, 128, 512])
    s['o_dn'], s['st_dn'], s['inv_dn'] = _dn_fwd(s['qkvn'], s['gb'])
    (s['on_dn'],) = _rowwise('post_dn', _f_post, [_whole(s['o_dn']), (h, 512, 3)], [o['dn_norm_r']], [512], out_dtype=bf16)
    s['y_a'] = _mm(s['on_dn'], o['w_br_dn'], 'nn', 'mm_br')
    (s['cfp'],) = _rowwise('glu_cf', _f_glu, [(h, 512, 4), (h, 512, 5)], [], [512], out_dtype=bf16)
    s['cc'] = _conv_fwd('conv_cf', s['cfp'], 512, 0, o['cf_dw32'], 31)
    (s['cfo'],) = _rowwise('post_cf', _f_cfpost, [_whole(s['cc'])], [o['cf_bias_r'], o['cf_g_r'], o['cf_b_r']], [512], out_dtype=bf16)
    s['y_b'] = _mm(s['cfo'], o['w_br_cf'], 'nn', 'mm_br')
    s['ys5'], s['xr'], s['xi'] = _s5_fwd(h, o['abar'], o['bmat_re'], o['bmat_im'], o['cmat_re'], o['cmat_im'], o['dvec'])
    (s['z'],) = _rowwise('gelu', _f_gelu, [_whole(s['ys5'])], [], [512], out_dtype=bf16)
    s['zz'] = _mm(s['z'], o['w_br_s5'], 'nn', 'mm_br_s5')
    s['o_gla'], s['st_gla'] = _gla_fwd(h, s['la'])
    (s['on_gla'],) = _rowwise('post_gla', _f_post, [_whole(s['o_gla']), (h, 512, 18)], [o['gla_norm_r']], [512], out_dtype=bf16)
    s['y_d'] = _mm(s['on_gla'], o['w_br_gla'], 'nn', 'mm_br')
    (s['merged'],) = _rowwise('merge', _f_merge, _merge_ins(h, s), [], [1024], tile=128, out_dtype=bf16)
    s['mix'] = _mm(s['merged'], o['w_o'], 'nn', 'mm_o')
    (s['x1'],) = _rowwise('ln', _f_ln, [_whole(x), _whole(s['mix'])], [o['ln1_g_r'], o['ln1_b_r']], [1024])
    s['up'] = _mm(s['x1'], o['w_up'], 'nn', 'mm_up', out_dtype=bf16)
    s['u'] = _conv_fwd('conv_ffn', s['up'], 2 * D_FF, 0, o['ffn_conv8'], 3)
    (s['act'],) = _rowwise('act', _f_act, [(s['u'], D_FF, 0), (s['u'], D_FF, 1)], [], [D_FF], out_dtype=bf16)
    s['ffn'] = _mm(s['act'], o['w_down'], 'nn', 'mm_down')
    (x2,) = _rowwise('ln', _f_ln, [_whole(s['x1']), _whole(s['ffn'])], [o['ln2_g_r'], o['ln2_b_r']], [1024])
    return x2, s


def _layer_bwd(x, o, s, dparts):
    h, g = s['h'], {}
    (dx1_a, dffn), (g['ln2_g'], g['ln2_b']) = _rowwise_bwd(
        'ln_bwd', _f_ln, [_whole(s['x1']), _whole(s['ffn'])], [o['ln2_g_r'], o['ln2_b_r']], [[_whole(d) for d in dparts]], [f32, bf16])
    dact = _mm(dffn, o['w_down'], 'nt', 'mm_down_dx')
    g['w_down'] = _mm(s['act'], dffn, 'tn', 'mm_down_dw', out_dtype=bf16)
    (du_a, du_b), _ = _rowwise_bwd('act_bwd', _f_act, [(s['u'], D_FF, 0), (s['u'], D_FF, 1)], [], [[_whole(dact)]], [bf16, bf16])
    dup, dw = _conv_bwd('conv_ffn_bwd', s['up'], 2 * D_FF, 0, o['ffn_conv8'], 3, jnp.concatenate([du_a, du_b], axis=1), dx_dtype=bf16)
    g['ffn_conv'] = dw[:3]
    dx1_b = _mm(dup, o['w_up'], 'nt', 'mm_up_dx')
    g['w_up'] = _mm(s['x1'], dup, 'tn', 'mm_up_dw', out_dtype=bf16)
    (dx_a, dmix), (g['ln1_g'], g['ln1_b']) = _rowwise_bwd(
        'ln_bwd', _f_ln, [_whole(x), _whole(s['mix'])], [o['ln1_g_r'], o['ln1_b_r']], [[_whole(dx1_a), _whole(dx1_b)]], [f32, bf16])
    dmerged = _mm(dmix, o['w_o'], 'nt', 'mm_o_dx')
    g['w_o'] = _mm(s['merged'], dmix, 'tn', 'mm_o_dw', out_dtype=bf16)
    (dga, dgb_, dgc, dgd, dya, dyb, dzv, dzg, dyd), _ = _rowwise_bwd(
        'merge_bwd', _f_merge, _merge_ins(h, s), [], [[_whole(dmerged)]], [bf16] * 9, tile=128)
    dzz = jnp.concatenate([dzv, dzg], axis=1)
    don = _mm(dya, o['w_br_dn'], 'nt', 'mm_br_dx')
    g['w_br_dn'] = _mm(s['on_dn'], dya, 'tn', 'mm_br_dw', out_dtype=bf16)
    (do_dn, dgate_dn), (g['dn_norm'],) = _rowwise_bwd(
        'post_bwd', _f_post, [_whole(s['o_dn']), (h, 512, 3)], [o['dn_norm_r']], [[_whole(don)]], [f32, bf16])
    dqkvn, dgb = _dn_bwd(s['qkvn'], s['gb'], s['st_dn'], s['inv_dn'], do_dn)
    don = _mm(dyd, o['w_br_gla'], 'nt', 'mm_br_dx')
    g['w_br_gla'] = _mm(s['on_gla'], dyd, 'tn', 'mm_br_dw', out_dtype=bf16)
    (do_gla, dgate_gla), (g['gla_norm'],) = _rowwise_bwd(
        'post_bwd', _f_post, [_whole(s['o_gla']), (h, 512, 18)], [o['gla_norm_r']], [[_whole(don)]], [f32, bf16])
    dq_gla, dk_gla, dv_gla, dla = _gla_bwd(h, s['la'], s['st_gla'], do_gla)
    (dc1, dhs), (d_alog, d_dtb, d_walpha, d_balpha) = _rowwise_bwd(
        'pre_bwd', _f_pre, [_whole(s['c1']), _whole(s['hs'])], [o['a_log_r'], o['dt_bias_r'], o['w_alpha_p'], o['b_alpha_r']],
        [[_whole(dqkvn)], [_whole(dgb)], [_whole(dla)]], [bf16, bf16])
    g['dn_a_log'], g['dn_dt_bias'] = d_alog[0, :DN_HEADS], d_dtb[0, :DN_HEADS]
    g['gla_w_alpha'], g['gla_b_alpha'] = _unpad_heads(d_walpha[8:24]), _unpad_heads(d_balpha)[0]
    d_dnqkv, dw = _conv_bwd('conv_dn_bwd', h, 1536, 0, o['dn_conv8'], 4, dc1, dx_dtype=bf16)
    g['dn_conv'] = dw[:4]
    dcfo = _mm(dyb, o['w_br_cf'], 'nt', 'mm_br_dx')
    g['w_br_cf'] = _mm(s['cfo'], dyb, 'tn', 'mm_br_dw', out_dtype=bf16)
    (dcc,), (g['cf_dw_bias'], g['cf_ln_g'], g['cf_ln_b']) = _rowwise_bwd(
        'post_cf_bwd', _f_cfpost, [_whole(s['cc'])], [o['cf_bias_r'], o['cf_g_r'], o['cf_b_r']], [[_whole(dcfo)]], [bf16])
    dcfp, dw = _conv_bwd('conv_cf_bwd', s['cfp'], 512, 0, o['cf_dw32'], 31, dcc)
    g['cf_dw'] = dw[:31]
    (dcf_a, dcf_g), _ = _rowwise_bwd('glu_bwd', _f_glu, [(h, 512, 4), (h, 512, 5)], [], [[_whole(dcfp)]], [bf16, bf16])
    dz = _mm(dzz, o['w_br_s5'], 'nt', 'mm_br_s5_dx')
    g['w_br_s5'] = _mm(s['z'], dzz, 'tn', 'mm_br_s5_dw', out_dtype=bf16)
    (dys5,), _ = _rowwise_bwd('gelu_bwd', _f_gelu, [_whole(s['ys5'])], [], [[_whole(dz)]], [f32])
    du_s5, d_abar, dbm_re, dbm_im, dcm_re, dcm_im, d_dvec = _s5_bwd(
        h, o['abar'], o['bmat_re'], o['bmat_im'], o['cmat_re'], o['cmat_im'], o['dvec'], s['xr'], s['xi'], dys5)
    d_bbar = [_blockdiag_extract(m, S5_GROUP, S5_STATE).transpose(1, 0, 2) for m in (dbm_re, dbm_im)]
    da_re, da_im, dlog_dt, db_re, db_im = _s5_params_bwd(
        *o['s5_in'], d_abar[0].reshape(S5_GROUPS, S5_STATE), d_abar[1].reshape(S5_GROUPS, S5_STATE), *d_bbar)
    g['s5_a_re'], g['s5_a_im'], g['s5_log_dt'] = da_re, da_im, dlog_dt[:, 0]
    g['s5_b_re'], g['s5_b_im'] = db_re.transpose(1, 2, 0), db_im.transpose(1, 2, 0)
    g['s5_c_re'], g['s5_c_im'] = (_blockdiag_extract(m, S5_STATE, S5_GROUP).transpose(0, 2, 1) for m in (dcm_re, dcm_im))
    g['s5_d'] = d_dvec[0]
    for n in ('dn_norm', 'gla_norm', 'cf_dw_bias', 'cf_ln_g', 'cf_ln_b', 'ln1_g', 'ln1_b', 'ln2_g', 'ln2_b'):
        g[n] = g[n][0]
    dh = jnp.concatenate([d_dnqkv, dgate_dn, dcf_a, dcf_g, dga, dgb_, dgc, dgd, du_s5, dq_gla, dk_gla, dv_gla, dgate_gla], axis=1)
    dwm = _mm(x, dh, 'tn', 'mm_h_dw', out_dtype=bf16)
    dws = _mm(x, dhs, 'tn', 'mm_hs_dw', out_dtype=bf16)
    g['w_in'] = jnp.concatenate([
        dwm[:, 0:1536], dws[:, 0:8], dwm[:, 1536:2048], dwm[:, 2048:3072], dwm[:, 7168:7680], _unpad_heads(dwm[:, 7680:8192]),
        _unpad_heads(dwm[:, 8192:8704]), dwm[:, 8704:9216], dwm[:, 9216:9728], dws[:, 8:24], dwm[:, 3072:7168]], axis=1)
    return [dx_a, _mm(dh, o['wm'], 'nt', 'mm_h_dx'), _mm(dhs, o['ws'], 'nt', 'mm_hs_dx')], g


WEIGHTS = ('w_in', 'dn_conv', 'dn_a_log', 'dn_dt_bias', 'dn_norm', 'w_br_dn', 'cf_dw', 'cf_dw_bias', 'cf_ln_g', 'cf_ln_b',
           'w_br_cf', 's5_a_re', 's5_a_im', 's5_log_dt', 's5_b_re', 's5_b_im', 's5_c_re', 's5_c_im', 's5_d', 'w_br_s5',
           'gla_w_alpha', 'gla_b_alpha', 'gla_norm', 'w_br_gla', 'w_o', 'ln1_g', 'ln1_b', 'w_up', 'ffn_conv', 'w_down',
           'ln2_g', 'ln2_b')
LARGE = ('w_in', 'w_br_dn', 'w_br_cf', 'w_br_s5', 'w_br_gla', 'w_o', 'w_up', 'w_down')
SHARD_AXIS = dict(w_in=2, w_br_dn=2, w_br_cf=2, w_br_s5=2, w_br_gla=2, w_o=1, w_up=2, w_down=1,
                  dn_conv=2, cf_dw=2, gla_w_alpha=2, ffn_conv=2)
SMALL = tuple(n for n in WEIGHTS if n not in LARGE)
SMALL_SHARDED = tuple(n for n in SMALL if n in SHARD_AXIS)


def _local_step(x, target, full):
    ops, saved, xs = [], [], [x]
    for l in range(DEPTH):
        o = _layer_operands({n: full[n][l] for n in WEIGHTS})
        y, s = _layer_fwd(xs[-1], o)
        ops.append(o)
        saved.append(s)
        xs.append(y)
    loss, dy = _loss_head(xs[-1], target)
    dparts, grads = [dy], [None] * DEPTH
    for l in reversed(range(DEPTH)):
        dparts, grads[l] = _layer_bwd(xs[l], ops[l], saved[l], dparts)
    grad_x = _sum_parts('sum_dx', dparts)
    small = {n: jnp.stack([grads[l][n] for l in range(DEPTH)]) for n in SMALL}
    return loss[0, 0], grad_x, small, {n: [grads[l][n] for l in range(DEPTH)] for n in LARGE}


def _pack(arrs, rows, dtype=f32):
    flat = jnp.concatenate([a.reshape(-1).astype(dtype) for a in arrs])
    return jnp.pad(flat, (0, rows * 1024 - flat.shape[0])).reshape(rows, 1024)


def _unpack(buf, shapes):
    flat, out, pos = buf.reshape(-1), [], 0
    for shp in shapes:
        n = 1
        for d in shp:
            n *= d
        out.append(flat[pos:pos + n].reshape(shp))
        pos += n
    return out


def _rows_for(shapes, mult):
    n = 0
    for shp in shapes:
        k = 1
        for d in shp:
            k *= d
        n += k
    rows = -(-n // 1024)
    return -(-rows // mult) * mult


def _shard(a, axis, chip):
    size = a.shape[axis] // N_CHIPS
    return lax.dynamic_slice_in_dim(a, chip * size, size, axis)


def kernel(x, w_in, dn_conv, dn_a_log, dn_dt_bias, dn_norm, w_br_dn, cf_dw, cf_dw_bias, cf_ln_g, cf_ln_b, w_br_cf, s5_a_re, s5_a_im, s5_log_dt, s5_b_re, s5_b_im, s5_c_re, s5_c_im, s5_d, w_br_s5, gla_w_alpha, gla_b_alpha, gla_norm, w_br_gla, w_o, ln1_g, ln1_b, w_up, ffn_conv, w_down, ln2_g, ln2_b, loss_target, m_w_in, m_dn_conv, m_dn_a_log, m_dn_dt_bias, m_dn_norm, m_w_br_dn, m_cf_dw, m_cf_dw_bias, m_cf_ln_g, m_cf_ln_b, m_w_br_cf, m_s5_a_re, m_s5_a_im, m_s5_log_dt, m_s5_b_re, m_s5_b_im, m_s5_c_re, m_s5_c_im, m_s5_d, m_w_br_s5, m_gla_w_alpha, m_gla_b_alpha, m_gla_norm, m_w_br_gla, m_w_o, m_ln1_g, m_ln1_b, m_w_up, m_ffn_conv, m_w_down, m_ln2_g, m_ln2_b, v_w_in, v_dn_conv, v_dn_a_log, v_dn_dt_bias, v_dn_norm, v_w_br_dn, v_cf_dw, v_cf_dw_bias, v_cf_ln_g, v_cf_ln_b, v_w_br_cf, v_s5_a_re, v_s5_a_im, v_s5_log_dt, v_s5_b_re, v_s5_b_im, v_s5_c_re, v_s5_c_im, v_s5_d, v_w_br_s5, v_gla_w_alpha, v_gla_b_alpha, v_gla_norm, v_w_br_gla, v_w_o, v_ln1_g, v_ln1_b, v_w_up, v_ffn_conv, v_w_down, v_ln2_g, v_ln2_b):
    env = locals()
    w = {n: env[n] for n in WEIGHTS}
    m = {n: env['m_' + n] for n in WEIGHTS}
    v = {n: env['v_' + n] for n in WEIGHTS}
    chip = 2 * lax.axis_index("x") + lax.axis_index("y")

    large_shapes = [w[n].shape for n in LARGE]
    ssh_shapes = [w[n].shape for n in SMALL_SHARDED]
    large_rows, ssh_rows = _rows_for(large_shapes, 512), _rows_for(ssh_shapes, 8)
    got_large, got_ssh = _gather_chips(_pack([w[n] for n in LARGE], large_rows, bf16),
                                       _pack([w[n] for n in SMALL_SHARDED], ssh_rows))
    full = {n: w[n] for n in SMALL if n not in SHARD_AXIS}
    per_chip = [dict(zip(LARGE + SMALL_SHARDED, _unpack(got_large[k], large_shapes) + _unpack(got_ssh[k], ssh_shapes)))
                for k in range(N_CHIPS)]
    for n in LARGE + SMALL_SHARDED:
        full[n] = jnp.concatenate([per_chip[k][n] for k in range(N_CHIPS)], axis=SHARD_AXIS[n])

    loss, grad_x, g, g_large = _local_step(x[0], loss_target[0], full)
    loss = lax.psum(loss, ("x", "y", "c"))

    def piece(k):
        parts = []
        for n in LARGE:
            axis = SHARD_AXIS[n] - 1
            size = g_large[n][0].shape[axis] // N_CHIPS
            parts += [lax.slice_in_dim(layer, k * size, (k + 1) * size, axis=axis) for layer in g_large[n]]
        return _pack(parts, large_rows, bf16)

    half = large_rows // 2
    core = lax.axis_index("c")
    pieces = jnp.stack([piece(k) for k in range(N_CHIPS)])
    p_mine = lax.dynamic_slice_in_dim(pieces, core * half, half, axis=1).reshape(N_CHIPS * half, 1024)
    p_theirs = lax.dynamic_slice_in_dim(pieces, (1 - core) * half, half, axis=1).reshape(N_CHIPS * half, 1024)
    pair = _sum_parts('sum_pair', [p_mine, _swap_sibling(p_theirs)], bf16).reshape(N_CHIPS, half, 1024)
    from_chips = _scatter_chips(pair)
    own = lax.dynamic_index_in_dim(pair, chip, 0, keepdims=False)
    chip_sum = _join_halves(_sum_parts('sum_chips', [own, from_chips[0], from_chips[1], from_chips[2]]))
    res = {0: dict(zip(LARGE, _unpack(chip_sum, large_shapes))), 1: {}, 2: {}, 3: {}}
    for n in LARGE:
        two_d = (w[n].shape[0] * w[n].shape[1], w[n].shape[2])
        upd = _adamw('adamw_large', w[n].reshape(two_d), m[n].reshape(two_d), v[n].reshape(two_d), res[0][n].reshape(two_d))
        for kind in range(3):
            res[kind + 1][n] = upd[kind].reshape(w[n].shape)

    small_full_shapes = [g[n].shape for n in SMALL]
    small_rows = _rows_for(small_full_shapes, 8)
    small_sum = _sum_slots('sum_devices', _gather_all(_pack([g[n] for n in SMALL], small_rows)))
    gs = dict(zip(SMALL, _unpack(small_sum, small_full_shapes)))
    for n in SMALL_SHARDED:
        gs[n] = _shard(gs[n], SHARD_AXIS[n], chip)
    small_shapes = [w[n].shape for n in SMALL]
    upd_rows = _rows_for(small_shapes, 8)
    upd = _adamw('adamw_small', _pack([w[n] for n in SMALL], upd_rows), _pack([m[n] for n in SMALL], upd_rows),
                 _pack([v[n] for n in SMALL], upd_rows), _pack([gs[n] for n in SMALL], upd_rows))
    res[0].update(gs)
    for kind in range(3):
        res[kind + 1].update(zip(SMALL, _unpack(upd[kind], small_shapes)))
    return (loss, grad_x[None], *[res[kind][n] for kind in range(4) for n in WEIGHTS])
```

```python
import functools

import jax
import jax.numpy as jnp
from jax import lax
from jax.experimental import pallas as pl
from jax.experimental.pallas import tpu as pltpu

f32 = jnp.float32
bf16 = jnp.bfloat16
HI = lax.Precision.HIGHEST

D_MODEL = 1024
DEPTH = 4
DN_HEADS, DN_DK, DN_CHUNK = 4, 128, 64
DN_STEP_CHUNKS = 2
GLA_HEADS, GLA_DK, GLA_CHUNK, GLA_TAU = 4, 64, 16, 16.0
GLA_BLOCK = 128
S5_GROUPS, S5_GROUP, S5_STATE = 32, 16, 64
S5_SB = 4
S5_TILE = 256
D_FF = 2816
LN_EPS = 1e-5
ALPHA = (2.0 * DEPTH) ** 0.25
ADAM_LR, ADAM_B1, ADAM_B2, ADAM_EPS, ADAM_WD, ADAM_STEP = 0.001, 0.9, 0.999, 1e-08, 0.01, 10

VMEM_LIMIT_V7X = 56 * 1024 * 1024
MM_A_BLOCK_BYTES = 8 * 1024 * 1024
MM_B_BLOCK_BYTES = 6 * 1024 * 1024
MM_OUT_BLOCK_BYTES = 10 * 1024 * 1024
MM_TN_MAX = 2560
HALO = 32
N_CHIPS = 4
N_DEV = 8

IN_ORIG = dict(dn_qkv=(0, 1536), dn_a=(1536, 1540), dn_b=(1540, 1544), dn_gate=(1544, 2056), cf=(2056, 3080),
               s5=(3080, 3592), gla_q=(3592, 3848), gla_k=(3848, 4104), gla_v=(4104, 4616), gla_g=(4616, 5128),
               gla_lr=(5128, 5144), gates=(5144, 9240))
IN_COLS = 9240
WM_COLS = 9728
WS_COLS = 128


def _cparams(sem):
    return pltpu.CompilerParams(dimension_semantics=sem, vmem_limit_bytes=VMEM_LIMIT_V7X)


def _pick(dim, pref):
    for t in (pref, 512, 256, 128, 64, 32, 16, 8):
        if t <= pref and dim % t == 0:
            return t
    return dim


NN = (((1,), (0,)), ((), ()))
NT = (((1,), (1,)), ((), ()))
TN = (((0,), (0,)), ((), ()))


def _dot(a, b, dims=NN):
    return lax.dot_general(a, b, dims, precision=HI, preferred_element_type=f32)


def _round(a):
    return a.astype(bf16).astype(f32)


def _rdot(a, b, dims=NN):
    return lax.dot_general(a.astype(bf16), b.astype(bf16), dims, preferred_element_type=f32)


@functools.partial(jax.custom_vjp, nondiff_argnums=(2,))
def _bdot(a, b, dims=NN):
    return _rdot(a, b, dims)


def _bdot_fwd(a, b, dims):
    return _rdot(a, b, dims), (a, b)


def _bdot_bwd(dims, res, g):
    a, b = res
    if dims == NN:
        return _rdot(g, b, NT), _rdot(a, g, TN)
    if dims == NT:
        return _rdot(g, b, NN), _rdot(g, a, TN)
    assert dims == TN
    return _rdot(b, g, NT), _rdot(a, g, NN)


_bdot.defvjp(_bdot_fwd, _bdot_bwd)


def _iota(shape, axis):
    return lax.broadcasted_iota(jnp.int32, shape, axis)


def _mm(a, b, mode, name, out_dtype=f32):
    if mode == 'nn':
        (m, k), n = a.shape, b.shape[1]
    elif mode == 'nt':
        (m, k), n = a.shape, b.shape[0]
    else:
        (k, m), n = a.shape, b.shape[1]
    tm = _pick(m, 512 if mode == 'tn' else 1024)
    nk = 1
    while (k // nk) * tm * a.dtype.itemsize > MM_A_BLOCK_BYTES or k % nk or (k // nk) % 128:
        nk += 1
    tk = k // nk
    assert nk == 1 or out_dtype == f32
    tn = 128
    for cand in range(128, min(n, MM_TN_MAX) + 1, 128):
        if n % cand == 0 and tk * cand * b.dtype.itemsize <= MM_B_BLOCK_BYTES and tm * cand * 4 <= MM_OUT_BLOCK_BYTES:
            tn = cand
    if mode == 'nn':
        dims = NN
        a_spec = pl.BlockSpec((tm, tk), lambda i, j, kk: (i, kk))
        b_spec = pl.BlockSpec((tk, tn), lambda i, j, kk: (kk, j))
    elif mode == 'nt':
        dims = NT
        a_spec = pl.BlockSpec((tm, tk), lambda i, j, kk: (i, kk))
        b_spec = pl.BlockSpec((tn, tk), lambda i, j, kk: (j, kk))
    else:
        dims = TN
        a_spec = pl.BlockSpec((tk, tm), lambda i, j, kk: (kk, i))
        b_spec = pl.BlockSpec((tk, tn), lambda i, j, kk: (kk, j))

    def kern(a_ref, b_ref, o_ref):
        p = lax.dot_general(a_ref[...].astype(bf16), b_ref[...].astype(bf16), dims, preferred_element_type=f32)
        if nk == 1:
            o_ref[...] = p.astype(o_ref.dtype)
        else:
            kk = pl.program_id(2)

            @pl.when(kk == 0)
            def _():
                o_ref[...] = p

            @pl.when(kk > 0)
            def _():
                o_ref[...] += p

    return pl.pallas_call(
        kern, name=name, grid=(m // tm, n // tn, nk),
        in_specs=[a_spec, b_spec], out_specs=pl.BlockSpec((tm, tn), lambda i, j, kk: (i, j)),
        out_shape=jax.ShapeDtypeStruct((m, n), out_dtype),
        compiler_params=_cparams(("parallel", "parallel", "arbitrary")),
    )(a, b)


def _full_spec(p):
    nd = p.ndim
    return pl.BlockSpec(p.shape, lambda *_, nd=nd: (0,) * nd)


def _rowwise(name, f, ins, params, out_widths, tile=256, out_dtype=f32):
    rows = ins[0][0].shape[0]
    tile = _pick(rows, tile)
    n_x = len(ins) + len(params)

    def kern(*refs):
        for o_ref, r in zip(refs[n_x:], f(*[r[...] for r in refs[:n_x]])):
            o_ref[...] = r.astype(o_ref.dtype)

    in_specs = [pl.BlockSpec((tile, w), lambda i, c=c: (i, c)) for (_, w, c) in ins] + [_full_spec(p) for p in params]
    return pl.pallas_call(
        kern, name=name, grid=(rows // tile,), in_specs=in_specs,
        out_specs=[pl.BlockSpec((tile, w), lambda i: (i, 0)) for w in out_widths],
        out_shape=[jax.ShapeDtypeStruct((rows, w), out_dtype) for w in out_widths],
        compiler_params=_cparams(("parallel",)),
    )(*[a for (a, _, _) in ins], *params)


def _rowwise_bwd(name, f, ins, params, douts, want, tile=256):
    rows = ins[0][0].shape[0]
    tile = _pick(rows, tile)
    n_in, n_p = len(ins), len(params)
    parts = [p for d in douts for p in d]
    n_x, n_d = n_in + n_p, len(parts)

    def kern(*refs):
        xs = [r[...] for r in refs[:n_x]]
        d_refs, o_refs = refs[n_x:n_x + n_d], refs[n_x + n_d:]
        cts, pos = [], 0
        for d in douts:
            acc = d_refs[pos][...]
            for r in d_refs[pos + 1:pos + len(d)]:
                acc = acc + r[...]
            pos += len(d)
            cts.append(acc)
        grads = jax.vjp(f, *xs)[1](tuple(cts))
        k = 0
        for j in range(n_in):
            if want[j]:
                o_refs[k][...] = grads[j].astype(o_refs[k].dtype)
                k += 1
        first = pl.program_id(0) == 0
        for j in range(n_p):
            g, o_ref = grads[n_in + j], o_refs[k + j]

            @pl.when(first)
            def _(o_ref=o_ref, g=g):
                o_ref[...] = g

            @pl.when(jnp.logical_not(first))
            def _(o_ref=o_ref, g=g):
                o_ref[...] += g

    in_specs = ([pl.BlockSpec((tile, w), lambda i, c=c: (i, c)) for (_, w, c) in ins] + [_full_spec(p) for p in params]
                + [pl.BlockSpec((tile, w), lambda i, c=c: (i, c)) for (_, w, c) in parts])
    out_specs, out_shape = [], []
    for j in range(n_in):
        if want[j]:
            out_specs.append(pl.BlockSpec((tile, ins[j][1]), lambda i: (i, 0)))
            out_shape.append(jax.ShapeDtypeStruct((rows, ins[j][1]), want[j]))
    n_g = len(out_specs)
    for p in params:
        out_specs.append(_full_spec(p))
        out_shape.append(jax.ShapeDtypeStruct(p.shape, f32))
    res = pl.pallas_call(
        kern, name=name, grid=(rows // tile,), in_specs=in_specs, out_specs=out_specs, out_shape=out_shape,
        compiler_params=_cparams(("arbitrary",)),
    )(*[a for (a, _, _) in ins], *params, *[a for (a, _, _) in parts])
    return list(res[:n_g]), list(res[n_g:])


_sigmoid = jax.nn.sigmoid
_silu = jax.nn.silu
_softplus = jax.nn.softplus
_log_sigmoid = jax.nn.log_sigmoid


def _f_ln(x, r, g, b):
    t = ALPHA * x + r
    mu = jnp.mean(t, -1, keepdims=True)
    var = jnp.mean(jnp.square(t - mu), -1, keepdims=True)
    return ((t - mu) * lax.rsqrt(var + LN_EPS) * g + b,)


def _f_pre(c1, hs, a_log, dt_bias, w_alpha, b_alpha):
    s = _silu(c1)
    outs = []
    for j in range(3 * DN_HEADS):
        t = s[:, j * DN_DK:(j + 1) * DN_DK]
        if j < 2 * DN_HEADS:
            t = t * lax.rsqrt(jnp.sum(t * t, -1, keepdims=True) + 1e-6)
        if j < DN_HEADS:
            t = t * (DN_DK ** -0.5)
        outs.append(t)
    qkvn = jnp.concatenate(outs, axis=1)
    lane = _iota(hs.shape, 1)
    g = -jnp.exp(a_log) * _softplus(hs + dt_bias)
    beta = _sigmoid(hs)
    gb = jnp.where(lane < DN_HEADS, g, jnp.where(lane < 2 * DN_HEADS, beta, 0.0))
    la = _log_sigmoid(_bdot(hs, w_alpha) + b_alpha) * (1.0 / GLA_TAU)
    lane5 = _iota(la.shape, 1)
    la = jnp.where((lane5 % 128) < GLA_DK, la, 0.0)
    return qkvn, gb, la


def _f_post(o, gate, w):
    outs = []
    for j in range(4):
        t = o[:, j * 128:(j + 1) * 128]
        outs.append(t * lax.rsqrt(jnp.mean(t * t, -1, keepdims=True) + LN_EPS) * w)
    return (jnp.concatenate(outs, axis=1) * _silu(gate),)


def _f_glu(a, g):
    return (a * _sigmoid(g),)


def _f_cfpost(c, bias, g, b):
    t = c + bias
    mu = jnp.mean(t, -1, keepdims=True)
    var = jnp.mean(jnp.square(t - mu), -1, keepdims=True)
    return (_silu((t - mu) * lax.rsqrt(var + LN_EPS) * g + b),)


def _f_gelu(y):
    return (jax.nn.gelu(y),)


def _f_merge(ga, gb_, gc, gd, ya, yb, zv, zg, yd):
    return (_sigmoid(ga) * ya + _sigmoid(gb_) * yb + _sigmoid(gc) * (zv * _sigmoid(zg)) + _sigmoid(gd) * yd,)


def _f_act(a, b):
    return (_silu(a) * b,)


def _conv_tiles(rows, ch):
    return _pick(rows, 512), _pick(ch, 512)


def _conv_fwd(name, x, width, colblk0, w, taps):
    rows = x.shape[0]
    tr, cb = _conv_tiles(rows, width)
    hb = tr // HALO

    def kern(prev_ref, x_ref, w_ref, o_ref, ext):
        i = pl.program_id(1)
        ext[pl.ds(0, HALO), :] = jnp.where(i > 0, _round(prev_ref[...]), 0.0)
        ext[pl.ds(HALO, tr), :] = _round(x_ref[...])
        wv = _round(w_ref[...])
        acc = jnp.zeros((tr, cb), f32)
        for k in range(taps):
            acc = acc + wv[k:k + 1, :] * ext[pl.ds(HALO - taps + 1 + k, tr), :]
        o_ref[...] = acc

    c0 = colblk0 * (width // cb)
    return pl.pallas_call(
        kern, name=name, grid=(width // cb, rows // tr),
        in_specs=[pl.BlockSpec((HALO, cb), lambda c, i: (jnp.maximum(i * hb - 1, 0), c0 + c)),
                  pl.BlockSpec((tr, cb), lambda c, i: (i, c0 + c)),
                  pl.BlockSpec((w.shape[0], cb), lambda c, i: (0, c))],
        out_specs=pl.BlockSpec((tr, cb), lambda c, i: (i, c)),
        out_shape=jax.ShapeDtypeStruct((rows, width), f32),
        scratch_shapes=[pltpu.VMEM((HALO + tr, cb), f32)],
        compiler_params=_cparams(("parallel", "arbitrary")),
    )(x, x, w)


def _conv_bwd(name, x, width, colblk0, w, taps, dy, dx_dtype=f32):
    rows = x.shape[0]
    tr, cb = _conv_tiles(rows, width)
    hb = tr // HALO
    nt = rows // tr
    wr = w.shape[0]

    def kern(prev_ref, x_ref, w_ref, dy_ref, next_ref, dx_ref, dw_ref, ext, dext):
        i = pl.program_id(1)
        ext[pl.ds(0, HALO), :] = jnp.where(i > 0, _round(prev_ref[...]), 0.0)
        ext[pl.ds(HALO, tr), :] = _round(x_ref[...])
        dyv = _round(dy_ref[...])
        dext[pl.ds(0, tr), :] = dyv
        dext[pl.ds(tr, HALO), :] = jnp.where(i < nt - 1, _round(next_ref[...]), 0.0)
        wv = _round(w_ref[...])
        acc = jnp.zeros((tr, cb), f32)
        rows_w = []
        for k in range(taps):
            acc = acc + wv[k:k + 1, :] * dext[pl.ds(taps - 1 - k, tr), :]
            rows_w.append(jnp.sum(dyv * ext[pl.ds(HALO - taps + 1 + k, tr), :], axis=0, keepdims=True))
        dx_ref[...] = acc.astype(dx_ref.dtype)
        if wr > taps:
            rows_w.append(jnp.zeros((wr - taps, cb), f32))
        dwv = jnp.concatenate(rows_w, axis=0)

        @pl.when(i == 0)
        def _():
            dw_ref[...] = dwv

        @pl.when(i > 0)
        def _():
            dw_ref[...] += dwv

    c0 = colblk0 * (width // cb)
    return pl.pallas_call(
        kern, name=name, grid=(width // cb, nt),
        in_specs=[pl.BlockSpec((HALO, cb), lambda c, i: (jnp.maximum(i * hb - 1, 0), c0 + c)),
                  pl.BlockSpec((tr, cb), lambda c, i: (i, c0 + c)),
                  pl.BlockSpec((wr, cb), lambda c, i: (0, c)),
                  pl.BlockSpec((tr, cb), lambda c, i: (i, c)),
                  pl.BlockSpec((HALO, cb), lambda c, i: (jnp.minimum((i + 1) * hb, nt * hb - 1), c))],
        out_specs=[pl.BlockSpec((tr, cb), lambda c, i: (i, c)), pl.BlockSpec((wr, cb), lambda c, i: (0, c))],
        out_shape=[jax.ShapeDtypeStruct((rows, width), dx_dtype), jax.ShapeDtypeStruct((wr, width), f32)],
        scratch_shapes=[pltpu.VMEM((HALO + tr, cb), f32), pltpu.VMEM((HALO + tr, cb), f32)],
        compiler_params=_cparams(("parallel", "arbitrary")),
    )(x, x, w, dy, dy)


def _series_inverse(neg):
    n = neg.shape[0]
    inv = jnp.where(_iota((n, n), 0) == _iota((n, n), 1), 1.0, 0.0) + neg
    p = neg
    for _ in range(5):
        p = _dot(p, p)
        inv = inv + _dot(inv, p)
    return inv


def _inverse_bwd(inv, g):
    return _dot(_dot(inv, g, TN), inv, NT)


@jax.custom_vjp
def _unit_lower_inverse(neg):
    return _series_inverse(neg)


_unit_lower_inverse.defvjp(lambda neg: (_series_inverse(neg),) * 2, lambda inv, g: (_inverse_bwd(inv, g),))


@jax.custom_vjp
def _known_inverse(neg, inv):
    return inv


_known_inverse.defvjp(lambda neg, inv: (inv, inv), lambda inv, g: (_inverse_bwd(inv, g), jnp.zeros_like(inv)))


def _dn_head(state, q, k, v, gc, gl, beta, inv_saved=None):
    c = DN_CHUNK
    ii, jj = _iota((c, c), 0), _iota((c, c), 1)
    causal, strict = ii >= jj, ii > jj
    gcb = jnp.broadcast_to(gc, (c, c))
    decay = jnp.where(causal, jnp.exp(jnp.where(causal, gcb - gcb.T, 0.0)), 0.0)
    kb = k * beta
    neg = -jnp.where(strict, _bdot(kb, k, NT) * decay, 0.0)
    inv = _unit_lower_inverse(neg) if inv_saved is None else _known_inverse(neg, inv_saved)
    egc = jnp.exp(gc)
    sol = _dot(inv, jnp.concatenate([v * beta, kb * egc], axis=1))
    u, w = sol[:, :DN_DK], sol[:, DN_DK:]
    intra = _bdot(q, k, NT) * decay
    v_new = u - _bdot(w, state)
    o = _bdot(q * egc, state) + _bdot(intra, v_new)
    new_state = state * jnp.exp(gl) + _bdot(k * jnp.exp(gl - gc), v_new, TN)
    return (o, new_state, inv) if inv_saved is None else (o, new_state)


def _dn_cum(gb):
    c = DN_CHUNK
    tril = jnp.where(_iota((c, c), 0) >= _iota((c, c), 1), 1.0, 0.0)
    return _dot(tril, gb), jnp.sum(gb, axis=0, keepdims=True)


def _dn_block_head(state, q, k, v, gc, gls, beta, invs=None):
    c = DN_CHUNK
    outs, new_invs = [], []
    for n, gl in enumerate(gls):
        sl = slice(n * c, (n + 1) * c)
        if invs is None:
            o, state, inv = _dn_head(state, q[sl], k[sl], v[sl], gc[sl], gl, beta[sl])
            new_invs.append(inv)
        else:
            o, state = _dn_head(state, q[sl], k[sl], v[sl], gc[sl], gl, beta[sl], inv_saved=invs[n])
        outs.append(o)
    o = jnp.concatenate(outs, axis=0)
    return (o, state, new_invs) if invs is None else (o, state)


def _dn_block_cum(gbv, nbk):
    c = DN_CHUNK
    cums, tots = zip(*[_dn_cum(gbv[n * c:(n + 1) * c]) for n in range(nbk)])
    return jnp.concatenate(cums, axis=0), tots


def _dn_fwd(qkvn, gb):
    rows = qkvn.shape[0]
    c, h, d = DN_CHUNK, DN_HEADS, DN_DK
    nbk = DN_STEP_CHUNKS if rows % (c * DN_STEP_CHUNKS) == 0 else 1
    b, nb = c * nbk, rows // (c * nbk)

    def kern(qkv_ref, gb_ref, o_ref, st_ref, inv_ref, state):
        @pl.when(pl.program_id(0) == 0)
        def _():
            state[...] = jnp.zeros_like(state)

        gbv = gb_ref[...]
        cum, tots = _dn_block_cum(gbv, nbk)
        for j in range(h):
            st = state[j]
            st_ref[0, j] = st
            o, new, invs = _dn_block_head(st, qkv_ref[:, j * d:(j + 1) * d], qkv_ref[:, (h + j) * d:(h + j + 1) * d],
                                          qkv_ref[:, (2 * h + j) * d:(2 * h + j + 1) * d],
                                          cum[:, j:j + 1], [t[:, j:j + 1] for t in tots], gbv[:, h + j:h + j + 1])
            o_ref[:, j * d:(j + 1) * d] = o
            for n in range(nbk):
                inv_ref[0, n * h + j] = invs[n]
            state[j] = new

    return pl.pallas_call(
        kern, name="dn_fwd", grid=(nb,),
        in_specs=[pl.BlockSpec((b, 3 * h * d), lambda i: (i, 0)), pl.BlockSpec((b, 128), lambda i: (i, 0))],
        out_specs=[pl.BlockSpec((b, h * d), lambda i: (i, 0)), pl.BlockSpec((1, h, d, d), lambda i: (i, 0, 0, 0)),
                   pl.BlockSpec((1, nbk * h, c, c), lambda i: (i, 0, 0, 0))],
        out_shape=[jax.ShapeDtypeStruct((rows, h * d), f32), jax.ShapeDtypeStruct((nb, h, d, d), f32),
                   jax.ShapeDtypeStruct((nb, nbk * h, c, c), f32)],
        scratch_shapes=[pltpu.VMEM((h, d, d), f32)],
        compiler_params=_cparams(("arbitrary",)),
    )(qkvn, gb)


def _dn_bwd(qkvn, gb, states, invs, do):
    rows = qkvn.shape[0]
    c, h, d = DN_CHUNK, DN_HEADS, DN_DK
    nb = states.shape[0]
    nbk = rows // (c * nb)
    b = c * nbk

    def kern(qkv_ref, gb_ref, st_ref, inv_ref, do_ref, dqkv_ref, dgb_ref, dstate):
        @pl.when(pl.program_id(0) == 0)
        def _():
            dstate[...] = jnp.zeros_like(dstate)

        gbv = gb_ref[...]
        cum, tots = _dn_block_cum(gbv, nbk)
        lane = _iota((b, 128), 1)
        lane_c = _iota((c, 128), 1)
        dcum = jnp.zeros((b, 128), f32)
        dgb = jnp.zeros((b, 128), f32)
        for j in range(h):
            args = (st_ref[0, j], qkv_ref[:, j * d:(j + 1) * d], qkv_ref[:, (h + j) * d:(h + j + 1) * d],
                    qkv_ref[:, (2 * h + j) * d:(2 * h + j + 1) * d],
                    cum[:, j:j + 1], [t[:, j:j + 1] for t in tots], gbv[:, h + j:h + j + 1])
            head = functools.partial(_dn_block_head, invs=[inv_ref[0, n * h + j] for n in range(nbk)])
            ds, dq, dk, dv, dgc, dgls, dbeta = jax.vjp(head, *args)[1]((do_ref[:, j * d:(j + 1) * d], dstate[j]))
            dstate[j] = ds
            dqkv_ref[:, j * d:(j + 1) * d] = dq
            dqkv_ref[:, (h + j) * d:(h + j + 1) * d] = dk
            dqkv_ref[:, (2 * h + j) * d:(2 * h + j + 1) * d] = dv
            dcum = dcum + jnp.where(lane == j, dgc, 0.0)
            dgb = dgb + jnp.where(lane == h + j, dbeta, 0.0) + jnp.concatenate(
                [jnp.where(lane_c == j, dgl, 0.0) for dgl in dgls], axis=0)
        triu = jnp.where(_iota((c, c), 0) <= _iota((c, c), 1), 1.0, 0.0)
        dgb_ref[...] = dgb + jnp.concatenate([_dot(triu, dcum[n * c:(n + 1) * c]) for n in range(nbk)], axis=0)

    rev = lambda i: (nb - 1 - i, 0)
    rev4 = lambda i: (nb - 1 - i, 0, 0, 0)
    return pl.pallas_call(
        kern, name="dn_bwd", grid=(nb,),
        in_specs=[pl.BlockSpec((b, 3 * h * d), rev), pl.BlockSpec((b, 128), rev),
                  pl.BlockSpec((1, h, d, d), rev4), pl.BlockSpec((1, nbk * h, c, c), rev4), pl.BlockSpec((b, h * d), rev)],
        out_specs=[pl.BlockSpec((b, 3 * h * d), rev), pl.BlockSpec((b, 128), rev)],
        out_shape=[jax.ShapeDtypeStruct((rows, 3 * h * d), f32), jax.ShapeDtypeStruct((rows, 128), f32)],
        scratch_shapes=[pltpu.VMEM((h, d, d), f32)],
        compiler_params=_cparams(("arbitrary",)),
    )(qkvn, gb, states, invs, do)


def _gla_block(state_t, q, k, v, la):
    c, b = GLA_CHUNK, q.shape[0]
    ii, jj = _iota((b, b), 0), _iota((b, b), 1)
    causal = (ii >= jj) & (ii - jj < c) & (jnp.bitwise_and(ii, c - 1) >= jnp.bitwise_and(jj, c - 1))
    gc = _dot(jnp.where(causal, 1.0, 0.0), la)
    q_dec = q * (GLA_DK ** -0.5) * jnp.exp(gc)
    scores = jnp.where(causal, _bdot(q_dec, k * jnp.exp(-gc), NT), 0.0)
    intra = _bdot(scores, v)
    outs = []
    for n in range(b // c):
        sl = slice(n * c, (n + 1) * c)
        gl = gc[(n + 1) * c - 1:(n + 1) * c, :]
        outs.append(_bdot(q_dec[sl], state_t, NT))
        state_t = state_t * jnp.exp(gl) + _bdot(v[sl], k[sl] * jnp.exp(gl - gc[sl]), TN)
    return jnp.concatenate(outs, axis=0) + intra, state_t


GLA_Q0, GLA_K0, GLA_V0 = 7680 // 512, 8192 // 512, 8704 // 512


def _gla_fwd(hmain, la):
    rows = hmain.shape[0]
    b, h = _pick(rows, GLA_BLOCK), GLA_HEADS
    nb = rows // b

    def kern(q_ref, k_ref, v_ref, la_ref, o_ref, st_ref, state):
        @pl.when(pl.program_id(0) == 0)
        def _():
            state[...] = jnp.zeros_like(state)

        for j in range(h):
            sl = slice(j * 128, (j + 1) * 128)
            st = state[j]
            st_ref[0, j] = st
            o, new = _gla_block(st, q_ref[:, sl], k_ref[:, sl], v_ref[:, sl], la_ref[:, sl])
            o_ref[:, sl] = o
            state[j] = new

    return pl.pallas_call(
        kern, name="gla_fwd", grid=(nb,),
        in_specs=[pl.BlockSpec((b, 512), lambda i: (i, GLA_Q0)), pl.BlockSpec((b, 512), lambda i: (i, GLA_K0)),
                  pl.BlockSpec((b, 512), lambda i: (i, GLA_V0)), pl.BlockSpec((b, 512), lambda i: (i, 0))],
        out_specs=[pl.BlockSpec((b, 512), lambda i: (i, 0)), pl.BlockSpec((1, h, 128, 128), lambda i: (i, 0, 0, 0))],
        out_shape=[jax.ShapeDtypeStruct((rows, h * 128), f32), jax.ShapeDtypeStruct((nb, h, 128, 128), f32)],
        scratch_shapes=[pltpu.VMEM((h, 128, 128), f32)],
        compiler_params=_cparams(("arbitrary",)),
    )(hmain, hmain, hmain, la)


def _gla_bwd(hmain, la, states, do):
    rows = hmain.shape[0]
    b, h = _pick(rows, GLA_BLOCK), GLA_HEADS
    nb = rows // b

    def kern(q_ref, k_ref, v_ref, la_ref, st_ref, do_ref, dq_ref, dk_ref, dv_ref, dla_ref, dstate):
        @pl.when(pl.program_id(0) == 0)
        def _():
            dstate[...] = jnp.zeros_like(dstate)

        for j in range(h):
            sl = slice(j * 128, (j + 1) * 128)
            args = (st_ref[0, j], q_ref[:, sl], k_ref[:, sl], v_ref[:, sl], la_ref[:, sl])
            ds, dq, dk, dv, dla = jax.vjp(_gla_block, *args)[1]((do_ref[:, sl], dstate[j]))
            dstate[j] = ds
            dq_ref[:, sl] = dq.astype(bf16)
            dk_ref[:, sl] = dk.astype(bf16)
            dv_ref[:, sl] = dv.astype(bf16)
            dla_ref[:, sl] = dla

    rev = lambda i: (nb - 1 - i, 0)
    return pl.pallas_call(
        kern, name="gla_bwd", grid=(nb,),
        in_specs=[pl.BlockSpec((b, 512), lambda i: (nb - 1 - i, GLA_Q0)), pl.BlockSpec((b, 512), lambda i: (nb - 1 - i, GLA_K0)),
                  pl.BlockSpec((b, 512), lambda i: (nb - 1 - i, GLA_V0)), pl.BlockSpec((b, 512), rev),
                  pl.BlockSpec((1, h, 128, 128), lambda i: (nb - 1 - i, 0, 0, 0)), pl.BlockSpec((b, 512), rev)],
        out_specs=[pl.BlockSpec((b, 512), rev)] * 4,
        out_shape=[jax.ShapeDtypeStruct((rows, h * 128), bf16)] * 3 + [jax.ShapeDtypeStruct((rows, h * 128), f32)],
        scratch_shapes=[pltpu.VMEM((h, 128, 128), f32)],
        compiler_params=_cparams(("arbitrary",)),
    )(hmain, hmain, hmain, la, states, do)


def _f_s5_params(a_re, a_im, log_dt, b_re, b_im):
    dt = jnp.exp(log_dt)
    mag = jnp.exp(dt * a_re)
    abar_re, abar_im = mag * jnp.cos(dt * a_im), mag * jnp.sin(dt * a_im)
    den = a_re * a_re + a_im * a_im
    nr, ni = abar_re - 1.0, abar_im
    fr, fi = (nr * a_re + ni * a_im) / den, (ni * a_re - nr * a_im) / den
    return abar_re, abar_im, fr[None] * b_re - fi[None] * b_im, fr[None] * b_im + fi[None] * b_re


def _s5_params(a_re, a_im, log_dt, b_re, b_im):
    def kern(*refs):
        for o_ref, r in zip(refs[5:], _f_s5_params(*[r[...] for r in refs[:5]])):
            o_ref[...] = r

    ins = (a_re, a_im, log_dt, b_re, b_im)
    return pl.pallas_call(
        kern, name="s5_params", out_shape=[jax.ShapeDtypeStruct(a_re.shape, f32)] * 2 + [jax.ShapeDtypeStruct(b_re.shape, f32)] * 2,
    )(*ins)


def _s5_params_bwd(a_re, a_im, log_dt, b_re, b_im, d_ar, d_ai, d_br, d_bi):
    def kern(*refs):
        grads = jax.vjp(_f_s5_params, *[r[...] for r in refs[:5]])[1](tuple(r[...] for r in refs[5:9]))
        for o_ref, g in zip(refs[9:], grads):
            o_ref[...] = g

    ins = (a_re, a_im, log_dt, b_re, b_im)
    return pl.pallas_call(
        kern, name="s5_params_bwd", out_shape=[jax.ShapeDtypeStruct(t.shape, f32) for t in ins],
    )(*ins, d_ar, d_ai, d_br, d_bi)


def _cmul(ar, ai, br, bi):
    return ar * br - ai * bi, ar * bi + ai * br


def _s5_scan(xr, xi, ar, ai, reverse):
    t = xr.shape[0]
    row = _iota(xr.shape, 0)
    s = 1
    while s < t:
        if reverse:
            keep = row < t - s
            sr, si = pltpu.roll(xr, t - s, 0), pltpu.roll(xi, t - s, 0)
        else:
            keep = row >= s
            sr, si = pltpu.roll(xr, s, 0), pltpu.roll(xi, s, 0)
        sr, si = jnp.where(keep, sr, 0.0), jnp.where(keep, si, 0.0)
        pr, pi = _cmul(ar, ai, sr, si)
        xr, xi = xr + pr, xi + pi
        ar, ai = _cmul(ar, ai, ar, ai)
        s *= 2
    return xr, xi


def _s5_powers(ar, ai, t, reverse):
    row = _iota((t, ar.shape[1]), 0)
    at = (row == (t - 1 if reverse else 0))
    return _s5_scan(jnp.where(at, ar, 0.0), jnp.where(at, ai, 0.0), ar, ai, reverse)


S5_U0 = 7168 // 128


def _s5_fwd(hmain, abar, bmat_re, bmat_im, cmat_re, cmat_im, dvec):
    rows = hmain.shape[0]
    t = _pick(rows, S5_TILE)
    nt, ns = rows // t, 512

    def kern(u_ref, a_ref, br_ref, bi_ref, cr_ref, ci_ref, d_ref, y_ref, xr_ref, xi_ref, pw, carry):
        ar, ai = a_ref[0, 0], a_ref[1, 0]

        @pl.when(pl.program_id(1) == 0)
        def _():
            pr, pi = _s5_powers(ar, ai, t, False)
            pw[0], pw[1] = pr, pi
            carry[...] = jnp.zeros_like(carry)

        u = u_ref[...]
        xr, xi = _s5_scan(_rdot(u, br_ref[0]), _rdot(u, bi_ref[0]), ar, ai, False)
        cr, ci = carry[0:1, :], carry[1:2, :]
        qr, qi = _cmul(pw[0], pw[1], cr, ci)
        xr, xi = xr + qr, xi + qi
        xr_ref[...] = xr
        xi_ref[...] = xi
        carry[0:1, :] = xr[t - 1:t, :]
        carry[1:2, :] = xi[t - 1:t, :]
        y_ref[...] = _rdot(xr, cr_ref[0]) - _rdot(xi, ci_ref[0]) + d_ref[...] * u

    sb3 = lambda b, i: (b, 0, 0)
    return pl.pallas_call(
        kern, name="s5_fwd", grid=(S5_SB, nt),
        in_specs=[pl.BlockSpec((t, 128), lambda b, i: (i, S5_U0 + b)), pl.BlockSpec((2, 1, 1, ns), lambda b, i: (0, b, 0, 0)),
                  pl.BlockSpec((1, 128, ns), sb3), pl.BlockSpec((1, 128, ns), sb3),
                  pl.BlockSpec((1, ns, 128), sb3), pl.BlockSpec((1, ns, 128), sb3), pl.BlockSpec((1, 128), lambda b, i: (0, b))],
        out_specs=[pl.BlockSpec((t, 128), lambda b, i: (i, b)), pl.BlockSpec((t, ns), lambda b, i: (i, b)),
                   pl.BlockSpec((t, ns), lambda b, i: (i, b))],
        out_shape=[jax.ShapeDtypeStruct((rows, 512), f32), jax.ShapeDtypeStruct((rows, S5_SB * ns), f32),
                   jax.ShapeDtypeStruct((rows, S5_SB * ns), f32)],
        scratch_shapes=[pltpu.VMEM((2, t, ns), f32), pltpu.VMEM((8, ns), f32)],
        compiler_params=_cparams(("parallel", "arbitrary")),
    )(hmain, abar, bmat_re, bmat_im, cmat_re, cmat_im, dvec)


def _s5_bwd(hmain, abar, bmat_re, bmat_im, cmat_re, cmat_im, dvec, x_re, x_im, dy):
    rows = hmain.shape[0]
    t = _pick(rows, S5_TILE)
    nt, ns = rows // t, 512
    t8 = t // 8

    def kern(u_ref, a_ref, br_ref, bi_ref, cr_ref, ci_ref, d_ref, xr_ref, xi_ref, xpr_ref, xpi_ref, dy_ref,
             du_ref, da_ref, dbr_ref, dbi_ref, dcr_ref, dci_ref, dd_ref, pw, carry):
        i = pl.program_id(1)
        ar, ai = a_ref[0, 0], -a_ref[1, 0]

        @pl.when(i == 0)
        def _():
            pr, pi = _s5_powers(ar, ai, t, True)
            pw[0], pw[1] = pr, pi
            carry[...] = jnp.zeros_like(carry)

        u, gy = u_ref[...], dy_ref[...]
        lr, li = _s5_scan(_rdot(gy, cr_ref[0], NT), -_rdot(gy, ci_ref[0], NT), ar, ai, True)
        qr, qi = _cmul(pw[0], pw[1], carry[0:1, :], carry[1:2, :])
        lr, li = lr + qr, li + qi
        carry[0:1, :] = lr[0:1, :]
        carry[1:2, :] = li[0:1, :]
        du_ref[...] = (_rdot(lr, br_ref[0], NT) + _rdot(li, bi_ref[0], NT) + d_ref[...] * gy).astype(bf16)
        xr, xi = xr_ref[...], xi_ref[...]
        row = _iota(xr.shape, 0)
        first_r = jnp.where(i < nt - 1, xpr_ref[7:8, :], 0.0)
        first_i = jnp.where(i < nt - 1, xpi_ref[7:8, :], 0.0)
        xpr = jnp.where(row == 0, first_r, pltpu.roll(xr, 1, 0))
        xpi = jnp.where(row == 0, first_i, pltpu.roll(xi, 1, 0))
        da_r = jnp.sum(lr * xpr + li * xpi, axis=0, keepdims=True)
        da_i = jnp.sum(li * xpr - lr * xpi, axis=0, keepdims=True)
        upd = [(da_ref.at[0, 0], da_r), (da_ref.at[1, 0], da_i),
               (dbr_ref.at[0], _rdot(u, lr, TN)), (dbi_ref.at[0], _rdot(u, li, TN)),
               (dcr_ref.at[0], _rdot(xr, gy, TN)), (dci_ref.at[0], -_rdot(xi, gy, TN)),
               (dd_ref, jnp.sum(gy * u, axis=0, keepdims=True))]

        @pl.when(i == 0)
        def _():
            for ref, val in upd:
                ref[...] = val

        @pl.when(i > 0)
        def _():
            for ref, val in upd:
                ref[...] += val

    sb3 = lambda b, i: (b, 0, 0)
    rev = lambda b, i: (nt - 1 - i, b)
    prev8 = lambda b, i: (jnp.maximum((nt - 1 - i) * t8 - 1, 0), b)
    return pl.pallas_call(
        kern, name="s5_bwd", grid=(S5_SB, nt),
        in_specs=[pl.BlockSpec((t, 128), lambda b, i: (nt - 1 - i, S5_U0 + b)), pl.BlockSpec((2, 1, 1, ns), lambda b, i: (0, b, 0, 0)),
                  pl.BlockSpec((1, 128, ns), sb3), pl.BlockSpec((1, 128, ns), sb3),
                  pl.BlockSpec((1, ns, 128), sb3), pl.BlockSpec((1, ns, 128), sb3), pl.BlockSpec((1, 128), lambda b, i: (0, b)),
                  pl.BlockSpec((t, ns), rev), pl.BlockSpec((t, ns), rev), pl.BlockSpec((8, ns), prev8), pl.BlockSpec((8, ns), prev8),
                  pl.BlockSpec((t, 128), rev)],
        out_specs=[pl.BlockSpec((t, 128), rev), pl.BlockSpec((2, 1, 1, ns), lambda b, i: (0, b, 0, 0)),
                   pl.BlockSpec((1, 128, ns), sb3), pl.BlockSpec((1, 128, ns), sb3),
                   pl.BlockSpec((1, ns, 128), sb3), pl.BlockSpec((1, ns, 128), sb3), pl.BlockSpec((1, 128), lambda b, i: (0, b))],
        out_shape=[jax.ShapeDtypeStruct((rows, 512), bf16), jax.ShapeDtypeStruct((2, S5_SB, 1, ns), f32),
                   jax.ShapeDtypeStruct((S5_SB, 128, ns), f32), jax.ShapeDtypeStruct((S5_SB, 128, ns), f32),
                   jax.ShapeDtypeStruct((S5_SB, ns, 128), f32), jax.ShapeDtypeStruct((S5_SB, ns, 128), f32),
                   jax.ShapeDtypeStruct((1, 512), f32)],
        scratch_shapes=[pltpu.VMEM((2, t, ns), f32), pltpu.VMEM((8, ns), f32)],
        compiler_params=_cparams(("parallel", "arbitrary")),
    )(hmain, abar, bmat_re, bmat_im, cmat_re, cmat_im, dvec, x_re, x_im, x_re, x_im, dy)


def _loss_head(y, target):
    rows, feat = y.shape
    tile = _pick(rows, 256)

    def kern(y_ref, t_ref, l_ref, dy_ref):
        e = y_ref[...] - t_ref[...]
        dy_ref[...] = e * (1.0 / feat)
        part = jnp.broadcast_to(0.5 * jnp.sum(e * e) * (1.0 / feat), l_ref.shape)

        @pl.when(pl.program_id(0) == 0)
        def _():
            l_ref[...] = part

        @pl.when(pl.program_id(0) > 0)
        def _():
            l_ref[...] += part

    return pl.pallas_call(
        kern, name="loss_head", grid=(rows // tile,),
        in_specs=[pl.BlockSpec((tile, feat), lambda i: (i, 0))] * 2,
        out_specs=[pl.BlockSpec((8, 128), lambda i: (0, 0)), pl.BlockSpec((tile, feat), lambda i: (i, 0))],
        out_shape=[jax.ShapeDtypeStruct((8, 128), f32), jax.ShapeDtypeStruct((rows, feat), f32)],
        compiler_params=_cparams(("arbitrary",)),
    )(y, target)


def _sum_parts(name, parts, out_dtype=f32):
    rows, cols = parts[0].shape
    tile = _pick(rows, 512)

    def kern(*refs):
        acc = refs[0][...].astype(f32)
        for r in refs[1:-1]:
            acc = acc + r[...].astype(f32)
        refs[-1][...] = acc.astype(out_dtype)

    return pl.pallas_call(
        kern, name=name, grid=(rows // tile,),
        in_specs=[pl.BlockSpec((tile, cols), lambda i: (i, 0))] * len(parts),
        out_specs=pl.BlockSpec((tile, cols), lambda i: (i, 0)),
        out_shape=jax.ShapeDtypeStruct((rows, cols), out_dtype),
        compiler_params=_cparams(("parallel",)),
    )(*parts)


ADAMW_BLOCK_BYTES = 1536 * 1024


def _adamw(name, w, m, v, g):
    rows, cols = w.shape
    tile = _pick(rows, 512)
    while tile > 8 and tile * cols * 4 > ADAMW_BLOCK_BYTES and rows % (tile // 2) == 0:
        tile //= 2
    c1, c2 = 1.0 / (1.0 - ADAM_B1 ** ADAM_STEP), 1.0 / (1.0 - ADAM_B2 ** ADAM_STEP)

    def kern(w_ref, m_ref, v_ref, g_ref, d_ref, nm_ref, nv_ref):
        gv = g_ref[...]
        nm = ADAM_B1 * m_ref[...] + (1.0 - ADAM_B1) * gv
        nv = ADAM_B2 * v_ref[...] + (1.0 - ADAM_B2) * (gv * gv)
        nm_ref[...] = nm
        nv_ref[...] = nv
        d_ref[...] = -ADAM_LR * ((nm * c1) / (jnp.sqrt(nv * c2) + ADAM_EPS) + ADAM_WD * w_ref[...])

    spec = pl.BlockSpec((tile, cols), lambda i: (i, 0))
    return pl.pallas_call(
        kern, name=name, grid=(rows // tile,), in_specs=[spec] * 4, out_specs=[spec] * 3,
        out_shape=[jax.ShapeDtypeStruct((rows, cols), f32)] * 3,
        compiler_params=_cparams(("parallel",)),
    )(w, m, v, g)


MESH = pl.DeviceIdType.MESH
HBM_SPEC = pl.BlockSpec(memory_space=pltpu.HBM)


def _other_chips(x, y):
    return [(1 - x, y), (x, 1 - y), (1 - x, 1 - y)]


def _gather_chips(big, small):
    rows, cols = big.shape
    half = rows // 2

    def kern(big_ref, small_ref, bout, sout, ici_send, ici_recv, d2d_send, d2d_recv, small_send, small_recv):
        x, y, c = lax.axis_index("x"), lax.axis_index("y"), lax.axis_index("c")
        me, sibling = 2 * x + y, (x, y, 1 - c)
        chips = _other_chips(x, y)
        slots = [2 * chip[0] + chip[1] for chip in chips]
        sends = [pltpu.make_async_remote_copy(
            src_ref=big_ref.at[c], dst_ref=bout.at[me, c], send_sem=ici_send.at[k], recv_sem=ici_recv.at[k],
            device_id=(*chip, c), device_id_type=MESH) for k, chip in enumerate(chips)]
        sends += [pltpu.make_async_remote_copy(
            src_ref=small_ref, dst_ref=sout.at[me], send_sem=small_send.at[k], recv_sem=small_recv.at[k],
            device_id=(*chip, c), device_id_type=MESH) for k, chip in enumerate(chips)]
        for cp in sends:
            cp.start()
        for k, chip in enumerate(chips):
            pltpu.make_async_remote_copy(
                src_ref=big_ref.at[c], dst_ref=bout.at[slots[k], c], send_sem=ici_send.at[k], recv_sem=ici_recv.at[k],
                device_id=(*chip, c), device_id_type=MESH).wait_recv()
            passed = pltpu.make_async_remote_copy(
                src_ref=bout.at[slots[k], c], dst_ref=bout.at[slots[k], c], send_sem=d2d_send.at[k],
                recv_sem=d2d_recv.at[k], device_id=sibling, device_id_type=MESH)
            passed.start()
            sends.append(passed)
        for k, chip in enumerate(chips):
            pltpu.make_async_remote_copy(
                src_ref=bout.at[slots[k], 1 - c], dst_ref=bout.at[slots[k], 1 - c], send_sem=d2d_send.at[k],
                recv_sem=d2d_recv.at[k], device_id=sibling, device_id_type=MESH).wait_recv()
            pltpu.make_async_remote_copy(
                src_ref=small_ref, dst_ref=sout.at[slots[k]], send_sem=small_send.at[k], recv_sem=small_recv.at[k],
                device_id=(*chip, c), device_id_type=MESH).wait_recv()
        for cp in sends:
            cp.wait_send()

    got_big, got_small = pl.pallas_call(
        kern, name="gather_chips", in_specs=[HBM_SPEC] * 2, out_specs=[HBM_SPEC] * 2,
        out_shape=[jax.ShapeDtypeStruct((N_CHIPS, 2, half, cols), big.dtype), jax.ShapeDtypeStruct((N_CHIPS,) + small.shape, small.dtype)],
        scratch_shapes=[pltpu.SemaphoreType.DMA((3,))] * 6,
    )(big.reshape(2, half, cols), small)
    me = 2 * lax.axis_index("x") + lax.axis_index("y")
    got_big = lax.dynamic_update_index_in_dim(got_big.reshape(N_CHIPS, rows, cols), big, me, 0)
    return got_big, lax.dynamic_update_index_in_dim(got_small, small, me, 0)


def _join_halves(part):
    other = _swap_sibling(part)
    first = lax.axis_index("c") == 0
    return jnp.concatenate([jnp.where(first, part, other), jnp.where(first, other, part)], axis=0)


def _scatter_chips(pieces):
    _, rows, cols = pieces.shape

    def kern(p_ref, out_ref, send_sems, recv_sems):
        x, y, c = lax.axis_index("x"), lax.axis_index("y"), lax.axis_index("c")
        chips = _other_chips(x, y)
        sends = [pltpu.make_async_remote_copy(
            src_ref=p_ref.at[2 * chip[0] + chip[1]], dst_ref=out_ref.at[k], send_sem=send_sems.at[k],
            recv_sem=recv_sems.at[k], device_id=(*chip, c), device_id_type=MESH) for k, chip in enumerate(chips)]
        for cp in sends:
            cp.start()
        for cp in sends:
            cp.wait()

    return pl.pallas_call(
        kern, name="scatter_chips", in_specs=[HBM_SPEC], out_specs=HBM_SPEC,
        out_shape=jax.ShapeDtypeStruct((3, rows, cols), pieces.dtype),
        scratch_shapes=[pltpu.SemaphoreType.DMA((3,)), pltpu.SemaphoreType.DMA((3,))],
    )(pieces)


def _swap_sibling(buf):
    def kern(b_ref, out_ref, send_sem, recv_sem):
        x, y, c = lax.axis_index("x"), lax.axis_index("y"), lax.axis_index("c")
        cp = pltpu.make_async_remote_copy(src_ref=b_ref, dst_ref=out_ref, send_sem=send_sem, recv_sem=recv_sem,
                                          device_id=(x, y, 1 - c), device_id_type=MESH)
        cp.start()
        cp.wait()

    return pl.pallas_call(
        kern, name="swap_sibling", in_specs=[HBM_SPEC], out_specs=HBM_SPEC,
        out_shape=jax.ShapeDtypeStruct(buf.shape, buf.dtype),
        scratch_shapes=[pltpu.SemaphoreType.DMA, pltpu.SemaphoreType.DMA],
    )(buf)


def _gather_all(buf):
    rows, cols = buf.shape

    def kern(b_ref, out_ref, send_sems, recv_sems):
        x, y, c = lax.axis_index("x"), lax.axis_index("y"), lax.axis_index("c")
        me = 4 * x + 2 * y + c
        peers = []
        for k in range(1, N_DEV):
            fx, fy, fc = (k >> 2) & 1, (k >> 1) & 1, k & 1
            peers.append((x ^ fx, y ^ fy, c ^ fc))
        sends = [pltpu.make_async_remote_copy(
            src_ref=b_ref, dst_ref=out_ref.at[me], send_sem=send_sems.at[k], recv_sem=recv_sems.at[k],
            device_id=peer, device_id_type=MESH) for k, peer in enumerate(peers)]
        for cp in sends:
            cp.start()
        for k, peer in enumerate(peers):
            pltpu.make_async_remote_copy(
                src_ref=b_ref, dst_ref=out_ref.at[4 * peer[0] + 2 * peer[1] + peer[2]], send_sem=send_sems.at[k],
                recv_sem=recv_sems.at[k], device_id=peer, device_id_type=MESH).wait_recv()
        for cp in sends:
            cp.wait_send()

    got = pl.pallas_call(
        kern, name="gather_all", in_specs=[HBM_SPEC], out_specs=HBM_SPEC,
        out_shape=jax.ShapeDtypeStruct((N_DEV, rows, cols), buf.dtype),
        scratch_shapes=[pltpu.SemaphoreType.DMA((N_DEV - 1,)), pltpu.SemaphoreType.DMA((N_DEV - 1,))],
    )(buf)
    me = 4 * lax.axis_index("x") + 2 * lax.axis_index("y") + lax.axis_index("c")
    return lax.dynamic_update_index_in_dim(got, buf, me, 0)


def _sum_slots(name, buf):
    n, rows, cols = buf.shape
    tile = _pick(rows, 256)

    def kern(b_ref, o_ref):
        acc = b_ref[0]
        for k in range(1, n):
            acc = acc + b_ref[k]
        o_ref[...] = acc

    return pl.pallas_call(
        kern, name=name, grid=(rows // tile,),
        in_specs=[pl.BlockSpec((n, tile, cols), lambda i: (0, i, 0))], out_specs=pl.BlockSpec((tile, cols), lambda i: (i, 0)),
        out_shape=jax.ShapeDtypeStruct((rows, cols), f32), compiler_params=_cparams(("parallel",)),
    )(buf)


def _pad_heads(t):
    r = t.shape[0]
    return jnp.pad(t.reshape(r, GLA_HEADS, GLA_DK), ((0, 0), (0, 0), (0, 128 - GLA_DK))).reshape(r, GLA_HEADS * 128)


def _unpad_heads(t):
    r = t.shape[0]
    return t.reshape(r, GLA_HEADS, 128)[:, :, :GLA_DK].reshape(r, GLA_HEADS * GLA_DK)


def _blockdiag(t):
    _, r, c = t.shape
    eye = jnp.eye(8, dtype=t.dtype).reshape(1, 8, 1, 8, 1)
    return (t.reshape(S5_SB, 8, r, 1, c) * eye).reshape(S5_SB, 8 * r, 8 * c)


def _blockdiag_extract(m, r, c):
    m5 = m.reshape(S5_SB, 8, r, 8, c)
    return jnp.stack([m5[:, g, :, g, :] for g in range(8)], axis=1).reshape(S5_GROUPS, r, c)


def _row(v, width=None):
    v = v[None]
    return v if width is None else jnp.pad(v, ((0, 0), (0, width - v.shape[1])))


def _layer_operands(p):
    w_in = p['w_in']

    def seg(n):
        return w_in[:, IN_ORIG[n][0]:IN_ORIG[n][1]]

    o = dict(p)
    o['wm'] = jnp.concatenate([seg('dn_qkv'), seg('dn_gate'), seg('cf'), seg('gates'), seg('s5'), _pad_heads(seg('gla_q')),
                               _pad_heads(seg('gla_k')), seg('gla_v'), seg('gla_g')], axis=1)
    o['ws'] = jnp.pad(jnp.concatenate([seg('dn_a'), seg('dn_b'), seg('gla_lr')], axis=1), ((0, 0), (0, WS_COLS - 24)))
    o['dn_conv8'] = jnp.pad(p['dn_conv'], ((0, 4), (0, 0)))
    o['a_log_r'] = _row(p['dn_a_log'], 128)
    o['dt_bias_r'] = _row(p['dn_dt_bias'], 128)
    o['dn_norm_r'] = _row(p['dn_norm'])
    o['cf_dw32'] = jnp.pad(p['cf_dw'], ((0, 1), (0, 0)))
    o['cf_bias_r'], o['cf_g_r'], o['cf_b_r'] = _row(p['cf_dw_bias']), _row(p['cf_ln_g']), _row(p['cf_ln_b'])
    o['w_alpha_p'] = jnp.pad(_pad_heads(p['gla_w_alpha']), ((8, 128 - 24), (0, 0)))
    o['b_alpha_r'] = _pad_heads(_row(p['gla_b_alpha']))
    o['gla_norm_r'] = _row(p['gla_norm'])
    o['ln1_g_r'], o['ln1_b_r'], o['ln2_g_r'], o['ln2_b_r'] = (_row(p[n]) for n in ('ln1_g', 'ln1_b', 'ln2_g', 'ln2_b'))
    o['ffn_conv8'] = jnp.pad(p['ffn_conv'], ((0, 5), (0, 0)))
    o['s5_in'] = (p['s5_a_re'], p['s5_a_im'], p['s5_log_dt'][:, None],
                  p['s5_b_re'].transpose(2, 0, 1), p['s5_b_im'].transpose(2, 0, 1))
    abar_re, abar_im, bbar_re, bbar_im = _s5_params(*o['s5_in'])
    o['abar'] = jnp.stack([abar_re, abar_im]).reshape(2, S5_SB, 1, 512)
    o['bmat_re'], o['bmat_im'] = _blockdiag(bbar_re.transpose(1, 0, 2)), _blockdiag(bbar_im.transpose(1, 0, 2))
    o['cmat_re'], o['cmat_im'] = _blockdiag(p['s5_c_re'].transpose(0, 2, 1)), _blockdiag(p['s5_c_im'].transpose(0, 2, 1))
    o['dvec'] = _row(p['s5_d'])
    return o


def _whole(a):
    return (a, a.shape[1], 0)


def _merge_ins(h, s):
    return [(h, 1024, 3), (h, 1024, 4), (h, 1024, 5), (h, 1024, 6), (s['y_a'], 1024, 0), (s['y_b'], 1024, 0),
            (s['zz'], 1024, 0), (s['zz'], 1024, 1), (s['y_d'], 1024, 0)]


def _layer_fwd(x, o):
    s = {}
    h = s['h'] = _mm(x, o['wm'], 'nn', 'mm_h')
    hs = s['hs'] = _mm(x, o['ws'], 'nn', 'mm_hs')
    s['c1'] = _conv_fwd('conv_dn', h, 1536, 0, o['dn_conv8'], 4)
    s['qkvn'], s['gb'], s['la'] = _rowwise('pre', _f_pre, [_whole(s['c1']), _whole(hs)],
                                           [o['a_log_r'], o['dt_bias_r'], o['w_alpha_p'], o['b_alpha_r']], [1536, 128, 512])
    s['o_dn'], s['st_dn'], s['inv_dn'] = _dn_fwd(s['qkvn'], s['gb'])
    (s['on_dn'],) = _rowwise('post_dn', _f_post, [_whole(s['o_dn']), (h, 512, 3)], [o['dn_norm_r']], [512], out_dtype=bf16)
    s['y_a'] = _mm(s['on_dn'], o['w_br_dn'], 'nn', 'mm_br')
    (s['cfp'],) = _rowwise('glu_cf', _f_glu, [(h, 512, 4), (h, 512, 5)], [], [512], out_dtype=bf16)
    s['cc'] = _conv_fwd('conv_cf', s['cfp'], 512, 0, o['cf_dw32'], 31)
    (s['cfo'],) = _rowwise('post_cf', _f_cfpost, [_whole(s['cc'])], [o['cf_bias_r'], o['cf_g_r'], o['cf_b_r']], [512], out_dtype=bf16)
    s['y_b'] = _mm(s['cfo'], o['w_br_cf'], 'nn', 'mm_br')
    s['ys5'], s['xr'], s['xi'] = _s5_fwd(h, o['abar'], o['bmat_re'], o['bmat_im'], o['cmat_re'], o['cmat_im'], o['dvec'])
    (s['z'],) = _rowwise('gelu', _f_gelu, [_whole(s['ys5'])], [], [512], out_dtype=bf16)
    s['zz'] = _mm(s['z'], o['w_br_s5'], 'nn', 'mm_br_s5')
    s['o_gla'], s['st_gla'] = _gla_fwd(h, s['la'])
    (s['on_gla'],) = _rowwise('post_gla', _f_post, [_whole(s['o_gla']), (h, 512, 18)], [o['gla_norm_r']], [512], out_dtype=bf16)
    s['y_d'] = _mm(s['on_gla'], o['w_br_gla'], 'nn', 'mm_br')
    (s['merged'],) = _rowwise('merge', _f_merge, _merge_ins(h, s), [], [1024], tile=128, out_dtype=bf16)
    s['mix'] = _mm(s['merged'], o['w_o'], 'nn', 'mm_o')
    (s['x1'],) = _rowwise('ln', _f_ln, [_whole(x), _whole(s['mix'])], [o['ln1_g_r'], o['ln1_b_r']], [1024])
    s['up'] = _mm(s['x1'], o['w_up'], 'nn', 'mm_up', out_dtype=bf16)
    s['u'] = _conv_fwd('conv_ffn', s['up'], 2 * D_FF, 0, o['ffn_conv8'], 3)
    (s['act'],) = _rowwise('act', _f_act, [(s['u'], D_FF, 0), (s['u'], D_FF, 1)], [], [D_FF], out_dtype=bf16)
    s['ffn'] = _mm(s['act'], o['w_down'], 'nn', 'mm_down')
    (x2,) = _rowwise('ln', _f_ln, [_whole(s['x1']), _whole(s['ffn'])], [o['ln2_g_r'], o['ln2_b_r']], [1024])
    return x2, s


def _layer_bwd(x, o, s, dparts):
    h, g = s['h'], {}
    (dx1_a, dffn), (g['ln2_g'], g['ln2_b']) = _rowwise_bwd(
        'ln_bwd', _f_ln, [_whole(s['x1']), _whole(s['ffn'])], [o['ln2_g_r'], o['ln2_b_r']], [[_whole(d) for d in dparts]], [f32, bf16])
    dact = _mm(dffn, o['w_down'], 'nt', 'mm_down_dx')
    g['w_down'] = _mm(s['act'], dffn, 'tn', 'mm_down_dw', out_dtype=bf16)
    (du_a, du_b), _ = _rowwise_bwd('act_bwd', _f_act, [(s['u'], D_FF, 0), (s['u'], D_FF, 1)], [], [[_whole(dact)]], [bf16, bf16])
    dup, dw = _conv_bwd('conv_ffn_bwd', s['up'], 2 * D_FF, 0, o['ffn_conv8'], 3, jnp.concatenate([du_a, du_b], axis=1), dx_dtype=bf16)
    g['ffn_conv'] = dw[:3]
    dx1_b = _mm(dup, o['w_up'], 'nt', 'mm_up_dx')
    g['w_up'] = _mm(s['x1'], dup, 'tn', 'mm_up_dw', out_dtype=bf16)
    (dx_a, dmix), (g['ln1_g'], g['ln1_b']) = _rowwise_bwd(
        'ln_bwd', _f_ln, [_whole(x), _whole(s['mix'])], [o['ln1_g_r'], o['ln1_b_r']], [[_whole(dx1_a), _whole(dx1_b)]], [f32, bf16])
    dmerged = _mm(dmix, o['w_o'], 'nt', 'mm_o_dx')
    g['w_o'] = _mm(s['merged'], dmix, 'tn', 'mm_o_dw', out_dtype=bf16)
    (dga, dgb_, dgc, dgd, dya, dyb, dzv, dzg, dyd), _ = _rowwise_bwd(
        'merge_bwd', _f_merge, _merge_ins(h, s), [], [[_whole(dmerged)]], [bf16] * 9, tile=128)
    dzz = jnp.concatenate([dzv, dzg], axis=1)
    don = _mm(dya, o['w_br_dn'], 'nt', 'mm_br_dx')
    g['w_br_dn'] = _mm(s['on_dn'], dya, 'tn', 'mm_br_dw', out_dtype=bf16)
    (do_dn, dgate_dn), (g['dn_norm'],) = _rowwise_bwd(
        'post_bwd', _f_post, [_whole(s['o_dn']), (h, 512, 3)], [o['dn_norm_r']], [[_whole(don)]], [f32, bf16])
    dqkvn, dgb = _dn_bwd(s['qkvn'], s['gb'], s['st_dn'], s['inv_dn'], do_dn)
    don = _mm(dyd, o['w_br_gla'], 'nt', 'mm_br_dx')
    g['w_br_gla'] = _mm(s['on_gla'], dyd, 'tn', 'mm_br_dw', out_dtype=bf16)
    (do_gla, dgate_gla), (g['gla_norm'],) = _rowwise_bwd(
        'post_bwd', _f_post, [_whole(s['o_gla']), (h, 512, 18)], [o['gla_norm_r']], [[_whole(don)]], [f32, bf16])
    dq_gla, dk_gla, dv_gla, dla = _gla_bwd(h, s['la'], s['st_gla'], do_gla)
    (dc1, dhs), (d_alog, d_dtb, d_walpha, d_balpha) = _rowwise_bwd(
        'pre_bwd', _f_pre, [_whole(s['c1']), _whole(s['hs'])], [o['a_log_r'], o['dt_bias_r'], o['w_alpha_p'], o['b_alpha_r']],
        [[_whole(dqkvn)], [_whole(dgb)], [_whole(dla)]], [bf16, bf16])
    g['dn_a_log'], g['dn_dt_bias'] = d_alog[0, :DN_HEADS], d_dtb[0, :DN_HEADS]
    g['gla_w_alpha'], g['gla_b_alpha'] = _unpad_heads(d_walpha[8:24]), _unpad_heads(d_balpha)[0]
    d_dnqkv, dw = _conv_bwd('conv_dn_bwd', h, 1536, 0, o['dn_conv8'], 4, dc1, dx_dtype=bf16)
    g['dn_conv'] = dw[:4]
    dcfo = _mm(dyb, o['w_br_cf'], 'nt', 'mm_br_dx')
    g['w_br_cf'] = _mm(s['cfo'], dyb, 'tn', 'mm_br_dw', out_dtype=bf16)
    (dcc,), (g['cf_dw_bias'], g['cf_ln_g'], g['cf_ln_b']) = _rowwise_bwd(
        'post_cf_bwd', _f_cfpost, [_whole(s['cc'])], [o['cf_bias_r'], o['cf_g_r'], o['cf_b_r']], [[_whole(dcfo)]], [bf16])
    dcfp, dw = _conv_bwd('conv_cf_bwd', s['cfp'], 512, 0, o['cf_dw32'], 31, dcc)
    g['cf_dw'] = dw[:31]
    (dcf_a, dcf_g), _ = _rowwise_bwd('glu_bwd', _f_glu, [(h, 512, 4), (h, 512, 5)], [], [[_whole(dcfp)]], [bf16, bf16])
    dz = _mm(dzz, o['w_br_s5'], 'nt', 'mm_br_s5_dx')
    g['w_br_s5'] = _mm(s['z'], dzz, 'tn', 'mm_br_s5_dw', out_dtype=bf16)
    (dys5,), _ = _rowwise_bwd('gelu_bwd', _f_gelu, [_whole(s['ys5'])], [], [[_whole(dz)]], [f32])
    du_s5, d_abar, dbm_re, dbm_im, dcm_re, dcm_im, d_dvec = _s5_bwd(
        h, o['abar'], o['bmat_re'], o['bmat_im'], o['cmat_re'], o['cmat_im'], o['dvec'], s['xr'], s['xi'], dys5)
    d_bbar = [_blockdiag_extract(m, S5_GROUP, S5_STATE).transpose(1, 0, 2) for m in (dbm_re, dbm_im)]
    da_re, da_im, dlog_dt, db_re, db_im = _s5_params_bwd(
        *o['s5_in'], d_abar[0].reshape(S5_GROUPS, S5_STATE), d_abar[1].reshape(S5_GROUPS, S5_STATE), *d_bbar)
    g['s5_a_re'], g['s5_a_im'], g['s5_log_dt'] = da_re, da_im, dlog_dt[:, 0]
    g['s5_b_re'], g['s5_b_im'] = db_re.transpose(1, 2, 0), db_im.transpose(1, 2, 0)
    g['s5_c_re'], g['s5_c_im'] = (_blockdiag_extract(m, S5_STATE, S5_GROUP).transpose(0, 2, 1) for m in (dcm_re, dcm_im))
    g['s5_d'] = d_dvec[0]
    for n in ('dn_norm', 'gla_norm', 'cf_dw_bias', 'cf_ln_g', 'cf_ln_b', 'ln1_g', 'ln1_b', 'ln2_g', 'ln2_b'):
        g[n] = g[n][0]
    dh = jnp.concatenate([d_dnqkv, dgate_dn, dcf_a, dcf_g, dga, dgb_, dgc, dgd, du_s5, dq_gla, dk_gla, dv_gla, dgate_gla], axis=1)
    dwm = _mm(x, dh, 'tn', 'mm_h_dw', out_dtype=bf16)
    dws = _mm(x, dhs, 'tn', 'mm_hs_dw', out_dtype=bf16)
    g['w_in'] = jnp.concatenate([
        dwm[:, 0:1536], dws[:, 0:8], dwm[:, 1536:2048], dwm[:, 2048:3072], dwm[:, 7168:7680], _unpad_heads(dwm[:, 7680:8192]),
        _unpad_heads(dwm[:, 8192:8704]), dwm[:, 8704:9216], dwm[:, 9216:9728], dws[:, 8:24], dwm[:, 3072:7168]], axis=1)
    return [dx_a, _mm(dh, o['wm'], 'nt', 'mm_h_dx'), _mm(dhs, o['ws'], 'nt', 'mm_hs_dx')], g


WEIGHTS = ('w_in', 'dn_conv', 'dn_a_log', 'dn_dt_bias', 'dn_norm', 'w_br_dn', 'cf_dw', 'cf_dw_bias', 'cf_ln_g', 'cf_ln_b',
           'w_br_cf', 's5_a_re', 's5_a_im', 's5_log_dt', 's5_b_re', 's5_b_im', 's5_c_re', 's5_c_im', 's5_d', 'w_br_s5',
           'gla_w_alpha', 'gla_b_alpha', 'gla_norm', 'w_br_gla', 'w_o', 'ln1_g', 'ln1_b', 'w_up', 'ffn_conv', 'w_down',
           'ln2_g', 'ln2_b')
LARGE = ('w_in', 'w_br_dn', 'w_br_cf', 'w_br_s5', 'w_br_gla', 'w_o', 'w_up', 'w_down')
SHARD_AXIS = dict(w_in=2, w_br_dn=2, w_br_cf=2, w_br_s5=2, w_br_gla=2, w_o=1, w_up=2, w_down=1,
                  dn_conv=2, cf_dw=2, gla_w_alpha=2, ffn_conv=2)
SMALL = tuple(n for n in WEIGHTS if n not in LARGE)
SMALL_SHARDED = tuple(n for n in SMALL if n in SHARD_AXIS)


def _local_step(x, target, full):
    ops, saved, xs = [], [], [x]
    for l in range(DEPTH):
        o = _layer_operands({n: full[n][l] for n in WEIGHTS})
        y, s = _layer_fwd(xs[-1], o)
        ops.append(o)
        saved.append(s)
        xs.append(y)
    loss, dy = _loss_head(xs[-1], target)
    dparts, grads = [dy], [None] * DEPTH
    for l in reversed(range(DEPTH)):
        dparts, grads[l] = _layer_bwd(xs[l], ops[l], saved[l], dparts)
    grad_x = _sum_parts('sum_dx', dparts)
    small = {n: jnp.stack([grads[l][n] for l in range(DEPTH)]) for n in SMALL}
    return loss[0, 0], grad_x, small, {n: [grads[l][n] for l in range(DEPTH)] for n in LARGE}


def _pack(arrs, rows, dtype=f32):
    flat = jnp.concatenate([a.reshape(-1).astype(dtype) for a in arrs])
    return jnp.pad(flat, (0, rows * 1024 - flat.shape[0])).reshape(rows, 1024)


def _unpack(buf, shapes):
    flat, out, pos = buf.reshape(-1), [], 0
    for shp in shapes:
        n = 1
        for d in shp:
            n *= d
        out.append(flat[pos:pos + n].reshape(shp))
        pos += n
    return out


def _rows_for(shapes, mult):
    n = 0
    for shp in shapes:
        k = 1
        for d in shp:
            k *= d
        n += k
    rows = -(-n // 1024)
    return -(-rows // mult) * mult


def _shard(a, axis, chip):
    size = a.shape[axis] // N_CHIPS
    return lax.dynamic_slice_in_dim(a, chip * size, size, axis)


def kernel(x, w_in, dn_conv, dn_a_log, dn_dt_bias, dn_norm, w_br_dn, cf_dw, cf_dw_bias, cf_ln_g, cf_ln_b, w_br_cf, s5_a_re, s5_a_im, s5_log_dt, s5_b_re, s5_b_im, s5_c_re, s5_c_im, s5_d, w_br_s5, gla_w_alpha, gla_b_alpha, gla_norm, w_br_gla, w_o, ln1_g, ln1_b, w_up, ffn_conv, w_down, ln2_g, ln2_b, loss_target, m_w_in, m_dn_conv, m_dn_a_log, m_dn_dt_bias, m_dn_norm, m_w_br_dn, m_cf_dw, m_cf_dw_bias, m_cf_ln_g, m_cf_ln_b, m_w_br_cf, m_s5_a_re, m_s5_a_im, m_s5_log_dt, m_s5_b_re, m_s5_b_im, m_s5_c_re, m_s5_c_im, m_s5_d, m_w_br_s5, m_gla_w_alpha, m_gla_b_alpha, m_gla_norm, m_w_br_gla, m_w_o, m_ln1_g, m_ln1_b, m_w_up, m_ffn_conv, m_w_down, m_ln2_g, m_ln2_b, v_w_in, v_dn_conv, v_dn_a_log, v_dn_dt_bias, v_dn_norm, v_w_br_dn, v_cf_dw, v_cf_dw_bias, v_cf_ln_g, v_cf_ln_b, v_w_br_cf, v_s5_a_re, v_s5_a_im, v_s5_log_dt, v_s5_b_re, v_s5_b_im, v_s5_c_re, v_s5_c_im, v_s5_d, v_w_br_s5, v_gla_w_alpha, v_gla_b_alpha, v_gla_norm, v_w_br_gla, v_w_o, v_ln1_g, v_ln1_b, v_w_up, v_ffn_conv, v_w_down, v_ln2_g, v_ln2_b):
    env = locals()
    w = {n: env[n] for n in WEIGHTS}
    m = {n: env['m_' + n] for n in WEIGHTS}
    v = {n: env['v_' + n] for n in WEIGHTS}
    chip = 2 * lax.axis_index("x") + lax.axis_index("y")

    large_shapes = [w[n].shape for n in LARGE]
    ssh_shapes = [w[n].shape for n in SMALL_SHARDED]
    large_rows, ssh_rows = _rows_for(large_shapes, 512), _rows_for(ssh_shapes, 8)
    got_large, got_ssh = _gather_chips(_pack([w[n] for n in LARGE], large_rows, bf16),
                                       _pack([w[n] for n in SMALL_SHARDED], ssh_rows))
    full = {n: w[n] for n in SMALL if n not in SHARD_AXIS}
    per_chip = [dict(zip(LARGE + SMALL_SHARDED, _unpack(got_large[k], large_shapes) + _unpack(got_ssh[k], ssh_shapes)))
                for k in range(N_CHIPS)]
    for n in LARGE + SMALL_SHARDED:
        full[n] = jnp.concatenate([per_chip[k][n] for k in range(N_CHIPS)], axis=SHARD_AXIS[n])

    loss, grad_x, g, g_large = _local_step(x[0], loss_target[0], full)
    loss = lax.psum(loss, ("x", "y", "c"))

    def piece(k):
        parts = []
        for n in LARGE:
            axis = SHARD_AXIS[n] - 1
            size = g_large[n][0].shape[axis] // N_CHIPS
            parts += [lax.slice_in_dim(layer, k * size, (k + 1) * size, axis=axis) for layer in g_large[n]]
        return _pack(parts, large_rows, bf16)

    half = large_rows // 2
    core = lax.axis_index("c")
    pieces = jnp.stack([piece(k) for k in range(N_CHIPS)])
    p_mine = lax.dynamic_slice_in_dim(pieces, core * half, half, axis=1).reshape(N_CHIPS * half, 1024)
    p_theirs = lax.dynamic_slice_in_dim(pieces, (1 - core) * half, half, axis=1).reshape(N_CHIPS * half, 1024)
    pair = _sum_parts('sum_pair', [p_mine, _swap_sibling(p_theirs)], bf16).reshape(N_CHIPS, half, 1024)
    from_chips = _scatter_chips(pair)
    own = lax.dynamic_index_in_dim(pair, chip, 0, keepdims=False)
    chip_sum = _join_halves(_sum_parts('sum_chips', [own, from_chips[0], from_chips[1], from_chips[2]]))
    res = {0: dict(zip(LARGE, _unpack(chip_sum, large_shapes))), 1: {}, 2: {}, 3: {}}
    for n in LARGE:
        two_d = (w[n].shape[0] * w[n].shape[1], w[n].shape[2])
        upd = _adamw('adamw_large', w[n].reshape(two_d), m[n].reshape(two_d), v[n].reshape(two_d), res[0][n].reshape(two_d))
        for kind in range(3):
            res[kind + 1][n] = upd[kind].reshape(w[n].shape)

    small_full_shapes = [g[n].shape for n in SMALL]
    small_rows = _rows_for(small_full_shapes, 8)
    small_sum = _sum_slots('sum_devices', _gather_all(_pack([g[n] for n in SMALL], small_rows)))
    gs = dict(zip(SMALL, _unpack(small_sum, small_full_shapes)))
    for n in SMALL_SHARDED:
        gs[n] = _shard(gs[n], SHARD_AXIS[n], chip)
    small_shapes = [w[n].shape for n in SMALL]
    upd_rows = _rows_for(small_shapes, 8)
    upd = _adamw('adamw_small', _pack([w[n] for n in SMALL], upd_rows), _pack([m[n] for n in SMALL], upd_rows),
                 _pack([v[n] for n in SMALL], upd_rows), _pack([gs[n] for n in SMALL], upd_rows))
    res[0].update(gs)
    for kind in range(3):
        res[kind + 1].update(zip(SMALL, _unpack(upd[kind], small_shapes)))
    return (loss, grad_x[None], *[res[kind][n] for kind in range(4) for n in WEIGHTS])
```
